```python
import functools
import jax, jax.numpy as jnp
from jax import lax
import numpy as np

D_MODEL = 1024
BATCH = 1
SEQ = 16384
DEPTH = 1
DEC_BATCH = 16
DEC_SEQ = 64
PAST_LEN = 2048

CHUNK = 64
FOX_HEADS = 8
FOX_HEAD_DIM = 64
FOX_WIDTH = FOX_HEADS * FOX_HEAD_DIM
LRU_WIDTH = D_MODEL - FOX_WIDTH
LRU_BLOCKS = 8
LRU_BLOCK_DIM = LRU_WIDTH // LRU_BLOCKS
LRU_C = 8.0
CONV_WIDTH = 4
N_MEM = 256
MEM_HEADS = 4
MEM_HEAD_DIM = D_MODEL // MEM_HEADS
N_EXPERTS = 32
TOP_K = 4
D_FF = D_MODEL
SWIGLU_LIMIT = 7.0
SWIGLU_ALPHA = 1.702
Q_BLOCK = 128
MOE_BLOCK = 128
RMS_EPS = 1e-6
NEG_INF = -1e30
SPLITS = (FOX_WIDTH, 2 * FOX_WIDTH, 3 * FOX_WIDTH, 3 * FOX_WIDTH + FOX_HEADS,
          3 * FOX_WIDTH + FOX_HEADS + LRU_WIDTH)
IN_COLS = 3 * FOX_WIDTH + FOX_HEADS + 2 * LRU_WIDTH

kernel_name = "fox_rglru_moe_stream_step"


def rms_norm(x, g):
    xf = x.astype(jnp.float32)
    y = xf * lax.rsqrt(jnp.mean(xf * xf, axis=-1, keepdims=True) + RMS_EPS)
    return (y * g.astype(jnp.float32)).astype(x.dtype)


def mix_inputs(xn, w_in, b_f):
    B, T, _ = xn.shape
    z = xn @ w_in
    q, k, v, f, xr, gate = jnp.split(z, SPLITS, axis=-1)
    shp = (B, T, FOX_HEADS, FOX_HEAD_DIM)
    logf = jax.nn.log_sigmoid((f + b_f).astype(jnp.float32))
    return q.reshape(shp), k.reshape(shp), v.reshape(shp), logf, xr, gate


def fox_block(q, k, v, fq, fk, q_pos, k_pos):
    s = jnp.einsum('bqhd,bkhd->bhqk', q.astype(jnp.float32), k.astype(jnp.float32)) * (FOX_HEAD_DIM ** -0.5)
    s = s + jnp.swapaxes(fq, 1, 2)[..., :, None] - jnp.swapaxes(fk, 1, 2)[..., None, :]
    s = jnp.where(k_pos[None, :] <= q_pos[:, None], s, NEG_INF)
    p = jax.nn.softmax(s, axis=-1)
    return jnp.einsum('bhqk,bkhd->bqhd', p, v.astype(jnp.float32)).astype(v.dtype)


def fox_prompt(q, k, v, logf):
    B, S, H, Dh = q.shape
    F = jnp.cumsum(logf, axis=1)
    nb = S // Q_BLOCK
    qb = jnp.moveaxis(q.reshape(B, nb, Q_BLOCK, H, Dh), 1, 0)
    Fb = jnp.moveaxis(F.reshape(B, nb, Q_BLOCK, H), 1, 0)
    pb = jnp.arange(S).reshape(nb, Q_BLOCK)
    kpos = jnp.arange(S)
    out = lax.map(lambda a: fox_block(a[0], k, v, a[1], F, a[2], kpos), (qb, Fb, pb))
    return jnp.moveaxis(out, 0, 1).reshape(B, S, H, Dh)


def fox_sample(k_past, v_past, logf_past, q, k, v, logf):
    past = k_past.shape[1]
    T = q.shape[1]
    k_all = jnp.concatenate([k_past.astype(k.dtype), k], axis=1)
    v_all = jnp.concatenate([v_past.astype(v.dtype), v], axis=1)
    F_all = jnp.cumsum(jnp.concatenate([logf_past.astype(jnp.float32), logf], axis=1), axis=1)
    return fox_block(q, k_all, v_all, F_all[:, past:], F_all, past + jnp.arange(T), jnp.arange(past + T))


def causal_conv(x, prev, w, b):
    T = x.shape[1]
    xp = jnp.concatenate([prev.astype(x.dtype), x], axis=1)
    y = b + sum(xp[:, j:j + T] * w[j] for j in range(CONV_WIDTH))
    return y, xp[:, xp.shape[1] - (CONV_WIDTH - 1):]


def block_diag(x, w, b):
    B, T, _ = x.shape
    xb = x.reshape(B, T, LRU_BLOCKS, LRU_BLOCK_DIM)
    return jnp.einsum('btni,nij->btnj', xb, w.astype(jnp.float32)).reshape(B, T, LRU_WIDTH) + b.astype(jnp.float32)


def _lin_combine(left, right):
    a_l, b_l = left
    a_r, b_r = right
    return a_l * a_r, a_r * b_l + b_r


def rg_lru(x, h0, w_a, b_a, w_i, b_i, lam):
    xf = x.astype(jnp.float32)
    r = jax.nn.sigmoid(block_diag(xf, w_a, b_a))
    i = jax.nn.sigmoid(block_diag(xf, w_i, b_i))
    log_a = -LRU_C * r * jax.nn.softplus(-lam.astype(jnp.float32))
    a = jnp.exp(log_a)
    mult = jnp.sqrt(-jnp.expm1(2.0 * log_a))
    if h0 is None:
        bterm = mult.at[:, 0].set(1.0) * i * xf
    else:
        bterm = (mult * i * xf).at[:, 0].add(a[:, 0] * h0.astype(jnp.float32))
    _, h = lax.associative_scan(_lin_combine, (a, bterm), axis=1)
    return h, h[:, -1]


def mix_out(fo, hl, gate, w_out):
    B, T = gate.shape[:2]
    lru = hl.astype(gate.dtype) * jax.nn.gelu(gate)
    return jnp.concatenate([fo.reshape(B, T, FOX_WIDTH), lru], axis=-1) @ w_out


def mem_kv(mem, g_mem, w_ck, w_cv):
    B = mem.shape[0]
    mn = rms_norm(mem, g_mem)
    shp = (B, N_MEM, MEM_HEADS, MEM_HEAD_DIM)
    return (mn @ w_ck).reshape(shp), (mn @ w_cv).reshape(shp)


def cross_attend(xn, mk, mv, w_cq, w_co):
    B, T, _ = xn.shape
    q = (xn @ w_cq).reshape(B, T, MEM_HEADS, MEM_HEAD_DIM)
    s = jnp.einsum('bthd,bmhd->bhtm', q.astype(jnp.float32), mk.astype(jnp.float32)) * (MEM_HEAD_DIM ** -0.5)
    p = jax.nn.softmax(s, axis=-1)
    o = jnp.einsum('bhtm,bmhd->bthd', p, mv.astype(jnp.float32)).reshape(B, T, D_MODEL).astype(xn.dtype)
    return o @ w_co


def moe(xn, w_router, b_router, w_gate, b_gate, w_up, b_up, w_down, b_down):
    B, T, D = xn.shape
    N = B * T
    xf = xn.reshape(N, D)
    logits = (xf @ w_router + b_router).astype(jnp.float32)
    top_logits, top_idx = lax.top_k(logits, TOP_K)
    top_w = jax.nn.softmax(top_logits, axis=-1)
    A = N * TOP_K
    e_flat = top_idx.reshape(A)
    tok_flat = jnp.arange(A) // TOP_K
    w_flat = top_w.reshape(A)
    order = jnp.argsort(e_flat)
    e_sorted = e_flat[order]
    counts = jnp.bincount(e_flat, length=N_EXPERTS)
    padded = (counts + MOE_BLOCK - 1) // MOE_BLOCK * MOE_BLOCK
    pad_end = jnp.cumsum(padded)
    pad_start = pad_end - padded
    start = jnp.cumsum(counts) - counts
    dest = pad_start[e_sorted] + jnp.arange(A) - start[e_sorted]
    n_blocks = -(-A // MOE_BLOCK) + N_EXPERTS
    P = n_blocks * MOE_BLOCK
    slot_tok = jnp.full((P,), N, jnp.int32).at[dest].set(tok_flat[order])
    slot_w = jnp.zeros((P,), jnp.float32).at[dest].set(w_flat[order])
    block_e = jnp.minimum(jnp.searchsorted(pad_end, jnp.arange(n_blocks) * MOE_BLOCK, side='right'), N_EXPERTS - 1)
    x_pad = jnp.concatenate([xf, jnp.zeros((1, D), xf.dtype)], axis=0)
    xs = x_pad[slot_tok].reshape(n_blocks, MOE_BLOCK, D)

    def expert_block(args):
        xb, e = args
        g = jnp.minimum(xb @ w_gate[e] + b_gate[e], SWIGLU_LIMIT)
        u = jnp.clip(xb @ w_up[e] + b_up[e], -SWIGLU_LIMIT, SWIGLU_LIMIT)
        hdn = g * jax.nn.sigmoid(SWIGLU_ALPHA * g) * (u + 1.0)
        return hdn @ w_down[e] + b_down[e]

    ys = lax.map(expert_block, (xs, block_e)).reshape(P, D)
    y = jax.ops.segment_sum(ys.astype(jnp.float32) * slot_w[:, None], slot_tok, num_segments=N + 1)[:N]
    return y.astype(xn.dtype).reshape(B, T, D)


def trunk_layer(h, fox_attend, conv_prev, lru_h0, mk, mv,
                g_mix, w_in, b_f, conv_w, conv_b, w_a, b_a, w_i, b_i, lam, w_out,
                g_cross, w_cq, w_co, g_moe, w_router, b_router,
                w_gate, b_gate, w_up, b_up, w_down, b_down):
    xn = rms_norm(h, g_mix)
    q, k, v, logf, xr, gate = mix_inputs(xn, w_in, b_f)
    fo = fox_attend(q, k, v, logf)
    xc, conv_tail = causal_conv(xr, conv_prev, conv_w, conv_b)
    hl, h_last = rg_lru(xc, lru_h0, w_a, b_a, w_i, b_i, lam)
    h = h + mix_out(fo, hl, gate, w_out)
    h = h + cross_attend(rms_norm(h, g_cross), mk, mv, w_cq, w_co)
    h = h + moe(rms_norm(h, g_moe), w_router, b_router, w_gate, b_gate, w_up, b_up, w_down, b_down)
    return h, k, v, logf, h_last, conv_tail


def setup_inputs(seed: int = 0) -> dict:
    key = jax.random.key(seed)
    ks = iter(jax.random.split(key, 48))
    nrm = lambda shape, scale=1.0: jax.random.normal(next(ks), shape, jnp.float32) * scale
    D, L = D_MODEL, DEPTH
    u = jax.random.uniform(next(ks), (L, LRU_WIDTH), jnp.float32, 0.9, 0.999)
    s = u ** (1.0 / LRU_C)
    return {
        'x_prompt': nrm((BATCH, SEQ, D)),
        'x_sample': nrm((DEC_BATCH, DEC_SEQ, D)),
        'mem_prompt': nrm((BATCH, N_MEM, D)),
        'cache_fox_k': nrm((L, DEC_BATCH, PAST_LEN, FOX_HEADS, FOX_HEAD_DIM)),
        'cache_fox_v': nrm((L, DEC_BATCH, PAST_LEN, FOX_HEADS, FOX_HEAD_DIM)),
        'cache_fox_logf': jax.nn.log_sigmoid(2.0 + nrm((L, DEC_BATCH, PAST_LEN, FOX_HEADS))),
        'state_lru_h': nrm((L, DEC_BATCH, LRU_WIDTH), 0.5),
        'state_conv': nrm((L, DEC_BATCH, CONV_WIDTH - 1, LRU_WIDTH)),
        'cache_mem_k': nrm((L, DEC_BATCH, N_MEM, MEM_HEADS, MEM_HEAD_DIM)),
        'cache_mem_v': nrm((L, DEC_BATCH, N_MEM, MEM_HEADS, MEM_HEAD_DIM)),
        'g_mix': 1.0 + nrm((L, D), 0.02),
        'w_in': nrm((L, D, IN_COLS), D ** -0.5),
        'b_f': 2.0 + nrm((L, FOX_HEADS), 0.5),
        'conv_w': nrm((L, CONV_WIDTH, LRU_WIDTH), CONV_WIDTH ** -0.5),
        'conv_b': nrm((L, LRU_WIDTH), 0.01),
        'w_a': nrm((L, LRU_BLOCKS, LRU_BLOCK_DIM, LRU_BLOCK_DIM), LRU_BLOCK_DIM ** -0.5),
        'b_a': nrm((L, LRU_WIDTH), 0.01),
        'w_i': nrm((L, LRU_BLOCKS, LRU_BLOCK_DIM, LRU_BLOCK_DIM), LRU_BLOCK_DIM ** -0.5),
        'b_i': nrm((L, LRU_WIDTH), 0.01),
        'lam': jnp.log(s) - jnp.log1p(-s),
        'w_out': nrm((L, D, D), D ** -0.5),
        'g_cross': 1.0 + nrm((L, D), 0.02),
        'g_mem': 1.0 + nrm((L, D), 0.02),
        'w_cq': nrm((L, D, D), D ** -0.5),
        'w_ck': nrm((L, D, D), D ** -0.5),
        'w_cv': nrm((L, D, D), D ** -0.5),
        'w_co': nrm((L, D, D), D ** -0.5),
        'g_moe': 1.0 + nrm((L, D), 0.02),
        'w_router': nrm((L, D, N_EXPERTS), D ** -0.5),
        'b_router': nrm((L, N_EXPERTS), 0.01),
        'w_gate': nrm((L, N_EXPERTS, D, D_FF), D ** -0.5),
        'b_gate': nrm((L, N_EXPERTS, D_FF), 0.01),
        'w_up': nrm((L, N_EXPERTS, D, D_FF), D ** -0.5),
        'b_up': nrm((L, N_EXPERTS, D_FF), 0.01),
        'w_down': nrm((L, N_EXPERTS, D_FF, D), D_FF ** -0.5),
        'b_down': nrm((L, N_EXPERTS, D), 0.01),
        'g_final': 1.0 + nrm((D,), 0.02),
    }


def reference(x_prompt, x_sample, mem_prompt, cache_fox_k, cache_fox_v, cache_fox_logf,
              state_lru_h, state_conv, cache_mem_k, cache_mem_v,
              g_mix, w_in, b_f, conv_w, conv_b, w_a, b_a, w_i, b_i, lam, w_out,
              g_cross, g_mem, w_cq, w_ck, w_cv, w_co, g_moe, w_router, b_router,
              w_gate, b_gate, w_up, b_up, w_down, b_down, g_final):
    def layer_w(l):
        return (g_mix[l], w_in[l], b_f[l], conv_w[l], conv_b[l], w_a[l], b_a[l], w_i[l], b_i[l],
                lam[l], w_out[l], g_cross[l], w_cq[l], w_co[l], g_moe[l], w_router[l], b_router[l],
                w_gate[l], b_gate[l], w_up[l], b_up[l], w_down[l], b_down[l])

    hp, hs = x_prompt, x_sample
    pk, pv, pf, ph, pc, pmk, pmv = [], [], [], [], [], [], []
    sk, sv, sf, sh, sc = [], [], [], [], []
    for l in range(DEPTH):
        mk, mv = mem_kv(mem_prompt, g_mem[l], w_ck[l], w_cv[l])
        zero_conv = jnp.zeros((hp.shape[0], CONV_WIDTH - 1, LRU_WIDTH), hp.dtype)
        hp, k, v, lf, hl, ct = trunk_layer(hp, fox_prompt, zero_conv, None, mk, mv, *layer_w(l))
        pk.append(k); pv.append(v); pf.append(lf); ph.append(hl); pc.append(ct); pmk.append(mk); pmv.append(mv)
        fox_s = functools.partial(fox_sample, cache_fox_k[l], cache_fox_v[l], cache_fox_logf[l])
        hs, k, v, lf, hl, ct = trunk_layer(hs, fox_s, state_conv[l], state_lru_h[l],
                                           cache_mem_k[l], cache_mem_v[l], *layer_w(l))
        sk.append(k); sv.append(v); sf.append(lf); sh.append(hl); sc.append(ct)
    y_prompt = rms_norm(hp, g_final)
    y_sample = rms_norm(hs, g_final)
    return (y_prompt, y_sample,
            jnp.stack(pk), jnp.stack(pv), jnp.stack(pf), jnp.stack(ph), jnp.stack(pc),
            jnp.stack(pmk), jnp.stack(pmv),
            jnp.stack(sk), jnp.stack(sv), jnp.stack(sf), jnp.stack(sh), jnp.stack(sc))
```

```python
import functools

import jax
import jax.numpy as jnp
import numpy as np
from jax import lax
from jax.experimental import pallas as pl
from jax.experimental.pallas import tpu as pltpu

F32 = jnp.float32
BF16 = jnp.bfloat16
I32 = jnp.int32

D_MODEL = 1024
FOX_HEADS = 8
FOX_HEAD_DIM = 64
FOX_WIDTH = FOX_HEADS * FOX_HEAD_DIM
LRU_WIDTH = D_MODEL - FOX_WIDTH
LRU_BLOCKS = 8
LRU_C = 8.0
CONV_WIDTH = 4
MEM_HEADS = 4
MEM_HEAD_DIM = D_MODEL // MEM_HEADS
N_EXPERTS = 32
TOP_K = 4
SWIGLU_LIMIT = 7.0
SWIGLU_ALPHA = 1.702
RMS_EPS = 1e-6
NEG_INF = -1e30

LANES = 128
SUBLANES = 8
AUG = LANES * FOX_HEADS
VMEM_LIMIT = 56 * 1024 * 1024

MIX_TILE = 256
FOX_BQ = 512
MOE_BM = 256
COMBINE_TILE = 128

C_Q, C_KA, C_K, C_V, C_F, C_XR, C_G, C_END = 0, 1024, 2048, 2560, 3072, 3200, 3712, 4224


def _dot(a, b):
    return jnp.dot(a, b, preferred_element_type=F32)


def _dot_nt(a, b):
    return lax.dot_general(a, b, (((1,), (1,)), ((), ())), preferred_element_type=F32)


def _split3(x):
    hi = x.astype(BF16)
    r = x - hi.astype(F32)
    mid = r.astype(BF16)
    lo = (r - mid.astype(F32)).astype(BF16)
    return hi, mid, lo


def _dot3(a, x):
    hi, mid, lo = _split3(x)
    return _dot(a, hi) + _dot(a, mid) + _dot(a, lo)


def _lane_split3(x):
    hi, mid, lo = _split3(x)
    lane = lax.broadcasted_iota(I32, x.shape, 1)
    return jnp.where(lane < 8, hi, jnp.where(lane < 16, mid, lo))


def _softplus(x):
    return jnp.maximum(x, 0.0) + jnp.log1p(jnp.exp(-jnp.abs(x)))


def _rms(x, g):
    return x * lax.rsqrt(jnp.mean(x * x, axis=-1, keepdims=True) + RMS_EPS) * g


def _tri(n, kind):
    r = lax.broadcasted_iota(I32, (n, n), 0)
    c = lax.broadcasted_iota(I32, (n, n), 1)
    m = {"le": c <= r, "lt": c < r, "gt": c > r}[kind]
    return jnp.where(m, 1.0, 0.0).astype(BF16)


def _mix_in_kernel(x_ref, g_ref, w_ref, bfp_ref, eq_ref, ek_ref, cq_ref, ck_ref,
                   cw_ref, cb_ref, wa_ref, ba_ref, wi_ref, bi_ref, lam_ref, cprev_ref, h0_ref,
                   qa_ref, ka_ref, k_ref, v_ref, vb_ref, lf_ref, lru_ref, hl_ref, ct_ref,
                   fcar, hcar, xp_ref, sa_ref, sb_ref, *, tm, pad, streaming):
    step = pl.program_id(0)
    first = step == 0

    if streaming:
        @pl.when(first)
        def _():
            fcar[...] = jnp.zeros_like(fcar)
            hcar[...] = jnp.zeros_like(hcar)
            xp_ref[0:SUBLANES, :] = jnp.zeros((SUBLANES, LRU_WIDTH), F32)
    else:
        fcar[...] = jnp.zeros_like(fcar)
        hcar[...] = jnp.broadcast_to(h0_ref[0], hcar.shape)
        xp_ref[0:SUBLANES, :] = cprev_ref[0]

    x = x_ref[...]
    xn = _rms(x, g_ref[...]).astype(BF16)
    z = _dot(xn, w_ref[...])

    k_ref[...] = z[:, C_K:C_V]
    vv = z[:, C_V:C_F]
    v_ref[...] = vv
    vb_ref[...] = vv.astype(BF16)

    lane = lax.broadcasted_iota(I32, (tm, LANES), 1)
    lf = jnp.where(lane < 3 * FOX_HEADS, -_softplus(-(z[:, C_F:C_XR] + bfp_ref[...])), 0.0)
    lf_ref[...] = lf[:, :FOX_HEADS]
    cum = _dot3(_tri(tm, "le"), lf) + fcar[0:1, :]
    fcar[...] = jnp.broadcast_to(cum[tm - 1:tm, :], fcar.shape)
    fs = _lane_split3(cum)
    qa_ref[...] = (z[:, C_Q:C_KA] + _dot(fs, eq_ref[...]) + cq_ref[...]).astype(BF16)
    ka_ref[...] = (z[:, C_KA:C_K] - _dot(fs, ek_ref[...]) + ck_ref[...]).astype(BF16)

    xr = z[:, C_XR:C_G]
    xp_ref[SUBLANES:SUBLANES + tm, :] = xr
    xc = cb_ref[...] + xr * cw_ref[CONV_WIDTH - 1:CONV_WIDTH, :]
    for j in range(CONV_WIDTH - 1):
        sh = CONV_WIDTH - 1 - j
        xc = xc + xp_ref[SUBLANES - sh:SUBLANES - sh + tm, :] * cw_ref[j:j + 1, :]
    tail = xp_ref[tm:tm + SUBLANES, :]
    ct_ref[0] = tail
    xp_ref[0:SUBLANES, :] = tail

    xcb = xc.astype(BF16)
    r = jax.nn.sigmoid(_dot(xcb, wa_ref[...]) + ba_ref[...])
    ig = jax.nn.sigmoid(_dot(xcb, wi_ref[...]) + bi_ref[...])
    log_a = (-LRU_C) * r * _softplus(-lam_ref[...])
    a = jnp.exp(log_a)
    mult = jnp.sqrt(-jnp.tanh(log_a) * (a * a + 1.0))
    if streaming:
        row = lax.broadcasted_iota(I32, (tm, LRU_WIDTH), 0)
        mult = jnp.where(jnp.logical_and(row == 0, first), 1.0, mult)
    b = mult * ig * xc

    sa_ref[0:pad, :] = jnp.ones((pad, LRU_WIDTH), F32)
    sb_ref[0:pad, :] = jnp.zeros((pad, LRU_WIDTH), F32)
    d = 1
    while d < tm:
        sa_ref[pad:pad + tm, :] = a
        sb_ref[pad:pad + tm, :] = b
        b = a * sb_ref[pad - d:pad - d + tm, :] + b
        a = a * sa_ref[pad - d:pad - d + tm, :]
        d *= 2
    h = a * hcar[0:1, :] + b
    hlast = h[tm - 1:tm, :]
    hcar[...] = jnp.broadcast_to(hlast, hcar.shape)
    hl_ref[0] = hlast
    lru_ref[...] = (h * jax.nn.gelu(z[:, C_G:C_END])).astype(BF16)


def _mix_in(x, cprev, h0, wts, *, tm, streaming):
    n = x.shape[0]
    steps = n // tm
    nseg = 1 if streaming else steps
    pad = max(tm // 2, SUBLANES)
    seg = (lambda i: (0, 0, 0)) if streaming else (lambda i: (i, 0, 0))
    full = lambda a: pl.BlockSpec(a.shape, lambda i: (0,) * a.ndim)
    rows = lambda w: pl.BlockSpec((tm, w), lambda i: (i, 0))
    names = ("g_mix", "w_all", "bf_pad", "eq", "ek", "cq", "ck", "conv_w", "conv_b",
             "wa", "ba", "wi", "bi", "lam")
    ws = [wts[k] for k in names]
    out_shape = (
        jax.ShapeDtypeStruct((n, AUG), BF16), jax.ShapeDtypeStruct((n, AUG), BF16),
        jax.ShapeDtypeStruct((n, FOX_WIDTH), F32), jax.ShapeDtypeStruct((n, FOX_WIDTH), F32),
        jax.ShapeDtypeStruct((n, FOX_WIDTH), BF16), jax.ShapeDtypeStruct((n, FOX_HEADS), F32),
        jax.ShapeDtypeStruct((n, LRU_WIDTH), BF16),
        jax.ShapeDtypeStruct((nseg, 1, LRU_WIDTH), F32), jax.ShapeDtypeStruct((nseg, SUBLANES, LRU_WIDTH), F32),
    )
    out_specs = (
        rows(AUG), rows(AUG), rows(FOX_WIDTH), rows(FOX_WIDTH), rows(FOX_WIDTH), rows(FOX_HEADS),
        rows(LRU_WIDTH),
        pl.BlockSpec((1, 1, LRU_WIDTH), seg), pl.BlockSpec((1, SUBLANES, LRU_WIDTH), seg),
    )
    in_specs = [rows(D_MODEL)] + [full(w) for w in ws] + [
        pl.BlockSpec((1, SUBLANES, LRU_WIDTH), seg), pl.BlockSpec((1, 1, LRU_WIDTH), seg)]
    return pl.pallas_call(
        functools.partial(_mix_in_kernel, tm=tm, pad=pad, streaming=streaming),
        grid=(steps,), in_specs=in_specs, out_specs=out_specs, out_shape=out_shape,
        scratch_shapes=[
            pltpu.VMEM((SUBLANES, LANES), F32), pltpu.VMEM((SUBLANES, LRU_WIDTH), F32),
            pltpu.VMEM((tm + SUBLANES, LRU_WIDTH), F32),
            pltpu.VMEM((pad + tm, LRU_WIDTH), F32), pltpu.VMEM((pad + tm, LRU_WIDTH), F32)],
        compiler_params=pltpu.CompilerParams(dimension_semantics=("arbitrary",), vmem_limit_bytes=VMEM_LIMIT),
        name="mix_in_stream" if streaming else "mix_in_segments",
    )(x, *ws, cprev, h0)


def _softmax_step(s, vb, m_ref, l_ref, acc_ref):
    m_old = m_ref[...]
    m_new = jnp.maximum(m_old, jnp.max(s, axis=1, keepdims=True))
    alpha = jnp.exp(m_old - m_new)
    p = jnp.exp(s - m_new)
    l_ref[...] = alpha * l_ref[...] + jnp.sum(p, axis=1, keepdims=True)
    acc_ref[...] = alpha * acc_ref[...] + _dot(p.astype(BF16), vb)
    m_ref[...] = m_new


def _fox_prompt_kernel(q_ref, k_ref, v_ref, o_ref, m_ref, l_ref, acc_ref, *, bq):
    qi = pl.program_id(1)
    outs = []
    for c in range(2):
        cols = slice(LANES * c, LANES * (c + 1))
        q = q_ref[:, cols]
        m_ref[...] = jnp.full(m_ref.shape, NEG_INF, F32)
        l_ref[...] = jnp.zeros(l_ref.shape, F32)
        acc_ref[...] = jnp.zeros(acc_ref.shape, F32)

        def body(j, carry):
            start = pl.multiple_of(j * bq, bq)
            s = _dot_nt(q, k_ref[pl.ds(start, bq), cols])
            _softmax_step(s, v_ref[pl.ds(start, bq), :], m_ref, l_ref, acc_ref)
            return carry

        lax.fori_loop(0, qi, body, 0)
        start = pl.multiple_of(qi * bq, bq)
        s = _dot_nt(q, k_ref[pl.ds(start, bq), cols])
        r = lax.broadcasted_iota(I32, (bq, bq), 0)
        cc = lax.broadcasted_iota(I32, (bq, bq), 1)
        s = jnp.where(cc <= r, s, NEG_INF)
        _softmax_step(s, v_ref[pl.ds(start, bq), :], m_ref, l_ref, acc_ref)
        outs.append(acc_ref[...] / l_ref[...])
    lane = lax.broadcasted_iota(I32, (bq, LANES), 1)
    o_ref[...] = jnp.where(lane < FOX_HEAD_DIM, outs[0], outs[1]).astype(BF16)


def _fox_prompt(qa, ka, vb, *, bq):
    s = qa.shape[0]
    pairs = FOX_HEADS // 2
    return pl.pallas_call(
        functools.partial(_fox_prompt_kernel, bq=bq),
        grid=(pairs, s // bq),
        in_specs=[pl.BlockSpec((bq, 2 * LANES), lambda p, i: (i, p)),
                  pl.BlockSpec((s, 2 * LANES), lambda p, i: (0, p)),
                  pl.BlockSpec((s, LANES), lambda p, i: (0, p))],
        out_specs=pl.BlockSpec((bq, LANES), lambda p, i: (i, p)),
        out_shape=jax.ShapeDtypeStruct((s, FOX_WIDTH), BF16),
        scratch_shapes=[pltpu.VMEM((bq, 1), F32), pltpu.VMEM((bq, 1), F32), pltpu.VMEM((bq, LANES), F32)],
        compiler_params=pltpu.CompilerParams(dimension_semantics=("arbitrary", "arbitrary"),
                                             vmem_limit_bytes=VMEM_LIMIT),
        name="fox_prompt",
    )(qa, ka, vb)


def _fox_sample_kernel(q_ref, kn_ref, vn_ref, ck_ref, cv_ref, clf_ref, ek_ref, ckc_ref, o_ref, g_ref,
                       *, t, past, chunk):
    car = jnp.zeros((1, LANES), F32)
    upper = _tri(chunk, "gt")
    for ci in reversed(range(past // chunk)):
        lf = clf_ref[0, ci * chunk:(ci + 1) * chunk, :]
        g_ref[ci * chunk:(ci + 1) * chunk, :] = _dot3(upper, lf) + car
        car = car + jnp.sum(lf, axis=0, keepdims=True)
    gs = _lane_split3(g_ref[...])

    lane = lax.broadcasted_iota(I32, (past, LANES), 1)
    r = lax.broadcasted_iota(I32, (t, t), 0)
    cc = lax.broadcasted_iota(I32, (t, t), 1)
    olane = lax.broadcasted_iota(I32, (t, LANES), 1)
    for p in range(FOX_HEADS // 2):
        pc = slice(LANES * p, LANES * (p + 1))
        kpair = ck_ref[0, :, pc]
        vpast = cv_ref[0, :, pc].astype(BF16)
        vnew = vn_ref[:, pc]
        outs = []
        for c in range(2):
            h = 2 * p + c
            hc = slice(LANES * h, LANES * (h + 1))
            main = (lane < FOX_HEAD_DIM) if c == 0 else (lane >= FOX_HEAD_DIM)
            extras = _dot(gs, ek_ref[:, hc]) + ckc_ref[:, hc]
            kpast = (jnp.where(main, kpair, 0.0) + extras).astype(BF16)
            q = q_ref[:, hc]
            sp = _dot_nt(q, kpast)
            sn = jnp.where(cc <= r, _dot_nt(q, kn_ref[:, hc]), NEG_INF)
            m = jnp.maximum(jnp.max(sp, axis=1, keepdims=True), jnp.max(sn, axis=1, keepdims=True))
            pp = jnp.exp(sp - m)
            pn = jnp.exp(sn - m)
            l = jnp.sum(pp, axis=1, keepdims=True) + jnp.sum(pn, axis=1, keepdims=True)
            outs.append((_dot(pp.astype(BF16), vpast) + _dot(pn.astype(BF16), vnew)) / l)
        o_ref[:, pc] = jnp.where(olane < FOX_HEAD_DIM, outs[0], outs[1]).astype(BF16)


def _fox_sample(qa, ka, vb, cache_k, cache_v, cache_lf3, ek, ck, *, t):
    nb, past = cache_k.shape[0], cache_k.shape[1]
    full = lambda a: pl.BlockSpec(a.shape, lambda b: (0,) * a.ndim)
    return pl.pallas_call(
        functools.partial(_fox_sample_kernel, t=t, past=past, chunk=256),
        grid=(nb,),
        in_specs=[pl.BlockSpec((t, AUG), lambda b: (b, 0)), pl.BlockSpec((t, AUG), lambda b: (b, 0)),
                  pl.BlockSpec((t, FOX_WIDTH), lambda b: (b, 0)),
                  pl.BlockSpec((1, past, FOX_WIDTH), lambda b: (b, 0, 0)),
                  pl.BlockSpec((1, past, FOX_WIDTH), lambda b: (b, 0, 0)),
                  pl.BlockSpec((1, past, LANES), lambda b: (b, 0, 0)),
                  full(ek), full(ck)],
        out_specs=pl.BlockSpec((t, FOX_WIDTH), lambda b: (b, 0)),
        out_shape=jax.ShapeDtypeStruct((nb * t, FOX_WIDTH), BF16),
        scratch_shapes=[pltpu.VMEM((past, LANES), F32)],
        compiler_params=pltpu.CompilerParams(dimension_semantics=("arbitrary",), vmem_limit_bytes=VMEM_LIMIT),
        name="fox_sample",
    )(qa, ka, vb, cache_k, cache_v, cache_lf3, ek, ck)


def _mem_kv_kernel(m_ref, g_ref, wk_ref, wv_ref, k_ref, v_ref):
    mn = _rms(m_ref[...], g_ref[...]).astype(BF16)
    k_ref[...] = _dot(mn, wk_ref[...])
    v_ref[...] = _dot(mn, wv_ref[...])


def _mem_kv(mem, g, wk, wv):
    n = mem.shape[0]
    return pl.pallas_call(
        _mem_kv_kernel,
        out_shape=(jax.ShapeDtypeStruct((n, D_MODEL), F32), jax.ShapeDtypeStruct((n, D_MODEL), F32)),
        compiler_params=pltpu.CompilerParams(vmem_limit_bytes=VMEM_LIMIT),
        name="mem_kv",
    )(mem, g, wk, wv)


def _post_kernel(h_ref, fo_ref, lru_ref, wo_ref, gc_ref, wcq_ref, mk_ref, mv_ref, wco_ref,
                 gm_ref, wr_ref, br_ref, cnt_in_ref,
                 h2_ref, xn_ref, idx_ref, tw_ref, rank_ref, cnt_ref, car_ref, *, tm):
    @pl.when(pl.program_id(0) == 0)
    def _():
        car_ref[...] = cnt_in_ref[...]

    h1 = h_ref[...] + _dot(fo_ref[...], wo_ref[0:FOX_WIDTH, :]) + _dot(lru_ref[...], wo_ref[FOX_WIDTH:D_MODEL, :])

    q = _dot(_rms(h1, gc_ref[...]).astype(BF16), wcq_ref[...])
    heads = []
    for hd in range(MEM_HEADS):
        hc = slice(MEM_HEAD_DIM * hd, MEM_HEAD_DIM * (hd + 1))
        s = _dot_nt(q[:, hc].astype(BF16), mk_ref[0, :, hc].astype(BF16)) * (MEM_HEAD_DIM ** -0.5)
        p = jnp.exp(s - jnp.max(s, axis=1, keepdims=True))
        o = _dot(p.astype(BF16), mv_ref[0, :, hc].astype(BF16)) / jnp.sum(p, axis=1, keepdims=True)
        heads.append(o.astype(BF16))
    h2 = h1 + _dot(jnp.concatenate(heads, axis=1), wco_ref[...])
    h2_ref[...] = h2

    xn = _rms(h2, gm_ref[...])
    xn_ref[...] = xn

    lane = lax.broadcasted_iota(I32, (tm, LANES), 1).astype(F32)
    logits = jnp.where(lane < N_EXPERTS, _dot(xn.astype(BF16), wr_ref[...]) + br_ref[...], -jnp.inf)
    vals, idxs = [], []
    for _ in range(TOP_K):
        mx = jnp.max(logits, axis=1, keepdims=True)
        ix = jnp.min(jnp.where(logits == mx, lane, float(LANES)), axis=1, keepdims=True)
        vals.append(mx)
        idxs.append(ix)
        logits = jnp.where(lane == ix, -jnp.inf, logits)
    es = [jnp.exp(v - vals[0]) for v in vals]
    den = es[0] + es[1] + es[2] + es[3]

    onehot = jnp.zeros((tm, LANES), F32)
    for ix in idxs:
        onehot = onehot + jnp.where(lane == ix, 1.0, 0.0)
    before = _dot(_tri(tm, "lt"), onehot.astype(BF16)) + car_ref[0:1, :]
    car = car_ref[0:1, :] + jnp.sum(onehot, axis=0, keepdims=True)
    car_ref[...] = jnp.broadcast_to(car, car_ref.shape)
    cnt_ref[...] = jnp.broadcast_to(car, cnt_ref.shape)

    idx_o = jnp.zeros((tm, LANES), F32)
    tw_o = jnp.zeros((tm, LANES), F32)
    rk_o = jnp.zeros((tm, LANES), F32)
    for j in range(TOP_K):
        rk = jnp.sum(jnp.where(lane == idxs[j], before, 0.0), axis=1, keepdims=True)
        idx_o = jnp.where(lane == j, idxs[j], idx_o)
        tw_o = jnp.where(lane == j, es[j] / den, tw_o)
        rk_o = jnp.where(lane == j, rk, rk_o)
    idx_ref[...] = idx_o.astype(I32)
    tw_ref[...] = tw_o
    rank_ref[...] = rk_o.astype(I32)


def _post(h, fo, lru, mk, mv, cnt_in, wts, *, tm, per_step_memory):
    n = h.shape[0]
    nm = mk.shape[1]
    full = lambda a: pl.BlockSpec(a.shape, lambda i: (0,) * a.ndim)
    rows = lambda w: pl.BlockSpec((tm, w), lambda i: (i, 0))
    mem = pl.BlockSpec((1, nm, D_MODEL), (lambda i: (i, 0, 0)) if per_step_memory else (lambda i: (0, 0, 0)))
    w = [wts[k] for k in ("w_out", "g_cross", "w_cq")]
    w2 = [wts[k] for k in ("w_co", "g_moe", "w_router", "b_router")]
    return pl.pallas_call(
        functools.partial(_post_kernel, tm=tm),
        grid=(n // tm,),
        in_specs=[rows(D_MODEL), rows(FOX_WIDTH), rows(LRU_WIDTH)] + [full(a) for a in w] + [mem, mem]
                 + [full(a) for a in w2] + [full(cnt_in)],
        out_specs=(rows(D_MODEL), rows(D_MODEL), rows(LANES), rows(LANES), rows(LANES),
                   pl.BlockSpec((SUBLANES, LANES), lambda i: (0, 0))),
        out_shape=(jax.ShapeDtypeStruct((n, D_MODEL), F32), jax.ShapeDtypeStruct((n, D_MODEL), F32),
                   jax.ShapeDtypeStruct((n, LANES), I32), jax.ShapeDtypeStruct((n, LANES), F32),
                   jax.ShapeDtypeStruct((n, LANES), I32), jax.ShapeDtypeStruct((SUBLANES, LANES), F32)),
        scratch_shapes=[pltpu.VMEM((SUBLANES, LANES), F32)],
        compiler_params=pltpu.CompilerParams(dimension_semantics=("arbitrary",), vmem_limit_bytes=VMEM_LIMIT),
        name="post_shared_mem" if not per_step_memory else "post_batch_mem",
    )(h, fo, lru, *w, mk, mv, *w2, cnt_in)


def _dispatch_kernel(dest_ref, x_ref, xs_in_ref, xs_ref, sem, *, tm):
    del xs_in_ref
    base = pl.program_id(0) * (tm * TOP_K)

    def copy(t, j):
        d = dest_ref[base + t * TOP_K + j]
        return pltpu.make_async_copy(x_ref.at[pl.ds(t, 1), :], xs_ref.at[pl.ds(d, 1), :], sem)

    def start(t, carry):
        for j in range(TOP_K):
            copy(t, j).start()
        return carry

    def wait(t, carry):
        for j in range(TOP_K):
            copy(t, j).wait()
        return carry

    lax.fori_loop(0, tm, start, 0)
    lax.fori_loop(0, tm, wait, 0)


def _dispatch(dest_flat, xn, xs_init, *, tm):
    n = xn.shape[0]
    return pl.pallas_call(
        functools.partial(_dispatch_kernel, tm=tm),
        grid_spec=pltpu.PrefetchScalarGridSpec(
            num_scalar_prefetch=1, grid=(n // tm,),
            in_specs=[pl.BlockSpec((tm, D_MODEL), lambda i, d: (i, 0)), pl.BlockSpec(memory_space=pl.ANY)],
            out_specs=pl.BlockSpec(memory_space=pl.ANY),
            scratch_shapes=[pltpu.SemaphoreType.DMA(())]),
        out_shape=jax.ShapeDtypeStruct(xs_init.shape, F32),
        input_output_aliases={2: 0},
        compiler_params=pltpu.CompilerParams(dimension_semantics=("arbitrary",), vmem_limit_bytes=VMEM_LIMIT),
        name="moe_dispatch",
    )(dest_flat, xn, xs_init)


def _expert_kernel(be_ref, nu_ref, xs_ref, wg_ref, bg_ref, wu_ref, bu_ref, wd_ref, bd_ref, ys_ref,
                   wgb, wub, wdb):
    i = pl.program_id(0)
    prev = be_ref[jnp.maximum(i - 1, 0)]
    fresh = jnp.logical_or(i == 0, be_ref[i] != prev)
    live = i < nu_ref[0]

    @pl.when(jnp.logical_and(live, fresh))
    def _():
        wgb[...] = wg_ref[0].astype(BF16)
        wub[...] = wu_ref[0].astype(BF16)
        wdb[...] = wd_ref[0].astype(BF16)

    @pl.when(live)
    def _():
        x = xs_ref[...].astype(BF16)
        g = jnp.minimum(_dot(x, wgb[...]) + bg_ref[0], SWIGLU_LIMIT)
        u = jnp.clip(_dot(x, wub[...]) + bu_ref[0], -SWIGLU_LIMIT, SWIGLU_LIMIT)
        hdn = g * jax.nn.sigmoid(SWIGLU_ALPHA * g) * (u + 1.0)
        ys_ref[...] = _dot(hdn.astype(BF16), wdb[...]) + bd_ref[0]

    @pl.when(jnp.logical_not(live))
    def _():
        ys_ref[...] = jnp.zeros(ys_ref.shape, F32)


def _experts(block_e, n_used, xs, wts, *, bm):
    p = xs.shape[0]
    blk = lambda i, be, nu: (jnp.minimum(i, nu[0] - 1), 0)
    oblk = lambda i, be, nu: (i, 0)
    wsel = lambda i, be, nu: (be[i], 0, 0)
    wspec = pl.BlockSpec((1, D_MODEL, D_MODEL), wsel)
    bspec = pl.BlockSpec((1, 1, D_MODEL), wsel)
    return pl.pallas_call(
        _expert_kernel,
        grid_spec=pltpu.PrefetchScalarGridSpec(
            num_scalar_prefetch=2, grid=(p // bm,),
            in_specs=[pl.BlockSpec((bm, D_MODEL), blk), wspec, bspec, wspec, bspec, wspec, bspec],
            out_specs=pl.BlockSpec((bm, D_MODEL), oblk),
            scratch_shapes=[pltpu.VMEM((D_MODEL, D_MODEL), BF16)] * 3),
        out_shape=jax.ShapeDtypeStruct((p, D_MODEL), F32),
        compiler_params=pltpu.CompilerParams(dimension_semantics=("arbitrary",), vmem_limit_bytes=VMEM_LIMIT),
        name="moe_experts",
    )(block_e, n_used, xs, wts["w_gate"], wts["b_gate"], wts["w_up"], wts["b_up"], wts["w_down"], wts["b_down"])


def _combine_kernel(dest_ref, h_ref, tw_ref, gf_ref, ys_ref, y_ref, buf, sem, *, tm):
    base = pl.program_id(0) * (tm * TOP_K)

    def copy(t, j):
        d = dest_ref[base + t * TOP_K + j]
        return pltpu.make_async_copy(ys_ref.at[pl.ds(d, 1), :], buf.at[j, pl.ds(t, 1), :], sem)

    def start(t, carry):
        for j in range(TOP_K):
            copy(t, j).start()
        return carry

    def wait(t, carry):
        for j in range(TOP_K):
            copy(t, j).wait()
        return carry

    lax.fori_loop(0, tm, start, 0)
    lax.fori_loop(0, tm, wait, 0)
    tw = tw_ref[...]
    y = h_ref[...]
    for j in range(TOP_K):
        y = y + buf[j] * tw[:, j:j + 1]
    y_ref[...] = _rms(y, gf_ref[...])


def _combine(dest_flat, h2, tw, g_final, ys, *, tm):
    n = h2.shape[0]
    return pl.pallas_call(
        functools.partial(_combine_kernel, tm=tm),
        grid_spec=pltpu.PrefetchScalarGridSpec(
            num_scalar_prefetch=1, grid=(n // tm,),
            in_specs=[pl.BlockSpec((tm, D_MODEL), lambda i, d: (i, 0)),
                      pl.BlockSpec((tm, LANES), lambda i, d: (i, 0)),
                      pl.BlockSpec((1, D_MODEL), lambda i, d: (0, 0)),
                      pl.BlockSpec(memory_space=pl.ANY)],
            out_specs=pl.BlockSpec((tm, D_MODEL), lambda i, d: (i, 0)),
            scratch_shapes=[pltpu.VMEM((TOP_K, tm, D_MODEL), F32), pltpu.SemaphoreType.DMA(())]),
        out_shape=jax.ShapeDtypeStruct((n, D_MODEL), F32),
        compiler_params=pltpu.CompilerParams(dimension_semantics=("arbitrary",), vmem_limit_bytes=VMEM_LIMIT),
        name="moe_combine",
    )(dest_flat, h2, tw, g_final, ys)


def _aug_layout():
    main, extra = [], []
    for h in range(FOX_HEADS):
        even = h % 2 == 0
        main.append(LANES * h + (0 if even else FOX_HEAD_DIM))
        extra.append(LANES * h + (FOX_HEAD_DIM if even else 0))
    return main, extra


def _prep_mix_weights(g_mix, w_in, b_f, conv_w, conv_b, w_a, b_a, w_i, b_i, lam):
    main, extra = _aug_layout()
    wq, wk, wv = w_in[:, 0:512], w_in[:, 512:1024], w_in[:, 1024:1536]
    wf, wxr, wg = w_in[:, 1536:1544], w_in[:, 1544:2056], w_in[:, 2056:2568]
    src = np.arange(FOX_WIDTH)
    dst = np.array([main[c // FOX_HEAD_DIM] + c % FOX_HEAD_DIM for c in src])
    wq_aug = jnp.zeros((D_MODEL, AUG), F32).at[:, dst].set(wq * (FOX_HEAD_DIM ** -0.5))
    wk_aug = jnp.zeros((D_MODEL, AUG), F32).at[:, dst].set(wk)
    wf_pad = jnp.zeros((D_MODEL, LANES), F32).at[:, 0:24].set(jnp.tile(wf, (1, 3)))
    w_all = jnp.concatenate([wq_aug, wk_aug, wk, wv, wf_pad, wxr, wg], axis=1).astype(BF16)
    bf_pad = jnp.zeros((1, LANES), F32).at[0, 0:24].set(jnp.tile(b_f, 3))
    eq = np.zeros((LANES, AUG), np.float32)
    ek = np.zeros((LANES, AUG), np.float32)
    cq = np.zeros((1, AUG), np.float32)
    ck = np.zeros((1, AUG), np.float32)
    for h in range(FOX_HEADS):
        for part in range(3):
            eq[part * 8 + h, extra[h] + part] = 1.0
            ek[part * 8 + h, extra[h] + 3 + part] = 1.0
            cq[0, extra[h] + 3 + part] = 1.0
            ck[0, extra[h] + part] = 1.0
    dense = lambda w: jax.scipy.linalg.block_diag(*[w[i] for i in range(LRU_BLOCKS)]).astype(BF16)
    row = lambda a: a.reshape(1, -1)
    return dict(g_mix=row(g_mix), w_all=w_all, bf_pad=bf_pad, eq=jnp.asarray(eq, BF16), ek=jnp.asarray(ek, BF16),
                cq=jnp.asarray(cq), ck=jnp.asarray(ck), conv_w=conv_w, conv_b=row(conv_b),
                wa=dense(w_a), ba=row(b_a), wi=dense(w_i), bi=row(b_i), lam=row(lam))


def kernel(x_prompt, x_sample, mem_prompt, cache_fox_k, cache_fox_v, cache_fox_logf, state_lru_h, state_conv, cache_mem_k, cache_mem_v, g_mix, w_in, b_f, conv_w, conv_b, w_a, b_a, w_i, b_i, lam, w_out, g_cross, g_mem, w_cq, w_ck, w_cv, w_co, g_moe, w_router, b_router, w_gate, b_gate, w_up, b_up, w_down, b_down, g_final):
    nb_p, seq, _ = x_prompt.shape
    nb_s, t_s, _ = x_sample.shape
    past = cache_fox_k.shape[2]
    n_mem = mem_prompt.shape[1]
    assert nb_p == 1 and g_mix.shape[0] == 1
    n_p, n_s = seq, nb_s * t_s
    row = lambda a: a.reshape(1, -1)

    mw = _prep_mix_weights(g_mix[0], w_in[0], b_f[0], conv_w[0], conv_b[0], w_a[0], b_a[0], w_i[0], b_i[0], lam[0])
    xp = x_prompt.reshape(n_p, D_MODEL)
    xs_ = x_sample.reshape(n_s, D_MODEL)

    zero_prev = jnp.zeros((1, SUBLANES, LRU_WIDTH), F32)
    zero_h = jnp.zeros((1, 1, LRU_WIDTH), F32)
    (qa_p, ka_p, k_p, v_p, vb_p, lf_p, lru_p, hl_p, ct_p) = _mix_in(
        xp, zero_prev, zero_h, mw, tm=MIX_TILE, streaming=True)
    prev_s = jnp.pad(state_conv[0], ((0, 0), (SUBLANES - (CONV_WIDTH - 1), 0), (0, 0)))
    (qa_s, ka_s, k_s, v_s, vb_s, lf_s, lru_s, hl_s, ct_s) = _mix_in(
        xs_, prev_s, state_lru_h[0].reshape(nb_s, 1, LRU_WIDTH), mw, tm=t_s, streaming=False)

    fo_p = _fox_prompt(qa_p, ka_p, vb_p, bq=FOX_BQ)
    clf3 = jnp.pad(jnp.tile(cache_fox_logf[0], (1, 1, 3)), ((0, 0), (0, 0), (0, LANES - 3 * FOX_HEADS)))
    fo_s = _fox_sample(qa_s, ka_s, vb_s, cache_fox_k[0].reshape(nb_s, past, FOX_WIDTH),
                       cache_fox_v[0].reshape(nb_s, past, FOX_WIDTH), clf3, mw["ek"], mw["ck"], t=t_s)

    mk_p, mv_p = _mem_kv(mem_prompt[0], row(g_mem[0]), w_ck[0].astype(BF16), w_cv[0].astype(BF16))

    wr_pad = jnp.zeros((D_MODEL, LANES), F32).at[:, :N_EXPERTS].set(w_router[0]).astype(BF16)
    br_pad = jnp.zeros((1, LANES), F32).at[0, :N_EXPERTS].set(b_router[0])
    pw = dict(w_out=w_out[0].astype(BF16), g_cross=row(g_cross[0]), w_cq=w_cq[0].astype(BF16),
              w_co=w_co[0].astype(BF16), g_moe=row(g_moe[0]), w_router=wr_pad, b_router=br_pad)
    cnt0 = jnp.zeros((SUBLANES, LANES), F32)
    h2_p, xn_p, idx_p, tw_p, rk_p, cnt_p = _post(
        xp, fo_p, lru_p, mk_p.reshape(1, n_mem, D_MODEL), mv_p.reshape(1, n_mem, D_MODEL), cnt0, pw,
        tm=MIX_TILE, per_step_memory=False)
    h2_s, xn_s, idx_s, tw_s, rk_s, cnt = _post(
        xs_, fo_s, lru_s, cache_mem_k[0].reshape(nb_s, n_mem, D_MODEL), cache_mem_v[0].reshape(nb_s, n_mem, D_MODEL),
        cnt_p, pw, tm=t_s, per_step_memory=True)

    n = n_p + n_s
    cat = lambda a, b: jnp.concatenate([a, b], axis=0)
    h2, xn, tw = cat(h2_p, h2_s), cat(xn_p, xn_s), cat(tw_p, tw_s)
    idx = cat(idx_p, idx_s)[:, :TOP_K]
    rank = cat(rk_p, rk_s)[:, :TOP_K]
    counts = cnt[0, :N_EXPERTS].astype(I32)
    padded = (counts + MOE_BM - 1) // MOE_BM * MOE_BM
    pad_end = jnp.cumsum(padded)
    pad_start = pad_end - padded
    dest = (pad_start[idx] + rank).reshape(n * TOP_K)
    n_blocks = -(-(n * TOP_K) // MOE_BM) + N_EXPERTS
    block_e = jnp.minimum(jnp.searchsorted(pad_end, jnp.arange(n_blocks, dtype=I32) * MOE_BM, side="right"),
                          N_EXPERTS - 1).astype(I32)
    n_used = (pad_end[-1] // MOE_BM).reshape(1).astype(I32)

    xs_sorted = _dispatch(dest, xn, jnp.zeros((n_blocks * MOE_BM, D_MODEL), F32), tm=MIX_TILE)
    ew = dict(w_gate=w_gate[0], w_up=w_up[0], w_down=w_down[0],
              b_gate=b_gate[0].reshape(N_EXPERTS, 1, D_MODEL), b_up=b_up[0].reshape(N_EXPERTS, 1, D_MODEL),
              b_down=b_down[0].reshape(N_EXPERTS, 1, D_MODEL))
    ys = _experts(block_e, n_used, xs_sorted, ew, bm=MOE_BM)
    y = _combine(dest, h2, tw, row(g_final), ys, tm=COMBINE_TILE)

    shp_p = (1, nb_p, seq, FOX_HEADS, FOX_HEAD_DIM)
    shp_s = (1, nb_s, t_s, FOX_HEADS, FOX_HEAD_DIM)
    tail = slice(SUBLANES - (CONV_WIDTH - 1), SUBLANES)
    return (y[:n_p].reshape(nb_p, seq, D_MODEL), y[n_p:].reshape(nb_s, t_s, D_MODEL),
            k_p.reshape(shp_p), v_p.reshape(shp_p), lf_p.reshape(1, nb_p, seq, FOX_HEADS),
            hl_p.reshape(1, nb_p, LRU_WIDTH), ct_p[:, tail, :].reshape(1, nb_p, CONV_WIDTH - 1, LRU_WIDTH),
            mk_p.reshape(1, nb_p, n_mem, MEM_HEADS, MEM_HEAD_DIM), mv_p.reshape(1, nb_p, n_mem, MEM_HEADS, MEM_HEAD_DIM),
            k_s.reshape(shp_s), v_s.reshape(shp_s), lf_s.reshape(1, nb_s, t_s, FOX_HEADS),
            hl_s.reshape(1, nb_s, LRU_WIDTH), ct_s[:, tail, :].reshape(1, nb_s, CONV_WIDTH - 1, LRU_WIDTH))
```

```python
import functools
import math

import jax
import jax.numpy as jnp
import numpy as np
from jax import lax
from jax.experimental import pallas as pl
from jax.experimental.pallas import tpu as pltpu

F32 = jnp.float32
BF16 = jnp.bfloat16
I32 = jnp.int32

D_MODEL = 1024
FOX_HEADS = 8
FOX_HEAD_DIM = 64
FOX_WIDTH = FOX_HEADS * FOX_HEAD_DIM
LRU_WIDTH = D_MODEL - FOX_WIDTH
LRU_BLOCKS = 8
LRU_C = 8.0
CONV_WIDTH = 4
MEM_HEADS = 4
MEM_HEAD_DIM = D_MODEL // MEM_HEADS
N_EXPERTS = 32
TOP_K = 4
SWIGLU_LIMIT = 7.0
SWIGLU_ALPHA = 1.702
RMS_EPS = 1e-6
NEG_INF = -1e30
LOG2E = math.log2(math.e)

LANES = 128
SUBLANES = 8
AUG = LANES * FOX_HEADS
VMEM_LIMIT = 56 * 1024 * 1024

MIX_TILE = 256
FOX_BQ = 512
MOE_BM = 512
COMBINE_TILE = 128
FOX_SKIP_GAP = 160.0
FOX_STATIC_SHIFT_RANGE = 100.0

C_Q, C_KA, C_VA, C_F, C_XR, C_G, C_END = 0, 1024, 2048, 3072, 3200, 3712, 4224


def _dot(a, b):
    return jnp.dot(a, b, preferred_element_type=F32)


def _dot_nt(a, b):
    return lax.dot_general(a, b, (((1,), (1,)), ((), ())), preferred_element_type=F32)


def _split3(x):
    hi = x.astype(BF16)
    r = x - hi.astype(F32)
    mid = r.astype(BF16)
    lo = (r - mid.astype(F32)).astype(BF16)
    return hi, mid, lo


def _dot3(a, x):
    hi, mid, lo = _split3(x)
    return _dot(a, hi) + _dot(a, mid) + _dot(a, lo)


def _lane_split3(x):
    hi, mid, lo = _split3(x)
    lane = lax.broadcasted_iota(I32, x.shape, 1)
    return jnp.where(lane < 8, hi, jnp.where(lane < 16, mid, lo))


def _softplus(x):
    return jnp.maximum(x, 0.0) + jnp.log1p(jnp.exp(-jnp.abs(x)))


def _rms(x, g):
    return x * lax.rsqrt(jnp.mean(x * x, axis=-1, keepdims=True) + RMS_EPS) * g


def _tri(n, kind):
    r = lax.broadcasted_iota(I32, (n, n), 0)
    c = lax.broadcasted_iota(I32, (n, n), 1)
    m = {"le": c <= r, "lt": c < r, "gt": c > r}[kind]
    return jnp.where(m, 1.0, 0.0).astype(BF16)


def _head_sumsq_max(zb, hsel):
    sq = zb.astype(F32)
    return jnp.max(_dot((sq * sq).astype(BF16), hsel), axis=0, keepdims=True)


def _mix_in_kernel(x_ref, g_ref, w_ref, bfp_ref, eq_ref, ek_ref, ec_ref, cq_ref, ck_ref, cv_ref, hs_ref,
                   cw_ref, cb_ref, wa_ref, ba_ref, wi_ref, bi_ref, lam_ref, cprev_ref, h0_ref,
                   qa_ref, ka_ref, va_ref, k_ref, v_ref, vb_ref, lf_ref, f2_ref, qn_ref, kn_ref,
                   lru_ref, hl_ref, ct_ref,
                   fcar, hcar, xp_ref, sa_ref, sb_ref, *, tm, pad, streaming):
    step = pl.program_id(0)
    first = step == 0

    if streaming:
        @pl.when(first)
        def _():
            fcar[...] = jnp.zeros_like(fcar)
            hcar[...] = jnp.zeros_like(hcar)
            xp_ref[0:SUBLANES, :] = jnp.zeros((SUBLANES, LRU_WIDTH), F32)
    else:
        fcar[...] = jnp.zeros_like(fcar)
        hcar[...] = jnp.broadcast_to(h0_ref[0], hcar.shape)
        xp_ref[0:SUBLANES, :] = cprev_ref[0]

    x = x_ref[...]
    xn = _rms(x, g_ref[...]).astype(BF16)
    z = _dot(xn, w_ref[...])

    for p in range(FOX_HEADS // 2):
        a, b = 2 * p * LANES, (2 * p + 1) * LANES
        kk = z[:, C_KA + a:C_KA + b] + z[:, C_KA + b:C_KA + b + LANES]
        vv = z[:, C_VA + a:C_VA + b] + z[:, C_VA + b:C_VA + b + LANES]
        k_ref[:, p * LANES:(p + 1) * LANES] = kk
        v_ref[:, p * LANES:(p + 1) * LANES] = vv
        vb_ref[:, p * LANES:(p + 1) * LANES] = vv.astype(BF16)
    va_ref[...] = (z[:, C_VA:C_F] + cv_ref[...]).astype(BF16)

    lane = lax.broadcasted_iota(I32, (tm, LANES), 1)
    lf = jnp.where(lane < 3 * FOX_HEADS, -_softplus(-(z[:, C_F:C_XR] + bfp_ref[...])), 0.0)
    lf_ref[...] = lf[:, :FOX_HEADS]
    cum = _dot3(_tri(tm, "le"), lf) + fcar[0:1, :]
    fcar[...] = jnp.broadcast_to(cum[tm - 1:tm, :], fcar.shape)
    cum2 = cum * LOG2E
    f2_ref[...] = cum2[:, :FOX_HEADS]
    fs = _lane_split3(cum2)
    qb = z[:, C_Q:C_KA].astype(BF16)
    kb = z[:, C_KA:C_VA].astype(BF16)
    diag = _dot((qb.astype(F32) * kb.astype(F32)).astype(BF16), hs_ref[...]).astype(BF16)
    qa_ref[...] = qb + (_dot(fs, eq_ref[...]) + cq_ref[...] - _dot(diag, ec_ref[...])).astype(BF16)
    ka_ref[...] = kb + (ck_ref[...] - _dot(fs, ek_ref[...])).astype(BF16)
    qn_ref[...] = jnp.broadcast_to(_head_sumsq_max(qb, hs_ref[...]), qn_ref.shape)
    kn_ref[...] = jnp.broadcast_to(_head_sumsq_max(kb, hs_ref[...]), kn_ref.shape)

    xr = z[:, C_XR:C_G]
    xp_ref[SUBLANES:SUBLANES + tm, :] = xr
    xc = cb_ref[...] + xr * cw_ref[CONV_WIDTH - 1:CONV_WIDTH, :]
    for j in range(CONV_WIDTH - 1):
        sh = CONV_WIDTH - 1 - j
        xc = xc + xp_ref[SUBLANES - sh:SUBLANES - sh + tm, :] * cw_ref[j:j + 1, :]
    tail = xp_ref[tm:tm + SUBLANES, :]
    ct_ref[0] = tail
    xp_ref[0:SUBLANES, :] = tail

    xcb = xc.astype(BF16)
    r = jax.nn.sigmoid(_dot(xcb, wa_ref[...]) + ba_ref[...])
    ig = jax.nn.sigmoid(_dot(xcb, wi_ref[...]) + bi_ref[...])
    log_a = (-LRU_C) * r * _softplus(-lam_ref[...])
    a = jnp.exp(log_a)
    mult = jnp.sqrt(-jnp.tanh(log_a) * (a * a + 1.0))
    if streaming:
        row = lax.broadcasted_iota(I32, (tm, LRU_WIDTH), 0)
        mult = jnp.where(jnp.logical_and(row == 0, first), 1.0, mult)
    b = mult * ig * xc

    sa_ref[0:pad, :] = jnp.ones((pad, LRU_WIDTH), F32)
    sb_ref[0:pad, :] = jnp.zeros((pad, LRU_WIDTH), F32)
    d = 1
    while d < tm:
        sa_ref[pad:pad + tm, :] = a
        sb_ref[pad:pad + tm, :] = b
        b = a * sb_ref[pad - d:pad - d + tm, :] + b
        a = a * sa_ref[pad - d:pad - d + tm, :]
        d *= 2
    h = a * hcar[0:1, :] + b
    hlast = h[tm - 1:tm, :]
    hcar[...] = jnp.broadcast_to(hlast, hcar.shape)
    hl_ref[0] = hlast
    lru_ref[...] = (h * jax.nn.gelu(z[:, C_G:C_END])).astype(BF16)


def _mix_in(x, cprev, h0, wts, *, tm, streaming):
    n = x.shape[0]
    steps = n // tm
    nseg = 1 if streaming else steps
    pad = max(tm // 2, SUBLANES)
    seg = (lambda i: (0, 0, 0)) if streaming else (lambda i: (i, 0, 0))
    full = lambda a: pl.BlockSpec(a.shape, lambda i: (0,) * a.ndim)
    rows = lambda w: pl.BlockSpec((tm, w), lambda i: (i, 0))
    names = ("g_mix", "w_all", "bf_pad", "eq", "ek", "ec", "cq", "ck", "cv", "hsel", "conv_w", "conv_b",
             "wa", "ba", "wi", "bi", "lam")
    ws = [wts[k] for k in names]
    sds = jax.ShapeDtypeStruct
    out_shape = (
        sds((n, AUG), BF16), sds((n, AUG), BF16), sds((n, AUG), BF16),
        sds((n, FOX_WIDTH), F32), sds((n, FOX_WIDTH), F32), sds((n, FOX_WIDTH), BF16),
        sds((n, FOX_HEADS), F32), sds((n, FOX_HEADS), F32),
        sds((steps * SUBLANES, LANES), F32), sds((steps * SUBLANES, LANES), F32),
        sds((n, LRU_WIDTH), BF16),
        sds((nseg, 1, LRU_WIDTH), F32), sds((nseg, SUBLANES, LRU_WIDTH), F32),
    )
    out_specs = (
        rows(AUG), rows(AUG), rows(AUG), rows(FOX_WIDTH), rows(FOX_WIDTH), rows(FOX_WIDTH),
        rows(FOX_HEADS), rows(FOX_HEADS),
        pl.BlockSpec((SUBLANES, LANES), lambda i: (i, 0)), pl.BlockSpec((SUBLANES, LANES), lambda i: (i, 0)),
        rows(LRU_WIDTH),
        pl.BlockSpec((1, 1, LRU_WIDTH), seg), pl.BlockSpec((1, SUBLANES, LRU_WIDTH), seg),
    )
    in_specs = [rows(D_MODEL)] + [full(w) for w in ws] + [
        pl.BlockSpec((1, SUBLANES, LRU_WIDTH), seg), pl.BlockSpec((1, 1, LRU_WIDTH), seg)]
    return pl.pallas_call(
        functools.partial(_mix_in_kernel, tm=tm, pad=pad, streaming=streaming),
        grid=(steps,), in_specs=in_specs, out_specs=out_specs, out_shape=out_shape,
        scratch_shapes=[
            pltpu.VMEM((SUBLANES, LANES), F32), pltpu.VMEM((SUBLANES, LRU_WIDTH), F32),
            pltpu.VMEM((tm + SUBLANES, LRU_WIDTH), F32),
            pltpu.VMEM((pad + tm, LRU_WIDTH), F32), pltpu.VMEM((pad + tm, LRU_WIDTH), F32)],
        compiler_params=pltpu.CompilerParams(dimension_semantics=("arbitrary",), vmem_limit_bytes=VMEM_LIMIT),
        name="mix_in_stream" if streaming else "mix_in_segments",
    )(x, *ws, cprev, h0)


def _fox_prompt_kernel(nb_ref, st_ref, q_ref, k_ref, v_ref, o_ref, m_ref, acc_ref, *, bq):
    qi = pl.program_id(1)
    plan = pl.program_id(0) * pl.num_programs(1) + qi
    n_past = nb_ref[plan]
    cols = [slice(LANES * c, LANES * (c + 1)) for c in range(2)]
    qs = [q_ref[:, cols[c]] for c in range(2)]
    acc_ref[...] = jnp.zeros(acc_ref.shape, F32)

    def scores(c, start, causal):
        s = _dot_nt(qs[c], k_ref[pl.ds(start, bq), cols[c]])
        if causal:
            r = lax.broadcasted_iota(I32, (bq, bq), 0)
            cc = lax.broadcasted_iota(I32, (bq, bq), 1)
            s = jnp.where(cc <= r, s, NEG_INF)
        return s

    def static_block(start, causal):
        for c in range(2):
            p = jnp.exp2(scores(c, start, causal)).astype(BF16)
            acc_ref[c] += _dot(p, v_ref[pl.ds(start, bq), cols[c]])

    def online_block(start, causal):
        for c in range(2):
            s = scores(c, start, causal)
            m_old = m_ref[c]
            m_new = jnp.maximum(m_old, jnp.max(s, axis=1, keepdims=True))
            p = jnp.exp2(s - m_new).astype(BF16)
            acc_ref[c] = jnp.exp2(m_old - m_new) * acc_ref[c] + _dot(p, v_ref[pl.ds(start, bq), cols[c]])
            m_ref[c] = m_new

    def run(block):
        block(pl.multiple_of(qi * bq, bq), True)

        def body(t, carry):
            block(pl.multiple_of((qi - 1 - t) * bq, bq), False)
            return carry

        lax.fori_loop(0, n_past, body, 0)

    @pl.when(st_ref[plan] == 1)
    def _():
        run(static_block)

    @pl.when(st_ref[plan] != 1)
    def _():
        m_ref[...] = jnp.full(m_ref.shape, NEG_INF, F32)
        run(online_block)

    a0, a1 = acc_ref[0], acc_ref[1]
    o0 = a0 / a0[:, FOX_HEAD_DIM:FOX_HEAD_DIM + 1]
    o1 = a1 / a1[:, 0:1]
    lane = lax.broadcasted_iota(I32, (bq, LANES), 1)
    o_ref[...] = jnp.where(lane < FOX_HEAD_DIM, o0, o1).astype(BF16)


def _fox_prompt(n_past, static_ok, qa, ka, va, *, bq):
    s = qa.shape[0]
    pairs = FOX_HEADS // 2
    return pl.pallas_call(
        functools.partial(_fox_prompt_kernel, bq=bq),
        grid_spec=pltpu.PrefetchScalarGridSpec(
            num_scalar_prefetch=2, grid=(pairs, s // bq),
            in_specs=[pl.BlockSpec((bq, 2 * LANES), lambda p, i, nb, st: (i, p)),
                      pl.BlockSpec((s, 2 * LANES), lambda p, i, nb, st: (0, p)),
                      pl.BlockSpec((s, 2 * LANES), lambda p, i, nb, st: (0, p))],
            out_specs=pl.BlockSpec((bq, LANES), lambda p, i, nb, st: (i, p)),
            scratch_shapes=[pltpu.VMEM((2, bq, 1), F32), pltpu.VMEM((2, bq, LANES), F32)]),
        out_shape=jax.ShapeDtypeStruct((s, FOX_WIDTH), BF16),
        compiler_params=pltpu.CompilerParams(dimension_semantics=("arbitrary", "arbitrary"),
                                             vmem_limit_bytes=VMEM_LIMIT),
        name="fox_prompt",
    )(n_past, static_ok, qa, ka, va)


def _fox_plan(f2, qn, kn, *, bq, tile):
    s = f2.shape[0]
    nq = s // bq
    per = lambda a: a.reshape(s // tile, SUBLANES, LANES)[:, 0, :FOX_HEADS]
    qnorm = jnp.sqrt(jnp.max(per(qn).reshape(nq, bq // tile, FOX_HEADS), axis=1))
    knorm = jnp.sqrt(jnp.max(per(kn), axis=0))
    spread = 2.0 * 1.02 * qnorm * knorm[None, :]
    thr = spread + FOX_SKIP_GAP
    f_first = f2[0::bq]
    f_last = f2[bq - 1::bq]
    gap = f_last[None, :, :] - f_first[:, None, :]
    before = (jnp.arange(nq)[None, :] < jnp.arange(nq)[:, None])[:, :, None]
    need = jnp.sum(jnp.logical_and(before, gap < thr[:, None, :]), axis=1)
    pair = lambda a: a.reshape(nq, FOX_HEADS // 2, 2)
    n_past = jnp.max(pair(need), axis=2).T.reshape(-1).astype(I32)
    static_ok = jnp.all(pair(spread) <= FOX_STATIC_SHIFT_RANGE, axis=2).T.reshape(-1).astype(I32)
    return n_past, static_ok


def _fox_sample_kernel(q_ref, kn_ref, vn_ref, ck_ref, cv_ref, clf_ref, ek_ref, ckc_ref, o_ref, g_ref,
                       *, t, past, chunk):
    car = jnp.zeros((1, LANES), F32)
    upper = _tri(chunk, "gt")
    for ci in reversed(range(past // chunk)):
        lf = clf_ref[0, ci * chunk:(ci + 1) * chunk, :]
        g_ref[ci * chunk:(ci + 1) * chunk, :] = _dot3(upper, lf) + car
        car = car + jnp.sum(lf, axis=0, keepdims=True)
    gs = _lane_split3(g_ref[...] * LOG2E)

    lane = lax.broadcasted_iota(I32, (past, LANES), 1)
    r = lax.broadcasted_iota(I32, (t, t), 0)
    cc = lax.broadcasted_iota(I32, (t, t), 1)
    olane = lax.broadcasted_iota(I32, (t, LANES), 1)
    for p in range(FOX_HEADS // 2):
        pc = slice(LANES * p, LANES * (p + 1))
        kpair = ck_ref[0, :, pc]
        vpast = cv_ref[0, :, pc].astype(BF16)
        vnew = vn_ref[:, pc]
        outs = []
        for c in range(2):
            h = 2 * p + c
            hc = slice(LANES * h, LANES * (h + 1))
            main = (lane < FOX_HEAD_DIM) if c == 0 else (lane >= FOX_HEAD_DIM)
            extras = _dot(gs, ek_ref[:, hc]) + ckc_ref[:, hc]
            kpast = (jnp.where(main, kpair, 0.0) + extras).astype(BF16)
            q = q_ref[:, hc]
            sp = _dot_nt(q, kpast)
            sn = jnp.where(cc <= r, _dot_nt(q, kn_ref[:, hc]), NEG_INF)
            m = jnp.maximum(jnp.max(sp, axis=1, keepdims=True), jnp.max(sn, axis=1, keepdims=True))
            pp = jnp.exp2(sp - m)
            pn = jnp.exp2(sn - m)
            l = jnp.sum(pp, axis=1, keepdims=True) + jnp.sum(pn, axis=1, keepdims=True)
            outs.append((_dot(pp.astype(BF16), vpast) + _dot(pn.astype(BF16), vnew)) / l)
        o_ref[:, pc] = jnp.where(olane < FOX_HEAD_DIM, outs[0], outs[1]).astype(BF16)


def _fox_sample(qa, ka, vb, cache_k, cache_v, cache_lf3, ek, ck, *, t):
    nb, past = cache_k.shape[0], cache_k.shape[1]
    full = lambda a: pl.BlockSpec(a.shape, lambda b: (0,) * a.ndim)
    return pl.pallas_call(
        functools.partial(_fox_sample_kernel, t=t, past=past, chunk=256),
        grid=(nb,),
        in_specs=[pl.BlockSpec((t, AUG), lambda b: (b, 0)), pl.BlockSpec((t, AUG), lambda b: (b, 0)),
                  pl.BlockSpec((t, FOX_WIDTH), lambda b: (b, 0)),
                  pl.BlockSpec((1, past, FOX_WIDTH), lambda b: (b, 0, 0)),
                  pl.BlockSpec((1, past, FOX_WIDTH), lambda b: (b, 0, 0)),
                  pl.BlockSpec((1, past, LANES), lambda b: (b, 0, 0)),
                  full(ek), full(ck)],
        out_specs=pl.BlockSpec((t, FOX_WIDTH), lambda b: (b, 0)),
        out_shape=jax.ShapeDtypeStruct((nb * t, FOX_WIDTH), BF16),
        scratch_shapes=[pltpu.VMEM((past, LANES), F32)],
        compiler_params=pltpu.CompilerParams(dimension_semantics=("arbitrary",), vmem_limit_bytes=VMEM_LIMIT),
        name="fox_sample",
    )(qa, ka, vb, cache_k, cache_v, cache_lf3, ek, ck)


def _mem_kv_kernel(m_ref, g_ref, wk_ref, wv_ref, k_ref, v_ref):
    mn = _rms(m_ref[...], g_ref[...]).astype(BF16)
    k_ref[...] = _dot(mn, wk_ref[...])
    v_ref[...] = _dot(mn, wv_ref[...])


def _mem_kv(mem, g, wk, wv):
    n = mem.shape[0]
    return pl.pallas_call(
        _mem_kv_kernel,
        out_shape=(jax.ShapeDtypeStruct((n, D_MODEL), F32), jax.ShapeDtypeStruct((n, D_MODEL), F32)),
        compiler_params=pltpu.CompilerParams(vmem_limit_bytes=VMEM_LIMIT),
        name="mem_kv",
    )(mem, g, wk, wv)


def _post_kernel(h_ref, fo_ref, lru_ref, wo_ref, gc_ref, wcq_ref, mk_ref, mv_ref, wco_ref,
                 gm_ref, wr_ref, br_ref, cnt_in_ref,
                 h2_ref, xn_ref, idx_ref, tw_ref, rank_ref, cnt_ref, car_ref, *, tm):
    @pl.when(pl.program_id(0) == 0)
    def _():
        car_ref[...] = cnt_in_ref[...]

    h1 = h_ref[...] + _dot(fo_ref[...], wo_ref[0:FOX_WIDTH, :]) + _dot(lru_ref[...], wo_ref[FOX_WIDTH:D_MODEL, :])

    q = _dot(_rms(h1, gc_ref[...]).astype(BF16), wcq_ref[...])
    heads = []
    for hd in range(MEM_HEADS):
        hc = slice(MEM_HEAD_DIM * hd, MEM_HEAD_DIM * (hd + 1))
        s = _dot_nt(q[:, hc].astype(BF16), mk_ref[0, :, hc].astype(BF16)) * (MEM_HEAD_DIM ** -0.5)
        p = jnp.exp(s - jnp.max(s, axis=1, keepdims=True))
        o = _dot(p.astype(BF16), mv_ref[0, :, hc].astype(BF16)) / jnp.sum(p, axis=1, keepdims=True)
        heads.append(o.astype(BF16))
    h2 = h1 + _dot(jnp.concatenate(heads, axis=1), wco_ref[...])
    h2_ref[...] = h2

    xn = _rms(h2, gm_ref[...])
    xn_ref[...] = xn

    lane = lax.broadcasted_iota(I32, (tm, LANES), 1).astype(F32)
    logits = jnp.where(lane < N_EXPERTS, _dot(xn.astype(BF16), wr_ref[...]) + br_ref[...], -jnp.inf)
    vals, idxs = [], []
    for _ in range(TOP_K):
        mx = jnp.max(logits, axis=1, keepdims=True)
        ix = jnp.min(jnp.where(logits == mx, lane, float(LANES)), axis=1, keepdims=True)
        vals.append(mx)
        idxs.append(ix)
        logits = jnp.where(lane == ix, -jnp.inf, logits)
    es = [jnp.exp(v - vals[0]) for v in vals]
    den = es[0] + es[1] + es[2] + es[3]

    onehot = jnp.zeros((tm, LANES), F32)
    for ix in idxs:
        onehot = onehot + jnp.where(lane == ix, 1.0, 0.0)
    before = _dot(_tri(tm, "lt"), onehot.astype(BF16)) + car_ref[0:1, :]
    car = car_ref[0:1, :] + jnp.sum(onehot, axis=0, keepdims=True)
    car_ref[...] = jnp.broadcast_to(car, car_ref.shape)
    cnt_ref[...] = jnp.broadcast_to(car, cnt_ref.shape)

    idx_o = jnp.zeros((tm, LANES), F32)
    tw_o = jnp.zeros((tm, LANES), F32)
    rk_o = jnp.zeros((tm, LANES), F32)
    for j in range(TOP_K):
        rk = jnp.sum(jnp.where(lane == idxs[j], before, 0.0), axis=1, keepdims=True)
        idx_o = jnp.where(lane == j, idxs[j], idx_o)
        tw_o = jnp.where(lane == j, es[j] / den, tw_o)
        rk_o = jnp.where(lane == j, rk, rk_o)
    idx_ref[...] = idx_o.astype(I32)
    tw_ref[...] = tw_o
    rank_ref[...] = rk_o.astype(I32)


def _post(h, fo, lru, mk, mv, cnt_in, wts, *, tm, per_step_memory):
    n = h.shape[0]
    nm = mk.shape[1]
    full = lambda a: pl.BlockSpec(a.shape, lambda i: (0,) * a.ndim)
    rows = lambda w: pl.BlockSpec((tm, w), lambda i: (i, 0))
    mem = pl.BlockSpec((1, nm, D_MODEL), (lambda i: (i, 0, 0)) if per_step_memory else (lambda i: (0, 0, 0)))
    w = [wts[k] for k in ("w_out", "g_cross", "w_cq")]
    w2 = [wts[k] for k in ("w_co", "g_moe", "w_router", "b_router")]
    return pl.pallas_call(
        functools.partial(_post_kernel, tm=tm),
        grid=(n // tm,),
        in_specs=[rows(D_MODEL), rows(FOX_WIDTH), rows(LRU_WIDTH)] + [full(a) for a in w] + [mem, mem]
                 + [full(a) for a in w2] + [full(cnt_in)],
        out_specs=(rows(D_MODEL), rows(D_MODEL), rows(LANES), rows(LANES), rows(LANES),
                   pl.BlockSpec((SUBLANES, LANES), lambda i: (0, 0))),
        out_shape=(jax.ShapeDtypeStruct((n, D_MODEL), F32), jax.ShapeDtypeStruct((n, D_MODEL), F32),
                   jax.ShapeDtypeStruct((n, LANES), I32), jax.ShapeDtypeStruct((n, LANES), F32),
                   jax.ShapeDtypeStruct((n, LANES), I32), jax.ShapeDtypeStruct((SUBLANES, LANES), F32)),
        scratch_shapes=[pltpu.VMEM((SUBLANES, LANES), F32)],
        compiler_params=pltpu.CompilerParams(dimension_semantics=("arbitrary",), vmem_limit_bytes=VMEM_LIMIT),
        name="post_shared_mem" if not per_step_memory else "post_batch_mem",
    )(h, fo, lru, *w, mk, mv, *w2, cnt_in)


def _dispatch_kernel(dest_ref, last_ref, nu_ref, xp_ref, xs_ref, out_ref, zero_ref, sem, zsem,
                     *, tm, steps_p, n_blocks):
    i = pl.program_id(0)

    def zero_copy(b):
        return pltpu.make_async_copy(zero_ref, out_ref.at[pl.ds(pl.multiple_of(b * tm, tm), tm), :], zsem)

    @pl.when(i == 0)
    def _():
        zero_ref[...] = jnp.zeros(zero_ref.shape, F32)
        n_tail = n_blocks - nu_ref[0]

        def start_e(e, c):
            zero_copy(last_ref[e]).start()
            return c

        def start_t(b, c):
            zero_copy(nu_ref[0] + b).start()
            return c

        def wait_one(b, c):
            zero_copy(0).wait()
            return c

        lax.fori_loop(0, N_EXPERTS, start_e, 0)
        lax.fori_loop(0, n_tail, start_t, 0)
        lax.fori_loop(0, N_EXPERTS + n_tail, wait_one, 0)

    base = i * (tm * TOP_K)

    def scatter(src_ref):
        def start(t, c):
            for j in range(TOP_K):
                d = dest_ref[base + t * TOP_K + j]
                pltpu.make_async_copy(src_ref.at[pl.ds(t, 1), :], out_ref.at[pl.ds(d, 1), :], sem).start(priority=j % 2)
            return c

        lax.fori_loop(0, tm, start, 0, unroll=8)
        for _ in range(TOP_K):
            pltpu.make_async_copy(src_ref, out_ref.at[pl.ds(0, tm), :], sem).wait()

    @pl.when(i < steps_p)
    def _():
        scatter(xp_ref)

    @pl.when(i >= steps_p)
    def _():
        scatter(xs_ref)


def _dispatch(dest_flat, last_block, n_used, xn_p, xn_s, *, tm, n_blocks):
    steps_p, steps_s = xn_p.shape[0] // tm, xn_s.shape[0] // tm
    return pl.pallas_call(
        functools.partial(_dispatch_kernel, tm=tm, steps_p=steps_p, n_blocks=n_blocks),
        grid_spec=pltpu.PrefetchScalarGridSpec(
            num_scalar_prefetch=3, grid=(steps_p + steps_s,),
            in_specs=[pl.BlockSpec((tm, D_MODEL), lambda i, d, l, u: (jnp.minimum(i, steps_p - 1), 0)),
                      pl.BlockSpec((tm, D_MODEL), lambda i, d, l, u: (jnp.maximum(i - steps_p, 0), 0))],
            out_specs=pl.BlockSpec(memory_space=pl.ANY),
            scratch_shapes=[pltpu.VMEM((tm, D_MODEL), F32), pltpu.SemaphoreType.DMA(()),
                            pltpu.SemaphoreType.DMA(())]),
        out_shape=jax.ShapeDtypeStruct((n_blocks * tm, D_MODEL), F32),
        compiler_params=pltpu.CompilerParams(dimension_semantics=("arbitrary",), vmem_limit_bytes=VMEM_LIMIT),
        name="moe_dispatch",
    )(dest_flat, last_block, n_used, xn_p, xn_s)


def _expert_kernel(be_ref, nu_ref, xs_ref, wg_ref, bg_ref, wu_ref, bu_ref, wd_ref, bd_ref, ys_ref,
                   wgb, wub, wdb):
    i = pl.program_id(0)
    prev = be_ref[jnp.maximum(i - 1, 0)]
    fresh = jnp.logical_or(i == 0, be_ref[i] != prev)
    live = i < nu_ref[0]

    @pl.when(jnp.logical_and(live, fresh))
    def _():
        wgb[...] = wg_ref[0].astype(BF16)
        wub[...] = wu_ref[0].astype(BF16)
        wdb[...] = wd_ref[0].astype(BF16)

    @pl.when(live)
    def _():
        x = xs_ref[...].astype(BF16)
        g = jnp.minimum(_dot(x, wgb[...]) + bg_ref[0], SWIGLU_LIMIT)
        u = jnp.clip(_dot(x, wub[...]) + bu_ref[0], -SWIGLU_LIMIT, SWIGLU_LIMIT)
        hdn = g * jax.nn.sigmoid(SWIGLU_ALPHA * g) * (u + 1.0)
        ys_ref[...] = _dot(hdn.astype(BF16), wdb[...]) + bd_ref[0]

    @pl.when(jnp.logical_not(live))
    def _():
        ys_ref[...] = jnp.zeros(ys_ref.shape, F32)


def _experts(block_e, n_used, xs, wts, *, bm):
    p = xs.shape[0]
    blk = lambda i, be, nu: (jnp.minimum(i, nu[0] - 1), 0)
    oblk = lambda i, be, nu: (i, 0)
    wsel = lambda i, be, nu: (be[i], 0, 0)
    wspec = pl.BlockSpec((1, D_MODEL, D_MODEL), wsel)
    bspec = pl.BlockSpec((1, 1, D_MODEL), wsel)
    return pl.pallas_call(
        _expert_kernel,
        grid_spec=pltpu.PrefetchScalarGridSpec(
            num_scalar_prefetch=2, grid=(p // bm,),
            in_specs=[pl.BlockSpec((bm, D_MODEL), blk), wspec, bspec, wspec, bspec, wspec, bspec],
            out_specs=pl.BlockSpec((bm, D_MODEL), oblk),
            scratch_shapes=[pltpu.VMEM((D_MODEL, D_MODEL), BF16)] * 3),
        out_shape=jax.ShapeDtypeStruct((p, D_MODEL), F32),
        compiler_params=pltpu.CompilerParams(dimension_semantics=("arbitrary",), vmem_limit_bytes=VMEM_LIMIT),
        name="moe_experts",
    )(block_e, n_used, xs, wts["w_gate"], wts["b_gate"], wts["w_up"], wts["b_up"], wts["w_down"], wts["b_down"])


def _combine_kernel(dest_ref, hp_ref, hs_ref, twp_ref, tws_ref, gf_ref, ys_ref, yp_ref, ysm_ref, buf, sem,
                    *, tm, steps_p):
    i = pl.program_id(0)
    base = i * (tm * TOP_K)

    def start(t, c):
        for j in range(TOP_K):
            d = dest_ref[base + t * TOP_K + j]
            pltpu.make_async_copy(ys_ref.at[pl.ds(d, 1), :], buf.at[j, pl.ds(t, 1), :], sem).start(priority=j % 2)
        return c

    lax.fori_loop(0, tm, start, 0, unroll=8)
    for j in range(TOP_K):
        pltpu.make_async_copy(ys_ref.at[pl.ds(0, tm), :], buf.at[j], sem).wait()

    def finish(h_ref, tw_ref, y_ref):
        tw = tw_ref[...]
        y = h_ref[...]
        for j in range(TOP_K):
            y = y + buf[j] * tw[:, j:j + 1]
        y_ref[...] = _rms(y, gf_ref[...])

    @pl.when(i < steps_p)
    def _():
        finish(hp_ref, twp_ref, yp_ref)

    @pl.when(i >= steps_p)
    def _():
        finish(hs_ref, tws_ref, ysm_ref)


def _combine(dest_flat, h2_p, h2_s, tw_p, tw_s, g_final, ys, *, tm):
    steps_p, steps_s = h2_p.shape[0] // tm, h2_s.shape[0] // tm
    pblk = lambda w: pl.BlockSpec((tm, w), lambda i, d: (jnp.minimum(i, steps_p - 1), 0))
    sblk = lambda w: pl.BlockSpec((tm, w), lambda i, d: (jnp.maximum(i - steps_p, 0), 0))
    return pl.pallas_call(
        functools.partial(_combine_kernel, tm=tm, steps_p=steps_p),
        grid_spec=pltpu.PrefetchScalarGridSpec(
            num_scalar_prefetch=1, grid=(steps_p + steps_s,),
            in_specs=[pblk(D_MODEL), sblk(D_MODEL), pblk(LANES), sblk(LANES),
                      pl.BlockSpec((1, D_MODEL), lambda i, d: (0, 0)),
                      pl.BlockSpec(memory_space=pl.ANY)],
            out_specs=(pblk(D_MODEL), sblk(D_MODEL)),
            scratch_shapes=[pltpu.VMEM((TOP_K, tm, D_MODEL), F32), pltpu.SemaphoreType.DMA(())]),
        out_shape=(jax.ShapeDtypeStruct(h2_p.shape, F32), jax.ShapeDtypeStruct(h2_s.shape, F32)),
        compiler_params=pltpu.CompilerParams(dimension_semantics=("arbitrary",), vmem_limit_bytes=VMEM_LIMIT),
        name="moe_combine",
    )(dest_flat, h2_p, h2_s, tw_p, tw_s, g_final, ys)


def _aug_layout():
    main, extra = [], []
    for h in range(FOX_HEADS):
        even = h % 2 == 0
        main.append(LANES * h + (0 if even else FOX_HEAD_DIM))
        extra.append(LANES * h + (FOX_HEAD_DIM if even else 0))
    return main, extra


def _head_pad(w):
    d = w.shape[0]
    w4 = w.reshape(d, FOX_HEADS // 2, 2, FOX_HEAD_DIM)
    z = jnp.zeros((d, FOX_HEADS // 2, FOX_HEAD_DIM), w.dtype)
    return jnp.concatenate([w4[:, :, 0], z, z, w4[:, :, 1]], axis=-1).reshape(d, AUG)


def _prep_mix_weights(g_mix, w_in, b_f, conv_w, conv_b, w_a, b_a, w_i, b_i, lam):
    _, extra = _aug_layout()
    wq, wk, wv = w_in[:, 0:512], w_in[:, 512:1024], w_in[:, 1024:1536]
    wf, wxr, wg = w_in[:, 1536:1544], w_in[:, 1544:2056], w_in[:, 2056:2568]
    wf_pad = jnp.concatenate([wf, wf, wf, jnp.zeros((D_MODEL, LANES - 3 * FOX_HEADS), F32)], axis=1)
    w_all = jnp.concatenate([_head_pad(wq * (FOX_HEAD_DIM ** -0.5 * LOG2E)), _head_pad(wk), _head_pad(wv),
                             wf_pad, wxr, wg], axis=1).astype(BF16)
    bf_pad = jnp.concatenate([b_f, b_f, b_f, jnp.zeros((LANES - 3 * FOX_HEADS,), F32)]).reshape(1, LANES)
    eq = np.zeros((LANES, AUG), np.float32)
    ek = np.zeros((LANES, AUG), np.float32)
    ec = np.zeros((LANES, AUG), np.float32)
    cq = np.zeros((1, AUG), np.float32)
    ck = np.zeros((1, AUG), np.float32)
    cv = np.zeros((1, AUG), np.float32)
    hsel = np.zeros((AUG, LANES), np.float32)
    for h in range(FOX_HEADS):
        hsel[LANES * h:LANES * (h + 1), h] = 1.0
        cv[0, extra[h]] = 1.0
        ec[h, extra[h] + 6] = 1.0
        ck[0, extra[h] + 6] = 1.0
        for part in range(3):
            eq[part * 8 + h, extra[h] + part] = 1.0
            ek[part * 8 + h, extra[h] + 3 + part] = 1.0
            cq[0, extra[h] + 3 + part] = 1.0
            ck[0, extra[h] + part] = 1.0
    dense = lambda w: jax.scipy.linalg.block_diag(*[w[i] for i in range(LRU_BLOCKS)]).astype(BF16)
    row = lambda a: a.reshape(1, -1)
    return dict(g_mix=row(g_mix), w_all=w_all, bf_pad=bf_pad, eq=jnp.asarray(eq, BF16), ek=jnp.asarray(ek, BF16),
                ec=jnp.asarray(ec, BF16), cq=jnp.asarray(cq), ck=jnp.asarray(ck), cv=jnp.asarray(cv), hsel=jnp.asarray(hsel, BF16),
                conv_w=conv_w, conv_b=row(conv_b),
                wa=dense(w_a), ba=row(b_a), wi=dense(w_i), bi=row(b_i), lam=row(lam))


def kernel(x_prompt, x_sample, mem_prompt, cache_fox_k, cache_fox_v, cache_fox_logf, state_lru_h, state_conv, cache_mem_k, cache_mem_v, g_mix, w_in, b_f, conv_w, conv_b, w_a, b_a, w_i, b_i, lam, w_out, g_cross, g_mem, w_cq, w_ck, w_cv, w_co, g_moe, w_router, b_router, w_gate, b_gate, w_up, b_up, w_down, b_down, g_final):
    nb_p, seq, _ = x_prompt.shape
    nb_s, t_s, _ = x_sample.shape
    past = cache_fox_k.shape[2]
    n_mem = mem_prompt.shape[1]
    assert nb_p == 1 and g_mix.shape[0] == 1
    n_p, n_s = seq, nb_s * t_s
    row = lambda a: a.reshape(1, -1)

    mw = _prep_mix_weights(g_mix[0], w_in[0], b_f[0], conv_w[0], conv_b[0], w_a[0], b_a[0], w_i[0], b_i[0], lam[0])
    xp = x_prompt.reshape(n_p, D_MODEL)
    xs_ = x_sample.reshape(n_s, D_MODEL)

    zero_prev = jnp.zeros((1, SUBLANES, LRU_WIDTH), F32)
    zero_h = jnp.zeros((1, 1, LRU_WIDTH), F32)
    (qa_p, ka_p, va_p, k_p, v_p, _, lf_p, f2_p, qn_p, kn_p, lru_p, hl_p, ct_p) = _mix_in(
        xp, zero_prev, zero_h, mw, tm=MIX_TILE, streaming=True)
    prev_s = jnp.pad(state_conv[0], ((0, 0), (SUBLANES - (CONV_WIDTH - 1), 0), (0, 0)))
    (qa_s, ka_s, _, k_s, v_s, vb_s, lf_s, _, _, _, lru_s, hl_s, ct_s) = _mix_in(
        xs_, prev_s, state_lru_h[0].reshape(nb_s, 1, LRU_WIDTH), mw, tm=t_s, streaming=False)

    n_past, static_ok = _fox_plan(f2_p, qn_p, kn_p, bq=FOX_BQ, tile=MIX_TILE)
    fo_p = _fox_prompt(n_past, static_ok, qa_p, ka_p, va_p, bq=FOX_BQ)
    lf_c = cache_fox_logf[0]
    clf3 = jnp.concatenate([lf_c, lf_c, lf_c, jnp.zeros((nb_s, past, LANES - 3 * FOX_HEADS), F32)], axis=-1)
    fo_s = _fox_sample(qa_s, ka_s, vb_s, cache_fox_k[0].reshape(nb_s, past, FOX_WIDTH),
                       cache_fox_v[0].reshape(nb_s, past, FOX_WIDTH), clf3, mw["ek"], mw["ck"], t=t_s)

    mk_p, mv_p = _mem_kv(mem_prompt[0], row(g_mem[0]), w_ck[0].astype(BF16), w_cv[0].astype(BF16))

    wr_pad = jnp.pad(w_router[0], ((0, 0), (0, LANES - N_EXPERTS))).astype(BF16)
    br_pad = jnp.pad(b_router[0], (0, LANES - N_EXPERTS)).reshape(1, LANES)
    pw = dict(w_out=w_out[0].astype(BF16), g_cross=row(g_cross[0]), w_cq=w_cq[0].astype(BF16),
              w_co=w_co[0].astype(BF16), g_moe=row(g_moe[0]), w_router=wr_pad, b_router=br_pad)
    cnt0 = jnp.zeros((SUBLANES, LANES), F32)
    h2_p, xn_p, idx_p, tw_p, rk_p, cnt_p = _post(
        xp, fo_p, lru_p, mk_p.reshape(1, n_mem, D_MODEL), mv_p.reshape(1, n_mem, D_MODEL), cnt0, pw,
        tm=MIX_TILE, per_step_memory=False)
    h2_s, xn_s, idx_s, tw_s, rk_s, cnt = _post(
        xs_, fo_s, lru_s, cache_mem_k[0].reshape(nb_s, n_mem, D_MODEL), cache_mem_v[0].reshape(nb_s, n_mem, D_MODEL),
        cnt_p, pw, tm=t_s, per_step_memory=True)

    n = n_p + n_s
    idx = jnp.concatenate([idx_p[:, :TOP_K], idx_s[:, :TOP_K]], axis=0)
    rank = jnp.concatenate([rk_p[:, :TOP_K], rk_s[:, :TOP_K]], axis=0)
    counts = cnt[0, :N_EXPERTS].astype(I32)
    padded = (counts + MOE_BM - 1) // MOE_BM * MOE_BM
    pad_end = jnp.cumsum(padded)
    pad_start = pad_end - padded
    dest = (pad_start[idx] + rank).reshape(n * TOP_K)
    n_blocks = -(-(n * TOP_K) // MOE_BM) + N_EXPERTS
    blk_row = jnp.arange(n_blocks, dtype=I32) * MOE_BM
    block_e = jnp.minimum(jnp.sum(pad_end[None, :] <= blk_row[:, None], axis=1), N_EXPERTS - 1).astype(I32)
    n_used = (pad_end[-1] // MOE_BM).reshape(1).astype(I32)
    last_block = jnp.maximum(pad_end // MOE_BM - 1, 0).astype(I32)

    xs_sorted = _dispatch(dest, last_block, n_used, xn_p, xn_s, tm=MOE_BM, n_blocks=n_blocks)
    ew = dict(w_gate=w_gate[0], w_up=w_up[0], w_down=w_down[0],
              b_gate=b_gate[0].reshape(N_EXPERTS, 1, D_MODEL), b_up=b_up[0].reshape(N_EXPERTS, 1, D_MODEL),
              b_down=b_down[0].reshape(N_EXPERTS, 1, D_MODEL))
    ys = _experts(block_e, n_used, xs_sorted, ew, bm=MOE_BM)
    y_p, y_s = _combine(dest, h2_p, h2_s, tw_p, tw_s, row(g_final), ys, tm=COMBINE_TILE)

    shp_p = (1, nb_p, seq, FOX_HEADS, FOX_HEAD_DIM)
    shp_s = (1, nb_s, t_s, FOX_HEADS, FOX_HEAD_DIM)
    tail = slice(SUBLANES - (CONV_WIDTH - 1), SUBLANES)
    return (y_p.reshape(nb_p, seq, D_MODEL), y_s.reshape(nb_s, t_s, D_MODEL),
            k_p.reshape(shp_p), v_p.reshape(shp_p), lf_p.reshape(1, nb_p, seq, FOX_HEADS),
            hl_p.reshape(1, nb_p, LRU_WIDTH), ct_p[:, tail, :].reshape(1, nb_p, CONV_WIDTH - 1, LRU_WIDTH),
            mk_p.reshape(1, nb_p, n_mem, MEM_HEADS, MEM_HEAD_DIM), mv_p.reshape(1, nb_p, n_mem, MEM_HEADS, MEM_HEAD_DIM),
            k_s.reshape(shp_s), v_s.reshape(shp_s), lf_s.reshape(1, nb_s, t_s, FOX_HEADS),
            hl_s.reshape(1, nb_s, LRU_WIDTH), ct_s[:, tail, :].reshape(1, nb_s, CONV_WIDTH - 1, LRU_WIDTH))
```

```python
import functools
import math

import jax
import jax.numpy as jnp
import numpy as np
from jax import lax
from jax.experimental import pallas as pl
from jax.experimental.pallas import tpu as pltpu

F32 = jnp.float32
BF16 = jnp.bfloat16
I32 = jnp.int32

D_MODEL = 1024
FOX_HEADS = 8
FOX_HEAD_DIM = 64
FOX_WIDTH = FOX_HEADS * FOX_HEAD_DIM
LRU_WIDTH = D_MODEL - FOX_WIDTH
LRU_BLOCKS = 8
LRU_C = 8.0
CONV_WIDTH = 4
MEM_HEADS = 4
MEM_HEAD_DIM = D_MODEL // MEM_HEADS
N_EXPERTS = 32
TOP_K = 4
SWIGLU_LIMIT = 7.0
SWIGLU_ALPHA = 1.702
RMS_EPS = 1e-6
NEG_INF = -1e30
LOG2E = math.log2(math.e)

LANES = 128
SUBLANES = 8
AUG = LANES * FOX_HEADS
VMEM_LIMIT = 56 * 1024 * 1024

MIX_TILE = 256
POST_TILE = 512
FOX_BQ = 512
MOE_BM = 512
COMBINE_TILE = 256
FOX_SKIP_GAP = 160.0
FOX_STATIC_SHIFT_RANGE = 100.0

C_Q, C_KA, C_VA, C_F, C_XR, C_G, C_END = 0, 1024, 2048, 3072, 3200, 3712, 4224


def _dot(a, b):
    return jnp.dot(a, b, preferred_element_type=F32)


def _dot_nt(a, b):
    return lax.dot_general(a, b, (((1,), (1,)), ((), ())), preferred_element_type=F32)


def _split3(x):
    hi = x.astype(BF16)
    r = x - hi.astype(F32)
    mid = r.astype(BF16)
    lo = (r - mid.astype(F32)).astype(BF16)
    return hi, mid, lo


def _dot3(a, x):
    hi, mid, lo = _split3(x)
    return _dot(a, hi) + _dot(a, mid) + _dot(a, lo)


def _lane_split3(x):
    hi, mid, lo = _split3(x)
    lane = lax.broadcasted_iota(I32, x.shape, 1)
    return jnp.where(lane < 8, hi, jnp.where(lane < 16, mid, lo))


def _softplus(x):
    return jnp.maximum(x, 0.0) + jnp.log1p(jnp.exp(-jnp.abs(x)))


def _rms(x, g):
    return x * lax.rsqrt(jnp.mean(x * x, axis=-1, keepdims=True) + RMS_EPS) * g


def _tri(n, kind):
    r = lax.broadcasted_iota(I32, (n, n), 0)
    c = lax.broadcasted_iota(I32, (n, n), 1)
    m = {"le": c <= r, "lt": c < r, "gt": c > r}[kind]
    return jnp.where(m, 1.0, 0.0).astype(BF16)


def _head_sumsq_max(zb, hsel):
    sq = zb.astype(F32)
    return jnp.max(_dot((sq * sq).astype(BF16), hsel), axis=0, keepdims=True)


def _mix_in_kernel(x_ref, g_ref, w_ref, bfp_ref, eq_ref, ek_ref, ec_ref, cq_ref, ck_ref, cv_ref, hs_ref,
                   cw_ref, cb_ref, wa_ref, ba_ref, wi_ref, bi_ref, lam_ref, cprev_ref, h0_ref,
                   qa_ref, ka_ref, va_ref, k_ref, v_ref, vb_ref, lf_ref, f2_ref, qn_ref, kn_ref,
                   lru_ref, hl_ref, ct_ref,
                   fcar, hcar, xp_ref, sa_ref, sb_ref, *, tm, pad, streaming):
    step = pl.program_id(0)
    first = step == 0

    if streaming:
        @pl.when(first)
        def _():
            fcar[...] = jnp.zeros_like(fcar)
            hcar[...] = jnp.zeros_like(hcar)
            xp_ref[0:SUBLANES, :] = jnp.zeros((SUBLANES, LRU_WIDTH), F32)
    else:
        fcar[...] = jnp.zeros_like(fcar)
        hcar[...] = jnp.broadcast_to(h0_ref[0], hcar.shape)
        xp_ref[0:SUBLANES, :] = cprev_ref[0]

    x = x_ref[...]
    xn = _rms(x, g_ref[...]).astype(BF16)
    z = _dot(xn, w_ref[...])

    for p in range(FOX_HEADS // 2):
        a, b = 2 * p * LANES, (2 * p + 1) * LANES
        kk = z[:, C_KA + a:C_KA + b] + z[:, C_KA + b:C_KA + b + LANES]
        vv = z[:, C_VA + a:C_VA + b] + z[:, C_VA + b:C_VA + b + LANES]
        k_ref[:, p * LANES:(p + 1) * LANES] = kk
        v_ref[:, p * LANES:(p + 1) * LANES] = vv
        vb_ref[:, p * LANES:(p + 1) * LANES] = vv.astype(BF16)
    va_ref[...] = (z[:, C_VA:C_F] + cv_ref[...]).astype(BF16)

    lane = lax.broadcasted_iota(I32, (tm, LANES), 1)
    lf = jnp.where(lane < 3 * FOX_HEADS, -_softplus(-(z[:, C_F:C_XR] + bfp_ref[...])), 0.0)
    lf_ref[...] = lf[:, :FOX_HEADS]
    cum = _dot3(_tri(tm, "le"), lf) + fcar[0:1, :]
    fcar[...] = jnp.broadcast_to(cum[tm - 1:tm, :], fcar.shape)
    cum2 = cum * LOG2E
    f2_ref[...] = cum2[:, :FOX_HEADS]
    fs = _lane_split3(cum2)
    qb = z[:, C_Q:C_KA].astype(BF16)
    kb = z[:, C_KA:C_VA].astype(BF16)
    diag = _dot((qb.astype(F32) * kb.astype(F32)).astype(BF16), hs_ref[...]).astype(BF16)
    qa_ref[...] = qb + (_dot(fs, eq_ref[...]) + cq_ref[...] - _dot(diag, ec_ref[...])).astype(BF16)
    ka_ref[...] = kb + (ck_ref[...] - _dot(fs, ek_ref[...])).astype(BF16)
    qn_ref[...] = jnp.broadcast_to(_head_sumsq_max(qb, hs_ref[...]), qn_ref.shape)
    kn_ref[...] = jnp.broadcast_to(_head_sumsq_max(kb, hs_ref[...]), kn_ref.shape)

    xr = z[:, C_XR:C_G]
    xp_ref[SUBLANES:SUBLANES + tm, :] = xr
    xc = cb_ref[...] + xr * cw_ref[CONV_WIDTH - 1:CONV_WIDTH, :]
    for j in range(CONV_WIDTH - 1):
        sh = CONV_WIDTH - 1 - j
        xc = xc + xp_ref[SUBLANES - sh:SUBLANES - sh + tm, :] * cw_ref[j:j + 1, :]
    tail = xp_ref[tm:tm + SUBLANES, :]
    ct_ref[0] = tail
    xp_ref[0:SUBLANES, :] = tail

    xcb = xc.astype(BF16)
    r = jax.nn.sigmoid(_dot(xcb, wa_ref[...]) + ba_ref[...])
    ig = jax.nn.sigmoid(_dot(xcb, wi_ref[...]) + bi_ref[...])
    log_a = (-LRU_C) * r * _softplus(-lam_ref[...])
    a = jnp.exp(log_a)
    mult = jnp.sqrt(-jnp.tanh(log_a) * (a * a + 1.0))
    if streaming:
        row = lax.broadcasted_iota(I32, (tm, LRU_WIDTH), 0)
        mult = jnp.where(jnp.logical_and(row == 0, first), 1.0, mult)
    b = mult * ig * xc

    sa_ref[0:pad, :] = jnp.ones((pad, LRU_WIDTH), F32)
    sb_ref[0:pad, :] = jnp.zeros((pad, LRU_WIDTH), F32)
    d = 1
    while d < tm:
        sa_ref[pad:pad + tm, :] = a
        sb_ref[pad:pad + tm, :] = b
        b = a * sb_ref[pad - d:pad - d + tm, :] + b
        a = a * sa_ref[pad - d:pad - d + tm, :]
        d *= 2
    h = a * hcar[0:1, :] + b
    hlast = h[tm - 1:tm, :]
    hcar[...] = jnp.broadcast_to(hlast, hcar.shape)
    hl_ref[0] = hlast
    lru_ref[...] = (h * jax.nn.gelu(z[:, C_G:C_END])).astype(BF16)


def _mix_in(x, cprev, h0, wts, *, tm, streaming):
    n = x.shape[0]
    steps = n // tm
    nseg = 1 if streaming else steps
    pad = max(tm // 2, SUBLANES)
    seg = (lambda i: (0, 0, 0)) if streaming else (lambda i: (i, 0, 0))
    full = lambda a: pl.BlockSpec(a.shape, lambda i: (0,) * a.ndim)
    rows = lambda w: pl.BlockSpec((tm, w), lambda i: (i, 0))
    names = ("g_mix", "w_all", "bf_pad", "eq", "ek", "ec", "cq", "ck", "cv", "hsel", "conv_w", "conv_b",
             "wa", "ba", "wi", "bi", "lam")
    ws = [wts[k] for k in names]
    sds = jax.ShapeDtypeStruct
    out_shape = (
        sds((n, AUG), BF16), sds((n, AUG), BF16), sds((n, AUG), BF16),
        sds((n, FOX_WIDTH), F32), sds((n, FOX_WIDTH), F32), sds((n, FOX_WIDTH), BF16),
        sds((n, FOX_HEADS), F32), sds((n, FOX_HEADS), F32),
        sds((steps * SUBLANES, LANES), F32), sds((steps * SUBLANES, LANES), F32),
        sds((n, LRU_WIDTH), BF16),
        sds((nseg, 1, LRU_WIDTH), F32), sds((nseg, SUBLANES, LRU_WIDTH), F32),
    )
    out_specs = (
        rows(AUG), rows(AUG), rows(AUG), rows(FOX_WIDTH), rows(FOX_WIDTH), rows(FOX_WIDTH),
        rows(FOX_HEADS), rows(FOX_HEADS),
        pl.BlockSpec((SUBLANES, LANES), lambda i: (i, 0)), pl.BlockSpec((SUBLANES, LANES), lambda i: (i, 0)),
        rows(LRU_WIDTH),
        pl.BlockSpec((1, 1, LRU_WIDTH), seg), pl.BlockSpec((1, SUBLANES, LRU_WIDTH), seg),
    )
    in_specs = [rows(D_MODEL)] + [full(w) for w in ws] + [
        pl.BlockSpec((1, SUBLANES, LRU_WIDTH), seg), pl.BlockSpec((1, 1, LRU_WIDTH), seg)]
    return pl.pallas_call(
        functools.partial(_mix_in_kernel, tm=tm, pad=pad, streaming=streaming),
        grid=(steps,), in_specs=in_specs, out_specs=out_specs, out_shape=out_shape,
        scratch_shapes=[
            pltpu.VMEM((SUBLANES, LANES), F32), pltpu.VMEM((SUBLANES, LRU_WIDTH), F32),
            pltpu.VMEM((tm + SUBLANES, LRU_WIDTH), F32),
            pltpu.VMEM((pad + tm, LRU_WIDTH), F32), pltpu.VMEM((pad + tm, LRU_WIDTH), F32)],
        compiler_params=pltpu.CompilerParams(dimension_semantics=("arbitrary",), vmem_limit_bytes=VMEM_LIMIT),
        name="mix_in_stream" if streaming else "mix_in_segments",
    )(x, *ws, cprev, h0)


def _fox_prompt_kernel(nb_ref, st_ref, q_ref, k_ref, v_ref, o_ref, m_ref, acc_ref, *, bq):
    qi = pl.program_id(1)
    plan = pl.program_id(0) * pl.num_programs(1) + qi
    n_past = nb_ref[plan]
    cols = [slice(LANES * c, LANES * (c + 1)) for c in range(2)]
    qs = [q_ref[:, cols[c]] for c in range(2)]
    acc_ref[...] = jnp.zeros(acc_ref.shape, F32)

    def scores(c, start, causal):
        s = _dot_nt(qs[c], k_ref[pl.ds(start, bq), cols[c]])
        if causal:
            r = lax.broadcasted_iota(I32, (bq, bq), 0)
            cc = lax.broadcasted_iota(I32, (bq, bq), 1)
            s = jnp.where(cc <= r, s, NEG_INF)
        return s

    def static_block(start, causal):
        for c in range(2):
            p = jnp.exp2(scores(c, start, causal)).astype(BF16)
            acc_ref[c] += _dot(p, v_ref[pl.ds(start, bq), cols[c]])

    def online_block(start, causal):
        for c in range(2):
            s = scores(c, start, causal)
            m_old = m_ref[c]
            m_new = jnp.maximum(m_old, jnp.max(s, axis=1, keepdims=True))
            p = jnp.exp2(s - m_new).astype(BF16)
            acc_ref[c] = jnp.exp2(m_old - m_new) * acc_ref[c] + _dot(p, v_ref[pl.ds(start, bq), cols[c]])
            m_ref[c] = m_new

    def run(block):
        block(pl.multiple_of(qi * bq, bq), True)

        def body(t, carry):
            block(pl.multiple_of((qi - 1 - t) * bq, bq), False)
            return carry

        lax.fori_loop(0, n_past, body, 0)

    @pl.when(st_ref[plan] == 1)
    def _():
        run(static_block)

    @pl.when(st_ref[plan] != 1)
    def _():
        m_ref[...] = jnp.full(m_ref.shape, NEG_INF, F32)
        run(online_block)

    a0, a1 = acc_ref[0], acc_ref[1]
    o0 = a0 / a0[:, FOX_HEAD_DIM:FOX_HEAD_DIM + 1]
    o1 = a1 / a1[:, 0:1]
    lane = lax.broadcasted_iota(I32, (bq, LANES), 1)
    o_ref[...] = jnp.where(lane < FOX_HEAD_DIM, o0, o1).astype(BF16)


def _fox_prompt(n_past, static_ok, qa, ka, va, *, bq):
    s = qa.shape[0]
    pairs = FOX_HEADS // 2
    return pl.pallas_call(
        functools.partial(_fox_prompt_kernel, bq=bq),
        grid_spec=pltpu.PrefetchScalarGridSpec(
            num_scalar_prefetch=2, grid=(pairs, s // bq),
            in_specs=[pl.BlockSpec((bq, 2 * LANES), lambda p, i, nb, st: (i, p)),
                      pl.BlockSpec((s, 2 * LANES), lambda p, i, nb, st: (0, p)),
                      pl.BlockSpec((s, 2 * LANES), lambda p, i, nb, st: (0, p))],
            out_specs=pl.BlockSpec((bq, LANES), lambda p, i, nb, st: (i, p)),
            scratch_shapes=[pltpu.VMEM((2, bq, 1), F32), pltpu.VMEM((2, bq, LANES), F32)]),
        out_shape=jax.ShapeDtypeStruct((s, FOX_WIDTH), BF16),
        compiler_params=pltpu.CompilerParams(dimension_semantics=("arbitrary", "arbitrary"),
                                             vmem_limit_bytes=VMEM_LIMIT),
        name="fox_prompt",
    )(n_past, static_ok, qa, ka, va)


def _fox_plan(f2, qn, kn, *, bq, tile):
    s = f2.shape[0]
    nq = s // bq
    per = lambda a: a.reshape(s // tile, SUBLANES, LANES)[:, 0, :FOX_HEADS]
    qnorm = jnp.sqrt(jnp.max(per(qn).reshape(nq, bq // tile, FOX_HEADS), axis=1))
    knorm = jnp.sqrt(jnp.max(per(kn), axis=0))
    spread = 2.0 * 1.02 * qnorm * knorm[None, :]
    thr = spread + FOX_SKIP_GAP
    f_first = f2[0::bq]
    f_last = f2[bq - 1::bq]
    gap = f_last[None, :, :] - f_first[:, None, :]
    before = (jnp.arange(nq)[None, :] < jnp.arange(nq)[:, None])[:, :, None]
    need = jnp.sum(jnp.logical_and(before, gap < thr[:, None, :]), axis=1)
    pair = lambda a: a.reshape(nq, FOX_HEADS // 2, 2)
    n_past = jnp.max(pair(need), axis=2).T.reshape(-1).astype(I32)
    static_ok = jnp.all(pair(spread) <= FOX_STATIC_SHIFT_RANGE, axis=2).T.reshape(-1).astype(I32)
    return n_past, static_ok


def _fox_sample_kernel(q_ref, kn_ref, vn_ref, ck_ref, cv_ref, clf_ref, ek_ref, ckc_ref, o_ref, g_ref,
                       *, t, past, chunk):
    car = jnp.zeros((1, LANES), F32)
    upper = _tri(chunk, "gt")
    for ci in reversed(range(past // chunk)):
        lf = clf_ref[0, ci * chunk:(ci + 1) * chunk, :]
        g_ref[ci * chunk:(ci + 1) * chunk, :] = _dot3(upper, lf) + car
        car = car + jnp.sum(lf, axis=0, keepdims=True)
    gs = _lane_split3(g_ref[...] * LOG2E)

    lane = lax.broadcasted_iota(I32, (past, LANES), 1)
    r = lax.broadcasted_iota(I32, (t, t), 0)
    cc = lax.broadcasted_iota(I32, (t, t), 1)
    olane = lax.broadcasted_iota(I32, (t, LANES), 1)
    for p in range(FOX_HEADS // 2):
        pc = slice(LANES * p, LANES * (p + 1))
        kpair = ck_ref[0, :, pc]
        vpast = cv_ref[0, :, pc].astype(BF16)
        vnew = vn_ref[:, pc]
        outs = []
        for c in range(2):
            h = 2 * p + c
            hc = slice(LANES * h, LANES * (h + 1))
            main = (lane < FOX_HEAD_DIM) if c == 0 else (lane >= FOX_HEAD_DIM)
            extras = _dot(gs, ek_ref[:, hc]) + ckc_ref[:, hc]
            kpast = (jnp.where(main, kpair, 0.0) + extras).astype(BF16)
            q = q_ref[:, hc]
            sp = _dot_nt(q, kpast)
            sn = jnp.where(cc <= r, _dot_nt(q, kn_ref[:, hc]), NEG_INF)
            m = jnp.maximum(jnp.max(sp, axis=1, keepdims=True), jnp.max(sn, axis=1, keepdims=True))
            pp = jnp.exp2(sp - m)
            pn = jnp.exp2(sn - m)
            l = jnp.sum(pp, axis=1, keepdims=True) + jnp.sum(pn, axis=1, keepdims=True)
            outs.append((_dot(pp.astype(BF16), vpast) + _dot(pn.astype(BF16), vnew)) / l)
        o_ref[:, pc] = jnp.where(olane < FOX_HEAD_DIM, outs[0], outs[1]).astype(BF16)


def _fox_sample(qa, ka, vb, cache_k, cache_v, cache_lf3, ek, ck, *, t):
    nb, past = cache_k.shape[0], cache_k.shape[1]
    full = lambda a: pl.BlockSpec(a.shape, lambda b: (0,) * a.ndim)
    return pl.pallas_call(
        functools.partial(_fox_sample_kernel, t=t, past=past, chunk=256),
        grid=(nb,),
        in_specs=[pl.BlockSpec((t, AUG), lambda b: (b, 0)), pl.BlockSpec((t, AUG), lambda b: (b, 0)),
                  pl.BlockSpec((t, FOX_WIDTH), lambda b: (b, 0)),
                  pl.BlockSpec((1, past, FOX_WIDTH), lambda b: (b, 0, 0)),
                  pl.BlockSpec((1, past, FOX_WIDTH), lambda b: (b, 0, 0)),
                  pl.BlockSpec((1, past, LANES), lambda b: (b, 0, 0)),
                  full(ek), full(ck)],
        out_specs=pl.BlockSpec((t, FOX_WIDTH), lambda b: (b, 0)),
        out_shape=jax.ShapeDtypeStruct((nb * t, FOX_WIDTH), BF16),
        scratch_shapes=[pltpu.VMEM((past, LANES), F32)],
        compiler_params=pltpu.CompilerParams(dimension_semantics=("arbitrary",), vmem_limit_bytes=VMEM_LIMIT),
        name="fox_sample",
    )(qa, ka, vb, cache_k, cache_v, cache_lf3, ek, ck)


def _mem_kv_kernel(m_ref, g_ref, wk_ref, wv_ref, k_ref, v_ref):
    mn = _rms(m_ref[...], g_ref[...]).astype(BF16)
    k_ref[...] = _dot(mn, wk_ref[...])
    v_ref[...] = _dot(mn, wv_ref[...])


def _mem_kv(mem, g, wk, wv):
    n = mem.shape[0]
    return pl.pallas_call(
        _mem_kv_kernel,
        out_shape=(jax.ShapeDtypeStruct((n, D_MODEL), F32), jax.ShapeDtypeStruct((n, D_MODEL), F32)),
        compiler_params=pltpu.CompilerParams(vmem_limit_bytes=VMEM_LIMIT),
        name="mem_kv",
    )(mem, g, wk, wv)


def _post_kernel(h_ref, fo_ref, lru_ref, wo_ref, gc_ref, wcq_ref, mk_ref, mv_ref, wco_ref,
                 gm_ref, wr_ref, br_ref, cnt_in_ref,
                 h2_ref, xn_ref, idx_ref, tw_ref, rank_ref, cnt_ref, car_ref, *, tm):
    @pl.when(pl.program_id(0) == 0)
    def _():
        car_ref[...] = cnt_in_ref[...]

    h1 = h_ref[...] + _dot(fo_ref[...], wo_ref[0:FOX_WIDTH, :]) + _dot(lru_ref[...], wo_ref[FOX_WIDTH:D_MODEL, :])

    q = _dot(_rms(h1, gc_ref[...]).astype(BF16), wcq_ref[...])
    heads = []
    for hd in range(MEM_HEADS):
        hc = slice(MEM_HEAD_DIM * hd, MEM_HEAD_DIM * (hd + 1))
        s = _dot_nt(q[:, hc].astype(BF16), mk_ref[0, :, hc].astype(BF16)) * (MEM_HEAD_DIM ** -0.5)
        p = jnp.exp(s - jnp.max(s, axis=1, keepdims=True))
        o = _dot(p.astype(BF16), mv_ref[0, :, hc].astype(BF16)) / jnp.sum(p, axis=1, keepdims=True)
        heads.append(o.astype(BF16))
    h2 = h1 + _dot(jnp.concatenate(heads, axis=1), wco_ref[...])
    h2_ref[...] = h2

    xn = _rms(h2, gm_ref[...])
    xn_ref[...] = xn

    lane = lax.broadcasted_iota(I32, (tm, LANES), 1).astype(F32)
    logits = jnp.where(lane < N_EXPERTS, _dot(xn.astype(BF16), wr_ref[...]) + br_ref[...], -jnp.inf)
    vals, idxs = [], []
    for _ in range(TOP_K):
        mx = jnp.max(logits, axis=1, keepdims=True)
        ix = jnp.min(jnp.where(logits == mx, lane, float(LANES)), axis=1, keepdims=True)
        vals.append(mx)
        idxs.append(ix)
        logits = jnp.where(lane == ix, -jnp.inf, logits)
    es = [jnp.exp(v - vals[0]) for v in vals]
    den = es[0] + es[1] + es[2] + es[3]

    onehot = jnp.zeros((tm, LANES), F32)
    for ix in idxs:
        onehot = onehot + jnp.where(lane == ix, 1.0, 0.0)
    before = _dot(_tri(tm, "lt"), onehot.astype(BF16)) + car_ref[0:1, :]
    car = car_ref[0:1, :] + jnp.sum(onehot, axis=0, keepdims=True)
    car_ref[...] = jnp.broadcast_to(car, car_ref.shape)
    cnt_ref[...] = jnp.broadcast_to(car, cnt_ref.shape)

    idx_o = jnp.zeros((tm, LANES), F32)
    tw_o = jnp.zeros((tm, LANES), F32)
    rk_o = jnp.zeros((tm, LANES), F32)
    for j in range(TOP_K):
        rk = jnp.sum(jnp.where(lane == idxs[j], before, 0.0), axis=1, keepdims=True)
        idx_o = jnp.where(lane == j, idxs[j], idx_o)
        tw_o = jnp.where(lane == j, es[j] / den, tw_o)
        rk_o = jnp.where(lane == j, rk, rk_o)
    idx_ref[...] = idx_o.astype(I32)
    tw_ref[...] = tw_o
    rank_ref[...] = rk_o.astype(I32)


def _post(h, fo, lru, mk, mv, cnt_in, wts, *, tm, per_step_memory):
    n = h.shape[0]
    nm = mk.shape[1]
    full = lambda a: pl.BlockSpec(a.shape, lambda i: (0,) * a.ndim)
    rows = lambda w: pl.BlockSpec((tm, w), lambda i: (i, 0))
    mem = pl.BlockSpec((1, nm, D_MODEL), (lambda i: (i, 0, 0)) if per_step_memory else (lambda i: (0, 0, 0)))
    w = [wts[k] for k in ("w_out", "g_cross", "w_cq")]
    w2 = [wts[k] for k in ("w_co", "g_moe", "w_router", "b_router")]
    return pl.pallas_call(
        functools.partial(_post_kernel, tm=tm),
        grid=(n // tm,),
        in_specs=[rows(D_MODEL), rows(FOX_WIDTH), rows(LRU_WIDTH)] + [full(a) for a in w] + [mem, mem]
                 + [full(a) for a in w2] + [full(cnt_in)],
        out_specs=(rows(D_MODEL), rows(D_MODEL), rows(LANES), rows(LANES), rows(LANES),
                   pl.BlockSpec((SUBLANES, LANES), lambda i: (0, 0))),
        out_shape=(jax.ShapeDtypeStruct((n, D_MODEL), F32), jax.ShapeDtypeStruct((n, D_MODEL), F32),
                   jax.ShapeDtypeStruct((n, LANES), I32), jax.ShapeDtypeStruct((n, LANES), F32),
                   jax.ShapeDtypeStruct((n, LANES), I32), jax.ShapeDtypeStruct((SUBLANES, LANES), F32)),
        scratch_shapes=[pltpu.VMEM((SUBLANES, LANES), F32)],
        compiler_params=pltpu.CompilerParams(dimension_semantics=("arbitrary",), vmem_limit_bytes=VMEM_LIMIT),
        name="post_shared_mem" if not per_step_memory else "post_batch_mem",
    )(h, fo, lru, *w, mk, mv, *w2, cnt_in)


STAGES = 3


def _dispatch_kernel(dest_ref, last_ref, nu_ref, xp_ref, xs_ref, out_ref, zero_ref, stage, sems, gsem, zsem,
                     *, tm, steps_p, n_blocks):
    i = pl.program_id(0)

    def zero_copy(b):
        return pltpu.make_async_copy(zero_ref, out_ref.at[pl.ds(pl.multiple_of(b * tm, tm), tm), :], zsem)

    @pl.when(i == 0)
    def _():
        zero_ref[...] = jnp.zeros(zero_ref.shape, F32)
        n_tail = n_blocks - nu_ref[0]

        def start_e(e, c):
            zero_copy(last_ref[e]).start()
            return c

        def start_t(b, c):
            zero_copy(nu_ref[0] + b).start()
            return c

        def wait_one(b, c):
            zero_copy(0).wait()
            return c

        lax.fori_loop(0, N_EXPERTS, start_e, 0)
        lax.fori_loop(0, n_tail, start_t, 0)
        lax.fori_loop(0, N_EXPERTS + n_tail, wait_one, 0)

    groups = tm // SUBLANES
    steps = pl.num_programs(0)

    def stage_copy(src_ref, tile, slot):
        return pltpu.make_async_copy(src_ref.at[pl.ds(tile * groups, groups)], stage.at[slot], gsem.at[slot])

    def start_stage(tile, slot):
        @pl.when(tile < steps_p)
        def _():
            stage_copy(xp_ref, tile, slot).start()

        @pl.when(tile >= steps_p)
        def _():
            stage_copy(xs_ref, tile - steps_p, slot).start()

    @pl.when(i == 0)
    def _():
        start_stage(0, 0)

    @pl.when(i + 1 < steps)
    def _():
        start_stage(i + 1, (i + 1) % STAGES)

    slot = i % STAGES
    stage_copy(xp_ref, 0, slot).wait()

    base = i * (tm * TOP_K)

    def start(g, c):
        for u in range(SUBLANES):
            for j in range(TOP_K):
                d = dest_ref[base + (g * SUBLANES + u) * TOP_K + j]
                pltpu.make_async_copy(stage.at[slot, g, pl.ds(u, 1), :], out_ref.at[pl.ds(d, 1), :],
                                      sems.at[i % 2]).start(priority=j % 2)
        return c

    lax.fori_loop(0, groups, start, 0)

    def drain(parity):
        for _ in range(TOP_K):
            pltpu.make_async_copy(out_ref.at[pl.ds(0, tm), :], out_ref.at[pl.ds(0, tm), :], sems.at[parity]).wait()

    @pl.when(i > 0)
    def _():
        drain((i - 1) % 2)

    @pl.when(i == steps - 1)
    def _():
        drain(i % 2)


def _dispatch(dest_flat, last_block, n_used, xn_p, xn_s, *, tm, n_blocks):
    steps_p, steps_s = xn_p.shape[0] // tm, xn_s.shape[0] // tm
    return pl.pallas_call(
        functools.partial(_dispatch_kernel, tm=tm, steps_p=steps_p, n_blocks=n_blocks),
        grid_spec=pltpu.PrefetchScalarGridSpec(
            num_scalar_prefetch=3, grid=(steps_p + steps_s,),
            in_specs=[pl.BlockSpec(memory_space=pl.ANY), pl.BlockSpec(memory_space=pl.ANY)],
            out_specs=pl.BlockSpec(memory_space=pl.ANY),
            scratch_shapes=[pltpu.VMEM((tm, D_MODEL), F32),
                            pltpu.VMEM((STAGES, tm // SUBLANES, SUBLANES, D_MODEL), F32),
                            pltpu.SemaphoreType.DMA((2,)), pltpu.SemaphoreType.DMA((STAGES,)),
                            pltpu.SemaphoreType.DMA(())]),
        out_shape=jax.ShapeDtypeStruct((n_blocks * tm, D_MODEL), F32),
        compiler_params=pltpu.CompilerParams(dimension_semantics=("arbitrary",), vmem_limit_bytes=VMEM_LIMIT),
        name="moe_dispatch",
    )(dest_flat, last_block, n_used, xn_p.reshape(-1, SUBLANES, D_MODEL), xn_s.reshape(-1, SUBLANES, D_MODEL))


def _expert_kernel(be_ref, nu_ref, xs_ref, wg_ref, bg_ref, wu_ref, bu_ref, wd_ref, bd_ref, ys_ref,
                   wgb, wub, wdb):
    i = pl.program_id(0)
    prev = be_ref[jnp.maximum(i - 1, 0)]
    fresh = jnp.logical_or(i == 0, be_ref[i] != prev)
    live = i < nu_ref[0]

    @pl.when(jnp.logical_and(live, fresh))
    def _():
        wgb[...] = wg_ref[0].astype(BF16)
        wub[...] = wu_ref[0].astype(BF16)
        wdb[...] = wd_ref[0].astype(BF16)

    @pl.when(live)
    def _():
        x = xs_ref[...].astype(BF16)
        g = jnp.minimum(_dot(x, wgb[...]) + bg_ref[0], SWIGLU_LIMIT)
        u = jnp.clip(_dot(x, wub[...]) + bu_ref[0], -SWIGLU_LIMIT, SWIGLU_LIMIT)
        hdn = g * jax.nn.sigmoid(SWIGLU_ALPHA * g) * (u + 1.0)
        ys_ref[...] = _dot(hdn.astype(BF16), wdb[...]) + bd_ref[0]

    @pl.when(jnp.logical_not(live))
    def _():
        ys_ref[...] = jnp.zeros(ys_ref.shape, F32)


def _experts(block_e, n_used, xs, wts, *, bm):
    p = xs.shape[0]
    blk = lambda i, be, nu: (jnp.minimum(i, nu[0] - 1), 0)
    oblk = lambda i, be, nu: (i, 0)
    wsel = lambda i, be, nu: (be[i], 0, 0)
    wspec = pl.BlockSpec((1, D_MODEL, D_MODEL), wsel)
    bspec = pl.BlockSpec((1, 1, D_MODEL), wsel)
    return pl.pallas_call(
        _expert_kernel,
        grid_spec=pltpu.PrefetchScalarGridSpec(
            num_scalar_prefetch=2, grid=(p // bm,),
            in_specs=[pl.BlockSpec((bm, D_MODEL), blk), wspec, bspec, wspec, bspec, wspec, bspec],
            out_specs=pl.BlockSpec((bm, D_MODEL), oblk),
            scratch_shapes=[pltpu.VMEM((D_MODEL, D_MODEL), BF16)] * 3),
        out_shape=jax.ShapeDtypeStruct((p, D_MODEL), F32),
        compiler_params=pltpu.CompilerParams(dimension_semantics=("arbitrary",), vmem_limit_bytes=VMEM_LIMIT),
        name="moe_experts",
    )(block_e, n_used, xs, wts["w_gate"], wts["b_gate"], wts["w_up"], wts["b_up"], wts["w_down"], wts["b_down"])


def _combine_kernel(dest_ref, hp_ref, hs_ref, twp_ref, tws_ref, gf_ref, ys_ref, yp_ref, ysm_ref, buf, sems,
                    *, tm, steps_p):
    i = pl.program_id(0)
    slot = i % 2

    def issue(tile, into):
        base = tile * (tm * TOP_K)

        def start(g, c):
            for u in range(SUBLANES):
                for j in range(TOP_K):
                    d = dest_ref[base + (g * SUBLANES + u) * TOP_K + j]
                    pltpu.make_async_copy(ys_ref.at[pl.ds(d, 1), :], buf.at[into, j, g, pl.ds(u, 1), :],
                                          sems.at[into]).start(priority=j % 2)
            return c

        lax.fori_loop(0, tm // SUBLANES, start, 0)

    @pl.when(i == 0)
    def _():
        issue(0, 0)

    @pl.when(i + 1 < pl.num_programs(0))
    def _():
        issue(i + 1, (i + 1) % 2)

    for j in range(TOP_K):
        pltpu.make_async_copy(ys_ref.at[pl.ds(0, tm), :], ys_ref.at[pl.ds(0, tm), :], sems.at[slot]).wait()

    def finish(h_ref, tw_ref, y_ref):
        tw = tw_ref[...]
        y = h_ref[...]
        for j in range(TOP_K):
            y = y + buf[slot, j].reshape(tm, D_MODEL) * tw[:, j:j + 1]
        y_ref[...] = _rms(y, gf_ref[...])

    @pl.when(i < steps_p)
    def _():
        finish(hp_ref, twp_ref, yp_ref)

    @pl.when(i >= steps_p)
    def _():
        finish(hs_ref, tws_ref, ysm_ref)


def _combine(dest_flat, h2_p, h2_s, tw_p, tw_s, g_final, ys, *, tm):
    steps_p, steps_s = h2_p.shape[0] // tm, h2_s.shape[0] // tm
    pblk = lambda w: pl.BlockSpec((tm, w), lambda i, d: (jnp.minimum(i, steps_p - 1), 0))
    sblk = lambda w: pl.BlockSpec((tm, w), lambda i, d: (jnp.maximum(i - steps_p, 0), 0))
    return pl.pallas_call(
        functools.partial(_combine_kernel, tm=tm, steps_p=steps_p),
        grid_spec=pltpu.PrefetchScalarGridSpec(
            num_scalar_prefetch=1, grid=(steps_p + steps_s,),
            in_specs=[pblk(D_MODEL), sblk(D_MODEL), pblk(LANES), sblk(LANES),
                      pl.BlockSpec((1, D_MODEL), lambda i, d: (0, 0)),
                      pl.BlockSpec(memory_space=pl.ANY)],
            out_specs=(pblk(D_MODEL), sblk(D_MODEL)),
            scratch_shapes=[pltpu.VMEM((2, TOP_K, tm // SUBLANES, SUBLANES, D_MODEL), F32),
                            pltpu.SemaphoreType.DMA((2,))]),
        out_shape=(jax.ShapeDtypeStruct(h2_p.shape, F32), jax.ShapeDtypeStruct(h2_s.shape, F32)),
        compiler_params=pltpu.CompilerParams(dimension_semantics=("arbitrary",), vmem_limit_bytes=VMEM_LIMIT),
        name="moe_combine",
    )(dest_flat, h2_p, h2_s, tw_p, tw_s, g_final, ys)


def _aug_layout():
    main, extra = [], []
    for h in range(FOX_HEADS):
        even = h % 2 == 0
        main.append(LANES * h + (0 if even else FOX_HEAD_DIM))
        extra.append(LANES * h + (FOX_HEAD_DIM if even else 0))
    return main, extra


def _head_pad(w):
    d = w.shape[0]
    w4 = w.reshape(d, FOX_HEADS // 2, 2, FOX_HEAD_DIM)
    z = jnp.zeros((d, FOX_HEADS // 2, FOX_HEAD_DIM), w.dtype)
    return jnp.concatenate([w4[:, :, 0], z, z, w4[:, :, 1]], axis=-1).reshape(d, AUG)


def _prep_mix_weights(g_mix, w_in, b_f, conv_w, conv_b, w_a, b_a, w_i, b_i, lam):
    _, extra = _aug_layout()
    wq, wk, wv = w_in[:, 0:512], w_in[:, 512:1024], w_in[:, 1024:1536]
    wf, wxr, wg = w_in[:, 1536:1544], w_in[:, 1544:2056], w_in[:, 2056:2568]
    wf_pad = jnp.concatenate([wf, wf, wf, jnp.zeros((D_MODEL, LANES - 3 * FOX_HEADS), F32)], axis=1)
    w_all = jnp.concatenate([_head_pad(wq * (FOX_HEAD_DIM ** -0.5 * LOG2E)), _head_pad(wk), _head_pad(wv),
                             wf_pad, wxr, wg], axis=1).astype(BF16)
    bf_pad = jnp.concatenate([b_f, b_f, b_f, jnp.zeros((LANES - 3 * FOX_HEADS,), F32)]).reshape(1, LANES)
    eq = np.zeros((LANES, AUG), np.float32)
    ek = np.zeros((LANES, AUG), np.float32)
    ec = np.zeros((LANES, AUG), np.float32)
    cq = np.zeros((1, AUG), np.float32)
    ck = np.zeros((1, AUG), np.float32)
    cv = np.zeros((1, AUG), np.float32)
    hsel = np.zeros((AUG, LANES), np.float32)
    for h in range(FOX_HEADS):
        hsel[LANES * h:LANES * (h + 1), h] = 1.0
        cv[0, extra[h]] = 1.0
        ec[h, extra[h] + 6] = 1.0
        ck[0, extra[h] + 6] = 1.0
        for part in range(3):
            eq[part * 8 + h, extra[h] + part] = 1.0
            ek[part * 8 + h, extra[h] + 3 + part] = 1.0
            cq[0, extra[h] + 3 + part] = 1.0
            ck[0, extra[h] + part] = 1.0
    dense = lambda w: jax.scipy.linalg.block_diag(*[w[i] for i in range(LRU_BLOCKS)]).astype(BF16)
    row = lambda a: a.reshape(1, -1)
    return dict(g_mix=row(g_mix), w_all=w_all, bf_pad=bf_pad, eq=jnp.asarray(eq, BF16), ek=jnp.asarray(ek, BF16),
                ec=jnp.asarray(ec, BF16), cq=jnp.asarray(cq), ck=jnp.asarray(ck), cv=jnp.asarray(cv), hsel=jnp.asarray(hsel, BF16),
                conv_w=conv_w, conv_b=row(conv_b),
                wa=dense(w_a), ba=row(b_a), wi=dense(w_i), bi=row(b_i), lam=row(lam))


def kernel(x_prompt, x_sample, mem_prompt, cache_fox_k, cache_fox_v, cache_fox_logf, state_lru_h, state_conv, cache_mem_k, cache_mem_v, g_mix, w_in, b_f, conv_w, conv_b, w_a, b_a, w_i, b_i, lam, w_out, g_cross, g_mem, w_cq, w_ck, w_cv, w_co, g_moe, w_router, b_router, w_gate, b_gate, w_up, b_up, w_down, b_down, g_final):
    nb_p, seq, _ = x_prompt.shape
    nb_s, t_s, _ = x_sample.shape
    past = cache_fox_k.shape[2]
    n_mem = mem_prompt.shape[1]
    assert nb_p == 1 and g_mix.shape[0] == 1
    n_p, n_s = seq, nb_s * t_s
    row = lambda a: a.reshape(1, -1)

    mw = _prep_mix_weights(g_mix[0], w_in[0], b_f[0], conv_w[0], conv_b[0], w_a[0], b_a[0], w_i[0], b_i[0], lam[0])
    xp = x_prompt.reshape(n_p, D_MODEL)
    xs_ = x_sample.reshape(n_s, D_MODEL)

    zero_prev = jnp.zeros((1, SUBLANES, LRU_WIDTH), F32)
    zero_h = jnp.zeros((1, 1, LRU_WIDTH), F32)
    (qa_p, ka_p, va_p, k_p, v_p, _, lf_p, f2_p, qn_p, kn_p, lru_p, hl_p, ct_p) = _mix_in(
        xp, zero_prev, zero_h, mw, tm=MIX_TILE, streaming=True)
    prev_s = jnp.pad(state_conv[0], ((0, 0), (SUBLANES - (CONV_WIDTH - 1), 0), (0, 0)))
    (qa_s, ka_s, _, k_s, v_s, vb_s, lf_s, _, _, _, lru_s, hl_s, ct_s) = _mix_in(
        xs_, prev_s, state_lru_h[0].reshape(nb_s, 1, LRU_WIDTH), mw, tm=t_s, streaming=False)

    n_past, static_ok = _fox_plan(f2_p, qn_p, kn_p, bq=FOX_BQ, tile=MIX_TILE)
    fo_p = _fox_prompt(n_past, static_ok, qa_p, ka_p, va_p, bq=FOX_BQ)
    lf_c = cache_fox_logf[0]
    clf3 = jnp.concatenate([lf_c, lf_c, lf_c, jnp.zeros((nb_s, past, LANES - 3 * FOX_HEADS), F32)], axis=-1)
    fo_s = _fox_sample(qa_s, ka_s, vb_s, cache_fox_k[0].reshape(nb_s, past, FOX_WIDTH),
                       cache_fox_v[0].reshape(nb_s, past, FOX_WIDTH), clf3, mw["ek"], mw["ck"], t=t_s)

    mk_p, mv_p = _mem_kv(mem_prompt[0], row(g_mem[0]), w_ck[0].astype(BF16), w_cv[0].astype(BF16))

    wr_pad = jnp.pad(w_router[0], ((0, 0), (0, LANES - N_EXPERTS))).astype(BF16)
    br_pad = jnp.pad(b_router[0], (0, LANES - N_EXPERTS)).reshape(1, LANES)
    pw = dict(w_out=w_out[0].astype(BF16), g_cross=row(g_cross[0]), w_cq=w_cq[0].astype(BF16),
              w_co=w_co[0].astype(BF16), g_moe=row(g_moe[0]), w_router=wr_pad, b_router=br_pad)
    cnt0 = jnp.zeros((SUBLANES, LANES), F32)
    h2_p, xn_p, idx_p, tw_p, rk_p, cnt_p = _post(
        xp, fo_p, lru_p, mk_p.reshape(1, n_mem, D_MODEL), mv_p.reshape(1, n_mem, D_MODEL), cnt0, pw,
        tm=POST_TILE, per_step_memory=False)
    h2_s, xn_s, idx_s, tw_s, rk_s, cnt = _post(
        xs_, fo_s, lru_s, cache_mem_k[0].reshape(nb_s, n_mem, D_MODEL), cache_mem_v[0].reshape(nb_s, n_mem, D_MODEL),
        cnt_p, pw, tm=t_s, per_step_memory=True)

    n = n_p + n_s
    idx = jnp.concatenate([idx_p[:, :TOP_K], idx_s[:, :TOP_K]], axis=0)
    rank = jnp.concatenate([rk_p[:, :TOP_K], rk_s[:, :TOP_K]], axis=0)
    counts = cnt[0, :N_EXPERTS].astype(I32)
    padded = (counts + MOE_BM - 1) // MOE_BM * MOE_BM
    pad_end = jnp.cumsum(padded)
    pad_start = pad_end - padded
    dest = (pad_start[idx] + rank).reshape(n * TOP_K)
    n_blocks = -(-(n * TOP_K) // MOE_BM) + N_EXPERTS
    blk_row = jnp.arange(n_blocks, dtype=I32) * MOE_BM
    block_e = jnp.minimum(jnp.sum(pad_end[None, :] <= blk_row[:, None], axis=1), N_EXPERTS - 1).astype(I32)
    n_used = (pad_end[-1] // MOE_BM).reshape(1).astype(I32)
    last_block = jnp.maximum(pad_end // MOE_BM - 1, 0).astype(I32)

    xs_sorted = _dispatch(dest, last_block, n_used, xn_p, xn_s, tm=MOE_BM, n_blocks=n_blocks)
    ew = dict(w_gate=w_gate[0], w_up=w_up[0], w_down=w_down[0],
              b_gate=b_gate[0].reshape(N_EXPERTS, 1, D_MODEL), b_up=b_up[0].reshape(N_EXPERTS, 1, D_MODEL),
              b_down=b_down[0].reshape(N_EXPERTS, 1, D_MODEL))
    ys = _experts(block_e, n_used, xs_sorted, ew, bm=MOE_BM)
    y_p, y_s = _combine(dest, h2_p, h2_s, tw_p, tw_s, row(g_final), ys, tm=COMBINE_TILE)

    shp_p = (1, nb_p, seq, FOX_HEADS, FOX_HEAD_DIM)
    shp_s = (1, nb_s, t_s, FOX_HEADS, FOX_HEAD_DIM)
    tail = slice(SUBLANES - (CONV_WIDTH - 1), SUBLANES)
    return (y_p.reshape(nb_p, seq, D_MODEL), y_s.reshape(nb_s, t_s, D_MODEL),
            k_p.reshape(shp_p), v_p.reshape(shp_p), lf_p.reshape(1, nb_p, seq, FOX_HEADS),
            hl_p.reshape(1, nb_p, LRU_WIDTH), ct_p[:, tail, :].reshape(1, nb_p, CONV_WIDTH - 1, LRU_WIDTH),
            mk_p.reshape(1, nb_p, n_mem, MEM_HEADS, MEM_HEAD_DIM), mv_p.reshape(1, nb_p, n_mem, MEM_HEADS, MEM_HEAD_DIM),
            k_s.reshape(shp_s), v_s.reshape(shp_s), lf_s.reshape(1, nb_s, t_s, FOX_HEADS),
            hl_s.reshape(1, nb_s, LRU_WIDTH), ct_s[:, tail, :].reshape(1, nb_s, CONV_WIDTH - 1, LRU_WIDTH))
```

```python
import functools
import math

import jax
import jax.numpy as jnp
import numpy as np
from jax import lax
from jax.experimental import pallas as pl
from jax.experimental.pallas import tpu as pltpu

F32 = jnp.float32
BF16 = jnp.bfloat16
I32 = jnp.int32

D_MODEL = 1024
FOX_HEADS = 8
FOX_HEAD_DIM = 64
FOX_WIDTH = FOX_HEADS * FOX_HEAD_DIM
LRU_WIDTH = D_MODEL - FOX_WIDTH
LRU_BLOCKS = 8
LRU_C = 8.0
CONV_WIDTH = 4
MEM_HEADS = 4
MEM_HEAD_DIM = D_MODEL // MEM_HEADS
N_EXPERTS = 32
TOP_K = 4
SWIGLU_LIMIT = 7.0
SWIGLU_ALPHA = 1.702
RMS_EPS = 1e-6
NEG_INF = -1e30
LOG2E = math.log2(math.e)

LANES = 128
SUBLANES = 8
AUG = LANES * FOX_HEADS
VMEM_LIMIT = 56 * 1024 * 1024

MIX_TILE = 256
POST_TILE = 512
FOX_BQ = 512
MOE_BM = 512
COMBINE_TILE = 256
FOX_SKIP_GAP = 160.0
FOX_STATIC_SHIFT_RANGE = 100.0

C_Q, C_KA, C_VA, C_F, C_XR, C_G, C_END = 0, 1024, 2048, 3072, 3200, 3712, 4224


def _dot(a, b):
    return jnp.dot(a, b, preferred_element_type=F32)


def _dot_nt(a, b):
    return lax.dot_general(a, b, (((1,), (1,)), ((), ())), preferred_element_type=F32)


def _split3(x):
    hi = x.astype(BF16)
    r = x - hi.astype(F32)
    mid = r.astype(BF16)
    lo = (r - mid.astype(F32)).astype(BF16)
    return hi, mid, lo


def _dot3(a, x):
    hi, mid, lo = _split3(x)
    return _dot(a, hi) + _dot(a, mid) + _dot(a, lo)


def _lane_split3(x):
    hi, mid, lo = _split3(x)
    lane = lax.broadcasted_iota(I32, x.shape, 1)
    return jnp.where(lane < 8, hi, jnp.where(lane < 16, mid, lo))


def _softplus(x):
    return jnp.maximum(x, 0.0) + jnp.log1p(jnp.exp(-jnp.abs(x)))


def _rms(x, g):
    return x * lax.rsqrt(jnp.mean(x * x, axis=-1, keepdims=True) + RMS_EPS) * g


def _tri(n, kind):
    r = lax.broadcasted_iota(I32, (n, n), 0)
    c = lax.broadcasted_iota(I32, (n, n), 1)
    m = {"le": c <= r, "lt": c < r, "gt": c > r}[kind]
    return jnp.where(m, 1.0, 0.0).astype(BF16)


def _head_sumsq_max(zb, hsel):
    sq = zb.astype(F32)
    return jnp.max(_dot((sq * sq).astype(BF16), hsel), axis=0, keepdims=True)


def _mix_in_kernel(x_ref, g_ref, w_ref, bfp_ref, eq_ref, ek_ref, ec_ref, cq_ref, ck_ref, cv_ref, hs_ref,
                   cw_ref, cb_ref, wa_ref, ba_ref, wi_ref, bi_ref, lam_ref, cprev_ref, h0_ref,
                   qa_ref, ka_ref, va_ref, ko_ref, vo_ref, vb_ref, lf_ref, f2_ref, qn_ref, kn_ref,
                   lru_ref, hl_ref, ct_ref,
                   fcar, hcar, xp_ref, sa_ref, sb_ref, kbuf, vbuf, ksem, *, tm, pad, streaming):
    step = pl.program_id(0)
    first = step == 0
    slot = step % 2

    def kv_copies(s, row0):
        rows = pl.ds(row0, tm)
        copies = []
        for h in range(FOX_HEADS):
            copies.append(pltpu.make_async_copy(kbuf.at[s, h], ko_ref.at[rows, h, :], ksem.at[s]))
            copies.append(pltpu.make_async_copy(vbuf.at[s, h], vo_ref.at[rows, h, :], ksem.at[s]))
        return copies

    @pl.when(step >= 2)
    def _():
        for c in kv_copies(slot, 0):
            c.wait()

    if streaming:
        @pl.when(first)
        def _():
            fcar[...] = jnp.zeros_like(fcar)
            hcar[...] = jnp.zeros_like(hcar)
            xp_ref[0:SUBLANES, :] = jnp.zeros((SUBLANES, LRU_WIDTH), F32)
    else:
        fcar[...] = jnp.zeros_like(fcar)
        hcar[...] = jnp.broadcast_to(h0_ref[0], hcar.shape)
        xp_ref[0:SUBLANES, :] = cprev_ref[0]

    x = x_ref[...]
    xn = _rms(x, g_ref[...]).astype(BF16)
    z = _dot(xn, w_ref[...])

    main, _ = _aug_layout()
    for h in range(FOX_HEADS):
        kbuf[slot, h] = z[:, C_KA + main[h]:C_KA + main[h] + FOX_HEAD_DIM]
        vbuf[slot, h] = z[:, C_VA + main[h]:C_VA + main[h] + FOX_HEAD_DIM]
    for c in kv_copies(slot, pl.multiple_of(step * tm, tm)):
        c.start()
    for p in range(FOX_HEADS // 2):
        a, b = 2 * p * LANES, (2 * p + 1) * LANES
        vv = z[:, C_VA + a:C_VA + b] + z[:, C_VA + b:C_VA + b + LANES]
        vb_ref[:, p * LANES:(p + 1) * LANES] = vv.astype(BF16)
    va_ref[...] = (z[:, C_VA:C_F] + cv_ref[...]).astype(BF16)

    lane = lax.broadcasted_iota(I32, (tm, LANES), 1)
    lf = jnp.where(lane < 3 * FOX_HEADS, -_softplus(-(z[:, C_F:C_XR] + bfp_ref[...])), 0.0)
    lf_ref[...] = lf[:, :FOX_HEADS]
    cum = _dot3(_tri(tm, "le"), lf) + fcar[0:1, :]
    fcar[...] = jnp.broadcast_to(cum[tm - 1:tm, :], fcar.shape)
    cum2 = cum * LOG2E
    f2_ref[...] = cum2[:, :FOX_HEADS]
    fs = _lane_split3(cum2)
    qb = z[:, C_Q:C_KA].astype(BF16)
    kb = z[:, C_KA:C_VA].astype(BF16)
    diag = _dot((qb.astype(F32) * kb.astype(F32)).astype(BF16), hs_ref[...]).astype(BF16)
    qa_ref[...] = qb + (_dot(fs, eq_ref[...]) + cq_ref[...] - _dot(diag, ec_ref[...])).astype(BF16)
    ka_ref[...] = kb + (ck_ref[...] - _dot(fs, ek_ref[...])).astype(BF16)
    qn_ref[...] = jnp.broadcast_to(_head_sumsq_max(qb, hs_ref[...]), qn_ref.shape)
    kn_ref[...] = jnp.broadcast_to(_head_sumsq_max(kb, hs_ref[...]), kn_ref.shape)

    xr = z[:, C_XR:C_G]
    xp_ref[SUBLANES:SUBLANES + tm, :] = xr
    xc = cb_ref[...] + xr * cw_ref[CONV_WIDTH - 1:CONV_WIDTH, :]
    for j in range(CONV_WIDTH - 1):
        sh = CONV_WIDTH - 1 - j
        xc = xc + xp_ref[SUBLANES - sh:SUBLANES - sh + tm, :] * cw_ref[j:j + 1, :]
    tail = xp_ref[tm:tm + SUBLANES, :]
    ct_ref[0] = tail
    xp_ref[0:SUBLANES, :] = tail

    xcb = xc.astype(BF16)
    r = jax.nn.sigmoid(_dot(xcb, wa_ref[...]) + ba_ref[...])
    ig = jax.nn.sigmoid(_dot(xcb, wi_ref[...]) + bi_ref[...])
    log_a = (-LRU_C) * r * _softplus(-lam_ref[...])
    a = jnp.exp(log_a)
    mult = jnp.sqrt(-jnp.tanh(log_a) * (a * a + 1.0))
    if streaming:
        row = lax.broadcasted_iota(I32, (tm, LRU_WIDTH), 0)
        mult = jnp.where(jnp.logical_and(row == 0, first), 1.0, mult)
    b = mult * ig * xc

    sa_ref[0:pad, :] = jnp.ones((pad, LRU_WIDTH), F32)
    sb_ref[0:pad, :] = jnp.zeros((pad, LRU_WIDTH), F32)
    d = 1
    while d < tm:
        sa_ref[pad:pad + tm, :] = a
        sb_ref[pad:pad + tm, :] = b
        b = a * sb_ref[pad - d:pad - d + tm, :] + b
        a = a * sa_ref[pad - d:pad - d + tm, :]
        d *= 2
    h = a * hcar[0:1, :] + b
    hlast = h[tm - 1:tm, :]
    hcar[...] = jnp.broadcast_to(hlast, hcar.shape)
    hl_ref[0] = hlast
    lru_ref[...] = (h * jax.nn.gelu(z[:, C_G:C_END])).astype(BF16)

    last = pl.num_programs(0) - 1

    @pl.when(jnp.logical_and(step == last, step >= 1))
    def _():
        for c in kv_copies(1 - slot, 0):
            c.wait()

    @pl.when(step == last)
    def _():
        for c in kv_copies(slot, 0):
            c.wait()


def _mix_in(x, cprev, h0, wts, *, tm, streaming):
    n = x.shape[0]
    steps = n // tm
    nseg = 1 if streaming else steps
    pad = max(tm // 2, SUBLANES)
    seg = (lambda i: (0, 0, 0)) if streaming else (lambda i: (i, 0, 0))
    full = lambda a: pl.BlockSpec(a.shape, lambda i: (0,) * a.ndim)
    rows = lambda w: pl.BlockSpec((tm, w), lambda i: (i, 0))
    names = ("g_mix", "w_all", "bf_pad", "eq", "ek", "ec", "cq", "ck", "cv", "hsel", "conv_w", "conv_b",
             "wa", "ba", "wi", "bi", "lam")
    ws = [wts[k] for k in names]
    sds = jax.ShapeDtypeStruct
    out_shape = (
        sds((n, AUG), BF16), sds((n, AUG), BF16), sds((n, AUG), BF16),
        sds((n, FOX_HEADS, FOX_HEAD_DIM), F32), sds((n, FOX_HEADS, FOX_HEAD_DIM), F32), sds((n, FOX_WIDTH), BF16),
        sds((n, FOX_HEADS), F32), sds((n, FOX_HEADS), F32),
        sds((steps * SUBLANES, LANES), F32), sds((steps * SUBLANES, LANES), F32),
        sds((n, LRU_WIDTH), BF16),
        sds((nseg, 1, LRU_WIDTH), F32), sds((nseg, SUBLANES, LRU_WIDTH), F32),
    )
    out_specs = (
        rows(AUG), rows(AUG), rows(AUG), pl.BlockSpec(memory_space=pl.ANY), pl.BlockSpec(memory_space=pl.ANY),
        rows(FOX_WIDTH), rows(FOX_HEADS), rows(FOX_HEADS),
        pl.BlockSpec((SUBLANES, LANES), lambda i: (i, 0)), pl.BlockSpec((SUBLANES, LANES), lambda i: (i, 0)),
        rows(LRU_WIDTH),
        pl.BlockSpec((1, 1, LRU_WIDTH), seg), pl.BlockSpec((1, SUBLANES, LRU_WIDTH), seg),
    )
    in_specs = [rows(D_MODEL)] + [full(w) for w in ws] + [
        pl.BlockSpec((1, SUBLANES, LRU_WIDTH), seg), pl.BlockSpec((1, 1, LRU_WIDTH), seg)]
    return pl.pallas_call(
        functools.partial(_mix_in_kernel, tm=tm, pad=pad, streaming=streaming),
        grid=(steps,), in_specs=in_specs, out_specs=out_specs, out_shape=out_shape,
        scratch_shapes=[
            pltpu.VMEM((SUBLANES, LANES), F32), pltpu.VMEM((SUBLANES, LRU_WIDTH), F32),
            pltpu.VMEM((tm + SUBLANES, LRU_WIDTH), F32),
            pltpu.VMEM((pad + tm, LRU_WIDTH), F32), pltpu.VMEM((pad + tm, LRU_WIDTH), F32),
            pltpu.VMEM((2, FOX_HEADS, tm, FOX_HEAD_DIM), F32), pltpu.VMEM((2, FOX_HEADS, tm, FOX_HEAD_DIM), F32),
            pltpu.SemaphoreType.DMA((2,))],
        compiler_params=pltpu.CompilerParams(dimension_semantics=("arbitrary",), vmem_limit_bytes=VMEM_LIMIT),
        name="mix_in_stream" if streaming else "mix_in_segments",
    )(x, *ws, cprev, h0)


def _fox_prompt_kernel(nb_ref, st_ref, q_ref, k_ref, v_ref, o_ref, m_ref, acc_ref, *, bq):
    qi = pl.program_id(1)
    plan = pl.program_id(0) * pl.num_programs(1) + qi
    n_past = nb_ref[plan]
    cols = [slice(LANES * c, LANES * (c + 1)) for c in range(2)]
    qs = [q_ref[:, cols[c]] for c in range(2)]
    acc_ref[...] = jnp.zeros(acc_ref.shape, F32)

    def scores(c, start, causal):
        s = _dot_nt(qs[c], k_ref[pl.ds(start, bq), cols[c]])
        if causal:
            r = lax.broadcasted_iota(I32, (bq, bq), 0)
            cc = lax.broadcasted_iota(I32, (bq, bq), 1)
            s = jnp.where(cc <= r, s, NEG_INF)
        return s

    def static_block(start, causal):
        for c in range(2):
            p = jnp.exp2(scores(c, start, causal)).astype(BF16)
            acc_ref[c] += _dot(p, v_ref[pl.ds(start, bq), cols[c]])

    def online_block(start, causal):
        for c in range(2):
            s = scores(c, start, causal)
            m_old = m_ref[c]
            m_new = jnp.maximum(m_old, jnp.max(s, axis=1, keepdims=True))
            p = jnp.exp2(s - m_new).astype(BF16)
            acc_ref[c] = jnp.exp2(m_old - m_new) * acc_ref[c] + _dot(p, v_ref[pl.ds(start, bq), cols[c]])
            m_ref[c] = m_new

    def run(block):
        block(pl.multiple_of(qi * bq, bq), True)

        def body(t, carry):
            block(pl.multiple_of((qi - 1 - t) * bq, bq), False)
            return carry

        lax.fori_loop(0, n_past, body, 0)

    @pl.when(st_ref[plan] == 1)
    def _():
        run(static_block)

    @pl.when(st_ref[plan] != 1)
    def _():
        m_ref[...] = jnp.full(m_ref.shape, NEG_INF, F32)
        run(online_block)

    a0, a1 = acc_ref[0], acc_ref[1]
    o0 = a0 / a0[:, FOX_HEAD_DIM:FOX_HEAD_DIM + 1]
    o1 = a1 / a1[:, 0:1]
    lane = lax.broadcasted_iota(I32, (bq, LANES), 1)
    o_ref[...] = jnp.where(lane < FOX_HEAD_DIM, o0, o1).astype(BF16)


def _fox_prompt(n_past, static_ok, qa, ka, va, *, bq):
    s = qa.shape[0]
    pairs = FOX_HEADS // 2
    return pl.pallas_call(
        functools.partial(_fox_prompt_kernel, bq=bq),
        grid_spec=pltpu.PrefetchScalarGridSpec(
            num_scalar_prefetch=2, grid=(pairs, s // bq),
            in_specs=[pl.BlockSpec((bq, 2 * LANES), lambda p, i, nb, st: (i, p)),
                      pl.BlockSpec((s, 2 * LANES), lambda p, i, nb, st: (0, p)),
                      pl.BlockSpec((s, 2 * LANES), lambda p, i, nb, st: (0, p))],
            out_specs=pl.BlockSpec((bq, LANES), lambda p, i, nb, st: (i, p)),
            scratch_shapes=[pltpu.VMEM((2, bq, 1), F32), pltpu.VMEM((2, bq, LANES), F32)]),
        out_shape=jax.ShapeDtypeStruct((s, FOX_WIDTH), BF16),
        compiler_params=pltpu.CompilerParams(dimension_semantics=("arbitrary", "arbitrary"),
                                             vmem_limit_bytes=VMEM_LIMIT),
        name="fox_prompt",
    )(n_past, static_ok, qa, ka, va)


def _fox_plan(f2, qn, kn, *, bq, tile):
    s = f2.shape[0]
    nq = s // bq
    per = lambda a: a.reshape(s // tile, SUBLANES, LANES)[:, 0, :FOX_HEADS]
    qnorm = jnp.sqrt(jnp.max(per(qn).reshape(nq, bq // tile, FOX_HEADS), axis=1))
    knorm = jnp.sqrt(jnp.max(per(kn), axis=0))
    spread = 2.0 * 1.02 * qnorm * knorm[None, :]
    thr = spread + FOX_SKIP_GAP
    f_first = f2[0::bq]
    f_last = f2[bq - 1::bq]
    gap = f_last[None, :, :] - f_first[:, None, :]
    before = (jnp.arange(nq)[None, :] < jnp.arange(nq)[:, None])[:, :, None]
    need = jnp.sum(jnp.logical_and(before, gap < thr[:, None, :]), axis=1)
    pair = lambda a: a.reshape(nq, FOX_HEADS // 2, 2)
    n_past = jnp.max(pair(need), axis=2).T.reshape(-1).astype(I32)
    static_ok = jnp.all(pair(spread) <= FOX_STATIC_SHIFT_RANGE, axis=2).T.reshape(-1).astype(I32)
    return n_past, static_ok


def _fox_sample_kernel(q_ref, kn_ref, vn_ref, ck_ref, cv_ref, clf_ref, ek_ref, ckc_ref, o_ref, g_ref,
                       *, t, past, chunk):
    car = jnp.zeros((1, LANES), F32)
    upper = _tri(chunk, "gt")
    for ci in reversed(range(past // chunk)):
        lf = clf_ref[0, ci * chunk:(ci + 1) * chunk, :]
        g_ref[ci * chunk:(ci + 1) * chunk, :] = _dot3(upper, lf) + car
        car = car + jnp.sum(lf, axis=0, keepdims=True)
    gs = _lane_split3(g_ref[...] * LOG2E)

    lane = lax.broadcasted_iota(I32, (past, LANES), 1)
    r = lax.broadcasted_iota(I32, (t, t), 0)
    cc = lax.broadcasted_iota(I32, (t, t), 1)
    olane = lax.broadcasted_iota(I32, (t, LANES), 1)
    for p in range(FOX_HEADS // 2):
        pc = slice(LANES * p, LANES * (p + 1))
        kpair = ck_ref[0, :, pc]
        vpast = cv_ref[0, :, pc].astype(BF16)
        vnew = vn_ref[:, pc]
        outs = []
        for c in range(2):
            h = 2 * p + c
            hc = slice(LANES * h, LANES * (h + 1))
            main = (lane < FOX_HEAD_DIM) if c == 0 else (lane >= FOX_HEAD_DIM)
            extras = _dot(gs, ek_ref[:, hc]) + ckc_ref[:, hc]
            kpast = (jnp.where(main, kpair, 0.0) + extras).astype(BF16)
            q = q_ref[:, hc]
            sp = _dot_nt(q, kpast)
            sn = jnp.where(cc <= r, _dot_nt(q, kn_ref[:, hc]), NEG_INF)
            m = jnp.maximum(jnp.max(sp, axis=1, keepdims=True), jnp.max(sn, axis=1, keepdims=True))
            pp = jnp.exp2(sp - m)
            pn = jnp.exp2(sn - m)
            l = jnp.sum(pp, axis=1, keepdims=True) + jnp.sum(pn, axis=1, keepdims=True)
            outs.append((_dot(pp.astype(BF16), vpast) + _dot(pn.astype(BF16), vnew)) / l)
        o_ref[:, pc] = jnp.where(olane < FOX_HEAD_DIM, outs[0], outs[1]).astype(BF16)


def _fox_sample(qa, ka, vb, cache_k, cache_v, cache_lf3, ek, ck, *, t):
    nb, past = cache_k.shape[0], cache_k.shape[1]
    full = lambda a: pl.BlockSpec(a.shape, lambda b: (0,) * a.ndim)
    return pl.pallas_call(
        functools.partial(_fox_sample_kernel, t=t, past=past, chunk=256),
        grid=(nb,),
        in_specs=[pl.BlockSpec((t, AUG), lambda b: (b, 0)), pl.BlockSpec((t, AUG), lambda b: (b, 0)),
                  pl.BlockSpec((t, FOX_WIDTH), lambda b: (b, 0)),
                  pl.BlockSpec((1, past, FOX_WIDTH), lambda b: (b, 0, 0)),
                  pl.BlockSpec((1, past, FOX_WIDTH), lambda b: (b, 0, 0)),
                  pl.BlockSpec((1, past, LANES), lambda b: (b, 0, 0)),
                  full(ek), full(ck)],
        out_specs=pl.BlockSpec((t, FOX_WIDTH), lambda b: (b, 0)),
        out_shape=jax.ShapeDtypeStruct((nb * t, FOX_WIDTH), BF16),
        scratch_shapes=[pltpu.VMEM((past, LANES), F32)],
        compiler_params=pltpu.CompilerParams(dimension_semantics=("arbitrary",), vmem_limit_bytes=VMEM_LIMIT),
        name="fox_sample",
    )(qa, ka, vb, cache_k, cache_v, cache_lf3, ek, ck)


def _mem_kv_kernel(m_ref, g_ref, wk_ref, wv_ref, k_ref, v_ref):
    mn = _rms(m_ref[...], g_ref[...]).astype(BF16)
    k_ref[...] = _dot(mn, wk_ref[...])
    v_ref[...] = _dot(mn, wv_ref[...])


def _mem_kv(mem, g, wk, wv):
    n = mem.shape[0]
    return pl.pallas_call(
        _mem_kv_kernel,
        out_shape=(jax.ShapeDtypeStruct((n, D_MODEL), F32), jax.ShapeDtypeStruct((n, D_MODEL), F32)),
        compiler_params=pltpu.CompilerParams(vmem_limit_bytes=VMEM_LIMIT),
        name="mem_kv",
    )(mem, g, wk, wv)


def _post_kernel(h_ref, fo_ref, lru_ref, wo_ref, gc_ref, wcq_ref, mk_ref, mv_ref, wco_ref,
                 gm_ref, wr_ref, br_ref, cnt_in_ref,
                 h2_ref, xn_ref, idx_ref, tw_ref, rank_ref, cnt_ref, car_ref, *, tm):
    @pl.when(pl.program_id(0) == 0)
    def _():
        car_ref[...] = cnt_in_ref[...]

    h1 = h_ref[...] + _dot(fo_ref[...], wo_ref[0:FOX_WIDTH, :]) + _dot(lru_ref[...], wo_ref[FOX_WIDTH:D_MODEL, :])

    q = _dot(_rms(h1, gc_ref[...]).astype(BF16), wcq_ref[...])
    heads = []
    for hd in range(MEM_HEADS):
        hc = slice(MEM_HEAD_DIM * hd, MEM_HEAD_DIM * (hd + 1))
        s = _dot_nt(q[:, hc].astype(BF16), mk_ref[0, :, hc].astype(BF16)) * (MEM_HEAD_DIM ** -0.5)
        p = jnp.exp(s - jnp.max(s, axis=1, keepdims=True))
        o = _dot(p.astype(BF16), mv_ref[0, :, hc].astype(BF16)) / jnp.sum(p, axis=1, keepdims=True)
        heads.append(o.astype(BF16))
    h2 = h1 + _dot(jnp.concatenate(heads, axis=1), wco_ref[...])
    h2_ref[...] = h2

    xn = _rms(h2, gm_ref[...])
    xn_ref[...] = xn

    lane = lax.broadcasted_iota(I32, (tm, LANES), 1).astype(F32)
    logits = jnp.where(lane < N_EXPERTS, _dot(xn.astype(BF16), wr_ref[...]) + br_ref[...], -jnp.inf)
    vals, idxs = [], []
    for _ in range(TOP_K):
        mx = jnp.max(logits, axis=1, keepdims=True)
        ix = jnp.min(jnp.where(logits == mx, lane, float(LANES)), axis=1, keepdims=True)
        vals.append(mx)
        idxs.append(ix)
        logits = jnp.where(lane == ix, -jnp.inf, logits)
    es = [jnp.exp(v - vals[0]) for v in vals]
    den = es[0] + es[1] + es[2] + es[3]

    onehot = jnp.zeros((tm, LANES), F32)
    for ix in idxs:
        onehot = onehot + jnp.where(lane == ix, 1.0, 0.0)
    before = _dot(_tri(tm, "lt"), onehot.astype(BF16)) + car_ref[0:1, :]
    car = car_ref[0:1, :] + jnp.sum(onehot, axis=0, keepdims=True)
    car_ref[...] = jnp.broadcast_to(car, car_ref.shape)
    cnt_ref[...] = jnp.broadcast_to(car, cnt_ref.shape)

    idx_o = jnp.zeros((tm, LANES), F32)
    tw_o = jnp.zeros((tm, LANES), F32)
    rk_o = jnp.zeros((tm, LANES), F32)
    for j in range(TOP_K):
        rk = jnp.sum(jnp.where(lane == idxs[j], before, 0.0), axis=1, keepdims=True)
        idx_o = jnp.where(lane == j, idxs[j], idx_o)
        tw_o = jnp.where(lane == j, es[j] / den, tw_o)
        rk_o = jnp.where(lane == j, rk, rk_o)
    idx_ref[...] = idx_o.astype(I32)
    tw_ref[...] = tw_o
    rank_ref[...] = rk_o.astype(I32)


def _post(h, fo, lru, mk, mv, cnt_in, wts, *, tm, per_step_memory):
    n = h.shape[0]
    nm = mk.shape[1]
    full = lambda a: pl.BlockSpec(a.shape, lambda i: (0,) * a.ndim)
    rows = lambda w: pl.BlockSpec((tm, w), lambda i: (i, 0))
    mem = pl.BlockSpec((1, nm, D_MODEL), (lambda i: (i, 0, 0)) if per_step_memory else (lambda i: (0, 0, 0)))
    w = [wts[k] for k in ("w_out", "g_cross", "w_cq")]
    w2 = [wts[k] for k in ("w_co", "g_moe", "w_router", "b_router")]
    return pl.pallas_call(
        functools.partial(_post_kernel, tm=tm),
        grid=(n // tm,),
        in_specs=[rows(D_MODEL), rows(FOX_WIDTH), rows(LRU_WIDTH)] + [full(a) for a in w] + [mem, mem]
                 + [full(a) for a in w2] + [full(cnt_in)],
        out_specs=(rows(D_MODEL), rows(D_MODEL), rows(LANES), rows(LANES), rows(LANES),
                   pl.BlockSpec((SUBLANES, LANES), lambda i: (0, 0))),
        out_shape=(jax.ShapeDtypeStruct((n, D_MODEL), F32), jax.ShapeDtypeStruct((n, D_MODEL), F32),
                   jax.ShapeDtypeStruct((n, LANES), I32), jax.ShapeDtypeStruct((n, LANES), F32),
                   jax.ShapeDtypeStruct((n, LANES), I32), jax.ShapeDtypeStruct((SUBLANES, LANES), F32)),
        scratch_shapes=[pltpu.VMEM((SUBLANES, LANES), F32)],
        compiler_params=pltpu.CompilerParams(dimension_semantics=("arbitrary",), vmem_limit_bytes=VMEM_LIMIT),
        name="post_shared_mem" if not per_step_memory else "post_batch_mem",
    )(h, fo, lru, *w, mk, mv, *w2, cnt_in)


STAGES = 3


def _dispatch_kernel(dest_ref, last_ref, nu_ref, xp_ref, xs_ref, out_ref, zero_ref, stage, sems, gsem, zsem,
                     *, tm, steps_p, n_blocks):
    i = pl.program_id(0)

    def zero_copy(b):
        return pltpu.make_async_copy(zero_ref, out_ref.at[pl.ds(pl.multiple_of(b * tm, tm), tm), :], zsem)

    @pl.when(i == 0)
    def _():
        zero_ref[...] = jnp.zeros(zero_ref.shape, F32)
        n_tail = n_blocks - nu_ref[0]

        def start_e(e, c):
            zero_copy(last_ref[e]).start()
            return c

        def start_t(b, c):
            zero_copy(nu_ref[0] + b).start()
            return c

        def wait_one(b, c):
            zero_copy(0).wait()
            return c

        lax.fori_loop(0, N_EXPERTS, start_e, 0)
        lax.fori_loop(0, n_tail, start_t, 0)
        lax.fori_loop(0, N_EXPERTS + n_tail, wait_one, 0)

    groups = tm // SUBLANES
    steps = pl.num_programs(0)

    def stage_copy(src_ref, tile, slot):
        return pltpu.make_async_copy(src_ref.at[pl.ds(tile * groups, groups)], stage.at[slot], gsem.at[slot])

    def start_stage(tile, slot):
        @pl.when(tile < steps_p)
        def _():
            stage_copy(xp_ref, tile, slot).start()

        @pl.when(tile >= steps_p)
        def _():
            stage_copy(xs_ref, tile - steps_p, slot).start()

    @pl.when(i == 0)
    def _():
        start_stage(0, 0)

    @pl.when(i + 1 < steps)
    def _():
        start_stage(i + 1, (i + 1) % STAGES)

    slot = i % STAGES
    stage_copy(xp_ref, 0, slot).wait()

    base = i * (tm * TOP_K)

    def start(g, c):
        for u in range(SUBLANES):
            for j in range(TOP_K):
                d = dest_ref[base + (g * SUBLANES + u) * TOP_K + j]
                pltpu.make_async_copy(stage.at[slot, g, u], out_ref.at[d],
                                      sems.at[i % 2]).start(priority=j % 2)
        return c

    lax.fori_loop(0, groups, start, 0)

    def drain(parity):
        for _ in range(TOP_K):
            pltpu.make_async_copy(out_ref.at[pl.ds(0, tm), :], out_ref.at[pl.ds(0, tm), :], sems.at[parity]).wait()

    @pl.when(i > 0)
    def _():
        drain((i - 1) % 2)

    @pl.when(i == steps - 1)
    def _():
        drain(i % 2)


def _dispatch(dest_flat, last_block, n_used, xn_p, xn_s, *, tm, n_blocks):
    steps_p, steps_s = xn_p.shape[0] // tm, xn_s.shape[0] // tm
    return pl.pallas_call(
        functools.partial(_dispatch_kernel, tm=tm, steps_p=steps_p, n_blocks=n_blocks),
        grid_spec=pltpu.PrefetchScalarGridSpec(
            num_scalar_prefetch=3, grid=(steps_p + steps_s,),
            in_specs=[pl.BlockSpec(memory_space=pl.ANY), pl.BlockSpec(memory_space=pl.ANY)],
            out_specs=pl.BlockSpec(memory_space=pl.ANY),
            scratch_shapes=[pltpu.VMEM((tm, D_MODEL), F32),
                            pltpu.VMEM((STAGES, tm // SUBLANES, SUBLANES, D_MODEL), F32),
                            pltpu.SemaphoreType.DMA((2,)), pltpu.SemaphoreType.DMA((STAGES,)),
                            pltpu.SemaphoreType.DMA(())]),
        out_shape=jax.ShapeDtypeStruct((n_blocks * tm, D_MODEL), F32),
        compiler_params=pltpu.CompilerParams(dimension_semantics=("arbitrary",), vmem_limit_bytes=VMEM_LIMIT),
        name="moe_dispatch",
    )(dest_flat, last_block, n_used, xn_p.reshape(-1, SUBLANES, D_MODEL), xn_s.reshape(-1, SUBLANES, D_MODEL))


def _expert_kernel(be_ref, nu_ref, xs_ref, wg_ref, bg_ref, wu_ref, bu_ref, wd_ref, bd_ref, ys_ref,
                   wgb, wub, wdb):
    i = pl.program_id(0)
    prev = be_ref[jnp.maximum(i - 1, 0)]
    fresh = jnp.logical_or(i == 0, be_ref[i] != prev)
    live = i < nu_ref[0]

    @pl.when(jnp.logical_and(live, fresh))
    def _():
        wgb[...] = wg_ref[0].astype(BF16)
        wub[...] = wu_ref[0].astype(BF16)
        wdb[...] = wd_ref[0].astype(BF16)

    @pl.when(live)
    def _():
        x = xs_ref[...].astype(BF16)
        g = jnp.minimum(_dot(x, wgb[...]) + bg_ref[0], SWIGLU_LIMIT)
        u = jnp.clip(_dot(x, wub[...]) + bu_ref[0], -SWIGLU_LIMIT, SWIGLU_LIMIT)
        hdn = g * jax.nn.sigmoid(SWIGLU_ALPHA * g) * (u + 1.0)
        ys_ref[...] = _dot(hdn.astype(BF16), wdb[...]) + bd_ref[0]

    @pl.when(jnp.logical_not(live))
    def _():
        ys_ref[...] = jnp.zeros(ys_ref.shape, F32)


def _experts(block_e, n_used, xs, wts, *, bm):
    p = xs.shape[0]
    blk = lambda i, be, nu: (jnp.minimum(i, nu[0] - 1), 0)
    oblk = lambda i, be, nu: (i, 0)
    wsel = lambda i, be, nu: (be[i], 0, 0)
    wspec = pl.BlockSpec((1, D_MODEL, D_MODEL), wsel)
    bspec = pl.BlockSpec((1, 1, D_MODEL), wsel)
    return pl.pallas_call(
        _expert_kernel,
        grid_spec=pltpu.PrefetchScalarGridSpec(
            num_scalar_prefetch=2, grid=(p // bm,),
            in_specs=[pl.BlockSpec((bm, D_MODEL), blk), wspec, bspec, wspec, bspec, wspec, bspec],
            out_specs=pl.BlockSpec((bm, D_MODEL), oblk),
            scratch_shapes=[pltpu.VMEM((D_MODEL, D_MODEL), BF16)] * 3),
        out_shape=jax.ShapeDtypeStruct((p, D_MODEL), F32),
        compiler_params=pltpu.CompilerParams(dimension_semantics=("arbitrary",), vmem_limit_bytes=VMEM_LIMIT),
        name="moe_experts",
    )(block_e, n_used, xs, wts["w_gate"], wts["b_gate"], wts["w_up"], wts["b_up"], wts["w_down"], wts["b_down"])


def _combine_kernel(dest_ref, hp_ref, hs_ref, twp_ref, tws_ref, gf_ref, ys_ref, yp_ref, ysm_ref, buf, sems,
                    *, tm, steps_p):
    i = pl.program_id(0)
    slot = i % 2

    def issue(tile, into):
        base = tile * (tm * TOP_K)

        def start(g, c):
            for u in range(SUBLANES):
                for j in range(TOP_K):
                    d = dest_ref[base + (g * SUBLANES + u) * TOP_K + j]
                    pltpu.make_async_copy(ys_ref.at[pl.ds(d, 1), :], buf.at[into, j, g, pl.ds(u, 1), :],
                                          sems.at[into]).start(priority=j % 2)
            return c

        lax.fori_loop(0, tm // SUBLANES, start, 0)

    @pl.when(i == 0)
    def _():
        issue(0, 0)

    @pl.when(i + 1 < pl.num_programs(0))
    def _():
        issue(i + 1, (i + 1) % 2)

    for j in range(TOP_K):
        pltpu.make_async_copy(ys_ref.at[pl.ds(0, tm), :], ys_ref.at[pl.ds(0, tm), :], sems.at[slot]).wait()

    def finish(h_ref, tw_ref, y_ref):
        tw = tw_ref[...]
        y = h_ref[...]
        for j in range(TOP_K):
            y = y + buf[slot, j].reshape(tm, D_MODEL) * tw[:, j:j + 1]
        y_ref[...] = _rms(y, gf_ref[...])

    @pl.when(i < steps_p)
    def _():
        finish(hp_ref, twp_ref, yp_ref)

    @pl.when(i >= steps_p)
    def _():
        finish(hs_ref, tws_ref, ysm_ref)


def _combine(dest_flat, h2_p, h2_s, tw_p, tw_s, g_final, ys, *, tm):
    steps_p, steps_s = h2_p.shape[0] // tm, h2_s.shape[0] // tm
    pblk = lambda w: pl.BlockSpec((tm, w), lambda i, d: (jnp.minimum(i, steps_p - 1), 0))
    sblk = lambda w: pl.BlockSpec((tm, w), lambda i, d: (jnp.maximum(i - steps_p, 0), 0))
    return pl.pallas_call(
        functools.partial(_combine_kernel, tm=tm, steps_p=steps_p),
        grid_spec=pltpu.PrefetchScalarGridSpec(
            num_scalar_prefetch=1, grid=(steps_p + steps_s,),
            in_specs=[pblk(D_MODEL), sblk(D_MODEL), pblk(LANES), sblk(LANES),
                      pl.BlockSpec((1, D_MODEL), lambda i, d: (0, 0)),
                      pl.BlockSpec(memory_space=pl.ANY)],
            out_specs=(pblk(D_MODEL), sblk(D_MODEL)),
            scratch_shapes=[pltpu.VMEM((2, TOP_K, tm // SUBLANES, SUBLANES, D_MODEL), F32),
                            pltpu.SemaphoreType.DMA((2,))]),
        out_shape=(jax.ShapeDtypeStruct(h2_p.shape, F32), jax.ShapeDtypeStruct(h2_s.shape, F32)),
        compiler_params=pltpu.CompilerParams(dimension_semantics=("arbitrary",), vmem_limit_bytes=VMEM_LIMIT),
        name="moe_combine",
    )(dest_flat, h2_p, h2_s, tw_p, tw_s, g_final, ys)


def _aug_layout():
    main, extra = [], []
    for h in range(FOX_HEADS):
        even = h % 2 == 0
        main.append(LANES * h + (0 if even else FOX_HEAD_DIM))
        extra.append(LANES * h + (FOX_HEAD_DIM if even else 0))
    return main, extra


def _head_pad(w):
    d = w.shape[0]
    w4 = w.reshape(d, FOX_HEADS // 2, 2, FOX_HEAD_DIM)
    z = jnp.zeros((d, FOX_HEADS // 2, FOX_HEAD_DIM), w.dtype)
    return jnp.concatenate([w4[:, :, 0], z, z, w4[:, :, 1]], axis=-1).reshape(d, AUG)


def _prep_mix_weights(g_mix, w_in, b_f, conv_w, conv_b, w_a, b_a, w_i, b_i, lam):
    _, extra = _aug_layout()
    wq, wk, wv = w_in[:, 0:512], w_in[:, 512:1024], w_in[:, 1024:1536]
    wf, wxr, wg = w_in[:, 1536:1544], w_in[:, 1544:2056], w_in[:, 2056:2568]
    wf_pad = jnp.concatenate([wf, wf, wf, jnp.zeros((D_MODEL, LANES - 3 * FOX_HEADS), F32)], axis=1)
    w_all = jnp.concatenate([_head_pad(wq * (FOX_HEAD_DIM ** -0.5 * LOG2E)), _head_pad(wk), _head_pad(wv),
                             wf_pad, wxr, wg], axis=1).astype(BF16)
    bf_pad = jnp.concatenate([b_f, b_f, b_f, jnp.zeros((LANES - 3 * FOX_HEADS,), F32)]).reshape(1, LANES)
    eq = np.zeros((LANES, AUG), np.float32)
    ek = np.zeros((LANES, AUG), np.float32)
    ec = np.zeros((LANES, AUG), np.float32)
    cq = np.zeros((1, AUG), np.float32)
    ck = np.zeros((1, AUG), np.float32)
    cv = np.zeros((1, AUG), np.float32)
    hsel = np.zeros((AUG, LANES), np.float32)
    for h in range(FOX_HEADS):
        hsel[LANES * h:LANES * (h + 1), h] = 1.0
        cv[0, extra[h]] = 1.0
        ec[h, extra[h] + 6] = 1.0
        ck[0, extra[h] + 6] = 1.0
        for part in range(3):
            eq[part * 8 + h, extra[h] + part] = 1.0
            ek[part * 8 + h, extra[h] + 3 + part] = 1.0
            cq[0, extra[h] + 3 + part] = 1.0
            ck[0, extra[h] + part] = 1.0
    dense = lambda w: jax.scipy.linalg.block_diag(*[w[i] for i in range(LRU_BLOCKS)]).astype(BF16)
    row = lambda a: a.reshape(1, -1)
    return dict(g_mix=row(g_mix), w_all=w_all, bf_pad=bf_pad, eq=jnp.asarray(eq, BF16), ek=jnp.asarray(ek, BF16),
                ec=jnp.asarray(ec, BF16), cq=jnp.asarray(cq), ck=jnp.asarray(ck), cv=jnp.asarray(cv), hsel=jnp.asarray(hsel, BF16),
                conv_w=conv_w, conv_b=row(conv_b),
                wa=dense(w_a), ba=row(b_a), wi=dense(w_i), bi=row(b_i), lam=row(lam))


def kernel(x_prompt, x_sample, mem_prompt, cache_fox_k, cache_fox_v, cache_fox_logf, state_lru_h, state_conv, cache_mem_k, cache_mem_v, g_mix, w_in, b_f, conv_w, conv_b, w_a, b_a, w_i, b_i, lam, w_out, g_cross, g_mem, w_cq, w_ck, w_cv, w_co, g_moe, w_router, b_router, w_gate, b_gate, w_up, b_up, w_down, b_down, g_final):
    nb_p, seq, _ = x_prompt.shape
    nb_s, t_s, _ = x_sample.shape
    past = cache_fox_k.shape[2]
    n_mem = mem_prompt.shape[1]
    assert nb_p == 1 and g_mix.shape[0] == 1
    n_p, n_s = seq, nb_s * t_s
    row = lambda a: a.reshape(1, -1)

    mw = _prep_mix_weights(g_mix[0], w_in[0], b_f[0], conv_w[0], conv_b[0], w_a[0], b_a[0], w_i[0], b_i[0], lam[0])
    xp = x_prompt.reshape(n_p, D_MODEL)
    xs_ = x_sample.reshape(n_s, D_MODEL)

    zero_prev = jnp.zeros((1, SUBLANES, LRU_WIDTH), F32)
    zero_h = jnp.zeros((1, 1, LRU_WIDTH), F32)
    (qa_p, ka_p, va_p, k_p, v_p, _, lf_p, f2_p, qn_p, kn_p, lru_p, hl_p, ct_p) = _mix_in(
        xp, zero_prev, zero_h, mw, tm=MIX_TILE, streaming=True)
    prev_s = jnp.pad(state_conv[0], ((0, 0), (SUBLANES - (CONV_WIDTH - 1), 0), (0, 0)))
    (qa_s, ka_s, _, k_s, v_s, vb_s, lf_s, _, _, _, lru_s, hl_s, ct_s) = _mix_in(
        xs_, prev_s, state_lru_h[0].reshape(nb_s, 1, LRU_WIDTH), mw, tm=t_s, streaming=False)

    n_past, static_ok = _fox_plan(f2_p, qn_p, kn_p, bq=FOX_BQ, tile=MIX_TILE)
    fo_p = _fox_prompt(n_past, static_ok, qa_p, ka_p, va_p, bq=FOX_BQ)
    lf_c = cache_fox_logf[0]
    clf3 = jnp.concatenate([lf_c, lf_c, lf_c, jnp.zeros((nb_s, past, LANES - 3 * FOX_HEADS), F32)], axis=-1)
    fo_s = _fox_sample(qa_s, ka_s, vb_s, cache_fox_k[0].reshape(nb_s, past, FOX_WIDTH),
                       cache_fox_v[0].reshape(nb_s, past, FOX_WIDTH), clf3, mw["ek"], mw["ck"], t=t_s)

    mk_p, mv_p = _mem_kv(mem_prompt[0], row(g_mem[0]), w_ck[0].astype(BF16), w_cv[0].astype(BF16))

    wr_pad = jnp.pad(w_router[0], ((0, 0), (0, LANES - N_EXPERTS))).astype(BF16)
    br_pad = jnp.pad(b_router[0], (0, LANES - N_EXPERTS)).reshape(1, LANES)
    pw = dict(w_out=w_out[0].astype(BF16), g_cross=row(g_cross[0]), w_cq=w_cq[0].astype(BF16),
              w_co=w_co[0].astype(BF16), g_moe=row(g_moe[0]), w_router=wr_pad, b_router=br_pad)
    cnt0 = jnp.zeros((SUBLANES, LANES), F32)
    h2_p, xn_p, idx_p, tw_p, rk_p, cnt_p = _post(
        xp, fo_p, lru_p, mk_p.reshape(1, n_mem, D_MODEL), mv_p.reshape(1, n_mem, D_MODEL), cnt0, pw,
        tm=POST_TILE, per_step_memory=False)
    h2_s, xn_s, idx_s, tw_s, rk_s, cnt = _post(
        xs_, fo_s, lru_s, cache_mem_k[0].reshape(nb_s, n_mem, D_MODEL), cache_mem_v[0].reshape(nb_s, n_mem, D_MODEL),
        cnt_p, pw, tm=t_s, per_step_memory=True)

    n = n_p + n_s
    idx = jnp.concatenate([idx_p[:, :TOP_K], idx_s[:, :TOP_K]], axis=0)
    rank = jnp.concatenate([rk_p[:, :TOP_K], rk_s[:, :TOP_K]], axis=0)
    counts = cnt[0, :N_EXPERTS].astype(I32)
    padded = (counts + MOE_BM - 1) // MOE_BM * MOE_BM
    pad_end = jnp.cumsum(padded)
    pad_start = pad_end - padded
    dest = (pad_start[idx] + rank).reshape(n * TOP_K)
    n_blocks = -(-(n * TOP_K) // MOE_BM) + N_EXPERTS
    blk_row = jnp.arange(n_blocks, dtype=I32) * MOE_BM
    block_e = jnp.minimum(jnp.sum(pad_end[None, :] <= blk_row[:, None], axis=1), N_EXPERTS - 1).astype(I32)
    n_used = (pad_end[-1] // MOE_BM).reshape(1).astype(I32)
    last_block = jnp.maximum(pad_end // MOE_BM - 1, 0).astype(I32)

    xs_sorted = _dispatch(dest, last_block, n_used, xn_p, xn_s, tm=MOE_BM, n_blocks=n_blocks)
    ew = dict(w_gate=w_gate[0], w_up=w_up[0], w_down=w_down[0],
              b_gate=b_gate[0].reshape(N_EXPERTS, 1, D_MODEL), b_up=b_up[0].reshape(N_EXPERTS, 1, D_MODEL),
              b_down=b_down[0].reshape(N_EXPERTS, 1, D_MODEL))
    ys = _experts(block_e, n_used, xs_sorted, ew, bm=MOE_BM)
    y_p, y_s = _combine(dest, h2_p, h2_s, tw_p, tw_s, row(g_final), ys, tm=COMBINE_TILE)

    shp_p = (1, nb_p, seq, FOX_HEADS, FOX_HEAD_DIM)
    shp_s = (1, nb_s, t_s, FOX_HEADS, FOX_HEAD_DIM)
    tail = slice(SUBLANES - (CONV_WIDTH - 1), SUBLANES)
    return (y_p.reshape(nb_p, seq, D_MODEL), y_s.reshape(nb_s, t_s, D_MODEL),
            k_p.reshape(shp_p), v_p.reshape(shp_p), lf_p.reshape(1, nb_p, seq, FOX_HEADS),
            hl_p.reshape(1, nb_p, LRU_WIDTH), ct_p[:, tail, :].reshape(1, nb_p, CONV_WIDTH - 1, LRU_WIDTH),
            mk_p.reshape(1, nb_p, n_mem, MEM_HEADS, MEM_HEAD_DIM), mv_p.reshape(1, nb_p, n_mem, MEM_HEADS, MEM_HEAD_DIM),
            k_s.reshape(shp_s), v_s.reshape(shp_s), lf_s.reshape(1, nb_s, t_s, FOX_HEADS),
            hl_s.reshape(1, nb_s, LRU_WIDTH), ct_s[:, tail, :].reshape(1, nb_s, CONV_WIDTH - 1, LRU_WIDTH))
```

```python
import functools
import math

import jax
import jax.numpy as jnp
import numpy as np
from jax import lax
from jax.experimental import pallas as pl
from jax.experimental.pallas import tpu as pltpu

F32 = jnp.float32
BF16 = jnp.bfloat16
I32 = jnp.int32

D_MODEL = 1024
FOX_HEADS = 8
FOX_HEAD_DIM = 64
FOX_WIDTH = FOX_HEADS * FOX_HEAD_DIM
LRU_WIDTH = D_MODEL - FOX_WIDTH
LRU_BLOCKS = 8
LRU_C = 8.0
CONV_WIDTH = 4
MEM_HEADS = 4
MEM_HEAD_DIM = D_MODEL // MEM_HEADS
N_EXPERTS = 32
TOP_K = 4
SWIGLU_LIMIT = 7.0
SWIGLU_ALPHA = 1.702
RMS_EPS = 1e-6
NEG_INF = -1e30
LOG2E = math.log2(math.e)

LANES = 128
SUBLANES = 8
CHUNKS = D_MODEL // LANES
TOKEN_TILE = (CHUNKS, SUBLANES, LANES)
AUG = LANES * FOX_HEADS
VMEM_LIMIT = 56 * 1024 * 1024

MIX_TILE = 256
POST_TILE = 512
FOX_BQ = 512
MOE_BM = 512
COMBINE_TILE = 256
FOX_SKIP_GAP = 160.0
FOX_STATIC_SHIFT_RANGE = 100.0

C_Q, C_KA, C_VA, C_F, C_XR, C_G, C_END = 0, 1024, 2048, 3072, 3200, 3712, 4224


def _dot(a, b):
    return jnp.dot(a, b, preferred_element_type=F32)


def _dot_nt(a, b):
    return lax.dot_general(a, b, (((1,), (1,)), ((), ())), preferred_element_type=F32)


def _split3(x):
    hi = x.astype(BF16)
    r = x - hi.astype(F32)
    mid = r.astype(BF16)
    lo = (r - mid.astype(F32)).astype(BF16)
    return hi, mid, lo


def _dot3(a, x):
    hi, mid, lo = _split3(x)
    return _dot(a, hi) + _dot(a, mid) + _dot(a, lo)


def _lane_split3(x):
    hi, mid, lo = _split3(x)
    lane = lax.broadcasted_iota(I32, x.shape, 1)
    return jnp.where(lane < 8, hi, jnp.where(lane < 16, mid, lo))


def _softplus(x):
    return jnp.maximum(x, 0.0) + jnp.log1p(jnp.exp(-jnp.abs(x)))


def _rms(x, g):
    return x * lax.rsqrt(jnp.mean(x * x, axis=-1, keepdims=True) + RMS_EPS) * g


def _tri(n, kind):
    r = lax.broadcasted_iota(I32, (n, n), 0)
    c = lax.broadcasted_iota(I32, (n, n), 1)
    m = {"le": c <= r, "lt": c < r, "gt": c > r}[kind]
    return jnp.where(m, 1.0, 0.0).astype(BF16)


def _head_sumsq_max(zb, hsel):
    sq = zb.astype(F32)
    return jnp.max(_dot((sq * sq).astype(BF16), hsel), axis=0, keepdims=True)


def _mix_in_kernel(x_ref, g_ref, w_ref, bfp_ref, eq_ref, ek_ref, ec_ref, cq_ref, ck_ref, cv_ref, hs_ref,
                   cw_ref, cb_ref, wa_ref, ba_ref, wi_ref, bi_ref, lam_ref, cprev_ref, h0_ref,
                   qa_ref, ka_ref, va_ref, ko_ref, vo_ref, vb_ref, lf_ref, f2_ref, qn_ref, kn_ref,
                   lru_ref, hl_ref, ct_ref,
                   fcar, hcar, xp_ref, sa_ref, sb_ref, kbuf, vbuf, ksem, *, tm, pad, streaming):
    step = pl.program_id(0)
    first = step == 0
    slot = step % 2

    def kv_copies(s, row0):
        rows = pl.ds(row0, tm)
        copies = []
        for h in range(FOX_HEADS):
            copies.append(pltpu.make_async_copy(kbuf.at[s, h], ko_ref.at[rows, h, :], ksem.at[s]))
            copies.append(pltpu.make_async_copy(vbuf.at[s, h], vo_ref.at[rows, h, :], ksem.at[s]))
        return copies

    @pl.when(step >= 2)
    def _():
        for c in kv_copies(slot, 0):
            c.wait()

    if streaming:
        @pl.when(first)
        def _():
            fcar[...] = jnp.zeros_like(fcar)
            hcar[...] = jnp.zeros_like(hcar)
            xp_ref[0:SUBLANES, :] = jnp.zeros((SUBLANES, LRU_WIDTH), F32)
    else:
        fcar[...] = jnp.zeros_like(fcar)
        hcar[...] = jnp.broadcast_to(h0_ref[0], hcar.shape)
        xp_ref[0:SUBLANES, :] = cprev_ref[0]

    x = x_ref[...]
    xn = _rms(x, g_ref[...]).astype(BF16)
    z = _dot(xn, w_ref[...])

    main, _ = _aug_layout()
    for h in range(FOX_HEADS):
        kbuf[slot, h] = z[:, C_KA + main[h]:C_KA + main[h] + FOX_HEAD_DIM]
        vbuf[slot, h] = z[:, C_VA + main[h]:C_VA + main[h] + FOX_HEAD_DIM]
    for c in kv_copies(slot, pl.multiple_of(step * tm, tm)):
        c.start()
    for p in range(FOX_HEADS // 2):
        a, b = 2 * p * LANES, (2 * p + 1) * LANES
        vv = z[:, C_VA + a:C_VA + b] + z[:, C_VA + b:C_VA + b + LANES]
        vb_ref[:, p * LANES:(p + 1) * LANES] = vv.astype(BF16)
    va_ref[...] = (z[:, C_VA:C_F] + cv_ref[...]).astype(BF16)

    lane = lax.broadcasted_iota(I32, (tm, LANES), 1)
    lf = jnp.where(lane < 3 * FOX_HEADS, -_softplus(-(z[:, C_F:C_XR] + bfp_ref[...])), 0.0)
    lf_ref[...] = lf[:, :FOX_HEADS]
    cum = _dot3(_tri(tm, "le"), lf) + fcar[0:1, :]
    fcar[...] = jnp.broadcast_to(cum[tm - 1:tm, :], fcar.shape)
    cum2 = cum * LOG2E
    f2_ref[...] = cum2[:, :FOX_HEADS]
    fs = _lane_split3(cum2)
    qb = z[:, C_Q:C_KA].astype(BF16)
    kb = z[:, C_KA:C_VA].astype(BF16)
    diag = _dot((qb.astype(F32) * kb.astype(F32)).astype(BF16), hs_ref[...]).astype(BF16)
    qa_ref[...] = qb + (_dot(fs, eq_ref[...]) + cq_ref[...] - _dot(diag, ec_ref[...])).astype(BF16)
    ka_ref[...] = kb + (ck_ref[...] - _dot(fs, ek_ref[...])).astype(BF16)
    qn_ref[...] = jnp.broadcast_to(_head_sumsq_max(qb, hs_ref[...]), qn_ref.shape)
    kn_ref[...] = jnp.broadcast_to(_head_sumsq_max(kb, hs_ref[...]), kn_ref.shape)

    xr = z[:, C_XR:C_G]
    xp_ref[SUBLANES:SUBLANES + tm, :] = xr
    xc = cb_ref[...] + xr * cw_ref[CONV_WIDTH - 1:CONV_WIDTH, :]
    for j in range(CONV_WIDTH - 1):
        sh = CONV_WIDTH - 1 - j
        xc = xc + xp_ref[SUBLANES - sh:SUBLANES - sh + tm, :] * cw_ref[j:j + 1, :]
    tail = xp_ref[tm:tm + SUBLANES, :]
    ct_ref[0] = tail
    xp_ref[0:SUBLANES, :] = tail

    xcb = xc.astype(BF16)
    r = jax.nn.sigmoid(_dot(xcb, wa_ref[...]) + ba_ref[...])
    ig = jax.nn.sigmoid(_dot(xcb, wi_ref[...]) + bi_ref[...])
    log_a = (-LRU_C) * r * _softplus(-lam_ref[...])
    a = jnp.exp(log_a)
    mult = jnp.sqrt(-jnp.tanh(log_a) * (a * a + 1.0))
    if streaming:
        row = lax.broadcasted_iota(I32, (tm, LRU_WIDTH), 0)
        mult = jnp.where(jnp.logical_and(row == 0, first), 1.0, mult)
    b = mult * ig * xc

    sa_ref[0:pad, :] = jnp.ones((pad, LRU_WIDTH), F32)
    sb_ref[0:pad, :] = jnp.zeros((pad, LRU_WIDTH), F32)
    d = 1
    while d < tm:
        sa_ref[pad:pad + tm, :] = a
        sb_ref[pad:pad + tm, :] = b
        b = a * sb_ref[pad - d:pad - d + tm, :] + b
        a = a * sa_ref[pad - d:pad - d + tm, :]
        d *= 2
    h = a * hcar[0:1, :] + b
    hlast = h[tm - 1:tm, :]
    hcar[...] = jnp.broadcast_to(hlast, hcar.shape)
    hl_ref[0] = hlast
    lru_ref[...] = (h * jax.nn.gelu(z[:, C_G:C_END])).astype(BF16)

    last = pl.num_programs(0) - 1

    @pl.when(jnp.logical_and(step == last, step >= 1))
    def _():
        for c in kv_copies(1 - slot, 0):
            c.wait()

    @pl.when(step == last)
    def _():
        for c in kv_copies(slot, 0):
            c.wait()


def _mix_in(x, cprev, h0, wts, *, tm, streaming):
    n = x.shape[0]
    steps = n // tm
    nseg = 1 if streaming else steps
    pad = max(tm // 2, SUBLANES)
    seg = (lambda i: (0, 0, 0)) if streaming else (lambda i: (i, 0, 0))
    full = lambda a: pl.BlockSpec(a.shape, lambda i: (0,) * a.ndim)
    rows = lambda w: pl.BlockSpec((tm, w), lambda i: (i, 0))
    names = ("g_mix", "w_all", "bf_pad", "eq", "ek", "ec", "cq", "ck", "cv", "hsel", "conv_w", "conv_b",
             "wa", "ba", "wi", "bi", "lam")
    ws = [wts[k] for k in names]
    sds = jax.ShapeDtypeStruct
    out_shape = (
        sds((n, AUG), BF16), sds((n, AUG), BF16), sds((n, AUG), BF16),
        sds((n, FOX_HEADS, FOX_HEAD_DIM), F32), sds((n, FOX_HEADS, FOX_HEAD_DIM), F32), sds((n, FOX_WIDTH), BF16),
        sds((n, FOX_HEADS), F32), sds((n, FOX_HEADS), F32),
        sds((steps * SUBLANES, LANES), F32), sds((steps * SUBLANES, LANES), F32),
        sds((n, LRU_WIDTH), BF16),
        sds((nseg, 1, LRU_WIDTH), F32), sds((nseg, SUBLANES, LRU_WIDTH), F32),
    )
    out_specs = (
        rows(AUG), rows(AUG), rows(AUG), pl.BlockSpec(memory_space=pl.ANY), pl.BlockSpec(memory_space=pl.ANY),
        rows(FOX_WIDTH), rows(FOX_HEADS), rows(FOX_HEADS),
        pl.BlockSpec((SUBLANES, LANES), lambda i: (i, 0)), pl.BlockSpec((SUBLANES, LANES), lambda i: (i, 0)),
        rows(LRU_WIDTH),
        pl.BlockSpec((1, 1, LRU_WIDTH), seg), pl.BlockSpec((1, SUBLANES, LRU_WIDTH), seg),
    )
    in_specs = [rows(D_MODEL)] + [full(w) for w in ws] + [
        pl.BlockSpec((1, SUBLANES, LRU_WIDTH), seg), pl.BlockSpec((1, 1, LRU_WIDTH), seg)]
    return pl.pallas_call(
        functools.partial(_mix_in_kernel, tm=tm, pad=pad, streaming=streaming),
        grid=(steps,), in_specs=in_specs, out_specs=out_specs, out_shape=out_shape,
        scratch_shapes=[
            pltpu.VMEM((SUBLANES, LANES), F32), pltpu.VMEM((SUBLANES, LRU_WIDTH), F32),
            pltpu.VMEM((tm + SUBLANES, LRU_WIDTH), F32),
            pltpu.VMEM((pad + tm, LRU_WIDTH), F32), pltpu.VMEM((pad + tm, LRU_WIDTH), F32),
            pltpu.VMEM((2, FOX_HEADS, tm, FOX_HEAD_DIM), F32), pltpu.VMEM((2, FOX_HEADS, tm, FOX_HEAD_DIM), F32),
            pltpu.SemaphoreType.DMA((2,))],
        compiler_params=pltpu.CompilerParams(dimension_semantics=("arbitrary",), vmem_limit_bytes=VMEM_LIMIT),
        name="mix_in_stream" if streaming else "mix_in_segments",
    )(x, *ws, cprev, h0)


def _fox_prompt_kernel(nb_ref, st_ref, q_ref, k_ref, v_ref, o_ref, m_ref, acc_ref, *, bq):
    qi = pl.program_id(1)
    plan = pl.program_id(0) * pl.num_programs(1) + qi
    n_past = nb_ref[plan]
    cols = [slice(LANES * c, LANES * (c + 1)) for c in range(2)]
    qs = [q_ref[:, cols[c]] for c in range(2)]
    acc_ref[...] = jnp.zeros(acc_ref.shape, F32)

    def scores(c, start, causal):
        s = _dot_nt(qs[c], k_ref[pl.ds(start, bq), cols[c]])
        if causal:
            r = lax.broadcasted_iota(I32, (bq, bq), 0)
            cc = lax.broadcasted_iota(I32, (bq, bq), 1)
            s = jnp.where(cc <= r, s, NEG_INF)
        return s

    def static_block(start, causal):
        for c in range(2):
            p = jnp.exp2(scores(c, start, causal)).astype(BF16)
            acc_ref[c] += _dot(p, v_ref[pl.ds(start, bq), cols[c]])

    def online_block(start, causal):
        for c in range(2):
            s = scores(c, start, causal)
            m_old = m_ref[c]
            m_new = jnp.maximum(m_old, jnp.max(s, axis=1, keepdims=True))
            p = jnp.exp2(s - m_new).astype(BF16)
            acc_ref[c] = jnp.exp2(m_old - m_new) * acc_ref[c] + _dot(p, v_ref[pl.ds(start, bq), cols[c]])
            m_ref[c] = m_new

    def run(block):
        block(pl.multiple_of(qi * bq, bq), True)

        def body(t, carry):
            block(pl.multiple_of((qi - 1 - t) * bq, bq), False)
            return carry

        lax.fori_loop(0, n_past, body, 0)

    @pl.when(st_ref[plan] == 1)
    def _():
        run(static_block)

    @pl.when(st_ref[plan] != 1)
    def _():
        m_ref[...] = jnp.full(m_ref.shape, NEG_INF, F32)
        run(online_block)

    a0, a1 = acc_ref[0], acc_ref[1]
    o0 = a0 / a0[:, FOX_HEAD_DIM:FOX_HEAD_DIM + 1]
    o1 = a1 / a1[:, 0:1]
    lane = lax.broadcasted_iota(I32, (bq, LANES), 1)
    o_ref[...] = jnp.where(lane < FOX_HEAD_DIM, o0, o1).astype(BF16)


def _fox_prompt(n_past, static_ok, qa, ka, va, *, bq):
    s = qa.shape[0]
    pairs = FOX_HEADS // 2
    return pl.pallas_call(
        functools.partial(_fox_prompt_kernel, bq=bq),
        grid_spec=pltpu.PrefetchScalarGridSpec(
            num_scalar_prefetch=2, grid=(pairs, s // bq),
            in_specs=[pl.BlockSpec((bq, 2 * LANES), lambda p, i, nb, st: (i, p)),
                      pl.BlockSpec((s, 2 * LANES), lambda p, i, nb, st: (0, p)),
                      pl.BlockSpec((s, 2 * LANES), lambda p, i, nb, st: (0, p))],
            out_specs=pl.BlockSpec((bq, LANES), lambda p, i, nb, st: (i, p)),
            scratch_shapes=[pltpu.VMEM((2, bq, 1), F32), pltpu.VMEM((2, bq, LANES), F32)]),
        out_shape=jax.ShapeDtypeStruct((s, FOX_WIDTH), BF16),
        compiler_params=pltpu.CompilerParams(dimension_semantics=("arbitrary", "arbitrary"),
                                             vmem_limit_bytes=VMEM_LIMIT),
        name="fox_prompt",
    )(n_past, static_ok, qa, ka, va)


def _fox_plan(f2, qn, kn, *, bq, tile):
    s = f2.shape[0]
    nq = s // bq
    per = lambda a: a.reshape(s // tile, SUBLANES, LANES)[:, 0, :FOX_HEADS]
    qnorm = jnp.sqrt(jnp.max(per(qn).reshape(nq, bq // tile, FOX_HEADS), axis=1))
    knorm = jnp.sqrt(jnp.max(per(kn), axis=0))
    spread = 2.0 * 1.02 * qnorm * knorm[None, :]
    thr = spread + FOX_SKIP_GAP
    f_first = f2[0::bq]
    f_last = f2[bq - 1::bq]
    gap = f_last[None, :, :] - f_first[:, None, :]
    before = (jnp.arange(nq)[None, :] < jnp.arange(nq)[:, None])[:, :, None]
    need = jnp.sum(jnp.logical_and(before, gap < thr[:, None, :]), axis=1)
    pair = lambda a: a.reshape(nq, FOX_HEADS // 2, 2)
    n_past = jnp.max(pair(need), axis=2).T.reshape(-1).astype(I32)
    static_ok = jnp.all(pair(spread) <= FOX_STATIC_SHIFT_RANGE, axis=2).T.reshape(-1).astype(I32)
    return n_past, static_ok


def _fox_sample_kernel(q_ref, kn_ref, vn_ref, ck_ref, cv_ref, clf_ref, ek_ref, ckc_ref, o_ref, g_ref,
                       *, t, past, chunk):
    car = jnp.zeros((1, LANES), F32)
    upper = _tri(chunk, "gt")
    for ci in reversed(range(past // chunk)):
        lf = clf_ref[0, ci * chunk:(ci + 1) * chunk, :]
        g_ref[ci * chunk:(ci + 1) * chunk, :] = _dot3(upper, lf) + car
        car = car + jnp.sum(lf, axis=0, keepdims=True)
    gs = _lane_split3(g_ref[...] * LOG2E)

    lane = lax.broadcasted_iota(I32, (past, LANES), 1)
    r = lax.broadcasted_iota(I32, (t, t), 0)
    cc = lax.broadcasted_iota(I32, (t, t), 1)
    olane = lax.broadcasted_iota(I32, (t, LANES), 1)
    for p in range(FOX_HEADS // 2):
        pc = slice(LANES * p, LANES * (p + 1))
        kpair = ck_ref[0, :, pc]
        vpast = cv_ref[0, :, pc].astype(BF16)
        vnew = vn_ref[:, pc]
        outs = []
        for c in range(2):
            h = 2 * p + c
            hc = slice(LANES * h, LANES * (h + 1))
            main = (lane < FOX_HEAD_DIM) if c == 0 else (lane >= FOX_HEAD_DIM)
            extras = _dot(gs, ek_ref[:, hc]) + ckc_ref[:, hc]
            kpast = (jnp.where(main, kpair, 0.0) + extras).astype(BF16)
            q = q_ref[:, hc]
            sp = _dot_nt(q, kpast)
            sn = jnp.where(cc <= r, _dot_nt(q, kn_ref[:, hc]), NEG_INF)
            m = jnp.maximum(jnp.max(sp, axis=1, keepdims=True), jnp.max(sn, axis=1, keepdims=True))
            pp = jnp.exp2(sp - m)
            pn = jnp.exp2(sn - m)
            l = jnp.sum(pp, axis=1, keepdims=True) + jnp.sum(pn, axis=1, keepdims=True)
            outs.append((_dot(pp.astype(BF16), vpast) + _dot(pn.astype(BF16), vnew)) / l)
        o_ref[:, pc] = jnp.where(olane < FOX_HEAD_DIM, outs[0], outs[1]).astype(BF16)


def _fox_sample(qa, ka, vb, cache_k, cache_v, cache_lf3, ek, ck, *, t):
    nb, past = cache_k.shape[0], cache_k.shape[1]
    full = lambda a: pl.BlockSpec(a.shape, lambda b: (0,) * a.ndim)
    return pl.pallas_call(
        functools.partial(_fox_sample_kernel, t=t, past=past, chunk=256),
        grid=(nb,),
        in_specs=[pl.BlockSpec((t, AUG), lambda b: (b, 0)), pl.BlockSpec((t, AUG), lambda b: (b, 0)),
                  pl.BlockSpec((t, FOX_WIDTH), lambda b: (b, 0)),
                  pl.BlockSpec((1, past, FOX_WIDTH), lambda b: (b, 0, 0)),
                  pl.BlockSpec((1, past, FOX_WIDTH), lambda b: (b, 0, 0)),
                  pl.BlockSpec((1, past, LANES), lambda b: (b, 0, 0)),
                  full(ek), full(ck)],
        out_specs=pl.BlockSpec((t, FOX_WIDTH), lambda b: (b, 0)),
        out_shape=jax.ShapeDtypeStruct((nb * t, FOX_WIDTH), BF16),
        scratch_shapes=[pltpu.VMEM((past, LANES), F32)],
        compiler_params=pltpu.CompilerParams(dimension_semantics=("arbitrary",), vmem_limit_bytes=VMEM_LIMIT),
        name="fox_sample",
    )(qa, ka, vb, cache_k, cache_v, cache_lf3, ek, ck)


def _mem_kv_kernel(m_ref, g_ref, wk_ref, wv_ref, k_ref, v_ref):
    mn = _rms(m_ref[...], g_ref[...]).astype(BF16)
    k_ref[...] = _dot(mn, wk_ref[...])
    v_ref[...] = _dot(mn, wv_ref[...])


def _mem_kv(mem, g, wk, wv):
    n = mem.shape[0]
    return pl.pallas_call(
        _mem_kv_kernel,
        out_shape=(jax.ShapeDtypeStruct((n, D_MODEL), F32), jax.ShapeDtypeStruct((n, D_MODEL), F32)),
        compiler_params=pltpu.CompilerParams(vmem_limit_bytes=VMEM_LIMIT),
        name="mem_kv",
    )(mem, g, wk, wv)


def _post_kernel(h_ref, fo_ref, lru_ref, wo_ref, gc_ref, wcq_ref, mk_ref, mv_ref, wco_ref,
                 gm_ref, wr_ref, br_ref, cnt_in_ref,
                 h2_ref, xn_ref, idx_ref, tw_ref, rank_ref, cnt_ref, car_ref, *, tm):
    @pl.when(pl.program_id(0) == 0)
    def _():
        car_ref[...] = cnt_in_ref[...]

    h1 = h_ref[...] + _dot(fo_ref[...], wo_ref[0:FOX_WIDTH, :]) + _dot(lru_ref[...], wo_ref[FOX_WIDTH:D_MODEL, :])

    q = _dot(_rms(h1, gc_ref[...]).astype(BF16), wcq_ref[...])
    heads = []
    for hd in range(MEM_HEADS):
        hc = slice(MEM_HEAD_DIM * hd, MEM_HEAD_DIM * (hd + 1))
        s = _dot_nt(q[:, hc].astype(BF16), mk_ref[0, :, hc].astype(BF16)) * (MEM_HEAD_DIM ** -0.5)
        p = jnp.exp(s - jnp.max(s, axis=1, keepdims=True))
        o = _dot(p.astype(BF16), mv_ref[0, :, hc].astype(BF16)) / jnp.sum(p, axis=1, keepdims=True)
        heads.append(o.astype(BF16))
    h2 = h1 + _dot(jnp.concatenate(heads, axis=1), wco_ref[...])
    h2_ref[...] = h2

    xn = _rms(h2, gm_ref[...])
    for c in range(D_MODEL // LANES):
        xn_ref[:, c] = xn[:, c * LANES:(c + 1) * LANES].reshape(tm // SUBLANES, SUBLANES, LANES)

    lane = lax.broadcasted_iota(I32, (tm, LANES), 1).astype(F32)
    logits = jnp.where(lane < N_EXPERTS, _dot(xn.astype(BF16), wr_ref[...]) + br_ref[...], -jnp.inf)
    vals, idxs = [], []
    for _ in range(TOP_K):
        mx = jnp.max(logits, axis=1, keepdims=True)
        ix = jnp.min(jnp.where(logits == mx, lane, float(LANES)), axis=1, keepdims=True)
        vals.append(mx)
        idxs.append(ix)
        logits = jnp.where(lane == ix, -jnp.inf, logits)
    es = [jnp.exp(v - vals[0]) for v in vals]
    den = es[0] + es[1] + es[2] + es[3]

    onehot = jnp.zeros((tm, LANES), F32)
    for ix in idxs:
        onehot = onehot + jnp.where(lane == ix, 1.0, 0.0)
    before = _dot(_tri(tm, "lt"), onehot.astype(BF16)) + car_ref[0:1, :]
    car = car_ref[0:1, :] + jnp.sum(onehot, axis=0, keepdims=True)
    car_ref[...] = jnp.broadcast_to(car, car_ref.shape)
    cnt_ref[...] = jnp.broadcast_to(car, cnt_ref.shape)

    idx_o = jnp.zeros((tm, LANES), F32)
    tw_o = jnp.zeros((tm, LANES), F32)
    rk_o = jnp.zeros((tm, LANES), F32)
    for j in range(TOP_K):
        rk = jnp.sum(jnp.where(lane == idxs[j], before, 0.0), axis=1, keepdims=True)
        idx_o = jnp.where(lane == j, idxs[j], idx_o)
        tw_o = jnp.where(lane == j, es[j] / den, tw_o)
        rk_o = jnp.where(lane == j, rk, rk_o)
    idx_ref[...] = idx_o.astype(I32)
    tw_ref[...] = tw_o
    rank_ref[...] = rk_o.astype(I32)


def _post(h, fo, lru, mk, mv, cnt_in, wts, *, tm, per_step_memory):
    n = h.shape[0]
    nm = mk.shape[1]
    full = lambda a: pl.BlockSpec(a.shape, lambda i: (0,) * a.ndim)
    rows = lambda w: pl.BlockSpec((tm, w), lambda i: (i, 0))
    mem = pl.BlockSpec((1, nm, D_MODEL), (lambda i: (i, 0, 0)) if per_step_memory else (lambda i: (0, 0, 0)))
    w = [wts[k] for k in ("w_out", "g_cross", "w_cq")]
    w2 = [wts[k] for k in ("w_co", "g_moe", "w_router", "b_router")]
    return pl.pallas_call(
        functools.partial(_post_kernel, tm=tm),
        grid=(n // tm,),
        in_specs=[rows(D_MODEL), rows(FOX_WIDTH), rows(LRU_WIDTH)] + [full(a) for a in w] + [mem, mem]
                 + [full(a) for a in w2] + [full(cnt_in)],
        out_specs=(rows(D_MODEL), pl.BlockSpec((tm // SUBLANES,) + TOKEN_TILE, lambda i: (i, 0, 0, 0)),
                   rows(LANES), rows(LANES), rows(LANES),
                   pl.BlockSpec((SUBLANES, LANES), lambda i: (0, 0))),
        out_shape=(jax.ShapeDtypeStruct((n, D_MODEL), F32), jax.ShapeDtypeStruct((n // SUBLANES,) + TOKEN_TILE, F32),
                   jax.ShapeDtypeStruct((n, LANES), I32), jax.ShapeDtypeStruct((n, LANES), F32),
                   jax.ShapeDtypeStruct((n, LANES), I32), jax.ShapeDtypeStruct((SUBLANES, LANES), F32)),
        scratch_shapes=[pltpu.VMEM((SUBLANES, LANES), F32)],
        compiler_params=pltpu.CompilerParams(dimension_semantics=("arbitrary",), vmem_limit_bytes=VMEM_LIMIT),
        name="post_shared_mem" if not per_step_memory else "post_batch_mem",
    )(h, fo, lru, *w, mk, mv, *w2, cnt_in)


STAGES = 3


def _dispatch_kernel(dest_ref, last_ref, nu_ref, xp_ref, xs_ref, out_ref, zero_ref, stage, sems, gsem, zsem,
                     *, tm, steps_p, n_blocks):
    i = pl.program_id(0)

    def zero_copy(b):
        return pltpu.make_async_copy(zero_ref, out_ref.at[pl.ds(pl.multiple_of(b * tm, tm), tm)], zsem)

    @pl.when(i == 0)
    def _():
        zero_ref[...] = jnp.zeros(zero_ref.shape, F32)
        n_tail = n_blocks - nu_ref[0]

        def start_e(e, c):
            zero_copy(last_ref[e]).start()
            return c

        def start_t(b, c):
            zero_copy(nu_ref[0] + b).start()
            return c

        def wait_one(b, c):
            zero_copy(0).wait()
            return c

        lax.fori_loop(0, N_EXPERTS, start_e, 0)
        lax.fori_loop(0, n_tail, start_t, 0)
        lax.fori_loop(0, N_EXPERTS + n_tail, wait_one, 0)

    groups = tm // SUBLANES
    steps = pl.num_programs(0)

    def stage_copy(src_ref, tile, slot):
        return pltpu.make_async_copy(src_ref.at[pl.ds(tile * groups, groups)], stage.at[slot], gsem.at[slot])

    def start_stage(tile, slot):
        @pl.when(tile < steps_p)
        def _():
            stage_copy(xp_ref, tile, slot).start()

        @pl.when(tile >= steps_p)
        def _():
            stage_copy(xs_ref, tile - steps_p, slot).start()

    @pl.when(i == 0)
    def _():
        start_stage(0, 0)

    @pl.when(i + 1 < steps)
    def _():
        start_stage(i + 1, (i + 1) % STAGES)

    slot = i % STAGES
    stage_copy(xp_ref, 0, slot).wait()

    base = i * (tm * TOP_K)

    def start(g, c):
        for u in range(SUBLANES):
            for j in range(TOP_K):
                d = dest_ref[base + (g * SUBLANES + u) * TOP_K + j]
                pltpu.make_async_copy(stage.at[slot, g, :, u, :], out_ref.at[d],
                                      sems.at[i % 2]).start(priority=j % 2)
        return c

    lax.fori_loop(0, groups, start, 0)

    def drain(parity):
        for _ in range(TOP_K):
            pltpu.make_async_copy(out_ref.at[pl.ds(0, tm)], out_ref.at[pl.ds(0, tm)], sems.at[parity]).wait()

    @pl.when(i > 0)
    def _():
        drain((i - 1) % 2)

    @pl.when(i == steps - 1)
    def _():
        drain(i % 2)


def _dispatch(dest_flat, last_block, n_used, xn_p, xn_s, *, tm, n_blocks):
    steps_p, steps_s = xn_p.shape[0] * SUBLANES // tm, xn_s.shape[0] * SUBLANES // tm
    return pl.pallas_call(
        functools.partial(_dispatch_kernel, tm=tm, steps_p=steps_p, n_blocks=n_blocks),
        grid_spec=pltpu.PrefetchScalarGridSpec(
            num_scalar_prefetch=3, grid=(steps_p + steps_s,),
            in_specs=[pl.BlockSpec(memory_space=pl.ANY), pl.BlockSpec(memory_space=pl.ANY)],
            out_specs=pl.BlockSpec(memory_space=pl.ANY),
            scratch_shapes=[pltpu.VMEM((tm, CHUNKS, LANES), F32),
                            pltpu.VMEM((STAGES, tm // SUBLANES) + TOKEN_TILE, F32),
                            pltpu.SemaphoreType.DMA((2,)), pltpu.SemaphoreType.DMA((STAGES,)),
                            pltpu.SemaphoreType.DMA(())]),
        out_shape=jax.ShapeDtypeStruct((n_blocks * tm, CHUNKS, LANES), F32),
        compiler_params=pltpu.CompilerParams(dimension_semantics=("arbitrary",), vmem_limit_bytes=VMEM_LIMIT),
        name="moe_dispatch",
    )(dest_flat, last_block, n_used, xn_p, xn_s)


def _rows_from_tiles(view, rows):
    return jnp.concatenate([view[:, c].reshape(rows, LANES) for c in range(CHUNKS)], axis=1)


def _rows_to_tiles(view, val, rows):
    for c in range(CHUNKS):
        view[:, c] = val[:, c * LANES:(c + 1) * LANES].reshape(rows // SUBLANES, SUBLANES, LANES)


def _expert_kernel(be_ref, nu_ref, xs_ref, wg_ref, bg_ref, wu_ref, bu_ref, wd_ref, bd_ref, ys_ref,
                   wgb, wub, wdb, xbuf, ybuf, xsem, ysem, *, bm):
    i = pl.program_id(0)
    steps = pl.num_programs(0)
    groups = bm // SUBLANES
    slot = i % 2
    prev = be_ref[jnp.maximum(i - 1, 0)]
    fresh = jnp.logical_or(i == 0, be_ref[i] != prev)
    live = i < nu_ref[0]

    def in_copies(blk, s):
        g0 = pl.multiple_of(blk * groups, groups)
        return [pltpu.make_async_copy(xs_ref.at[pl.ds(g0, groups), u], xbuf.at[s, :, :, u, :], xsem.at[s])
                for u in range(SUBLANES)]

    def out_copies(blk, s):
        g0 = pl.multiple_of(blk * groups, groups)
        return [pltpu.make_async_copy(ybuf.at[s, :, :, u, :], ys_ref.at[pl.ds(g0, groups), u], ysem.at[s])
                for u in range(SUBLANES)]

    @pl.when(i == 0)
    def _():
        for c in in_copies(0, 0):
            c.start()

    @pl.when(i + 1 < nu_ref[0])
    def _():
        for c in in_copies(i + 1, (i + 1) % 2):
            c.start()

    @pl.when(i >= 2)
    def _():
        for c in out_copies(0, slot):
            c.wait()

    @pl.when(jnp.logical_and(live, fresh))
    def _():
        wgb[...] = wg_ref[0].astype(BF16)
        wub[...] = wu_ref[0].astype(BF16)
        wdb[...] = wd_ref[0].astype(BF16)

    @pl.when(live)
    def _():
        for c in in_copies(0, slot):
            c.wait()
        x = _rows_from_tiles(xbuf.at[slot], bm).astype(BF16)
        g = jnp.minimum(_dot(x, wgb[...]) + bg_ref[0], SWIGLU_LIMIT)
        u = jnp.clip(_dot(x, wub[...]) + bu_ref[0], -SWIGLU_LIMIT, SWIGLU_LIMIT)
        hdn = g * jax.nn.sigmoid(SWIGLU_ALPHA * g) * (u + 1.0)
        _rows_to_tiles(ybuf.at[slot], _dot(hdn.astype(BF16), wdb[...]) + bd_ref[0], bm)

    @pl.when(jnp.logical_not(live))
    def _():
        ybuf[slot] = jnp.zeros(ybuf.shape[1:], F32)

    for c in out_copies(i, slot):
        c.start()

    @pl.when(jnp.logical_and(i == steps - 1, i >= 1))
    def _():
        for c in out_copies(0, 1 - slot):
            c.wait()

    @pl.when(i == steps - 1)
    def _():
        for c in out_copies(0, slot):
            c.wait()


def _experts(block_e, n_used, xs, wts, *, bm):
    p = xs.shape[0]
    grouped = (p // SUBLANES, SUBLANES, CHUNKS, LANES)
    wsel = lambda i, be, nu: (be[i], 0, 0)
    wspec = pl.BlockSpec((1, D_MODEL, D_MODEL), wsel)
    bspec = pl.BlockSpec((1, 1, D_MODEL), wsel)
    tiles = (2, bm // SUBLANES) + TOKEN_TILE
    ys = pl.pallas_call(
        functools.partial(_expert_kernel, bm=bm),
        grid_spec=pltpu.PrefetchScalarGridSpec(
            num_scalar_prefetch=2, grid=(p // bm,),
            in_specs=[pl.BlockSpec(memory_space=pl.ANY), wspec, bspec, wspec, bspec, wspec, bspec],
            out_specs=pl.BlockSpec(memory_space=pl.ANY),
            scratch_shapes=[pltpu.VMEM((D_MODEL, D_MODEL), BF16)] * 3 + [
                pltpu.VMEM(tiles, F32), pltpu.VMEM(tiles, F32),
                pltpu.SemaphoreType.DMA((2,)), pltpu.SemaphoreType.DMA((2,))]),
        out_shape=jax.ShapeDtypeStruct(grouped, F32),
        compiler_params=pltpu.CompilerParams(dimension_semantics=("arbitrary",), vmem_limit_bytes=VMEM_LIMIT),
        name="moe_experts",
    )(block_e, n_used, xs.reshape(grouped), wts["w_gate"], wts["b_gate"], wts["w_up"], wts["b_up"],
      wts["w_down"], wts["b_down"])
    return ys.reshape(p, CHUNKS, LANES)


def _combine_kernel(dest_ref, hp_ref, hs_ref, twp_ref, tws_ref, gf_ref, ys_ref, yp_ref, ysm_ref, buf, sems,
                    *, tm, steps_p):
    i = pl.program_id(0)
    slot = i % 2

    def issue(tile, into):
        base = tile * (tm * TOP_K)

        def start(g, c):
            for u in range(SUBLANES):
                for j in range(TOP_K):
                    d = dest_ref[base + (g * SUBLANES + u) * TOP_K + j]
                    pltpu.make_async_copy(ys_ref.at[d], buf.at[into, j, g, :, u, :],
                                          sems.at[into]).start(priority=j % 2)
            return c

        lax.fori_loop(0, tm // SUBLANES, start, 0)

    @pl.when(i == 0)
    def _():
        issue(0, 0)

    @pl.when(i + 1 < pl.num_programs(0))
    def _():
        issue(i + 1, (i + 1) % 2)

    for j in range(TOP_K):
        pltpu.make_async_copy(ys_ref.at[pl.ds(0, tm)], ys_ref.at[pl.ds(0, tm)], sems.at[slot]).wait()

    def finish(h_ref, tw_ref, y_ref):
        tw = tw_ref[...]
        y = h_ref[...]
        for j in range(TOP_K):
            y = y + _rows_from_tiles(buf.at[slot, j], tm) * tw[:, j:j + 1]
        y_ref[...] = _rms(y, gf_ref[...])

    @pl.when(i < steps_p)
    def _():
        finish(hp_ref, twp_ref, yp_ref)

    @pl.when(i >= steps_p)
    def _():
        finish(hs_ref, tws_ref, ysm_ref)


def _combine(dest_flat, h2_p, h2_s, tw_p, tw_s, g_final, ys, *, tm):
    steps_p, steps_s = h2_p.shape[0] // tm, h2_s.shape[0] // tm
    pblk = lambda w: pl.BlockSpec((tm, w), lambda i, d: (jnp.minimum(i, steps_p - 1), 0))
    sblk = lambda w: pl.BlockSpec((tm, w), lambda i, d: (jnp.maximum(i - steps_p, 0), 0))
    return pl.pallas_call(
        functools.partial(_combine_kernel, tm=tm, steps_p=steps_p),
        grid_spec=pltpu.PrefetchScalarGridSpec(
            num_scalar_prefetch=1, grid=(steps_p + steps_s,),
            in_specs=[pblk(D_MODEL), sblk(D_MODEL), pblk(LANES), sblk(LANES),
                      pl.BlockSpec((1, D_MODEL), lambda i, d: (0, 0)),
                      pl.BlockSpec(memory_space=pl.ANY)],
            out_specs=(pblk(D_MODEL), sblk(D_MODEL)),
            scratch_shapes=[pltpu.VMEM((2, TOP_K, tm // SUBLANES) + TOKEN_TILE, F32),
                            pltpu.SemaphoreType.DMA((2,))]),
        out_shape=(jax.ShapeDtypeStruct(h2_p.shape, F32), jax.ShapeDtypeStruct(h2_s.shape, F32)),
        compiler_params=pltpu.CompilerParams(dimension_semantics=("arbitrary",), vmem_limit_bytes=VMEM_LIMIT),
        name="moe_combine",
    )(dest_flat, h2_p, h2_s, tw_p, tw_s, g_final, ys)


def _aug_layout():
    main, extra = [], []
    for h in range(FOX_HEADS):
        even = h % 2 == 0
        main.append(LANES * h + (0 if even else FOX_HEAD_DIM))
        extra.append(LANES * h + (FOX_HEAD_DIM if even else 0))
    return main, extra


def _head_pad(w):
    d = w.shape[0]
    w4 = w.reshape(d, FOX_HEADS // 2, 2, FOX_HEAD_DIM)
    z = jnp.zeros((d, FOX_HEADS // 2, FOX_HEAD_DIM), w.dtype)
    return jnp.concatenate([w4[:, :, 0], z, z, w4[:, :, 1]], axis=-1).reshape(d, AUG)


def _prep_mix_weights(g_mix, w_in, b_f, conv_w, conv_b, w_a, b_a, w_i, b_i, lam):
    _, extra = _aug_layout()
    wq, wk, wv = w_in[:, 0:512], w_in[:, 512:1024], w_in[:, 1024:1536]
    wf, wxr, wg = w_in[:, 1536:1544], w_in[:, 1544:2056], w_in[:, 2056:2568]
    wf_pad = jnp.concatenate([wf, wf, wf, jnp.zeros((D_MODEL, LANES - 3 * FOX_HEADS), F32)], axis=1)
    w_all = jnp.concatenate([_head_pad(wq * (FOX_HEAD_DIM ** -0.5 * LOG2E)), _head_pad(wk), _head_pad(wv),
                             wf_pad, wxr, wg], axis=1).astype(BF16)
    bf_pad = jnp.concatenate([b_f, b_f, b_f, jnp.zeros((LANES - 3 * FOX_HEADS,), F32)]).reshape(1, LANES)
    eq = np.zeros((LANES, AUG), np.float32)
    ek = np.zeros((LANES, AUG), np.float32)
    ec = np.zeros((LANES, AUG), np.float32)
    cq = np.zeros((1, AUG), np.float32)
    ck = np.zeros((1, AUG), np.float32)
    cv = np.zeros((1, AUG), np.float32)
    hsel = np.zeros((AUG, LANES), np.float32)
    for h in range(FOX_HEADS):
        hsel[LANES * h:LANES * (h + 1), h] = 1.0
        cv[0, extra[h]] = 1.0
        ec[h, extra[h] + 6] = 1.0
        ck[0, extra[h] + 6] = 1.0
        for part in range(3):
            eq[part * 8 + h, extra[h] + part] = 1.0
            ek[part * 8 + h, extra[h] + 3 + part] = 1.0
            cq[0, extra[h] + 3 + part] = 1.0
            ck[0, extra[h] + part] = 1.0
    dense = lambda w: jax.scipy.linalg.block_diag(*[w[i] for i in range(LRU_BLOCKS)]).astype(BF16)
    row = lambda a: a.reshape(1, -1)
    return dict(g_mix=row(g_mix), w_all=w_all, bf_pad=bf_pad, eq=jnp.asarray(eq, BF16), ek=jnp.asarray(ek, BF16),
                ec=jnp.asarray(ec, BF16), cq=jnp.asarray(cq), ck=jnp.asarray(ck), cv=jnp.asarray(cv), hsel=jnp.asarray(hsel, BF16),
                conv_w=conv_w, conv_b=row(conv_b),
                wa=dense(w_a), ba=row(b_a), wi=dense(w_i), bi=row(b_i), lam=row(lam))


def kernel(x_prompt, x_sample, mem_prompt, cache_fox_k, cache_fox_v, cache_fox_logf, state_lru_h, state_conv, cache_mem_k, cache_mem_v, g_mix, w_in, b_f, conv_w, conv_b, w_a, b_a, w_i, b_i, lam, w_out, g_cross, g_mem, w_cq, w_ck, w_cv, w_co, g_moe, w_router, b_router, w_gate, b_gate, w_up, b_up, w_down, b_down, g_final):
    nb_p, seq, _ = x_prompt.shape
    nb_s, t_s, _ = x_sample.shape
    past = cache_fox_k.shape[2]
    n_mem = mem_prompt.shape[1]
    assert nb_p == 1 and g_mix.shape[0] == 1
    n_p, n_s = seq, nb_s * t_s
    row = lambda a: a.reshape(1, -1)

    mw = _prep_mix_weights(g_mix[0], w_in[0], b_f[0], conv_w[0], conv_b[0], w_a[0], b_a[0], w_i[0], b_i[0], lam[0])
    xp = x_prompt.reshape(n_p, D_MODEL)
    xs_ = x_sample.reshape(n_s, D_MODEL)

    zero_prev = jnp.zeros((1, SUBLANES, LRU_WIDTH), F32)
    zero_h = jnp.zeros((1, 1, LRU_WIDTH), F32)
    (qa_p, ka_p, va_p, k_p, v_p, _, lf_p, f2_p, qn_p, kn_p, lru_p, hl_p, ct_p) = _mix_in(
        xp, zero_prev, zero_h, mw, tm=MIX_TILE, streaming=True)
    prev_s = jnp.pad(state_conv[0], ((0, 0), (SUBLANES - (CONV_WIDTH - 1), 0), (0, 0)))
    (qa_s, ka_s, _, k_s, v_s, vb_s, lf_s, _, _, _, lru_s, hl_s, ct_s) = _mix_in(
        xs_, prev_s, state_lru_h[0].reshape(nb_s, 1, LRU_WIDTH), mw, tm=t_s, streaming=False)

    n_past, static_ok = _fox_plan(f2_p, qn_p, kn_p, bq=FOX_BQ, tile=MIX_TILE)
    fo_p = _fox_prompt(n_past, static_ok, qa_p, ka_p, va_p, bq=FOX_BQ)
    lf_c = cache_fox_logf[0]
    clf3 = jnp.concatenate([lf_c, lf_c, lf_c, jnp.zeros((nb_s, past, LANES - 3 * FOX_HEADS), F32)], axis=-1)
    fo_s = _fox_sample(qa_s, ka_s, vb_s, cache_fox_k[0].reshape(nb_s, past, FOX_WIDTH),
                       cache_fox_v[0].reshape(nb_s, past, FOX_WIDTH), clf3, mw["ek"], mw["ck"], t=t_s)

    mk_p, mv_p = _mem_kv(mem_prompt[0], row(g_mem[0]), w_ck[0].astype(BF16), w_cv[0].astype(BF16))

    wr_pad = jnp.pad(w_router[0], ((0, 0), (0, LANES - N_EXPERTS))).astype(BF16)
    br_pad = jnp.pad(b_router[0], (0, LANES - N_EXPERTS)).reshape(1, LANES)
    pw = dict(w_out=w_out[0].astype(BF16), g_cross=row(g_cross[0]), w_cq=w_cq[0].astype(BF16),
              w_co=w_co[0].astype(BF16), g_moe=row(g_moe[0]), w_router=wr_pad, b_router=br_pad)
    cnt0 = jnp.zeros((SUBLANES, LANES), F32)
    h2_p, xn_p, idx_p, tw_p, rk_p, cnt_p = _post(
        xp, fo_p, lru_p, mk_p.reshape(1, n_mem, D_MODEL), mv_p.reshape(1, n_mem, D_MODEL), cnt0, pw,
        tm=POST_TILE, per_step_memory=False)
    h2_s, xn_s, idx_s, tw_s, rk_s, cnt = _post(
        xs_, fo_s, lru_s, cache_mem_k[0].reshape(nb_s, n_mem, D_MODEL), cache_mem_v[0].reshape(nb_s, n_mem, D_MODEL),
        cnt_p, pw, tm=t_s, per_step_memory=True)

    n = n_p + n_s
    idx = jnp.concatenate([idx_p[:, :TOP_K], idx_s[:, :TOP_K]], axis=0)
    rank = jnp.concatenate([rk_p[:, :TOP_K], rk_s[:, :TOP_K]], axis=0)
    counts = cnt[0, :N_EXPERTS].astype(I32)
    padded = (counts + MOE_BM - 1) // MOE_BM * MOE_BM
    pad_end = jnp.cumsum(padded)
    pad_start = pad_end - padded
    dest = (pad_start[idx] + rank).reshape(n * TOP_K)
    n_blocks = -(-(n * TOP_K) // MOE_BM) + N_EXPERTS
    blk_row = jnp.arange(n_blocks, dtype=I32) * MOE_BM
    block_e = jnp.minimum(jnp.sum(pad_end[None, :] <= blk_row[:, None], axis=1), N_EXPERTS - 1).astype(I32)
    n_used = (pad_end[-1] // MOE_BM).reshape(1).astype(I32)
    last_block = jnp.maximum(pad_end // MOE_BM - 1, 0).astype(I32)

    xs_sorted = _dispatch(dest, last_block, n_used, xn_p, xn_s, tm=MOE_BM, n_blocks=n_blocks)
    ew = dict(w_gate=w_gate[0], w_up=w_up[0], w_down=w_down[0],
              b_gate=b_gate[0].reshape(N_EXPERTS, 1, D_MODEL), b_up=b_up[0].reshape(N_EXPERTS, 1, D_MODEL),
              b_down=b_down[0].reshape(N_EXPERTS, 1, D_MODEL))
    ys = _experts(block_e, n_used, xs_sorted, ew, bm=MOE_BM)
    y_p, y_s = _combine(dest, h2_p, h2_s, tw_p, tw_s, row(g_final), ys, tm=COMBINE_TILE)

    shp_p = (1, nb_p, seq, FOX_HEADS, FOX_HEAD_DIM)
    shp_s = (1, nb_s, t_s, FOX_HEADS, FOX_HEAD_DIM)
    tail = slice(SUBLANES - (CONV_WIDTH - 1), SUBLANES)
    return (y_p.reshape(nb_p, seq, D_MODEL), y_s.reshape(nb_s, t_s, D_MODEL),
            k_p.reshape(shp_p), v_p.reshape(shp_p), lf_p.reshape(1, nb_p, seq, FOX_HEADS),
            hl_p.reshape(1, nb_p, LRU_WIDTH), ct_p[:, tail, :].reshape(1, nb_p, CONV_WIDTH - 1, LRU_WIDTH),
            mk_p.reshape(1, nb_p, n_mem, MEM_HEADS, MEM_HEAD_DIM), mv_p.reshape(1, nb_p, n_mem, MEM_HEADS, MEM_HEAD_DIM),
            k_s.reshape(shp_s), v_s.reshape(shp_s), lf_s.reshape(1, nb_s, t_s, FOX_HEADS),
            hl_s.reshape(1, nb_s, LRU_WIDTH), ct_s[:, tail, :].reshape(1, nb_s, CONV_WIDTH - 1, LRU_WIDTH))
```

```python
import functools
import math

import jax
import jax.numpy as jnp
import numpy as np
from jax import lax
from jax.experimental import pallas as pl
from jax.experimental.pallas import tpu as pltpu

F32 = jnp.float32
BF16 = jnp.bfloat16
I32 = jnp.int32

D_MODEL = 1024
FOX_HEADS = 8
FOX_HEAD_DIM = 64
FOX_WIDTH = FOX_HEADS * FOX_HEAD_DIM
LRU_WIDTH = D_MODEL - FOX_WIDTH
LRU_BLOCKS = 8
LRU_C = 8.0
CONV_WIDTH = 4
MEM_HEADS = 4
MEM_HEAD_DIM = D_MODEL // MEM_HEADS
N_EXPERTS = 32
TOP_K = 4
SWIGLU_LIMIT = 7.0
SWIGLU_ALPHA = 1.702
RMS_EPS = 1e-6
NEG_INF = -1e30
LOG2E = math.log2(math.e)

LANES = 128
SUBLANES = 8
CHUNKS = D_MODEL // LANES
TOKEN_TILE = (CHUNKS, SUBLANES, LANES)
AUG = LANES * FOX_HEADS
VMEM_LIMIT = 56 * 1024 * 1024

MIX_TILE = 256
POST_TILE = 512
FOX_BQ = 512
MOE_BM = 512
COMBINE_TILE = 256
FOX_SKIP_GAP = 160.0
FOX_STATIC_SHIFT_RANGE = 100.0

C_Q, C_K, C_V, C_F, C_XR, C_G, C_END = 0, 512, 1024, 1536, 1664, 2176, 2688


def _dot(a, b):
    return jnp.dot(a, b, preferred_element_type=F32)


def _dot_nt(a, b):
    return lax.dot_general(a, b, (((1,), (1,)), ((), ())), preferred_element_type=F32)


def _split3(x):
    hi = x.astype(BF16)
    r = x - hi.astype(F32)
    mid = r.astype(BF16)
    lo = (r - mid.astype(F32)).astype(BF16)
    return hi, mid, lo


def _dot3(a, x):
    hi, mid, lo = _split3(x)
    return _dot(a, hi) + _dot(a, mid) + _dot(a, lo)


def _lane_split3(x):
    hi, mid, lo = _split3(x)
    lane = lax.broadcasted_iota(I32, x.shape, 1)
    return jnp.where(lane < 8, hi, jnp.where(lane < 16, mid, lo))


def _softplus(x):
    return jnp.maximum(x, 0.0) + jnp.log1p(jnp.exp(-jnp.abs(x)))


def _rms(x, g):
    return x * lax.rsqrt(jnp.mean(x * x, axis=-1, keepdims=True) + RMS_EPS) * g


def _tri(n, kind):
    r = lax.broadcasted_iota(I32, (n, n), 0)
    c = lax.broadcasted_iota(I32, (n, n), 1)
    m = {"le": c <= r, "lt": c < r, "gt": c > r}[kind]
    return jnp.where(m, 1.0, 0.0).astype(BF16)


def _mix_in_kernel(x_ref, g_ref, w_ref, bfp_ref, eq_ref, ek_ref, cq_ref, ck_ref, cv_ref, hs_ref, hd_ref, hm_ref,
                   cw_ref, cb_ref, wa_ref, ba_ref, wi_ref, bi_ref, lam_ref, cprev_ref, h0_ref,
                   qa_ref, ka_ref, va_ref, ko_ref, vo_ref, vb_ref, lf_ref, f2_ref, qn_ref, kn_ref,
                   lru_ref, hl_ref, ct_ref,
                   fcar, hcar, xp_ref, sa_ref, sb_ref, kbuf, vbuf, ksem, *, tm, pad, streaming):
    step = pl.program_id(0)
    first = step == 0
    slot = step % 2

    def kv_copies(s, row0):
        rows = pl.ds(row0, tm)
        copies = []
        for h in range(FOX_HEADS):
            copies.append(pltpu.make_async_copy(kbuf.at[s, h], ko_ref.at[rows, h, :], ksem.at[s]))
            copies.append(pltpu.make_async_copy(vbuf.at[s, h], vo_ref.at[rows, h, :], ksem.at[s]))
        return copies

    @pl.when(step >= 2)
    def _():
        for c in kv_copies(slot, 0):
            c.wait()

    if streaming:
        @pl.when(first)
        def _():
            fcar[...] = jnp.zeros_like(fcar)
            hcar[...] = jnp.zeros_like(hcar)
            xp_ref[0:SUBLANES, :] = jnp.zeros((SUBLANES, LRU_WIDTH), F32)
    else:
        fcar[...] = jnp.zeros_like(fcar)
        hcar[...] = jnp.broadcast_to(h0_ref[0], hcar.shape)
        xp_ref[0:SUBLANES, :] = cprev_ref[0]

    x = x_ref[...]
    xn = _rms(x, g_ref[...]).astype(BF16)
    z = _dot(xn, w_ref[...])

    zq, zk, zv = z[:, C_Q:C_K], z[:, C_K:C_V], z[:, C_V:C_F]
    for h in range(FOX_HEADS):
        hc = slice(h * FOX_HEAD_DIM, (h + 1) * FOX_HEAD_DIM)
        kbuf[slot, h] = zk[:, hc]
        vbuf[slot, h] = zv[:, hc]
    for c in kv_copies(slot, pl.multiple_of(step * tm, tm)):
        c.start()
    qd, kd, vd = zq.astype(BF16), zk.astype(BF16), zv.astype(BF16)
    vb_ref[...] = vd

    def head_pad(d):
        blocks = []
        for p in range(FOX_HEADS // 2):
            blk = d[:, p * LANES:(p + 1) * LANES]
            blocks += [blk * hm_ref[0:1, :], blk * hm_ref[1:2, :]]
        return jnp.concatenate(blocks, axis=1)

    lane = lax.broadcasted_iota(I32, (tm, LANES), 1)
    lf = jnp.where(lane < 3 * FOX_HEADS, -_softplus(-(z[:, C_F:C_XR] + bfp_ref[...])), 0.0)
    lf_ref[...] = lf[:, :FOX_HEADS]
    cum = _dot3(_tri(tm, "le"), lf) + fcar[0:1, :]
    fcar[...] = jnp.broadcast_to(cum[tm - 1:tm, :], fcar.shape)
    cum2 = cum * LOG2E
    f2_ref[...] = cum2[:, :FOX_HEADS]
    qf, kf = qd.astype(F32), kd.astype(F32)
    diag = _dot((qf * kf).astype(BF16), hd_ref[...]).astype(BF16)
    hi, mid, lo = _split3(cum2)
    bias = jnp.where(lane < 8, hi, jnp.where(lane < 16, mid, jnp.where(lane < 24, lo, diag)))
    qa_ref[...] = head_pad(qd) + (_dot(bias, eq_ref[...]) + cq_ref[...]).astype(BF16)
    ka_ref[...] = head_pad(kd) + (ck_ref[...] - _dot(bias, ek_ref[...])).astype(BF16)
    va_ref[...] = head_pad(vd) + cv_ref[...].astype(BF16)
    qn_ref[...] = jnp.broadcast_to(jnp.max(_dot((qf * qf).astype(BF16), hs_ref[...]), axis=0, keepdims=True),
                                   qn_ref.shape)
    kn_ref[...] = jnp.broadcast_to(jnp.max(_dot((kf * kf).astype(BF16), hs_ref[...]), axis=0, keepdims=True),
                                   kn_ref.shape)

    xr = z[:, C_XR:C_G]
    xp_ref[SUBLANES:SUBLANES + tm, :] = xr
    xc = cb_ref[...] + xr * cw_ref[CONV_WIDTH - 1:CONV_WIDTH, :]
    for j in range(CONV_WIDTH - 1):
        sh = CONV_WIDTH - 1 - j
        xc = xc + xp_ref[SUBLANES - sh:SUBLANES - sh + tm, :] * cw_ref[j:j + 1, :]
    tail = xp_ref[tm:tm + SUBLANES, :]
    ct_ref[0] = tail
    xp_ref[0:SUBLANES, :] = tail

    xcb = xc.astype(BF16)
    r = jax.nn.sigmoid(_dot(xcb, wa_ref[...]) + ba_ref[...])
    ig = jax.nn.sigmoid(_dot(xcb, wi_ref[...]) + bi_ref[...])
    log_a = (-LRU_C) * r * _softplus(-lam_ref[...])
    a = jnp.exp(log_a)
    mult = jnp.sqrt(-jnp.tanh(log_a) * (a * a + 1.0))
    if streaming:
        row = lax.broadcasted_iota(I32, (tm, LRU_WIDTH), 0)
        mult = jnp.where(jnp.logical_and(row == 0, first), 1.0, mult)
    b = mult * ig * xc

    sa_ref[0:pad, :] = jnp.ones((pad, LRU_WIDTH), F32)
    sb_ref[0:pad, :] = jnp.zeros((pad, LRU_WIDTH), F32)
    d = 1
    while d < tm:
        sa_ref[pad:pad + tm, :] = a
        sb_ref[pad:pad + tm, :] = b
        b = a * sb_ref[pad - d:pad - d + tm, :] + b
        a = a * sa_ref[pad - d:pad - d + tm, :]
        d *= 2
    h = a * hcar[0:1, :] + b
    hlast = h[tm - 1:tm, :]
    hcar[...] = jnp.broadcast_to(hlast, hcar.shape)
    hl_ref[0] = hlast
    lru_ref[...] = (h * jax.nn.gelu(z[:, C_G:C_END])).astype(BF16)

    last = pl.num_programs(0) - 1

    @pl.when(jnp.logical_and(step == last, step >= 1))
    def _():
        for c in kv_copies(1 - slot, 0):
            c.wait()

    @pl.when(step == last)
    def _():
        for c in kv_copies(slot, 0):
            c.wait()


def _mix_in(x, cprev, h0, wts, *, tm, streaming):
    n = x.shape[0]
    steps = n // tm
    nseg = 1 if streaming else steps
    pad = max(tm // 2, SUBLANES)
    seg = (lambda i: (0, 0, 0)) if streaming else (lambda i: (i, 0, 0))
    full = lambda a: pl.BlockSpec(a.shape, lambda i: (0,) * a.ndim)
    rows = lambda w: pl.BlockSpec((tm, w), lambda i: (i, 0))
    names = ("g_mix", "w_all", "bf_pad", "eq", "ek", "cq", "ck", "cv", "hsel", "hdiag", "hmask", "conv_w", "conv_b",
             "wa", "ba", "wi", "bi", "lam")
    ws = [wts[k] for k in names]
    sds = jax.ShapeDtypeStruct
    out_shape = (
        sds((n, AUG), BF16), sds((n, AUG), BF16), sds((n, AUG), BF16),
        sds((n, FOX_HEADS, FOX_HEAD_DIM), F32), sds((n, FOX_HEADS, FOX_HEAD_DIM), F32), sds((n, FOX_WIDTH), BF16),
        sds((n, FOX_HEADS), F32), sds((n, FOX_HEADS), F32),
        sds((steps * SUBLANES, LANES), F32), sds((steps * SUBLANES, LANES), F32),
        sds((n, LRU_WIDTH), BF16),
        sds((nseg, 1, LRU_WIDTH), F32), sds((nseg, SUBLANES, LRU_WIDTH), F32),
    )
    out_specs = (
        rows(AUG), rows(AUG), rows(AUG), pl.BlockSpec(memory_space=pl.ANY), pl.BlockSpec(memory_space=pl.ANY),
        rows(FOX_WIDTH), rows(FOX_HEADS), rows(FOX_HEADS),
        pl.BlockSpec((SUBLANES, LANES), lambda i: (i, 0)), pl.BlockSpec((SUBLANES, LANES), lambda i: (i, 0)),
        rows(LRU_WIDTH),
        pl.BlockSpec((1, 1, LRU_WIDTH), seg), pl.BlockSpec((1, SUBLANES, LRU_WIDTH), seg),
    )
    in_specs = [rows(D_MODEL)] + [full(w) for w in ws] + [
        pl.BlockSpec((1, SUBLANES, LRU_WIDTH), seg), pl.BlockSpec((1, 1, LRU_WIDTH), seg)]
    return pl.pallas_call(
        functools.partial(_mix_in_kernel, tm=tm, pad=pad, streaming=streaming),
        grid=(steps,), in_specs=in_specs, out_specs=out_specs, out_shape=out_shape,
        scratch_shapes=[
            pltpu.VMEM((SUBLANES, LANES), F32), pltpu.VMEM((SUBLANES, LRU_WIDTH), F32),
            pltpu.VMEM((tm + SUBLANES, LRU_WIDTH), F32),
            pltpu.VMEM((pad + tm, LRU_WIDTH), F32), pltpu.VMEM((pad + tm, LRU_WIDTH), F32),
            pltpu.VMEM((2, FOX_HEADS, tm, FOX_HEAD_DIM), F32), pltpu.VMEM((2, FOX_HEADS, tm, FOX_HEAD_DIM), F32),
            pltpu.SemaphoreType.DMA((2,))],
        compiler_params=pltpu.CompilerParams(dimension_semantics=("arbitrary",), vmem_limit_bytes=VMEM_LIMIT),
        name="mix_in_stream" if streaming else "mix_in_segments",
    )(x, *ws, cprev, h0)


def _fox_prompt_kernel(nb_ref, st_ref, q_ref, k_ref, v_ref, o_ref, m_ref, acc_ref, *, bq):
    qi = pl.program_id(1)
    plan = pl.program_id(0) * pl.num_programs(1) + qi
    n_past = nb_ref[plan]
    cols = [slice(LANES * c, LANES * (c + 1)) for c in range(2)]
    qs = [q_ref[:, cols[c]] for c in range(2)]
    acc_ref[...] = jnp.zeros(acc_ref.shape, F32)

    def scores(c, start, causal):
        s = _dot_nt(qs[c], k_ref[pl.ds(start, bq), cols[c]])
        if causal:
            r = lax.broadcasted_iota(I32, (bq, bq), 0)
            cc = lax.broadcasted_iota(I32, (bq, bq), 1)
            s = jnp.where(cc <= r, s, NEG_INF)
        return s

    def static_block(start, causal):
        for c in range(2):
            p = jnp.exp2(scores(c, start, causal)).astype(BF16)
            acc_ref[c] += _dot(p, v_ref[pl.ds(start, bq), cols[c]])

    def online_block(start, causal):
        for c in range(2):
            s = scores(c, start, causal)
            m_old = m_ref[c]
            m_new = jnp.maximum(m_old, jnp.max(s, axis=1, keepdims=True))
            p = jnp.exp2(s - m_new).astype(BF16)
            acc_ref[c] = jnp.exp2(m_old - m_new) * acc_ref[c] + _dot(p, v_ref[pl.ds(start, bq), cols[c]])
            m_ref[c] = m_new

    def run(block):
        block(pl.multiple_of(qi * bq, bq), True)

        def body(t, carry):
            block(pl.multiple_of((qi - 1 - t) * bq, bq), False)
            return carry

        lax.fori_loop(0, n_past, body, 0)

    @pl.when(st_ref[plan] == 1)
    def _():
        run(static_block)

    @pl.when(st_ref[plan] != 1)
    def _():
        m_ref[...] = jnp.full(m_ref.shape, NEG_INF, F32)
        run(online_block)

    a0, a1 = acc_ref[0], acc_ref[1]
    o0 = a0 / a0[:, FOX_HEAD_DIM:FOX_HEAD_DIM + 1]
    o1 = a1 / a1[:, 0:1]
    lane = lax.broadcasted_iota(I32, (bq, LANES), 1)
    o_ref[...] = jnp.where(lane < FOX_HEAD_DIM, o0, o1).astype(BF16)


def _fox_prompt(n_past, static_ok, qa, ka, va, *, bq):
    s = qa.shape[0]
    pairs = FOX_HEADS // 2
    return pl.pallas_call(
        functools.partial(_fox_prompt_kernel, bq=bq),
        grid_spec=pltpu.PrefetchScalarGridSpec(
            num_scalar_prefetch=2, grid=(pairs, s // bq),
            in_specs=[pl.BlockSpec((bq, 2 * LANES), lambda p, i, nb, st: (i, p)),
                      pl.BlockSpec((s, 2 * LANES), lambda p, i, nb, st: (0, p)),
                      pl.BlockSpec((s, 2 * LANES), lambda p, i, nb, st: (0, p))],
            out_specs=pl.BlockSpec((bq, LANES), lambda p, i, nb, st: (i, p)),
            scratch_shapes=[pltpu.VMEM((2, bq, 1), F32), pltpu.VMEM((2, bq, LANES), F32)]),
        out_shape=jax.ShapeDtypeStruct((s, FOX_WIDTH), BF16),
        compiler_params=pltpu.CompilerParams(dimension_semantics=("arbitrary", "arbitrary"),
                                             vmem_limit_bytes=VMEM_LIMIT),
        name="fox_prompt",
    )(n_past, static_ok, qa, ka, va)


def _fox_plan(f2, qn, kn, *, bq, tile):
    s = f2.shape[0]
    nq = s // bq
    per = lambda a: a.reshape(s // tile, SUBLANES, LANES)[:, 0, :FOX_HEADS]
    qnorm = jnp.sqrt(jnp.max(per(qn).reshape(nq, bq // tile, FOX_HEADS), axis=1))
    knorm = jnp.sqrt(jnp.max(per(kn), axis=0))
    spread = 2.0 * 1.02 * qnorm * knorm[None, :]
    thr = spread + FOX_SKIP_GAP
    f_first = f2[0::bq]
    f_last = f2[bq - 1::bq]
    gap = f_last[None, :, :] - f_first[:, None, :]
    before = (jnp.arange(nq)[None, :] < jnp.arange(nq)[:, None])[:, :, None]
    need = jnp.sum(jnp.logical_and(before, gap < thr[:, None, :]), axis=1)
    pair = lambda a: a.reshape(nq, FOX_HEADS // 2, 2)
    n_past = jnp.max(pair(need), axis=2).T.reshape(-1).astype(I32)
    static_ok = jnp.all(pair(spread) <= FOX_STATIC_SHIFT_RANGE, axis=2).T.reshape(-1).astype(I32)
    return n_past, static_ok


def _fox_sample_kernel(q_ref, kn_ref, vn_ref, ck_ref, cv_ref, clf_ref, ek_ref, ckc_ref, o_ref, g_ref,
                       *, t, past, chunk):
    car = jnp.zeros((1, LANES), F32)
    upper = _tri(chunk, "gt")
    for ci in reversed(range(past // chunk)):
        lf = clf_ref[0, ci * chunk:(ci + 1) * chunk, :]
        g_ref[ci * chunk:(ci + 1) * chunk, :] = _dot3(upper, lf) + car
        car = car + jnp.sum(lf, axis=0, keepdims=True)
    gs = _lane_split3(g_ref[...] * LOG2E)

    lane = lax.broadcasted_iota(I32, (past, LANES), 1)
    r = lax.broadcasted_iota(I32, (t, t), 0)
    cc = lax.broadcasted_iota(I32, (t, t), 1)
    olane = lax.broadcasted_iota(I32, (t, LANES), 1)
    for p in range(FOX_HEADS // 2):
        pc = slice(LANES * p, LANES * (p + 1))
        kpair = ck_ref[0, :, pc]
        vpast = cv_ref[0, :, pc].astype(BF16)
        vnew = vn_ref[:, pc]
        pair_cols = slice(2 * LANES * p, 2 * LANES * (p + 1))
        pair_extras = _dot(gs, ek_ref[:, pair_cols]) + ckc_ref[:, pair_cols]
        outs = []
        for c in range(2):
            h = 2 * p + c
            hc = slice(LANES * h, LANES * (h + 1))
            main = (lane < FOX_HEAD_DIM) if c == 0 else (lane >= FOX_HEAD_DIM)
            extras = pair_extras[:, LANES * c:LANES * (c + 1)]
            kpast = (jnp.where(main, kpair, 0.0) + extras).astype(BF16)
            q = q_ref[:, hc]
            sp = _dot_nt(q, kpast)
            sn = jnp.where(cc <= r, _dot_nt(q, kn_ref[:, hc]), NEG_INF)
            m = jnp.maximum(jnp.max(sp, axis=1, keepdims=True), jnp.max(sn, axis=1, keepdims=True))
            pp = jnp.exp2(sp - m)
            pn = jnp.exp2(sn - m)
            l = jnp.sum(pp, axis=1, keepdims=True) + jnp.sum(pn, axis=1, keepdims=True)
            outs.append((_dot(pp.astype(BF16), vpast) + _dot(pn.astype(BF16), vnew)) / l)
        o_ref[:, pc] = jnp.where(olane < FOX_HEAD_DIM, outs[0], outs[1]).astype(BF16)


def _fox_sample(qa, ka, vb, cache_k, cache_v, cache_lf3, ek, ck, *, t):
    nb, past = cache_k.shape[0], cache_k.shape[1]
    full = lambda a: pl.BlockSpec(a.shape, lambda b: (0,) * a.ndim)
    return pl.pallas_call(
        functools.partial(_fox_sample_kernel, t=t, past=past, chunk=256),
        grid=(nb,),
        in_specs=[pl.BlockSpec((t, AUG), lambda b: (b, 0)), pl.BlockSpec((t, AUG), lambda b: (b, 0)),
                  pl.BlockSpec((t, FOX_WIDTH), lambda b: (b, 0)),
                  pl.BlockSpec((1, past, FOX_WIDTH), lambda b: (b, 0, 0)),
                  pl.BlockSpec((1, past, FOX_WIDTH), lambda b: (b, 0, 0)),
                  pl.BlockSpec((1, past, LANES), lambda b: (b, 0, 0)),
                  full(ek), full(ck)],
        out_specs=pl.BlockSpec((t, FOX_WIDTH), lambda b: (b, 0)),
        out_shape=jax.ShapeDtypeStruct((nb * t, FOX_WIDTH), BF16),
        scratch_shapes=[pltpu.VMEM((past, LANES), F32)],
        compiler_params=pltpu.CompilerParams(dimension_semantics=("arbitrary",), vmem_limit_bytes=VMEM_LIMIT),
        name="fox_sample",
    )(qa, ka, vb, cache_k, cache_v, cache_lf3, ek, ck)


def _mem_kv_kernel(m_ref, g_ref, wk_ref, wv_ref, k_ref, v_ref):
    mn = _rms(m_ref[...], g_ref[...]).astype(BF16)
    k_ref[...] = _dot(mn, wk_ref[...])
    v_ref[...] = _dot(mn, wv_ref[...])


def _mem_kv(mem, g, wk, wv):
    n = mem.shape[0]
    return pl.pallas_call(
        _mem_kv_kernel,
        out_shape=(jax.ShapeDtypeStruct((n, D_MODEL), F32), jax.ShapeDtypeStruct((n, D_MODEL), F32)),
        compiler_params=pltpu.CompilerParams(vmem_limit_bytes=VMEM_LIMIT),
        name="mem_kv",
    )(mem, g, wk, wv)


def _post_kernel(h_ref, fo_ref, lru_ref, wo_ref, gc_ref, wcq_ref, mk_ref, mv_ref, wco_ref,
                 gm_ref, wr_ref, br_ref, cnt_in_ref,
                 h2_ref, xn_ref, idx_ref, tw_ref, rank_ref, cnt_ref, car_ref, *, tm):
    @pl.when(pl.program_id(0) == 0)
    def _():
        car_ref[...] = cnt_in_ref[...]

    h1 = h_ref[...] + _dot(fo_ref[...], wo_ref[0:FOX_WIDTH, :]) + _dot(lru_ref[...], wo_ref[FOX_WIDTH:D_MODEL, :])

    q = _dot(_rms(h1, gc_ref[...]).astype(BF16), wcq_ref[...])
    heads = []
    for hd in range(MEM_HEADS):
        hc = slice(MEM_HEAD_DIM * hd, MEM_HEAD_DIM * (hd + 1))
        s = _dot_nt(q[:, hc].astype(BF16), mk_ref[0, :, hc].astype(BF16)) * (MEM_HEAD_DIM ** -0.5)
        p = jnp.exp(s - jnp.max(s, axis=1, keepdims=True))
        o = _dot(p.astype(BF16), mv_ref[0, :, hc].astype(BF16)) / jnp.sum(p, axis=1, keepdims=True)
        heads.append(o.astype(BF16))
    h2 = h1 + _dot(jnp.concatenate(heads, axis=1), wco_ref[...])
    h2_ref[...] = h2

    xn = _rms(h2, gm_ref[...])
    for c in range(D_MODEL // LANES):
        xn_ref[:, c] = xn[:, c * LANES:(c + 1) * LANES].reshape(tm // SUBLANES, SUBLANES, LANES)

    lane = lax.broadcasted_iota(I32, (tm, LANES), 1).astype(F32)
    logits = jnp.where(lane < N_EXPERTS, _dot(xn.astype(BF16), wr_ref[...]) + br_ref[...], -jnp.inf)
    vals, idxs = [], []
    for _ in range(TOP_K):
        mx = jnp.max(logits, axis=1, keepdims=True)
        ix = jnp.min(jnp.where(logits == mx, lane, float(LANES)), axis=1, keepdims=True)
        vals.append(mx)
        idxs.append(ix)
        logits = jnp.where(lane == ix, -jnp.inf, logits)
    es = [jnp.exp(v - vals[0]) for v in vals]
    den = es[0] + es[1] + es[2] + es[3]

    onehot = jnp.zeros((tm, LANES), F32)
    for ix in idxs:
        onehot = onehot + jnp.where(lane == ix, 1.0, 0.0)
    before = _dot(_tri(tm, "lt"), onehot.astype(BF16)) + car_ref[0:1, :]
    car = car_ref[0:1, :] + jnp.sum(onehot, axis=0, keepdims=True)
    car_ref[...] = jnp.broadcast_to(car, car_ref.shape)
    cnt_ref[...] = jnp.broadcast_to(car, cnt_ref.shape)

    idx_o = jnp.zeros((tm, LANES), F32)
    tw_o = jnp.zeros((tm, LANES), F32)
    rk_o = jnp.zeros((tm, LANES), F32)
    for j in range(TOP_K):
        rk = jnp.sum(jnp.where(lane == idxs[j], before, 0.0), axis=1, keepdims=True)
        idx_o = jnp.where(lane == j, idxs[j], idx_o)
        tw_o = jnp.where(lane == j, es[j] / den, tw_o)
        rk_o = jnp.where(lane == j, rk, rk_o)
    idx_ref[...] = idx_o.astype(I32)
    tw_ref[...] = tw_o
    rank_ref[...] = rk_o.astype(I32)


def _post(h, fo, lru, mk, mv, cnt_in, wts, *, tm, per_step_memory):
    n = h.shape[0]
    nm = mk.shape[1]
    full = lambda a: pl.BlockSpec(a.shape, lambda i: (0,) * a.ndim)
    rows = lambda w: pl.BlockSpec((tm, w), lambda i: (i, 0))
    mem = pl.BlockSpec((1, nm, D_MODEL), (lambda i: (i, 0, 0)) if per_step_memory else (lambda i: (0, 0, 0)))
    w = [wts[k] for k in ("w_out", "g_cross", "w_cq")]
    w2 = [wts[k] for k in ("w_co", "g_moe", "w_router", "b_router")]
    return pl.pallas_call(
        functools.partial(_post_kernel, tm=tm),
        grid=(n // tm,),
        in_specs=[rows(D_MODEL), rows(FOX_WIDTH), rows(LRU_WIDTH)] + [full(a) for a in w] + [mem, mem]
                 + [full(a) for a in w2] + [full(cnt_in)],
        out_specs=(rows(D_MODEL), pl.BlockSpec((tm // SUBLANES,) + TOKEN_TILE, lambda i: (i, 0, 0, 0)),
                   rows(LANES), rows(LANES), rows(LANES),
                   pl.BlockSpec((SUBLANES, LANES), lambda i: (0, 0))),
        out_shape=(jax.ShapeDtypeStruct((n, D_MODEL), F32), jax.ShapeDtypeStruct((n // SUBLANES,) + TOKEN_TILE, F32),
                   jax.ShapeDtypeStruct((n, LANES), I32), jax.ShapeDtypeStruct((n, LANES), F32),
                   jax.ShapeDtypeStruct((n, LANES), I32), jax.ShapeDtypeStruct((SUBLANES, LANES), F32)),
        scratch_shapes=[pltpu.VMEM((SUBLANES, LANES), F32)],
        compiler_params=pltpu.CompilerParams(dimension_semantics=("arbitrary",), vmem_limit_bytes=VMEM_LIMIT),
        name="post_shared_mem" if not per_step_memory else "post_batch_mem",
    )(h, fo, lru, *w, mk, mv, *w2, cnt_in)


STAGES = 3


def _dispatch_kernel(dest_ref, last_ref, nu_ref, xp_ref, xs_ref, out_ref, zero_ref, stage, sems, gsem, zsem,
                     *, tm, steps_p, n_blocks):
    i = pl.program_id(0)

    def zero_copy(b):
        return pltpu.make_async_copy(zero_ref, out_ref.at[pl.ds(pl.multiple_of(b * tm, tm), tm)], zsem)

    @pl.when(i == 0)
    def _():
        zero_ref[...] = jnp.zeros(zero_ref.shape, F32)
        n_tail = n_blocks - nu_ref[0]

        def start_e(e, c):
            zero_copy(last_ref[e]).start()
            return c

        def start_t(b, c):
            zero_copy(nu_ref[0] + b).start()
            return c

        def wait_one(b, c):
            zero_copy(0).wait()
            return c

        lax.fori_loop(0, N_EXPERTS, start_e, 0)
        lax.fori_loop(0, n_tail, start_t, 0)
        lax.fori_loop(0, N_EXPERTS + n_tail, wait_one, 0)

    groups = tm // SUBLANES
    steps = pl.num_programs(0)

    def stage_copy(src_ref, tile, slot):
        return pltpu.make_async_copy(src_ref.at[pl.ds(tile * groups, groups)], stage.at[slot], gsem.at[slot])

    def start_stage(tile, slot):
        @pl.when(tile < steps_p)
        def _():
            stage_copy(xp_ref, tile, slot).start()

        @pl.when(tile >= steps_p)
        def _():
            stage_copy(xs_ref, tile - steps_p, slot).start()

    @pl.when(i == 0)
    def _():
        start_stage(0, 0)

    @pl.when(i + 1 < steps)
    def _():
        start_stage(i + 1, (i + 1) % STAGES)

    slot = i % STAGES
    stage_copy(xp_ref, 0, slot).wait()

    base = i * (tm * TOP_K)

    def start(g, c):
        for u in range(SUBLANES):
            for j in range(TOP_K):
                d = dest_ref[base + (g * SUBLANES + u) * TOP_K + j]
                pltpu.make_async_copy(stage.at[slot, g, :, u, :], out_ref.at[d],
                                      sems.at[i % 2]).start(priority=j % 2)
        return c

    lax.fori_loop(0, groups, start, 0)

    def drain(parity):
        for _ in range(TOP_K):
            pltpu.make_async_copy(out_ref.at[pl.ds(0, tm)], out_ref.at[pl.ds(0, tm)], sems.at[parity]).wait()

    @pl.when(i > 0)
    def _():
        drain((i - 1) % 2)

    @pl.when(i == steps - 1)
    def _():
        drain(i % 2)


def _dispatch(dest_flat, last_block, n_used, xn_p, xn_s, *, tm, n_blocks):
    steps_p, steps_s = xn_p.shape[0] * SUBLANES // tm, xn_s.shape[0] * SUBLANES // tm
    return pl.pallas_call(
        functools.partial(_dispatch_kernel, tm=tm, steps_p=steps_p, n_blocks=n_blocks),
        grid_spec=pltpu.PrefetchScalarGridSpec(
            num_scalar_prefetch=3, grid=(steps_p + steps_s,),
            in_specs=[pl.BlockSpec(memory_space=pl.ANY), pl.BlockSpec(memory_space=pl.ANY)],
            out_specs=pl.BlockSpec(memory_space=pl.ANY),
            scratch_shapes=[pltpu.VMEM((tm, CHUNKS, LANES), F32),
                            pltpu.VMEM((STAGES, tm // SUBLANES) + TOKEN_TILE, F32),
                            pltpu.SemaphoreType.DMA((2,)), pltpu.SemaphoreType.DMA((STAGES,)),
                            pltpu.SemaphoreType.DMA(())]),
        out_shape=jax.ShapeDtypeStruct((n_blocks * tm, CHUNKS, LANES), F32),
        compiler_params=pltpu.CompilerParams(dimension_semantics=("arbitrary",), vmem_limit_bytes=VMEM_LIMIT),
        name="moe_dispatch",
    )(dest_flat, last_block, n_used, xn_p, xn_s)


def _rows_from_tiles(view, rows):
    return jnp.concatenate([view[:, c].reshape(rows, LANES) for c in range(CHUNKS)], axis=1)


def _rows_to_tiles(view, val, rows):
    for c in range(CHUNKS):
        view[:, c] = val[:, c * LANES:(c + 1) * LANES].reshape(rows // SUBLANES, SUBLANES, LANES)


def _expert_kernel(be_ref, nu_ref, nx_ref, xs_ref, wg_ref, bg_ref, wu_ref, bu_ref, wd_ref, bd_ref, ys_ref,
                   wgb, wub, wdb, wbuf, xbuf, ybuf, seq, wsem, xsem, ysem, *, bm):
    i = pl.program_id(0)
    steps = pl.num_programs(0)
    groups = bm // SUBLANES
    slot = i % 2
    prev = be_ref[jnp.maximum(i - 1, 0)]
    fresh = jnp.logical_or(i == 0, be_ref[i] != prev)
    live = i < nu_ref[0]

    def w_copies(e, s):
        return [pltpu.make_async_copy(w_ref.at[e], wbuf.at[s, k], wsem.at[s])
                for k, w_ref in enumerate((wg_ref, wu_ref, wd_ref))]

    @pl.when(i == 0)
    def _():
        seq[0] = 0
        for c in w_copies(be_ref[0], 0):
            c.start()

    def in_copies(blk, s):
        g0 = pl.multiple_of(blk * groups, groups)
        return [pltpu.make_async_copy(xs_ref.at[pl.ds(g0, groups), u], xbuf.at[s, :, :, u, :], xsem.at[s])
                for u in range(SUBLANES)]

    def out_copies(blk, s):
        g0 = pl.multiple_of(blk * groups, groups)
        return [pltpu.make_async_copy(ybuf.at[s, :, :, u, :], ys_ref.at[pl.ds(g0, groups), u], ysem.at[s])
                for u in range(SUBLANES)]

    @pl.when(i == 0)
    def _():
        for c in in_copies(0, 0):
            c.start()

    @pl.when(i + 1 < nu_ref[0])
    def _():
        for c in in_copies(i + 1, (i + 1) % 2):
            c.start()

    @pl.when(i >= 2)
    def _():
        for c in out_copies(0, slot):
            c.wait()

    @pl.when(jnp.logical_and(live, fresh))
    def _():
        s = seq[0] % 2
        for c in w_copies(0, s):
            c.wait()
        for static_s in range(2):
            @pl.when(s == static_s)
            def _():
                wgb[...] = wbuf[static_s, 0].astype(BF16)
                wub[...] = wbuf[static_s, 1].astype(BF16)
                wdb[...] = wbuf[static_s, 2].astype(BF16)

        nxt = nx_ref[be_ref[i]]

        @pl.when(nxt < nu_ref[0])
        def _():
            for c in w_copies(be_ref[jnp.minimum(nxt, steps - 1)], 1 - s):
                c.start()

        seq[0] = seq[0] + 1

    @pl.when(live)
    def _():
        for c in in_copies(0, slot):
            c.wait()
        x = _rows_from_tiles(xbuf.at[slot], bm).astype(BF16)
        g = jnp.minimum(_dot(x, wgb[...]) + bg_ref[0], SWIGLU_LIMIT)
        u = jnp.clip(_dot(x, wub[...]) + bu_ref[0], -SWIGLU_LIMIT, SWIGLU_LIMIT)
        hdn = g * jax.nn.sigmoid(SWIGLU_ALPHA * g) * (u + 1.0)
        _rows_to_tiles(ybuf.at[slot], _dot(hdn.astype(BF16), wdb[...]) + bd_ref[0], bm)

    @pl.when(jnp.logical_not(live))
    def _():
        ybuf[slot] = jnp.zeros(ybuf.shape[1:], F32)

    for c in out_copies(i, slot):
        c.start()

    @pl.when(jnp.logical_and(i == steps - 1, i >= 1))
    def _():
        for c in out_copies(0, 1 - slot):
            c.wait()

    @pl.when(i == steps - 1)
    def _():
        for c in out_copies(0, slot):
            c.wait()


def _experts(block_e, n_used, next_first, xs, wts, *, bm):
    p = xs.shape[0]
    grouped = (p // SUBLANES, SUBLANES, CHUNKS, LANES)
    anyspace = pl.BlockSpec(memory_space=pl.ANY)
    bspec = pl.BlockSpec((1, 1, D_MODEL), lambda i, be, nu, nx: (be[i], 0, 0))
    tiles = (2, bm // SUBLANES) + TOKEN_TILE
    ys = pl.pallas_call(
        functools.partial(_expert_kernel, bm=bm),
        grid_spec=pltpu.PrefetchScalarGridSpec(
            num_scalar_prefetch=3, grid=(p // bm,),
            in_specs=[anyspace, anyspace, bspec, anyspace, bspec, anyspace, bspec],
            out_specs=anyspace,
            scratch_shapes=[pltpu.VMEM((D_MODEL, D_MODEL), BF16)] * 3 + [
                pltpu.VMEM((2, 3, D_MODEL, D_MODEL), F32), pltpu.VMEM(tiles, F32), pltpu.VMEM(tiles, F32),
                pltpu.SMEM((1,), I32),
                pltpu.SemaphoreType.DMA((2,)), pltpu.SemaphoreType.DMA((2,)), pltpu.SemaphoreType.DMA((2,))]),
        out_shape=jax.ShapeDtypeStruct(grouped, F32),
        compiler_params=pltpu.CompilerParams(dimension_semantics=("arbitrary",), vmem_limit_bytes=VMEM_LIMIT),
        name="moe_experts",
    )(block_e, n_used, next_first, xs.reshape(grouped), wts["w_gate"], wts["b_gate"], wts["w_up"], wts["b_up"],
      wts["w_down"], wts["b_down"])
    return ys.reshape(p, CHUNKS, LANES)


def _combine_kernel(dest_ref, hp_ref, hs_ref, twp_ref, tws_ref, gf_ref, ys_ref, yp_ref, ysm_ref, buf, sems,
                    *, tm, steps_p):
    i = pl.program_id(0)
    slot = i % 2

    def issue(tile, into):
        base = tile * (tm * TOP_K)

        def start(g, c):
            for u in range(SUBLANES):
                for j in range(TOP_K):
                    d = dest_ref[base + (g * SUBLANES + u) * TOP_K + j]
                    pltpu.make_async_copy(ys_ref.at[d], buf.at[into, j, g, :, u, :],
                                          sems.at[into]).start(priority=j % 2)
            return c

        lax.fori_loop(0, tm // SUBLANES, start, 0)

    @pl.when(i == 0)
    def _():
        issue(0, 0)

    @pl.when(i + 1 < pl.num_programs(0))
    def _():
        issue(i + 1, (i + 1) % 2)

    for j in range(TOP_K):
        pltpu.make_async_copy(ys_ref.at[pl.ds(0, tm)], ys_ref.at[pl.ds(0, tm)], sems.at[slot]).wait()

    def finish(h_ref, tw_ref, y_ref):
        tw = tw_ref[...]
        y = h_ref[...]
        for j in range(TOP_K):
            y = y + _rows_from_tiles(buf.at[slot, j], tm) * tw[:, j:j + 1]
        y_ref[...] = _rms(y, gf_ref[...])

    @pl.when(i < steps_p)
    def _():
        finish(hp_ref, twp_ref, yp_ref)

    @pl.when(i >= steps_p)
    def _():
        finish(hs_ref, tws_ref, ysm_ref)


def _combine(dest_flat, h2_p, h2_s, tw_p, tw_s, g_final, ys, *, tm):
    steps_p, steps_s = h2_p.shape[0] // tm, h2_s.shape[0] // tm
    pblk = lambda w: pl.BlockSpec((tm, w), lambda i, d: (jnp.minimum(i, steps_p - 1), 0))
    sblk = lambda w: pl.BlockSpec((tm, w), lambda i, d: (jnp.maximum(i - steps_p, 0), 0))
    return pl.pallas_call(
        functools.partial(_combine_kernel, tm=tm, steps_p=steps_p),
        grid_spec=pltpu.PrefetchScalarGridSpec(
            num_scalar_prefetch=1, grid=(steps_p + steps_s,),
            in_specs=[pblk(D_MODEL), sblk(D_MODEL), pblk(LANES), sblk(LANES),
                      pl.BlockSpec((1, D_MODEL), lambda i, d: (0, 0)),
                      pl.BlockSpec(memory_space=pl.ANY)],
            out_specs=(pblk(D_MODEL), sblk(D_MODEL)),
            scratch_shapes=[pltpu.VMEM((2, TOP_K, tm // SUBLANES) + TOKEN_TILE, F32),
                            pltpu.SemaphoreType.DMA((2,))]),
        out_shape=(jax.ShapeDtypeStruct(h2_p.shape, F32), jax.ShapeDtypeStruct(h2_s.shape, F32)),
        compiler_params=pltpu.CompilerParams(dimension_semantics=("arbitrary",), vmem_limit_bytes=VMEM_LIMIT),
        name="moe_combine",
    )(dest_flat, h2_p, h2_s, tw_p, tw_s, g_final, ys)


def _aug_layout():
    main, extra = [], []
    for h in range(FOX_HEADS):
        even = h % 2 == 0
        main.append(LANES * h + (0 if even else FOX_HEAD_DIM))
        extra.append(LANES * h + (FOX_HEAD_DIM if even else 0))
    return main, extra


def _prep_mix_weights(g_mix, w_in, b_f, conv_w, conv_b, w_a, b_a, w_i, b_i, lam):
    _, extra = _aug_layout()
    wq = w_in[:, 0:FOX_WIDTH] * (FOX_HEAD_DIM ** -0.5 * LOG2E)
    wf = w_in[:, 3 * FOX_WIDTH:3 * FOX_WIDTH + FOX_HEADS]
    wf_pad = jnp.concatenate([wf, wf, wf, jnp.zeros((D_MODEL, LANES - 3 * FOX_HEADS), F32)], axis=1)
    w_all = jnp.concatenate([wq, w_in[:, FOX_WIDTH:3 * FOX_WIDTH], wf_pad, w_in[:, 3 * FOX_WIDTH + FOX_HEADS:]],
                            axis=1).astype(BF16)
    bf_pad = jnp.concatenate([b_f, b_f, b_f, jnp.zeros((LANES - 3 * FOX_HEADS,), F32)]).reshape(1, LANES)
    eq = np.zeros((LANES, AUG), np.float32)
    ek = np.zeros((LANES, AUG), np.float32)
    cq = np.zeros((1, AUG), np.float32)
    ck = np.zeros((1, AUG), np.float32)
    cv = np.zeros((1, AUG), np.float32)
    hsel = np.zeros((FOX_WIDTH, LANES), np.float32)
    hdiag = np.zeros((FOX_WIDTH, LANES), np.float32)
    hmask = np.zeros((2, LANES), np.float32)
    hmask[0, :FOX_HEAD_DIM] = 1.0
    hmask[1, FOX_HEAD_DIM:] = 1.0
    for h in range(FOX_HEADS):
        hsel[FOX_HEAD_DIM * h:FOX_HEAD_DIM * (h + 1), h] = 1.0
        hdiag[FOX_HEAD_DIM * h:FOX_HEAD_DIM * (h + 1), 3 * FOX_HEADS + h] = 1.0
        cv[0, extra[h]] = 1.0
        eq[3 * FOX_HEADS + h, extra[h] + 6] = -1.0
        ck[0, extra[h] + 6] = 1.0
        for part in range(3):
            eq[part * 8 + h, extra[h] + part] = 1.0
            ek[part * 8 + h, extra[h] + 3 + part] = 1.0
            cq[0, extra[h] + 3 + part] = 1.0
            ck[0, extra[h] + part] = 1.0
    dense = lambda w: jax.scipy.linalg.block_diag(*[w[i] for i in range(LRU_BLOCKS)]).astype(BF16)
    row = lambda a: a.reshape(1, -1)
    b16 = lambda a: jnp.asarray(a, BF16)
    return dict(g_mix=row(g_mix), w_all=w_all, bf_pad=bf_pad, eq=b16(eq), ek=b16(ek),
                cq=jnp.asarray(cq), ck=jnp.asarray(ck), cv=jnp.asarray(cv),
                hsel=b16(hsel), hdiag=b16(hdiag), hmask=b16(hmask), conv_w=conv_w, conv_b=row(conv_b),
                wa=dense(w_a), ba=row(b_a), wi=dense(w_i), bi=row(b_i), lam=row(lam))


def kernel(x_prompt, x_sample, mem_prompt, cache_fox_k, cache_fox_v, cache_fox_logf, state_lru_h, state_conv, cache_mem_k, cache_mem_v, g_mix, w_in, b_f, conv_w, conv_b, w_a, b_a, w_i, b_i, lam, w_out, g_cross, g_mem, w_cq, w_ck, w_cv, w_co, g_moe, w_router, b_router, w_gate, b_gate, w_up, b_up, w_down, b_down, g_final):
    nb_p, seq, _ = x_prompt.shape
    nb_s, t_s, _ = x_sample.shape
    past = cache_fox_k.shape[2]
    n_mem = mem_prompt.shape[1]
    assert nb_p == 1 and g_mix.shape[0] == 1
    n_p, n_s = seq, nb_s * t_s
    row = lambda a: a.reshape(1, -1)

    mw = _prep_mix_weights(g_mix[0], w_in[0], b_f[0], conv_w[0], conv_b[0], w_a[0], b_a[0], w_i[0], b_i[0], lam[0])
    xp = x_prompt.reshape(n_p, D_MODEL)
    xs_ = x_sample.reshape(n_s, D_MODEL)

    zero_prev = jnp.zeros((1, SUBLANES, LRU_WIDTH), F32)
    zero_h = jnp.zeros((1, 1, LRU_WIDTH), F32)
    (qa_p, ka_p, va_p, k_p, v_p, _, lf_p, f2_p, qn_p, kn_p, lru_p, hl_p, ct_p) = _mix_in(
        xp, zero_prev, zero_h, mw, tm=MIX_TILE, streaming=True)
    prev_s = jnp.pad(state_conv[0], ((0, 0), (SUBLANES - (CONV_WIDTH - 1), 0), (0, 0)))
    (qa_s, ka_s, _, k_s, v_s, vb_s, lf_s, _, _, _, lru_s, hl_s, ct_s) = _mix_in(
        xs_, prev_s, state_lru_h[0].reshape(nb_s, 1, LRU_WIDTH), mw, tm=t_s, streaming=False)

    n_past, static_ok = _fox_plan(f2_p, qn_p, kn_p, bq=FOX_BQ, tile=MIX_TILE)
    fo_p = _fox_prompt(n_past, static_ok, qa_p, ka_p, va_p, bq=FOX_BQ)
    lf_c = cache_fox_logf[0]
    clf3 = jnp.concatenate([lf_c, lf_c, lf_c, jnp.zeros((nb_s, past, LANES - 3 * FOX_HEADS), F32)], axis=-1)
    fo_s = _fox_sample(qa_s, ka_s, vb_s, cache_fox_k[0].reshape(nb_s, past, FOX_WIDTH),
                       cache_fox_v[0].reshape(nb_s, past, FOX_WIDTH), clf3, mw["ek"], mw["ck"], t=t_s)

    mk_p, mv_p = _mem_kv(mem_prompt[0], row(g_mem[0]), w_ck[0].astype(BF16), w_cv[0].astype(BF16))

    wr_pad = jnp.pad(w_router[0], ((0, 0), (0, LANES - N_EXPERTS))).astype(BF16)
    br_pad = jnp.pad(b_router[0], (0, LANES - N_EXPERTS)).reshape(1, LANES)
    pw = dict(w_out=w_out[0].astype(BF16), g_cross=row(g_cross[0]), w_cq=w_cq[0].astype(BF16),
              w_co=w_co[0].astype(BF16), g_moe=row(g_moe[0]), w_router=wr_pad, b_router=br_pad)
    cnt0 = jnp.zeros((SUBLANES, LANES), F32)
    h2_p, xn_p, idx_p, tw_p, rk_p, cnt_p = _post(
        xp, fo_p, lru_p, mk_p.reshape(1, n_mem, D_MODEL), mv_p.reshape(1, n_mem, D_MODEL), cnt0, pw,
        tm=POST_TILE, per_step_memory=False)
    h2_s, xn_s, idx_s, tw_s, rk_s, cnt = _post(
        xs_, fo_s, lru_s, cache_mem_k[0].reshape(nb_s, n_mem, D_MODEL), cache_mem_v[0].reshape(nb_s, n_mem, D_MODEL),
        cnt_p, pw, tm=t_s, per_step_memory=True)

    n = n_p + n_s
    idx = jnp.concatenate([idx_p[:, :TOP_K], idx_s[:, :TOP_K]], axis=0)
    rank = jnp.concatenate([rk_p[:, :TOP_K], rk_s[:, :TOP_K]], axis=0)
    counts = cnt[0, :N_EXPERTS].astype(I32)
    padded = (counts + MOE_BM - 1) // MOE_BM * MOE_BM
    pad_end = jnp.cumsum(padded)
    pad_start = pad_end - padded
    dest = (pad_start[idx] + rank).reshape(n * TOP_K)
    n_blocks = -(-(n * TOP_K) // MOE_BM) + N_EXPERTS
    blk_row = jnp.arange(n_blocks, dtype=I32) * MOE_BM
    block_e = jnp.minimum(jnp.sum(pad_end[None, :] <= blk_row[:, None], axis=1), N_EXPERTS - 1).astype(I32)
    n_used = (pad_end[-1] // MOE_BM).reshape(1).astype(I32)
    last_block = jnp.maximum(pad_end // MOE_BM - 1, 0).astype(I32)

    xs_sorted = _dispatch(dest, last_block, n_used, xn_p, xn_s, tm=MOE_BM, n_blocks=n_blocks)
    ew = dict(w_gate=w_gate[0], w_up=w_up[0], w_down=w_down[0],
              b_gate=b_gate[0].reshape(N_EXPERTS, 1, D_MODEL), b_up=b_up[0].reshape(N_EXPERTS, 1, D_MODEL),
              b_down=b_down[0].reshape(N_EXPERTS, 1, D_MODEL))
    ys = _experts(block_e, n_used, (pad_end // MOE_BM).astype(I32), xs_sorted, ew, bm=MOE_BM)
    y_p, y_s = _combine(dest, h2_p, h2_s, tw_p, tw_s, row(g_final), ys, tm=COMBINE_TILE)

    shp_p = (1, nb_p, seq, FOX_HEADS, FOX_HEAD_DIM)
    shp_s = (1, nb_s, t_s, FOX_HEADS, FOX_HEAD_DIM)
    tail = slice(SUBLANES - (CONV_WIDTH - 1), SUBLANES)
    return (y_p.reshape(nb_p, seq, D_MODEL), y_s.reshape(nb_s, t_s, D_MODEL),
            k_p.reshape(shp_p), v_p.reshape(shp_p), lf_p.reshape(1, nb_p, seq, FOX_HEADS),
            hl_p.reshape(1, nb_p, LRU_WIDTH), ct_p[:, tail, :].reshape(1, nb_p, CONV_WIDTH - 1, LRU_WIDTH),
            mk_p.reshape(1, nb_p, n_mem, MEM_HEADS, MEM_HEAD_DIM), mv_p.reshape(1, nb_p, n_mem, MEM_HEADS, MEM_HEAD_DIM),
            k_s.reshape(shp_s), v_s.reshape(shp_s), lf_s.reshape(1, nb_s, t_s, FOX_HEADS),
            hl_s.reshape(1, nb_s, LRU_WIDTH), ct_s[:, tail, :].reshape(1, nb_s, CONV_WIDTH - 1, LRU_WIDTH))
```

```python
import functools
import math

import jax
import jax.numpy as jnp
import numpy as np
from jax import lax
from jax.experimental import pallas as pl
from jax.experimental.pallas import tpu as pltpu

F32 = jnp.float32
BF16 = jnp.bfloat16
I32 = jnp.int32

D_MODEL = 1024
FOX_HEADS = 8
FOX_HEAD_DIM = 64
FOX_WIDTH = FOX_HEADS * FOX_HEAD_DIM
LRU_WIDTH = D_MODEL - FOX_WIDTH
LRU_BLOCKS = 8
LRU_C = 8.0
CONV_WIDTH = 4
MEM_HEADS = 4
MEM_HEAD_DIM = D_MODEL // MEM_HEADS
N_EXPERTS = 32
TOP_K = 4
SWIGLU_LIMIT = 7.0
SWIGLU_ALPHA = 1.702
RMS_EPS = 1e-6
NEG_INF = -1e30
LOG2E = math.log2(math.e)

LANES = 128
SUBLANES = 8
CHUNKS = D_MODEL // LANES
TOKEN_TILE = (CHUNKS, SUBLANES, LANES)
AUG = LANES * FOX_HEADS
VMEM_LIMIT = 56 * 1024 * 1024

MIX_TILE = 256
POST_TILE = 512
SAMPLE_POST_TILE = 256
FOX_BQ = 512
MOE_BM = 512
COMBINE_TILE = 256
FOX_SKIP_GAP = 160.0
FOX_STATIC_SHIFT_RANGE = 100.0

C_Q, C_K, C_V, C_F, C_XR, C_G, C_END = 0, 512, 1024, 1536, 1664, 2176, 2688


def _dot(a, b):
    return jnp.dot(a, b, preferred_element_type=F32)


def _dot_nt(a, b):
    return lax.dot_general(a, b, (((1,), (1,)), ((), ())), preferred_element_type=F32)


def _split3(x):
    hi = x.astype(BF16)
    r = x - hi.astype(F32)
    mid = r.astype(BF16)
    lo = (r - mid.astype(F32)).astype(BF16)
    return hi, mid, lo


def _dot3(a, x):
    hi, mid, lo = _split3(x)
    return _dot(a, hi) + _dot(a, mid) + _dot(a, lo)


def _lane_split3(x):
    hi, mid, lo = _split3(x)
    lane = lax.broadcasted_iota(I32, x.shape, 1)
    return jnp.where(lane < 8, hi, jnp.where(lane < 16, mid, lo))


def _softplus(x):
    return jnp.maximum(x, 0.0) + jnp.log1p(jnp.exp(-jnp.abs(x)))


def _rms(x, g):
    return x * lax.rsqrt(jnp.mean(x * x, axis=-1, keepdims=True) + RMS_EPS) * g


def _tri(n, kind):
    r = lax.broadcasted_iota(I32, (n, n), 0)
    c = lax.broadcasted_iota(I32, (n, n), 1)
    m = {"le": c <= r, "lt": c < r, "gt": c > r}[kind]
    return jnp.where(m, 1.0, 0.0).astype(BF16)


def _mix_in_kernel(x_ref, g_ref, w_ref, bfp_ref, eq_ref, ek_ref, cq_ref, ck_ref, cv_ref, hs_ref, hd_ref, hm_ref,
                   cw_ref, cb_ref, wa_ref, ba_ref, wi_ref, bi_ref, lam_ref, cprev_ref, h0_ref,
                   qa_ref, ka_ref, va_ref, ko_ref, vo_ref, vb_ref, lf_ref, f2_ref, qn_ref, kn_ref,
                   lru_ref, hl_ref, ct_ref,
                   fcar, hcar, xp_ref, sa_ref, sb_ref, kbuf, vbuf, ksem, *, tm, pad, streaming):
    step = pl.program_id(0)
    first = step == 0
    slot = step % 2

    def kv_copies(s, row0):
        rows = pl.ds(row0, tm)
        copies = []
        for h in range(FOX_HEADS):
            copies.append(pltpu.make_async_copy(kbuf.at[s, h], ko_ref.at[rows, h, :], ksem.at[s]))
            copies.append(pltpu.make_async_copy(vbuf.at[s, h], vo_ref.at[rows, h, :], ksem.at[s]))
        return copies

    @pl.when(step >= 2)
    def _():
        for c in kv_copies(slot, 0):
            c.wait()

    if streaming:
        @pl.when(first)
        def _():
            fcar[...] = jnp.zeros_like(fcar)
            hcar[...] = jnp.zeros_like(hcar)
            xp_ref[0:SUBLANES, :] = jnp.zeros((SUBLANES, LRU_WIDTH), F32)
    else:
        fcar[...] = jnp.zeros_like(fcar)
        hcar[...] = jnp.broadcast_to(h0_ref[0], hcar.shape)
        xp_ref[0:SUBLANES, :] = cprev_ref[0]

    x = x_ref[...]
    xn = _rms(x, g_ref[...]).astype(BF16)
    z = _dot(xn, w_ref[...])

    zq, zk, zv = z[:, C_Q:C_K], z[:, C_K:C_V], z[:, C_V:C_F]
    for h in range(FOX_HEADS):
        hc = slice(h * FOX_HEAD_DIM, (h + 1) * FOX_HEAD_DIM)
        kbuf[slot, h] = zk[:, hc]
        vbuf[slot, h] = zv[:, hc]
    for c in kv_copies(slot, pl.multiple_of(step * tm, tm)):
        c.start()
    qd, kd, vd = zq.astype(BF16), zk.astype(BF16), zv.astype(BF16)
    vb_ref[...] = vd

    def head_pad(d):
        blocks = []
        for p in range(FOX_HEADS // 2):
            blk = d[:, p * LANES:(p + 1) * LANES]
            blocks += [blk * hm_ref[0:1, :], blk * hm_ref[1:2, :]]
        return jnp.concatenate(blocks, axis=1)

    lane = lax.broadcasted_iota(I32, (tm, LANES), 1)
    lf = jnp.where(lane < 3 * FOX_HEADS, -_softplus(-(z[:, C_F:C_XR] + bfp_ref[...])), 0.0)
    lf_ref[...] = lf[:, :FOX_HEADS]
    cum = _dot3(_tri(tm, "le"), lf) + fcar[0:1, :]
    fcar[...] = jnp.broadcast_to(cum[tm - 1:tm, :], fcar.shape)
    cum2 = cum * LOG2E
    f2_ref[...] = cum2[:, :FOX_HEADS]
    qf, kf = qd.astype(F32), kd.astype(F32)
    diag = _dot((qf * kf).astype(BF16), hd_ref[...]).astype(BF16)
    hi, mid, lo = _split3(cum2)
    bias = jnp.where(lane < 8, hi, jnp.where(lane < 16, mid, jnp.where(lane < 24, lo, diag)))
    qa_ref[...] = head_pad(qd) + (_dot(bias, eq_ref[...]) + cq_ref[...]).astype(BF16)
    ka_ref[...] = head_pad(kd) + (ck_ref[...] - _dot(bias, ek_ref[...])).astype(BF16)
    va_ref[...] = head_pad(vd) + cv_ref[...].astype(BF16)
    qn_ref[...] = jnp.broadcast_to(jnp.max(_dot((qf * qf).astype(BF16), hs_ref[...]), axis=0, keepdims=True),
                                   qn_ref.shape)
    kn_ref[...] = jnp.broadcast_to(jnp.max(_dot((kf * kf).astype(BF16), hs_ref[...]), axis=0, keepdims=True),
                                   kn_ref.shape)

    xr = z[:, C_XR:C_G]
    xp_ref[SUBLANES:SUBLANES + tm, :] = xr
    xc = cb_ref[...] + xr * cw_ref[CONV_WIDTH - 1:CONV_WIDTH, :]
    for j in range(CONV_WIDTH - 1):
        sh = CONV_WIDTH - 1 - j
        xc = xc + xp_ref[SUBLANES - sh:SUBLANES - sh + tm, :] * cw_ref[j:j + 1, :]
    tail = xp_ref[tm:tm + SUBLANES, :]
    ct_ref[0] = tail
    xp_ref[0:SUBLANES, :] = tail

    xcb = xc.astype(BF16)
    r = jax.nn.sigmoid(_dot(xcb, wa_ref[...]) + ba_ref[...])
    ig = jax.nn.sigmoid(_dot(xcb, wi_ref[...]) + bi_ref[...])
    log_a = (-LRU_C) * r * _softplus(-lam_ref[...])
    a = jnp.exp(log_a)
    mult = jnp.sqrt(-jnp.tanh(log_a) * (a * a + 1.0))
    if streaming:
        row = lax.broadcasted_iota(I32, (tm, LRU_WIDTH), 0)
        mult = jnp.where(jnp.logical_and(row == 0, first), 1.0, mult)
    b = mult * ig * xc

    sa_ref[0:pad, :] = jnp.ones((pad, LRU_WIDTH), F32)
    sb_ref[0:pad, :] = jnp.zeros((pad, LRU_WIDTH), F32)
    d = 1
    while d < tm:
        sa_ref[pad:pad + tm, :] = a
        sb_ref[pad:pad + tm, :] = b
        b = a * sb_ref[pad - d:pad - d + tm, :] + b
        a = a * sa_ref[pad - d:pad - d + tm, :]
        d *= 2
    h = a * hcar[0:1, :] + b
    hlast = h[tm - 1:tm, :]
    hcar[...] = jnp.broadcast_to(hlast, hcar.shape)
    hl_ref[0] = hlast
    lru_ref[...] = (h * jax.nn.gelu(z[:, C_G:C_END])).astype(BF16)

    last = pl.num_programs(0) - 1

    @pl.when(jnp.logical_and(step == last, step >= 1))
    def _():
        for c in kv_copies(1 - slot, 0):
            c.wait()

    @pl.when(step == last)
    def _():
        for c in kv_copies(slot, 0):
            c.wait()


def _mix_in(x, cprev, h0, wts, *, tm, streaming):
    n = x.shape[0]
    steps = n // tm
    nseg = 1 if streaming else steps
    pad = max(tm // 2, SUBLANES)
    seg = (lambda i: (0, 0, 0)) if streaming else (lambda i: (i, 0, 0))
    full = lambda a: pl.BlockSpec(a.shape, lambda i: (0,) * a.ndim)
    rows = lambda w: pl.BlockSpec((tm, w), lambda i: (i, 0))
    names = ("g_mix", "w_all", "bf_pad", "eq", "ek", "cq", "ck", "cv", "hsel", "hdiag", "hmask", "conv_w", "conv_b",
             "wa", "ba", "wi", "bi", "lam")
    ws = [wts[k] for k in names]
    sds = jax.ShapeDtypeStruct
    out_shape = (
        sds((n, AUG), BF16), sds((n, AUG), BF16), sds((n, AUG), BF16),
        sds((n, FOX_HEADS, FOX_HEAD_DIM), F32), sds((n, FOX_HEADS, FOX_HEAD_DIM), F32), sds((n, FOX_WIDTH), BF16),
        sds((n, FOX_HEADS), F32), sds((n, FOX_HEADS), F32),
        sds((steps * SUBLANES, LANES), F32), sds((steps * SUBLANES, LANES), F32),
        sds((n, LRU_WIDTH), BF16),
        sds((nseg, 1, LRU_WIDTH), F32), sds((nseg, SUBLANES, LRU_WIDTH), F32),
    )
    out_specs = (
        rows(AUG), rows(AUG), rows(AUG), pl.BlockSpec(memory_space=pl.ANY), pl.BlockSpec(memory_space=pl.ANY),
        rows(FOX_WIDTH), rows(FOX_HEADS), rows(FOX_HEADS),
        pl.BlockSpec((SUBLANES, LANES), lambda i: (i, 0)), pl.BlockSpec((SUBLANES, LANES), lambda i: (i, 0)),
        rows(LRU_WIDTH),
        pl.BlockSpec((1, 1, LRU_WIDTH), seg), pl.BlockSpec((1, SUBLANES, LRU_WIDTH), seg),
    )
    in_specs = [rows(D_MODEL)] + [full(w) for w in ws] + [
        pl.BlockSpec((1, SUBLANES, LRU_WIDTH), seg), pl.BlockSpec((1, 1, LRU_WIDTH), seg)]
    return pl.pallas_call(
        functools.partial(_mix_in_kernel, tm=tm, pad=pad, streaming=streaming),
        grid=(steps,), in_specs=in_specs, out_specs=out_specs, out_shape=out_shape,
        scratch_shapes=[
            pltpu.VMEM((SUBLANES, LANES), F32), pltpu.VMEM((SUBLANES, LRU_WIDTH), F32),
            pltpu.VMEM((tm + SUBLANES, LRU_WIDTH), F32),
            pltpu.VMEM((pad + tm, LRU_WIDTH), F32), pltpu.VMEM((pad + tm, LRU_WIDTH), F32),
            pltpu.VMEM((2, FOX_HEADS, tm, FOX_HEAD_DIM), F32), pltpu.VMEM((2, FOX_HEADS, tm, FOX_HEAD_DIM), F32),
            pltpu.SemaphoreType.DMA((2,))],
        compiler_params=pltpu.CompilerParams(dimension_semantics=("arbitrary",), vmem_limit_bytes=VMEM_LIMIT),
        name="mix_in_stream" if streaming else "mix_in_segments",
    )(x, *ws, cprev, h0)


def _fox_prompt_kernel(nb_ref, st_ref, q_ref, k_ref, v_ref, o_ref, m_ref, acc_ref, *, bq):
    qi = pl.program_id(1)
    plan = pl.program_id(0) * pl.num_programs(1) + qi
    n_past = nb_ref[plan]
    cols = [slice(LANES * c, LANES * (c + 1)) for c in range(2)]
    qs = [q_ref[:, cols[c]] for c in range(2)]
    acc_ref[...] = jnp.zeros(acc_ref.shape, F32)

    def scores(c, start, causal):
        s = _dot_nt(qs[c], k_ref[pl.ds(start, bq), cols[c]])
        if causal:
            r = lax.broadcasted_iota(I32, (bq, bq), 0)
            cc = lax.broadcasted_iota(I32, (bq, bq), 1)
            s = jnp.where(cc <= r, s, NEG_INF)
        return s

    def static_block(start, causal):
        for c in range(2):
            p = jnp.exp2(scores(c, start, causal)).astype(BF16)
            acc_ref[c] += _dot(p, v_ref[pl.ds(start, bq), cols[c]])

    def online_block(start, causal):
        for c in range(2):
            s = scores(c, start, causal)
            m_old = m_ref[c]
            m_new = jnp.maximum(m_old, jnp.max(s, axis=1, keepdims=True))
            p = jnp.exp2(s - m_new).astype(BF16)
            acc_ref[c] = jnp.exp2(m_old - m_new) * acc_ref[c] + _dot(p, v_ref[pl.ds(start, bq), cols[c]])
            m_ref[c] = m_new

    def run(block):
        block(pl.multiple_of(qi * bq, bq), True)

        def body(t, carry):
            block(pl.multiple_of((qi - 1 - t) * bq, bq), False)
            return carry

        lax.fori_loop(0, n_past, body, 0)

    @pl.when(st_ref[plan] == 1)
    def _():
        run(static_block)

    @pl.when(st_ref[plan] != 1)
    def _():
        m_ref[...] = jnp.full(m_ref.shape, NEG_INF, F32)
        run(online_block)

    a0, a1 = acc_ref[0], acc_ref[1]
    o0 = a0 / a0[:, FOX_HEAD_DIM:FOX_HEAD_DIM + 1]
    o1 = a1 / a1[:, 0:1]
    lane = lax.broadcasted_iota(I32, (bq, LANES), 1)
    o_ref[...] = jnp.where(lane < FOX_HEAD_DIM, o0, o1).astype(BF16)


def _fox_prompt(n_past, static_ok, qa, ka, va, *, bq):
    s = qa.shape[0]
    pairs = FOX_HEADS // 2
    return pl.pallas_call(
        functools.partial(_fox_prompt_kernel, bq=bq),
        grid_spec=pltpu.PrefetchScalarGridSpec(
            num_scalar_prefetch=2, grid=(pairs, s // bq),
            in_specs=[pl.BlockSpec((bq, 2 * LANES), lambda p, i, nb, st: (i, p)),
                      pl.BlockSpec((s, 2 * LANES), lambda p, i, nb, st: (0, p)),
                      pl.BlockSpec((s, 2 * LANES), lambda p, i, nb, st: (0, p))],
            out_specs=pl.BlockSpec((bq, LANES), lambda p, i, nb, st: (i, p)),
            scratch_shapes=[pltpu.VMEM((2, bq, 1), F32), pltpu.VMEM((2, bq, LANES), F32)]),
        out_shape=jax.ShapeDtypeStruct((s, FOX_WIDTH), BF16),
        compiler_params=pltpu.CompilerParams(dimension_semantics=("arbitrary", "arbitrary"),
                                             vmem_limit_bytes=VMEM_LIMIT),
        name="fox_prompt",
    )(n_past, static_ok, qa, ka, va)


def _fox_plan(f2, qn, kn, *, bq, tile):
    s = f2.shape[0]
    nq = s // bq
    per = lambda a: a.reshape(s // tile, SUBLANES, LANES)[:, 0, :FOX_HEADS]
    qnorm = jnp.sqrt(jnp.max(per(qn).reshape(nq, bq // tile, FOX_HEADS), axis=1))
    knorm = jnp.sqrt(jnp.max(per(kn), axis=0))
    spread = 2.0 * 1.02 * qnorm * knorm[None, :]
    thr = spread + FOX_SKIP_GAP
    f_first = f2[0::bq]
    f_last = f2[bq - 1::bq]
    gap = f_last[None, :, :] - f_first[:, None, :]
    before = (jnp.arange(nq)[None, :] < jnp.arange(nq)[:, None])[:, :, None]
    need = jnp.sum(jnp.logical_and(before, gap < thr[:, None, :]), axis=1)
    pair = lambda a: a.reshape(nq, FOX_HEADS // 2, 2)
    n_past = jnp.max(pair(need), axis=2).T.reshape(-1).astype(I32)
    static_ok = jnp.all(pair(spread) <= FOX_STATIC_SHIFT_RANGE, axis=2).T.reshape(-1).astype(I32)
    return n_past, static_ok


def _fox_sample_kernel(q_ref, kn_ref, vn_ref, ck_ref, cv_ref, clf_ref, ek_ref, ckc_ref, o_ref, g_ref,
                       *, t, past, chunk):
    car = jnp.zeros((1, LANES), F32)
    upper = _tri(chunk, "gt")
    for ci in reversed(range(past // chunk)):
        lf = clf_ref[0, ci * chunk:(ci + 1) * chunk, :]
        g_ref[ci * chunk:(ci + 1) * chunk, :] = _dot3(upper, lf) + car
        car = car + jnp.sum(lf, axis=0, keepdims=True)
    gs = _lane_split3(g_ref[...] * LOG2E)

    lane = lax.broadcasted_iota(I32, (past, LANES), 1)
    r = lax.broadcasted_iota(I32, (t, t), 0)
    cc = lax.broadcasted_iota(I32, (t, t), 1)
    olane = lax.broadcasted_iota(I32, (t, LANES), 1)
    for p in range(FOX_HEADS // 2):
        pc = slice(LANES * p, LANES * (p + 1))
        kpair = ck_ref[0, :, pc]
        vpast = cv_ref[0, :, pc].astype(BF16)
        vnew = vn_ref[:, pc]
        pair_cols = slice(2 * LANES * p, 2 * LANES * (p + 1))
        pair_extras = _dot(gs, ek_ref[:, pair_cols]) + ckc_ref[:, pair_cols]
        outs = []
        for c in range(2):
            h = 2 * p + c
            hc = slice(LANES * h, LANES * (h + 1))
            main = (lane < FOX_HEAD_DIM) if c == 0 else (lane >= FOX_HEAD_DIM)
            extras = pair_extras[:, LANES * c:LANES * (c + 1)]
            kpast = (jnp.where(main, kpair, 0.0) + extras).astype(BF16)
            q = q_ref[:, hc]
            sp = _dot_nt(q, kpast)
            sn = jnp.where(cc <= r, _dot_nt(q, kn_ref[:, hc]), NEG_INF)
            m = jnp.maximum(jnp.max(sp, axis=1, keepdims=True), jnp.max(sn, axis=1, keepdims=True))
            pp = jnp.exp2(sp - m)
            pn = jnp.exp2(sn - m)
            l = jnp.sum(pp, axis=1, keepdims=True) + jnp.sum(pn, axis=1, keepdims=True)
            outs.append((_dot(pp.astype(BF16), vpast) + _dot(pn.astype(BF16), vnew)) / l)
        o_ref[:, pc] = jnp.where(olane < FOX_HEAD_DIM, outs[0], outs[1]).astype(BF16)


def _fox_sample(qa, ka, vb, cache_k, cache_v, cache_lf3, ek, ck, *, t):
    nb, past = cache_k.shape[0], cache_k.shape[1]
    full = lambda a: pl.BlockSpec(a.shape, lambda b: (0,) * a.ndim)
    return pl.pallas_call(
        functools.partial(_fox_sample_kernel, t=t, past=past, chunk=256),
        grid=(nb,),
        in_specs=[pl.BlockSpec((t, AUG), lambda b: (b, 0)), pl.BlockSpec((t, AUG), lambda b: (b, 0)),
                  pl.BlockSpec((t, FOX_WIDTH), lambda b: (b, 0)),
                  pl.BlockSpec((1, past, FOX_WIDTH), lambda b: (b, 0, 0)),
                  pl.BlockSpec((1, past, FOX_WIDTH), lambda b: (b, 0, 0)),
                  pl.BlockSpec((1, past, LANES), lambda b: (b, 0, 0)),
                  full(ek), full(ck)],
        out_specs=pl.BlockSpec((t, FOX_WIDTH), lambda b: (b, 0)),
        out_shape=jax.ShapeDtypeStruct((nb * t, FOX_WIDTH), BF16),
        scratch_shapes=[pltpu.VMEM((past, LANES), F32)],
        compiler_params=pltpu.CompilerParams(dimension_semantics=("arbitrary",), vmem_limit_bytes=VMEM_LIMIT),
        name="fox_sample",
    )(qa, ka, vb, cache_k, cache_v, cache_lf3, ek, ck)


def _mem_kv_kernel(m_ref, g_ref, wk_ref, wv_ref, k_ref, v_ref):
    mn = _rms(m_ref[...], g_ref[...]).astype(BF16)
    k_ref[...] = _dot(mn, wk_ref[...])
    v_ref[...] = _dot(mn, wv_ref[...])


def _mem_kv(mem, g, wk, wv):
    n = mem.shape[0]
    return pl.pallas_call(
        _mem_kv_kernel,
        out_shape=(jax.ShapeDtypeStruct((n, D_MODEL), F32), jax.ShapeDtypeStruct((n, D_MODEL), F32)),
        compiler_params=pltpu.CompilerParams(vmem_limit_bytes=VMEM_LIMIT),
        name="mem_kv",
    )(mem, g, wk, wv)


def _post_kernel(h_ref, fo_ref, lru_ref, wo_ref, gc_ref, wcq_ref, mk_ref, mv_ref, wco_ref,
                 gm_ref, wr_ref, br_ref, cnt_in_ref,
                 h2_ref, xn_ref, idx_ref, tw_ref, rank_ref, cnt_ref, car_ref, *mem_scratch, tm, seg, cached_memory):
    step = pl.program_id(0)
    nseg = tm // seg

    @pl.when(step == 0)
    def _():
        car_ref[...] = cnt_in_ref[...]

    if cached_memory:
        mbuf, msem = mem_scratch
        slot = step % 2

        def mem_copies(at_step, s):
            return [pltpu.make_async_copy(ref.at[0, at_step * nseg + j, :, hd, :], mbuf.at[s, j, t, hd], msem.at[s])
                    for j in range(nseg) for t, ref in enumerate((mk_ref, mv_ref)) for hd in range(MEM_HEADS)]

        @pl.when(step == 0)
        def _():
            for c in mem_copies(0, 0):
                c.start()

        @pl.when(step + 1 < pl.num_programs(0))
        def _():
            for c in mem_copies(step + 1, 1 - slot):
                c.start()

    h1 = h_ref[...] + _dot(fo_ref[...], wo_ref[0:FOX_WIDTH, :]) + _dot(lru_ref[...], wo_ref[FOX_WIDTH:D_MODEL, :])

    q = _dot(_rms(h1, gc_ref[...]).astype(BF16), wcq_ref[...])
    if cached_memory:
        for c in mem_copies(0, slot):
            c.wait()
    segs = []
    for j in range(nseg):
        rows = slice(j * seg, (j + 1) * seg)
        heads = []
        for hd in range(MEM_HEADS):
            hc = slice(MEM_HEAD_DIM * hd, MEM_HEAD_DIM * (hd + 1))
            mk = mbuf[slot, j, 0, hd] if cached_memory else mk_ref[0, :, hc]
            mv = mbuf[slot, j, 1, hd] if cached_memory else mv_ref[0, :, hc]
            s = _dot_nt(q[rows, hc].astype(BF16), mk.astype(BF16)) * (MEM_HEAD_DIM ** -0.5)
            p = jnp.exp(s - jnp.max(s, axis=1, keepdims=True))
            o = _dot(p.astype(BF16), mv.astype(BF16)) / jnp.sum(p, axis=1, keepdims=True)
            heads.append(o.astype(BF16))
        segs.append(jnp.concatenate(heads, axis=1))
    h2 = h1 + _dot(segs[0] if nseg == 1 else jnp.concatenate(segs, axis=0), wco_ref[...])
    h2_ref[...] = h2

    xn = _rms(h2, gm_ref[...])
    for c in range(D_MODEL // LANES):
        xn_ref[:, c] = xn[:, c * LANES:(c + 1) * LANES].reshape(tm // SUBLANES, SUBLANES, LANES)

    lane = lax.broadcasted_iota(I32, (tm, LANES), 1).astype(F32)
    logits = jnp.where(lane < N_EXPERTS, _dot(xn.astype(BF16), wr_ref[...]) + br_ref[...], -jnp.inf)
    vals, idxs = [], []
    for _ in range(TOP_K):
        mx = jnp.max(logits, axis=1, keepdims=True)
        ix = jnp.min(jnp.where(logits == mx, lane, float(LANES)), axis=1, keepdims=True)
        vals.append(mx)
        idxs.append(ix)
        logits = jnp.where(lane == ix, -jnp.inf, logits)
    es = [jnp.exp(v - vals[0]) for v in vals]
    den = es[0] + es[1] + es[2] + es[3]

    onehot = jnp.zeros((tm, LANES), F32)
    for ix in idxs:
        onehot = onehot + jnp.where(lane == ix, 1.0, 0.0)
    before = _dot(_tri(tm, "lt"), onehot.astype(BF16)) + car_ref[0:1, :]
    car = car_ref[0:1, :] + jnp.sum(onehot, axis=0, keepdims=True)
    car_ref[...] = jnp.broadcast_to(car, car_ref.shape)
    cnt_ref[...] = jnp.broadcast_to(car, cnt_ref.shape)

    idx_o = jnp.zeros((tm, LANES), F32)
    tw_o = jnp.zeros((tm, LANES), F32)
    rk_o = jnp.zeros((tm, LANES), F32)
    for j in range(TOP_K):
        rk = jnp.sum(jnp.where(lane == idxs[j], before, 0.0), axis=1, keepdims=True)
        idx_o = jnp.where(lane == j, idxs[j], idx_o)
        tw_o = jnp.where(lane == j, es[j] / den, tw_o)
        rk_o = jnp.where(lane == j, rk, rk_o)
    idx_ref[...] = idx_o.astype(I32)
    tw_ref[...] = tw_o
    rank_ref[...] = rk_o.astype(I32)


def _post(h, fo, lru, mk, mv, cnt_in, wts, *, tm, seg, cached_memory):
    n = h.shape[0]
    full = lambda a: pl.BlockSpec(a.shape, lambda i: (0,) * a.ndim)
    rows = lambda w: pl.BlockSpec((tm, w), lambda i: (i, 0))
    if cached_memory:
        nm = mk.shape[2]
        mem = pl.BlockSpec(memory_space=pl.ANY)
        mem_scratch = [pltpu.VMEM((2, tm // seg, 2, MEM_HEADS, nm, MEM_HEAD_DIM), F32),
                       pltpu.SemaphoreType.DMA((2,))]
    else:
        mem = full(mk)
        mem_scratch = []
    w = [wts[k] for k in ("w_out", "g_cross", "w_cq")]
    w2 = [wts[k] for k in ("w_co", "g_moe", "w_router", "b_router")]
    return pl.pallas_call(
        functools.partial(_post_kernel, tm=tm, seg=seg, cached_memory=cached_memory),
        grid=(n // tm,),
        in_specs=[rows(D_MODEL), rows(FOX_WIDTH), rows(LRU_WIDTH)] + [full(a) for a in w] + [mem, mem]
                 + [full(a) for a in w2] + [full(cnt_in)],
        out_specs=(rows(D_MODEL), pl.BlockSpec((tm // SUBLANES,) + TOKEN_TILE, lambda i: (i, 0, 0, 0)),
                   rows(LANES), rows(LANES), rows(LANES),
                   pl.BlockSpec((SUBLANES, LANES), lambda i: (0, 0))),
        out_shape=(jax.ShapeDtypeStruct((n, D_MODEL), F32), jax.ShapeDtypeStruct((n // SUBLANES,) + TOKEN_TILE, F32),
                   jax.ShapeDtypeStruct((n, LANES), I32), jax.ShapeDtypeStruct((n, LANES), F32),
                   jax.ShapeDtypeStruct((n, LANES), I32), jax.ShapeDtypeStruct((SUBLANES, LANES), F32)),
        scratch_shapes=[pltpu.VMEM((SUBLANES, LANES), F32)] + mem_scratch,
        compiler_params=pltpu.CompilerParams(dimension_semantics=("arbitrary",), vmem_limit_bytes=VMEM_LIMIT),
        name="post_batch_mem" if cached_memory else "post_shared_mem",
    )(h, fo, lru, *w, mk, mv, *w2, cnt_in)


STAGES = 3


def _dispatch_kernel(dest_ref, last_ref, nu_ref, xp_ref, xs_ref, out_ref, zero_ref, stage, sems, gsem, zsem,
                     *, tm, steps_p, n_blocks):
    i = pl.program_id(0)

    def zero_copy(b):
        return pltpu.make_async_copy(zero_ref, out_ref.at[pl.ds(pl.multiple_of(b * tm, tm), tm)], zsem)

    @pl.when(i == 0)
    def _():
        zero_ref[...] = jnp.zeros(zero_ref.shape, F32)
        n_tail = n_blocks - nu_ref[0]

        def start_e(e, c):
            zero_copy(last_ref[e]).start()
            return c

        def start_t(b, c):
            zero_copy(nu_ref[0] + b).start()
            return c

        def wait_one(b, c):
            zero_copy(0).wait()
            return c

        lax.fori_loop(0, N_EXPERTS, start_e, 0)
        lax.fori_loop(0, n_tail, start_t, 0)
        lax.fori_loop(0, N_EXPERTS + n_tail, wait_one, 0)

    groups = tm // SUBLANES
    steps = pl.num_programs(0)

    def stage_copy(src_ref, tile, slot):
        return pltpu.make_async_copy(src_ref.at[pl.ds(tile * groups, groups)], stage.at[slot], gsem.at[slot])

    def start_stage(tile, slot):
        @pl.when(tile < steps_p)
        def _():
            stage_copy(xp_ref, tile, slot).start()

        @pl.when(tile >= steps_p)
        def _():
            stage_copy(xs_ref, tile - steps_p, slot).start()

    @pl.when(i == 0)
    def _():
        start_stage(0, 0)

    @pl.when(i + 1 < steps)
    def _():
        start_stage(i + 1, (i + 1) % STAGES)

    slot = i % STAGES
    stage_copy(xp_ref, 0, slot).wait()

    base = i * (tm * TOP_K)

    def start(g, c):
        for u in range(SUBLANES):
            for j in range(TOP_K):
                d = dest_ref[base + (g * SUBLANES + u) * TOP_K + j]
                pltpu.make_async_copy(stage.at[slot, g, :, u, :], out_ref.at[d],
                                      sems.at[i % 2]).start(priority=j % 2)
        return c

    lax.fori_loop(0, groups, start, 0)

    def drain(parity):
        for _ in range(TOP_K):
            pltpu.make_async_copy(out_ref.at[pl.ds(0, tm)], out_ref.at[pl.ds(0, tm)], sems.at[parity]).wait()

    @pl.when(i > 0)
    def _():
        drain((i - 1) % 2)

    @pl.when(i == steps - 1)
    def _():
        drain(i % 2)


def _dispatch(dest_flat, last_block, n_used, xn_p, xn_s, *, tm, n_blocks):
    steps_p, steps_s = xn_p.shape[0] * SUBLANES // tm, xn_s.shape[0] * SUBLANES // tm
    return pl.pallas_call(
        functools.partial(_dispatch_kernel, tm=tm, steps_p=steps_p, n_blocks=n_blocks),
        grid_spec=pltpu.PrefetchScalarGridSpec(
            num_scalar_prefetch=3, grid=(steps_p + steps_s,),
            in_specs=[pl.BlockSpec(memory_space=pl.ANY), pl.BlockSpec(memory_space=pl.ANY)],
            out_specs=pl.BlockSpec(memory_space=pl.ANY),
            scratch_shapes=[pltpu.VMEM((tm, CHUNKS, LANES), F32),
                            pltpu.VMEM((STAGES, tm // SUBLANES) + TOKEN_TILE, F32),
                            pltpu.SemaphoreType.DMA((2,)), pltpu.SemaphoreType.DMA((STAGES,)),
                            pltpu.SemaphoreType.DMA(())]),
        out_shape=jax.ShapeDtypeStruct((n_blocks * tm, CHUNKS, LANES), F32),
        compiler_params=pltpu.CompilerParams(dimension_semantics=("arbitrary",), vmem_limit_bytes=VMEM_LIMIT),
        name="moe_dispatch",
    )(dest_flat, last_block, n_used, xn_p, xn_s)


def _rows_from_tiles(view, rows):
    return jnp.concatenate([view[:, c].reshape(rows, LANES) for c in range(CHUNKS)], axis=1)


def _rows_to_tiles(view, val, rows):
    for c in range(CHUNKS):
        view[:, c] = val[:, c * LANES:(c + 1) * LANES].reshape(rows // SUBLANES, SUBLANES, LANES)


def _expert_kernel(be_ref, nu_ref, nx_ref, xs_ref, wg_ref, bg_ref, wu_ref, bu_ref, wd_ref, bd_ref, ys_ref,
                   wgb, wub, wdb, wbuf, xbuf, ybuf, seq, wsem, xsem, ysem, *, bm):
    i = pl.program_id(0)
    steps = pl.num_programs(0)
    groups = bm // SUBLANES
    slot = i % 2
    prev = be_ref[jnp.maximum(i - 1, 0)]
    fresh = jnp.logical_or(i == 0, be_ref[i] != prev)
    live = i < nu_ref[0]

    def w_copies(e, s):
        return [pltpu.make_async_copy(w_ref.at[e], wbuf.at[s, k], wsem.at[s])
                for k, w_ref in enumerate((wg_ref, wu_ref, wd_ref))]

    @pl.when(i == 0)
    def _():
        seq[0] = 0
        for c in w_copies(be_ref[0], 0):
            c.start()

    def in_copies(blk, s):
        g0 = pl.multiple_of(blk * groups, groups)
        return [pltpu.make_async_copy(xs_ref.at[pl.ds(g0, groups), u], xbuf.at[s, :, :, u, :], xsem.at[s])
                for u in range(SUBLANES)]

    def out_copies(blk, s):
        g0 = pl.multiple_of(blk * groups, groups)
        return [pltpu.make_async_copy(ybuf.at[s, :, :, u, :], ys_ref.at[pl.ds(g0, groups), u], ysem.at[s])
                for u in range(SUBLANES)]

    @pl.when(i == 0)
    def _():
        for c in in_copies(0, 0):
            c.start()

    @pl.when(i + 1 < nu_ref[0])
    def _():
        for c in in_copies(i + 1, (i + 1) % 2):
            c.start()

    @pl.when(i >= 2)
    def _():
        for c in out_copies(0, slot):
            c.wait()

    @pl.when(jnp.logical_and(live, fresh))
    def _():
        s = seq[0] % 2
        for c in w_copies(0, s):
            c.wait()
        for static_s in range(2):
            @pl.when(s == static_s)
            def _():
                wgb[...] = wbuf[static_s, 0].astype(BF16)
                wub[...] = wbuf[static_s, 1].astype(BF16)
                wdb[...] = wbuf[static_s, 2].astype(BF16)

        nxt = nx_ref[be_ref[i]]

        @pl.when(nxt < nu_ref[0])
        def _():
            for c in w_copies(be_ref[jnp.minimum(nxt, steps - 1)], 1 - s):
                c.start()

        seq[0] = seq[0] + 1

    @pl.when(live)
    def _():
        for c in in_copies(0, slot):
            c.wait()
        x = _rows_from_tiles(xbuf.at[slot], bm).astype(BF16)
        g = jnp.minimum(_dot(x, wgb[...]) + bg_ref[0], SWIGLU_LIMIT)
        u = jnp.clip(_dot(x, wub[...]) + bu_ref[0], -SWIGLU_LIMIT, SWIGLU_LIMIT)
        hdn = g * jax.nn.sigmoid(SWIGLU_ALPHA * g) * (u + 1.0)
        _rows_to_tiles(ybuf.at[slot], _dot(hdn.astype(BF16), wdb[...]) + bd_ref[0], bm)

    @pl.when(jnp.logical_not(live))
    def _():
        ybuf[slot] = jnp.zeros(ybuf.shape[1:], F32)

    for c in out_copies(i, slot):
        c.start()

    @pl.when(jnp.logical_and(i == steps - 1, i >= 1))
    def _():
        for c in out_copies(0, 1 - slot):
            c.wait()

    @pl.when(i == steps - 1)
    def _():
        for c in out_copies(0, slot):
            c.wait()


def _experts(block_e, n_used, next_first, xs, wts, *, bm):
    p = xs.shape[0]
    grouped = (p // SUBLANES, SUBLANES, CHUNKS, LANES)
    anyspace = pl.BlockSpec(memory_space=pl.ANY)
    bspec = pl.BlockSpec((1, 1, D_MODEL), lambda i, be, nu, nx: (be[i], 0, 0))
    tiles = (2, bm // SUBLANES) + TOKEN_TILE
    ys = pl.pallas_call(
        functools.partial(_expert_kernel, bm=bm),
        grid_spec=pltpu.PrefetchScalarGridSpec(
            num_scalar_prefetch=3, grid=(p // bm,),
            in_specs=[anyspace, anyspace, bspec, anyspace, bspec, anyspace, bspec],
            out_specs=anyspace,
            scratch_shapes=[pltpu.VMEM((D_MODEL, D_MODEL), BF16)] * 3 + [
                pltpu.VMEM((2, 3, D_MODEL, D_MODEL), F32), pltpu.VMEM(tiles, F32), pltpu.VMEM(tiles, F32),
                pltpu.SMEM((1,), I32),
                pltpu.SemaphoreType.DMA((2,)), pltpu.SemaphoreType.DMA((2,)), pltpu.SemaphoreType.DMA((2,))]),
        out_shape=jax.ShapeDtypeStruct(grouped, F32),
        compiler_params=pltpu.CompilerParams(dimension_semantics=("arbitrary",), vmem_limit_bytes=VMEM_LIMIT),
        name="moe_experts",
    )(block_e, n_used, next_first, xs.reshape(grouped), wts["w_gate"], wts["b_gate"], wts["w_up"], wts["b_up"],
      wts["w_down"], wts["b_down"])
    return ys.reshape(p, CHUNKS, LANES)


def _combine_kernel(dest_ref, hp_ref, hs_ref, twp_ref, tws_ref, gf_ref, ys_ref, yp_ref, ysm_ref, buf, sems,
                    *, tm, steps_p):
    i = pl.program_id(0)
    slot = i % 2

    def issue(tile, into):
        base = tile * (tm * TOP_K)

        def start(g, c):
            for u in range(SUBLANES):
                for j in range(TOP_K):
                    d = dest_ref[base + (g * SUBLANES + u) * TOP_K + j]
                    pltpu.make_async_copy(ys_ref.at[d], buf.at[into, j, g, :, u, :],
                                          sems.at[into]).start(priority=j % 2)
            return c

        lax.fori_loop(0, tm // SUBLANES, start, 0)

    @pl.when(i == 0)
    def _():
        issue(0, 0)

    @pl.when(i + 1 < pl.num_programs(0))
    def _():
        issue(i + 1, (i + 1) % 2)

    for j in range(TOP_K):
        pltpu.make_async_copy(ys_ref.at[pl.ds(0, tm)], ys_ref.at[pl.ds(0, tm)], sems.at[slot]).wait()

    def finish(h_ref, tw_ref, y_ref):
        tw = tw_ref[...]
        y = h_ref[...]
        for j in range(TOP_K):
            y = y + _rows_from_tiles(buf.at[slot, j], tm) * tw[:, j:j + 1]
        y_ref[...] = _rms(y, gf_ref[...])

    @pl.when(i < steps_p)
    def _():
        finish(hp_ref, twp_ref, yp_ref)

    @pl.when(i >= steps_p)
    def _():
        finish(hs_ref, tws_ref, ysm_ref)


def _combine(dest_flat, h2_p, h2_s, tw_p, tw_s, g_final, ys, *, tm):
    steps_p, steps_s = h2_p.shape[0] // tm, h2_s.shape[0] // tm
    pblk = lambda w: pl.BlockSpec((tm, w), lambda i, d: (jnp.minimum(i, steps_p - 1), 0))
    sblk = lambda w: pl.BlockSpec((tm, w), lambda i, d: (jnp.maximum(i - steps_p, 0), 0))
    return pl.pallas_call(
        functools.partial(_combine_kernel, tm=tm, steps_p=steps_p),
        grid_spec=pltpu.PrefetchScalarGridSpec(
            num_scalar_prefetch=1, grid=(steps_p + steps_s,),
            in_specs=[pblk(D_MODEL), sblk(D_MODEL), pblk(LANES), sblk(LANES),
                      pl.BlockSpec((1, D_MODEL), lambda i, d: (0, 0)),
                      pl.BlockSpec(memory_space=pl.ANY)],
            out_specs=(pblk(D_MODEL), sblk(D_MODEL)),
            scratch_shapes=[pltpu.VMEM((2, TOP_K, tm // SUBLANES) + TOKEN_TILE, F32),
                            pltpu.SemaphoreType.DMA((2,))]),
        out_shape=(jax.ShapeDtypeStruct(h2_p.shape, F32), jax.ShapeDtypeStruct(h2_s.shape, F32)),
        compiler_params=pltpu.CompilerParams(dimension_semantics=("arbitrary",), vmem_limit_bytes=VMEM_LIMIT),
        name="moe_combine",
    )(dest_flat, h2_p, h2_s, tw_p, tw_s, g_final, ys)


def _aug_layout():
    main, extra = [], []
    for h in range(FOX_HEADS):
        even = h % 2 == 0
        main.append(LANES * h + (0 if even else FOX_HEAD_DIM))
        extra.append(LANES * h + (FOX_HEAD_DIM if even else 0))
    return main, extra


def _prep_mix_weights(g_mix, w_in, b_f, conv_w, conv_b, w_a, b_a, w_i, b_i, lam):
    _, extra = _aug_layout()
    wq = w_in[:, 0:FOX_WIDTH] * (FOX_HEAD_DIM ** -0.5 * LOG2E)
    wf = w_in[:, 3 * FOX_WIDTH:3 * FOX_WIDTH + FOX_HEADS]
    wf_pad = jnp.concatenate([wf, wf, wf, jnp.zeros((D_MODEL, LANES - 3 * FOX_HEADS), F32)], axis=1)
    w_all = jnp.concatenate([wq, w_in[:, FOX_WIDTH:3 * FOX_WIDTH], wf_pad, w_in[:, 3 * FOX_WIDTH + FOX_HEADS:]],
                            axis=1).astype(BF16)
    bf_pad = jnp.concatenate([b_f, b_f, b_f, jnp.zeros((LANES - 3 * FOX_HEADS,), F32)]).reshape(1, LANES)
    eq = np.zeros((LANES, AUG), np.float32)
    ek = np.zeros((LANES, AUG), np.float32)
    cq = np.zeros((1, AUG), np.float32)
    ck = np.zeros((1, AUG), np.float32)
    cv = np.zeros((1, AUG), np.float32)
    hsel = np.zeros((FOX_WIDTH, LANES), np.float32)
    hdiag = np.zeros((FOX_WIDTH, LANES), np.float32)
    hmask = np.zeros((2, LANES), np.float32)
    hmask[0, :FOX_HEAD_DIM] = 1.0
    hmask[1, FOX_HEAD_DIM:] = 1.0
    for h in range(FOX_HEADS):
        hsel[FOX_HEAD_DIM * h:FOX_HEAD_DIM * (h + 1), h] = 1.0
        hdiag[FOX_HEAD_DIM * h:FOX_HEAD_DIM * (h + 1), 3 * FOX_HEADS + h] = 1.0
        cv[0, extra[h]] = 1.0
        eq[3 * FOX_HEADS + h, extra[h] + 6] = -1.0
        ck[0, extra[h] + 6] = 1.0
        for part in range(3):
            eq[part * 8 + h, extra[h] + part] = 1.0
            ek[part * 8 + h, extra[h] + 3 + part] = 1.0
            cq[0, extra[h] + 3 + part] = 1.0
            ck[0, extra[h] + part] = 1.0
    dense = lambda w: jax.scipy.linalg.block_diag(*[w[i] for i in range(LRU_BLOCKS)]).astype(BF16)
    row = lambda a: a.reshape(1, -1)
    b16 = lambda a: jnp.asarray(a, BF16)
    return dict(g_mix=row(g_mix), w_all=w_all, bf_pad=bf_pad, eq=b16(eq), ek=b16(ek),
                cq=jnp.asarray(cq), ck=jnp.asarray(ck), cv=jnp.asarray(cv),
                hsel=b16(hsel), hdiag=b16(hdiag), hmask=b16(hmask), conv_w=conv_w, conv_b=row(conv_b),
                wa=dense(w_a), ba=row(b_a), wi=dense(w_i), bi=row(b_i), lam=row(lam))


def kernel(x_prompt, x_sample, mem_prompt, cache_fox_k, cache_fox_v, cache_fox_logf, state_lru_h, state_conv, cache_mem_k, cache_mem_v, g_mix, w_in, b_f, conv_w, conv_b, w_a, b_a, w_i, b_i, lam, w_out, g_cross, g_mem, w_cq, w_ck, w_cv, w_co, g_moe, w_router, b_router, w_gate, b_gate, w_up, b_up, w_down, b_down, g_final):
    nb_p, seq, _ = x_prompt.shape
    nb_s, t_s, _ = x_sample.shape
    past = cache_fox_k.shape[2]
    n_mem = mem_prompt.shape[1]
    assert nb_p == 1 and g_mix.shape[0] == 1
    n_p, n_s = seq, nb_s * t_s
    row = lambda a: a.reshape(1, -1)

    mw = _prep_mix_weights(g_mix[0], w_in[0], b_f[0], conv_w[0], conv_b[0], w_a[0], b_a[0], w_i[0], b_i[0], lam[0])
    xp = x_prompt.reshape(n_p, D_MODEL)
    xs_ = x_sample.reshape(n_s, D_MODEL)

    zero_prev = jnp.zeros((1, SUBLANES, LRU_WIDTH), F32)
    zero_h = jnp.zeros((1, 1, LRU_WIDTH), F32)
    (qa_p, ka_p, va_p, k_p, v_p, _, lf_p, f2_p, qn_p, kn_p, lru_p, hl_p, ct_p) = _mix_in(
        xp, zero_prev, zero_h, mw, tm=MIX_TILE, streaming=True)
    prev_s = jnp.pad(state_conv[0], ((0, 0), (SUBLANES - (CONV_WIDTH - 1), 0), (0, 0)))
    (qa_s, ka_s, _, k_s, v_s, vb_s, lf_s, _, _, _, lru_s, hl_s, ct_s) = _mix_in(
        xs_, prev_s, state_lru_h[0].reshape(nb_s, 1, LRU_WIDTH), mw, tm=t_s, streaming=False)

    n_past, static_ok = _fox_plan(f2_p, qn_p, kn_p, bq=FOX_BQ, tile=MIX_TILE)
    fo_p = _fox_prompt(n_past, static_ok, qa_p, ka_p, va_p, bq=FOX_BQ)
    lf_c = cache_fox_logf[0]
    clf3 = jnp.concatenate([lf_c, lf_c, lf_c, jnp.zeros((nb_s, past, LANES - 3 * FOX_HEADS), F32)], axis=-1)
    fo_s = _fox_sample(qa_s, ka_s, vb_s, cache_fox_k[0].reshape(nb_s, past, FOX_WIDTH),
                       cache_fox_v[0].reshape(nb_s, past, FOX_WIDTH), clf3, mw["ek"], mw["ck"], t=t_s)

    mk_p, mv_p = _mem_kv(mem_prompt[0], row(g_mem[0]), w_ck[0].astype(BF16), w_cv[0].astype(BF16))

    wr_pad = jnp.pad(w_router[0], ((0, 0), (0, LANES - N_EXPERTS))).astype(BF16)
    br_pad = jnp.pad(b_router[0], (0, LANES - N_EXPERTS)).reshape(1, LANES)
    pw = dict(w_out=w_out[0].astype(BF16), g_cross=row(g_cross[0]), w_cq=w_cq[0].astype(BF16),
              w_co=w_co[0].astype(BF16), g_moe=row(g_moe[0]), w_router=wr_pad, b_router=br_pad)
    cnt0 = jnp.zeros((SUBLANES, LANES), F32)
    h2_p, xn_p, idx_p, tw_p, rk_p, cnt_p = _post(
        xp, fo_p, lru_p, mk_p.reshape(1, n_mem, D_MODEL), mv_p.reshape(1, n_mem, D_MODEL), cnt0, pw,
        tm=POST_TILE, seg=POST_TILE, cached_memory=False)
    h2_s, xn_s, idx_s, tw_s, rk_s, cnt = _post(
        xs_, fo_s, lru_s, cache_mem_k, cache_mem_v, cnt_p, pw, tm=min(SAMPLE_POST_TILE, n_s), seg=t_s,
        cached_memory=True)

    n = n_p + n_s
    idx = jnp.concatenate([idx_p[:, :TOP_K], idx_s[:, :TOP_K]], axis=0)
    rank = jnp.concatenate([rk_p[:, :TOP_K], rk_s[:, :TOP_K]], axis=0)
    counts = cnt[0, :N_EXPERTS].astype(I32)
    padded = (counts + MOE_BM - 1) // MOE_BM * MOE_BM
    pad_end = jnp.cumsum(padded)
    pad_start = pad_end - padded
    dest = (pad_start[idx] + rank).reshape(n * TOP_K)
    n_blocks = -(-(n * TOP_K) // MOE_BM) + N_EXPERTS
    blk_row = jnp.arange(n_blocks, dtype=I32) * MOE_BM
    block_e = jnp.minimum(jnp.sum(pad_end[None, :] <= blk_row[:, None], axis=1), N_EXPERTS - 1).astype(I32)
    n_used = (pad_end[-1] // MOE_BM).reshape(1).astype(I32)
    last_block = jnp.maximum(pad_end // MOE_BM - 1, 0).astype(I32)

    xs_sorted = _dispatch(dest, last_block, n_used, xn_p, xn_s, tm=MOE_BM, n_blocks=n_blocks)
    ew = dict(w_gate=w_gate[0], w_up=w_up[0], w_down=w_down[0],
              b_gate=b_gate[0].reshape(N_EXPERTS, 1, D_MODEL), b_up=b_up[0].reshape(N_EXPERTS, 1, D_MODEL),
              b_down=b_down[0].reshape(N_EXPERTS, 1, D_MODEL))
    ys = _experts(block_e, n_used, (pad_end // MOE_BM).astype(I32), xs_sorted, ew, bm=MOE_BM)
    y_p, y_s = _combine(dest, h2_p, h2_s, tw_p, tw_s, row(g_final), ys, tm=COMBINE_TILE)

    shp_p = (1, nb_p, seq, FOX_HEADS, FOX_HEAD_DIM)
    shp_s = (1, nb_s, t_s, FOX_HEADS, FOX_HEAD_DIM)
    tail = slice(SUBLANES - (CONV_WIDTH - 1), SUBLANES)
    return (y_p.reshape(nb_p, seq, D_MODEL), y_s.reshape(nb_s, t_s, D_MODEL),
            k_p.reshape(shp_p), v_p.reshape(shp_p), lf_p.reshape(1, nb_p, seq, FOX_HEADS),
            hl_p.reshape(1, nb_p, LRU_WIDTH), ct_p[:, tail, :].reshape(1, nb_p, CONV_WIDTH - 1, LRU_WIDTH),
            mk_p.reshape(1, nb_p, n_mem, MEM_HEADS, MEM_HEAD_DIM), mv_p.reshape(1, nb_p, n_mem, MEM_HEADS, MEM_HEAD_DIM),
            k_s.reshape(shp_s), v_s.reshape(shp_s), lf_s.reshape(1, nb_s, t_s, FOX_HEADS),
            hl_s.reshape(1, nb_s, LRU_WIDTH), ct_s[:, tail, :].reshape(1, nb_s, CONV_WIDTH - 1, LRU_WIDTH))
```

```python
import functools
import math

import jax
import jax.numpy as jnp
import numpy as np
from jax import lax
from jax.experimental import pallas as pl
from jax.experimental.pallas import tpu as pltpu

F32 = jnp.float32
BF16 = jnp.bfloat16
I32 = jnp.int32

D_MODEL = 1024
FOX_HEADS = 8
FOX_HEAD_DIM = 64
FOX_WIDTH = FOX_HEADS * FOX_HEAD_DIM
LRU_WIDTH = D_MODEL - FOX_WIDTH
LRU_BLOCKS = 8
LRU_C = 8.0
CONV_WIDTH = 4
MEM_HEADS = 4
MEM_HEAD_DIM = D_MODEL // MEM_HEADS
N_EXPERTS = 32
TOP_K = 4
SWIGLU_LIMIT = 7.0
SWIGLU_ALPHA = 1.702
RMS_EPS = 1e-6
NEG_INF = -1e30
LOG2E = math.log2(math.e)

LANES = 128
SUBLANES = 8
CHUNKS = D_MODEL // LANES
TOKEN_TILE = (CHUNKS, SUBLANES, LANES)
AUG = LANES * FOX_HEADS
VMEM_LIMIT = 56 * 1024 * 1024

MIX_TILE = 256
POST_TILE = 512
SAMPLE_POST_TILE = 256
FOX_BQ = 512
MOE_BM = 512
COMBINE_TILE = 256
FOX_SKIP_GAP = 160.0
FOX_STATIC_SHIFT_RANGE = 100.0

C_Q, C_K, C_V, C_F, C_XR, C_G, C_END = 0, 512, 1024, 1536, 1664, 2176, 2688


def _dot(a, b):
    return jnp.dot(a, b, preferred_element_type=F32)


def _dot_nt(a, b):
    return lax.dot_general(a, b, (((1,), (1,)), ((), ())), preferred_element_type=F32)


def _split3(x):
    hi = x.astype(BF16)
    r = x - hi.astype(F32)
    mid = r.astype(BF16)
    lo = (r - mid.astype(F32)).astype(BF16)
    return hi, mid, lo


def _dot3(a, x):
    hi, mid, lo = _split3(x)
    return _dot(a, hi) + _dot(a, mid) + _dot(a, lo)


def _lane_split3(x):
    hi, mid, lo = _split3(x)
    lane = lax.broadcasted_iota(I32, x.shape, 1)
    return jnp.where(lane < 8, hi, jnp.where(lane < 16, mid, lo))


def _softplus(x):
    return jnp.maximum(x, 0.0) + jnp.log1p(jnp.exp(-jnp.abs(x)))


def _rms(x, g):
    return x * lax.rsqrt(jnp.mean(x * x, axis=-1, keepdims=True) + RMS_EPS) * g


def _tri(n, kind):
    r = lax.broadcasted_iota(I32, (n, n), 0)
    c = lax.broadcasted_iota(I32, (n, n), 1)
    m = {"le": c <= r, "lt": c < r, "gt": c > r}[kind]
    return jnp.where(m, 1.0, 0.0).astype(BF16)


def _mix_in_kernel(x_ref, g_ref, w_ref, bfp_ref, eq_ref, ek_ref, cq_ref, ck_ref, cv_ref, hs_ref, hd_ref, hm_ref,
                   cw_ref, cb_ref, wa_ref, ba_ref, wi_ref, bi_ref, lam_ref, cprev_ref, h0_ref,
                   qa_ref, ka_ref, va_ref, ko_ref, vo_ref, vb_ref, lf_ref, f2_ref, qn_ref, kn_ref,
                   lru_ref, hl_ref, ct_ref,
                   fcar, hcar, xp_ref, sa_ref, sb_ref, kbuf, vbuf, ksem, *, tm, pad, streaming):
    step = pl.program_id(0)
    first = step == 0
    slot = step % 2

    def kv_copies(s, row0):
        rows = pl.ds(row0, tm)
        copies = []
        for h in range(FOX_HEADS):
            copies.append(pltpu.make_async_copy(kbuf.at[s, h], ko_ref.at[rows, h, :], ksem.at[s]))
            copies.append(pltpu.make_async_copy(vbuf.at[s, h], vo_ref.at[rows, h, :], ksem.at[s]))
        return copies

    @pl.when(step >= 2)
    def _():
        for c in kv_copies(slot, 0):
            c.wait()

    if streaming:
        @pl.when(first)
        def _():
            fcar[...] = jnp.zeros_like(fcar)
            hcar[...] = jnp.zeros_like(hcar)
            xp_ref[0:SUBLANES, :] = jnp.zeros((SUBLANES, LRU_WIDTH), F32)
    else:
        fcar[...] = jnp.zeros_like(fcar)
        hcar[...] = jnp.broadcast_to(h0_ref[0], hcar.shape)
        xp_ref[0:SUBLANES, :] = cprev_ref[0]

    x = x_ref[...]
    xn = _rms(x, g_ref[...]).astype(BF16)
    z = _dot(xn, w_ref[...])

    zq, zk, zv = z[:, C_Q:C_K], z[:, C_K:C_V], z[:, C_V:C_F]
    for h in range(FOX_HEADS):
        hc = slice(h * FOX_HEAD_DIM, (h + 1) * FOX_HEAD_DIM)
        kbuf[slot, h] = zk[:, hc]
        vbuf[slot, h] = zv[:, hc]
    for c in kv_copies(slot, pl.multiple_of(step * tm, tm)):
        c.start()
    qd, kd, vd = zq.astype(BF16), zk.astype(BF16), zv.astype(BF16)
    vb_ref[...] = vd

    def head_pad(d):
        blocks = []
        for p in range(FOX_HEADS // 2):
            blk = d[:, p * LANES:(p + 1) * LANES]
            blocks += [blk * hm_ref[0:1, :], blk * hm_ref[1:2, :]]
        return jnp.concatenate(blocks, axis=1)

    lane = lax.broadcasted_iota(I32, (tm, LANES), 1)
    lf = jnp.where(lane < 3 * FOX_HEADS, -_softplus(-(z[:, C_F:C_XR] + bfp_ref[...])), 0.0)
    lf_ref[...] = lf[:, :FOX_HEADS]
    cum = _dot3(_tri(tm, "le"), lf) + fcar[0:1, :]
    fcar[...] = jnp.broadcast_to(cum[tm - 1:tm, :], fcar.shape)
    cum2 = cum * LOG2E
    f2_ref[...] = cum2[:, :FOX_HEADS]
    qf, kf = qd.astype(F32), kd.astype(F32)
    diag = _dot((qf * kf).astype(BF16), hd_ref[...]).astype(BF16)
    hi, mid, lo = _split3(cum2)
    bias = jnp.where(lane < 8, hi, jnp.where(lane < 16, mid, jnp.where(lane < 24, lo, diag)))
    qa_ref[...] = head_pad(qd) + (_dot(bias, eq_ref[...]) + cq_ref[...]).astype(BF16)
    ka_ref[...] = head_pad(kd) + (ck_ref[...] - _dot(bias, ek_ref[...])).astype(BF16)
    va_ref[...] = head_pad(vd) + cv_ref[...].astype(BF16)
    qn_ref[...] = jnp.broadcast_to(jnp.max(_dot((qf * qf).astype(BF16), hs_ref[...]), axis=0, keepdims=True),
                                   qn_ref.shape)
    kn_ref[...] = jnp.broadcast_to(jnp.max(_dot((kf * kf).astype(BF16), hs_ref[...]), axis=0, keepdims=True),
                                   kn_ref.shape)

    xr = z[:, C_XR:C_G]
    xp_ref[SUBLANES:SUBLANES + tm, :] = xr
    xc = cb_ref[...] + xr * cw_ref[CONV_WIDTH - 1:CONV_WIDTH, :]
    for j in range(CONV_WIDTH - 1):
        sh = CONV_WIDTH - 1 - j
        xc = xc + xp_ref[SUBLANES - sh:SUBLANES - sh + tm, :] * cw_ref[j:j + 1, :]
    tail = xp_ref[tm:tm + SUBLANES, :]
    ct_ref[0] = tail
    xp_ref[0:SUBLANES, :] = tail

    xcb = xc.astype(BF16)
    r = jax.nn.sigmoid(_dot(xcb, wa_ref[...]) + ba_ref[...])
    ig = jax.nn.sigmoid(_dot(xcb, wi_ref[...]) + bi_ref[...])
    log_a = (-LRU_C) * r * _softplus(-lam_ref[...])
    a = jnp.exp(log_a)
    mult = jnp.sqrt(-jnp.tanh(log_a) * (a * a + 1.0))
    if streaming:
        row = lax.broadcasted_iota(I32, (tm, LRU_WIDTH), 0)
        mult = jnp.where(jnp.logical_and(row == 0, first), 1.0, mult)
    b = mult * ig * xc

    sa_ref[0:pad, :] = jnp.ones((pad, LRU_WIDTH), F32)
    sb_ref[0:pad, :] = jnp.zeros((pad, LRU_WIDTH), F32)
    d = 1
    while d < tm:
        sa_ref[pad:pad + tm, :] = a
        sb_ref[pad:pad + tm, :] = b
        b = a * sb_ref[pad - d:pad - d + tm, :] + b
        a = a * sa_ref[pad - d:pad - d + tm, :]
        d *= 2
    h = a * hcar[0:1, :] + b
    hlast = h[tm - 1:tm, :]
    hcar[...] = jnp.broadcast_to(hlast, hcar.shape)
    hl_ref[0] = hlast
    lru_ref[...] = (h * jax.nn.gelu(z[:, C_G:C_END])).astype(BF16)

    last = pl.num_programs(0) - 1

    @pl.when(jnp.logical_and(step == last, step >= 1))
    def _():
        for c in kv_copies(1 - slot, 0):
            c.wait()

    @pl.when(step == last)
    def _():
        for c in kv_copies(slot, 0):
            c.wait()


def _mix_in(x, cprev, h0, wts, *, tm, streaming):
    n = x.shape[0]
    steps = n // tm
    nseg = 1 if streaming else steps
    pad = max(tm // 2, SUBLANES)
    seg = (lambda i: (0, 0, 0)) if streaming else (lambda i: (i, 0, 0))
    full = lambda a: pl.BlockSpec(a.shape, lambda i: (0,) * a.ndim)
    rows = lambda w: pl.BlockSpec((tm, w), lambda i: (i, 0))
    names = ("g_mix", "w_all", "bf_pad", "eq", "ek", "cq", "ck", "cv", "hsel", "hdiag", "hmask", "conv_w", "conv_b",
             "wa", "ba", "wi", "bi", "lam")
    ws = [wts[k] for k in names]
    sds = jax.ShapeDtypeStruct
    out_shape = (
        sds((n, AUG), BF16), sds((n, AUG), BF16), sds((n, AUG), BF16),
        sds((n, FOX_HEADS, FOX_HEAD_DIM), F32), sds((n, FOX_HEADS, FOX_HEAD_DIM), F32), sds((n, FOX_WIDTH), BF16),
        sds((n, FOX_HEADS), F32), sds((n, FOX_HEADS), F32),
        sds((steps * SUBLANES, LANES), F32), sds((steps * SUBLANES, LANES), F32),
        sds((n, LRU_WIDTH), BF16),
        sds((nseg, 1, LRU_WIDTH), F32), sds((nseg, SUBLANES, LRU_WIDTH), F32),
    )
    out_specs = (
        rows(AUG), rows(AUG), rows(AUG), pl.BlockSpec(memory_space=pl.ANY), pl.BlockSpec(memory_space=pl.ANY),
        rows(FOX_WIDTH), rows(FOX_HEADS), rows(FOX_HEADS),
        pl.BlockSpec((SUBLANES, LANES), lambda i: (i, 0)), pl.BlockSpec((SUBLANES, LANES), lambda i: (i, 0)),
        rows(LRU_WIDTH),
        pl.BlockSpec((1, 1, LRU_WIDTH), seg), pl.BlockSpec((1, SUBLANES, LRU_WIDTH), seg),
    )
    in_specs = [rows(D_MODEL)] + [full(w) for w in ws] + [
        pl.BlockSpec((1, SUBLANES, LRU_WIDTH), seg), pl.BlockSpec((1, 1, LRU_WIDTH), seg)]
    return pl.pallas_call(
        functools.partial(_mix_in_kernel, tm=tm, pad=pad, streaming=streaming),
        grid=(steps,), in_specs=in_specs, out_specs=out_specs, out_shape=out_shape,
        scratch_shapes=[
            pltpu.VMEM((SUBLANES, LANES), F32), pltpu.VMEM((SUBLANES, LRU_WIDTH), F32),
            pltpu.VMEM((tm + SUBLANES, LRU_WIDTH), F32),
            pltpu.VMEM((pad + tm, LRU_WIDTH), F32), pltpu.VMEM((pad + tm, LRU_WIDTH), F32),
            pltpu.VMEM((2, FOX_HEADS, tm, FOX_HEAD_DIM), F32), pltpu.VMEM((2, FOX_HEADS, tm, FOX_HEAD_DIM), F32),
            pltpu.SemaphoreType.DMA((2,))],
        compiler_params=pltpu.CompilerParams(dimension_semantics=("arbitrary",), vmem_limit_bytes=VMEM_LIMIT),
        name="mix_in_stream" if streaming else "mix_in_segments",
    )(x, *ws, cprev, h0)


def _fox_prompt_kernel(nb_ref, st_ref, q_ref, k_ref, v_ref, o_ref, m_ref, acc_ref, *, bq):
    qi = pl.program_id(1)
    plan = pl.program_id(0) * pl.num_programs(1) + qi
    n_past = nb_ref[plan]
    cols = [slice(LANES * c, LANES * (c + 1)) for c in range(2)]
    qs = [q_ref[:, cols[c]] for c in range(2)]
    acc_ref[...] = jnp.zeros(acc_ref.shape, F32)

    def scores(c, start, causal):
        s = _dot_nt(qs[c], k_ref[pl.ds(start, bq), cols[c]])
        if causal:
            r = lax.broadcasted_iota(I32, (bq, bq), 0)
            cc = lax.broadcasted_iota(I32, (bq, bq), 1)
            s = jnp.where(cc <= r, s, NEG_INF)
        return s

    def static_block(start, causal):
        for c in range(2):
            p = jnp.exp2(scores(c, start, causal)).astype(BF16)
            acc_ref[c] += _dot(p, v_ref[pl.ds(start, bq), cols[c]])

    def online_block(start, causal):
        for c in range(2):
            s = scores(c, start, causal)
            m_old = m_ref[c]
            m_new = jnp.maximum(m_old, jnp.max(s, axis=1, keepdims=True))
            p = jnp.exp2(s - m_new).astype(BF16)
            acc_ref[c] = jnp.exp2(m_old - m_new) * acc_ref[c] + _dot(p, v_ref[pl.ds(start, bq), cols[c]])
            m_ref[c] = m_new

    def run(block):
        block(pl.multiple_of(qi * bq, bq), True)

        def body(t, carry):
            block(pl.multiple_of((qi - 1 - t) * bq, bq), False)
            return carry

        lax.fori_loop(0, n_past, body, 0)

    @pl.when(st_ref[plan] == 1)
    def _():
        run(static_block)

    @pl.when(st_ref[plan] != 1)
    def _():
        m_ref[...] = jnp.full(m_ref.shape, NEG_INF, F32)
        run(online_block)

    a0, a1 = acc_ref[0], acc_ref[1]
    o0 = a0 / a0[:, FOX_HEAD_DIM:FOX_HEAD_DIM + 1]
    o1 = a1 / a1[:, 0:1]
    lane = lax.broadcasted_iota(I32, (bq, LANES), 1)
    o_ref[...] = jnp.where(lane < FOX_HEAD_DIM, o0, o1).astype(BF16)


def _fox_prompt(n_past, static_ok, qa, ka, va, *, bq):
    s = qa.shape[0]
    pairs = FOX_HEADS // 2
    return pl.pallas_call(
        functools.partial(_fox_prompt_kernel, bq=bq),
        grid_spec=pltpu.PrefetchScalarGridSpec(
            num_scalar_prefetch=2, grid=(pairs, s // bq),
            in_specs=[pl.BlockSpec((bq, 2 * LANES), lambda p, i, nb, st: (i, p)),
                      pl.BlockSpec((s, 2 * LANES), lambda p, i, nb, st: (0, p)),
                      pl.BlockSpec((s, 2 * LANES), lambda p, i, nb, st: (0, p))],
            out_specs=pl.BlockSpec((bq, LANES), lambda p, i, nb, st: (i, p)),
            scratch_shapes=[pltpu.VMEM((2, bq, 1), F32), pltpu.VMEM((2, bq, LANES), F32)]),
        out_shape=jax.ShapeDtypeStruct((s, FOX_WIDTH), BF16),
        compiler_params=pltpu.CompilerParams(dimension_semantics=("arbitrary", "arbitrary"),
                                             vmem_limit_bytes=VMEM_LIMIT),
        name="fox_prompt",
    )(n_past, static_ok, qa, ka, va)


def _fox_plan(f2, qn, kn, *, bq, tile):
    s = f2.shape[0]
    nq = s // bq
    per = lambda a: a.reshape(s // tile, SUBLANES, LANES)[:, 0, :FOX_HEADS]
    qnorm = jnp.sqrt(jnp.max(per(qn).reshape(nq, bq // tile, FOX_HEADS), axis=1))
    knorm = jnp.sqrt(jnp.max(per(kn), axis=0))
    spread = 2.0 * 1.02 * qnorm * knorm[None, :]
    thr = spread + FOX_SKIP_GAP
    f_first = f2[0::bq]
    f_last = f2[bq - 1::bq]
    gap = f_last[None, :, :] - f_first[:, None, :]
    before = (jnp.arange(nq)[None, :] < jnp.arange(nq)[:, None])[:, :, None]
    need = jnp.sum(jnp.logical_and(before, gap < thr[:, None, :]), axis=1)
    pair = lambda a: a.reshape(nq, FOX_HEADS // 2, 2)
    n_past = jnp.max(pair(need), axis=2).T.reshape(-1).astype(I32)
    static_ok = jnp.all(pair(spread) <= FOX_STATIC_SHIFT_RANGE, axis=2).T.reshape(-1).astype(I32)
    return n_past, static_ok


def _fox_sample_kernel(q_ref, kn_ref, vn_ref, ck_ref, cv_ref, clf_ref, ek_ref, ckc_ref, o_ref, g_ref,
                       kbuf, vbuf, csem, *, t, past, chunk):
    step = pl.program_id(0)
    slot = step % 2

    def cache_copies(b, s):
        return [pltpu.make_async_copy(ref.at[0, b, :, h, :], buf.at[s, h], csem.at[s])
                for ref, buf in ((ck_ref, kbuf), (cv_ref, vbuf)) for h in range(FOX_HEADS)]

    @pl.when(step == 0)
    def _():
        for c in cache_copies(0, 0):
            c.start()

    @pl.when(step + 1 < pl.num_programs(0))
    def _():
        for c in cache_copies(step + 1, 1 - slot):
            c.start()

    car = jnp.zeros((1, LANES), F32)
    upper = _tri(chunk, "gt")
    for ci in reversed(range(past // chunk)):
        lf = clf_ref[0, ci * chunk:(ci + 1) * chunk, :]
        g_ref[ci * chunk:(ci + 1) * chunk, :] = _dot3(upper, lf) + car
        car = car + jnp.sum(lf, axis=0, keepdims=True)
    gs = _lane_split3(g_ref[...] * LOG2E)

    for c in cache_copies(0, slot):
        c.wait()

    def swap_halves(x):
        return pltpu.roll(x.astype(F32), FOX_HEAD_DIM, axis=1).astype(BF16)

    r = lax.broadcasted_iota(I32, (t, t), 0)
    cc = lax.broadcasted_iota(I32, (t, t), 1)
    prow = lax.broadcasted_iota(I32, (FOX_HEAD_DIM, LANES), 0)
    plane = lax.broadcasted_iota(I32, (FOX_HEAD_DIM, LANES), 1)
    place = [jnp.where(plane == prow + FOX_HEAD_DIM * c, 1.0, 0.0).astype(BF16) for c in range(2)]
    for p in range(FOX_HEADS // 2):
        pc = slice(LANES * p, LANES * (p + 1))
        pair_cols = slice(2 * LANES * p, 2 * LANES * (p + 1))
        pair_extras = _dot(gs, ek_ref[:, pair_cols]) + ckc_ref[:, pair_cols]
        vpair = vn_ref[:, pc]
        o_pair = jnp.zeros((t, LANES), F32)
        for c in range(2):
            h = 2 * p + c
            hc = slice(LANES * h, LANES * (h + 1))
            q, kn, vn = q_ref[:, hc], kn_ref[:, hc], vpair
            if c == 1:
                q, kn, vn = swap_halves(q), swap_halves(kn), swap_halves(vn)
            extras = pair_extras[:, LANES * c:LANES * (c + 1)]
            kpast = (jnp.pad(kbuf[slot, h], ((0, 0), (0, LANES - FOX_HEAD_DIM))) + extras).astype(BF16)
            sp = _dot_nt(q, kpast)
            sn = jnp.where(cc <= r, _dot_nt(q, kn), NEG_INF)
            m = jnp.maximum(jnp.max(sp, axis=1, keepdims=True), jnp.max(sn, axis=1, keepdims=True))
            pp = jnp.exp2(sp - m)
            pn = jnp.exp2(sn - m)
            l = jnp.sum(pp, axis=1, keepdims=True) + jnp.sum(pn, axis=1, keepdims=True)
            o = (_dot(pp.astype(BF16), vbuf[slot, h].astype(BF16))
                 + _dot(pn.astype(BF16), vn[:, :FOX_HEAD_DIM])) / l
            o_pair = o_pair + _dot(o.astype(BF16), place[c])
        o_ref[:, pc] = o_pair.astype(BF16)


def _fox_sample(qa, ka, vb, cache_k, cache_v, cache_lf3, ek, ck, *, t):
    nb, past = cache_k.shape[1], cache_k.shape[2]
    full = lambda a: pl.BlockSpec(a.shape, lambda b: (0,) * a.ndim)
    slabs = pltpu.VMEM((2, FOX_HEADS, past, FOX_HEAD_DIM), F32)
    return pl.pallas_call(
        functools.partial(_fox_sample_kernel, t=t, past=past, chunk=256),
        grid=(nb,),
        in_specs=[pl.BlockSpec((t, AUG), lambda b: (b, 0)), pl.BlockSpec((t, AUG), lambda b: (b, 0)),
                  pl.BlockSpec((t, FOX_WIDTH), lambda b: (b, 0)),
                  pl.BlockSpec(memory_space=pl.ANY), pl.BlockSpec(memory_space=pl.ANY),
                  pl.BlockSpec((1, past, LANES), lambda b: (b, 0, 0)),
                  full(ek), full(ck)],
        out_specs=pl.BlockSpec((t, FOX_WIDTH), lambda b: (b, 0)),
        out_shape=jax.ShapeDtypeStruct((nb * t, FOX_WIDTH), BF16),
        scratch_shapes=[pltpu.VMEM((past, LANES), F32), slabs, slabs, pltpu.SemaphoreType.DMA((2,))],
        compiler_params=pltpu.CompilerParams(dimension_semantics=("arbitrary",), vmem_limit_bytes=VMEM_LIMIT),
        name="fox_sample",
    )(qa, ka, vb, cache_k, cache_v, cache_lf3, ek, ck)


def _mem_kv_kernel(m_ref, g_ref, wk_ref, wv_ref, k_ref, v_ref):
    mn = _rms(m_ref[...], g_ref[...]).astype(BF16)
    k_ref[...] = _dot(mn, wk_ref[...])
    v_ref[...] = _dot(mn, wv_ref[...])


def _mem_kv(mem, g, wk, wv):
    n = mem.shape[0]
    return pl.pallas_call(
        _mem_kv_kernel,
        out_shape=(jax.ShapeDtypeStruct((n, D_MODEL), F32), jax.ShapeDtypeStruct((n, D_MODEL), F32)),
        compiler_params=pltpu.CompilerParams(vmem_limit_bytes=VMEM_LIMIT),
        name="mem_kv",
    )(mem, g, wk, wv)


def _post_kernel(h_ref, fo_ref, lru_ref, wo_ref, gc_ref, wcq_ref, mk_ref, mv_ref, wco_ref,
                 gm_ref, wr_ref, br_ref, cnt_in_ref,
                 h2_ref, xn_ref, idx_ref, tw_ref, rank_ref, cnt_ref, car_ref, *mem_scratch, tm, seg, cached_memory):
    step = pl.program_id(0)
    nseg = tm // seg

    @pl.when(step == 0)
    def _():
        car_ref[...] = cnt_in_ref[...]

    if cached_memory:
        mbuf, msem = mem_scratch
        slot = step % 2

        def mem_copies(at_step, s):
            return [pltpu.make_async_copy(ref.at[0, at_step * nseg + j, :, hd, :], mbuf.at[s, j, t, hd], msem.at[s])
                    for j in range(nseg) for t, ref in enumerate((mk_ref, mv_ref)) for hd in range(MEM_HEADS)]

        @pl.when(step == 0)
        def _():
            for c in mem_copies(0, 0):
                c.start()

        @pl.when(step + 1 < pl.num_programs(0))
        def _():
            for c in mem_copies(step + 1, 1 - slot):
                c.start()

    h1 = h_ref[...] + _dot(fo_ref[...], wo_ref[0:FOX_WIDTH, :]) + _dot(lru_ref[...], wo_ref[FOX_WIDTH:D_MODEL, :])

    q = _dot(_rms(h1, gc_ref[...]).astype(BF16), wcq_ref[...])
    if cached_memory:
        for c in mem_copies(0, slot):
            c.wait()
    segs = []
    for j in range(nseg):
        rows = slice(j * seg, (j + 1) * seg)
        heads = []
        for hd in range(MEM_HEADS):
            hc = slice(MEM_HEAD_DIM * hd, MEM_HEAD_DIM * (hd + 1))
            mk = mbuf[slot, j, 0, hd] if cached_memory else mk_ref[0, :, hc]
            mv = mbuf[slot, j, 1, hd] if cached_memory else mv_ref[0, :, hc]
            s = _dot_nt(q[rows, hc].astype(BF16), mk.astype(BF16)) * (MEM_HEAD_DIM ** -0.5)
            p = jnp.exp(s - jnp.max(s, axis=1, keepdims=True))
            o = _dot(p.astype(BF16), mv.astype(BF16)) / jnp.sum(p, axis=1, keepdims=True)
            heads.append(o.astype(BF16))
        segs.append(jnp.concatenate(heads, axis=1))
    h2 = h1 + _dot(segs[0] if nseg == 1 else jnp.concatenate(segs, axis=0), wco_ref[...])
    h2_ref[...] = h2

    xn = _rms(h2, gm_ref[...])
    for c in range(D_MODEL // LANES):
        xn_ref[:, c] = xn[:, c * LANES:(c + 1) * LANES].reshape(tm // SUBLANES, SUBLANES, LANES)

    lane = lax.broadcasted_iota(I32, (tm, LANES), 1).astype(F32)
    logits = jnp.where(lane < N_EXPERTS, _dot(xn.astype(BF16), wr_ref[...]) + br_ref[...], -jnp.inf)
    vals, idxs = [], []
    for _ in range(TOP_K):
        mx = jnp.max(logits, axis=1, keepdims=True)
        ix = jnp.min(jnp.where(logits == mx, lane, float(LANES)), axis=1, keepdims=True)
        vals.append(mx)
        idxs.append(ix)
        logits = jnp.where(lane == ix, -jnp.inf, logits)
    es = [jnp.exp(v - vals[0]) for v in vals]
    den = es[0] + es[1] + es[2] + es[3]

    onehot = jnp.zeros((tm, LANES), F32)
    for ix in idxs:
        onehot = onehot + jnp.where(lane == ix, 1.0, 0.0)
    before = _dot(_tri(tm, "lt"), onehot.astype(BF16)) + car_ref[0:1, :]
    car = car_ref[0:1, :] + jnp.sum(onehot, axis=0, keepdims=True)
    car_ref[...] = jnp.broadcast_to(car, car_ref.shape)
    cnt_ref[...] = jnp.broadcast_to(car, cnt_ref.shape)

    idx_o = jnp.zeros((tm, LANES), F32)
    tw_o = jnp.zeros((tm, LANES), F32)
    rk_o = jnp.zeros((tm, LANES), F32)
    for j in range(TOP_K):
        rk = jnp.sum(jnp.where(lane == idxs[j], before, 0.0), axis=1, keepdims=True)
        idx_o = jnp.where(lane == j, idxs[j], idx_o)
        tw_o = jnp.where(lane == j, es[j] / den, tw_o)
        rk_o = jnp.where(lane == j, rk, rk_o)
    idx_ref[...] = idx_o.astype(I32)
    tw_ref[...] = tw_o
    rank_ref[...] = rk_o.astype(I32)


def _post(h, fo, lru, mk, mv, cnt_in, wts, *, tm, seg, cached_memory):
    n = h.shape[0]
    full = lambda a: pl.BlockSpec(a.shape, lambda i: (0,) * a.ndim)
    rows = lambda w: pl.BlockSpec((tm, w), lambda i: (i, 0))
    if cached_memory:
        nm = mk.shape[2]
        mem = pl.BlockSpec(memory_space=pl.ANY)
        mem_scratch = [pltpu.VMEM((2, tm // seg, 2, MEM_HEADS, nm, MEM_HEAD_DIM), F32),
                       pltpu.SemaphoreType.DMA((2,))]
    else:
        mem = full(mk)
        mem_scratch = []
    w = [wts[k] for k in ("w_out", "g_cross", "w_cq")]
    w2 = [wts[k] for k in ("w_co", "g_moe", "w_router", "b_router")]
    return pl.pallas_call(
        functools.partial(_post_kernel, tm=tm, seg=seg, cached_memory=cached_memory),
        grid=(n // tm,),
        in_specs=[rows(D_MODEL), rows(FOX_WIDTH), rows(LRU_WIDTH)] + [full(a) for a in w] + [mem, mem]
                 + [full(a) for a in w2] + [full(cnt_in)],
        out_specs=(rows(D_MODEL), pl.BlockSpec((tm // SUBLANES,) + TOKEN_TILE, lambda i: (i, 0, 0, 0)),
                   rows(LANES), rows(LANES), rows(LANES),
                   pl.BlockSpec((SUBLANES, LANES), lambda i: (0, 0))),
        out_shape=(jax.ShapeDtypeStruct((n, D_MODEL), F32), jax.ShapeDtypeStruct((n // SUBLANES,) + TOKEN_TILE, F32),
                   jax.ShapeDtypeStruct((n, LANES), I32), jax.ShapeDtypeStruct((n, LANES), F32),
                   jax.ShapeDtypeStruct((n, LANES), I32), jax.ShapeDtypeStruct((SUBLANES, LANES), F32)),
        scratch_shapes=[pltpu.VMEM((SUBLANES, LANES), F32)] + mem_scratch,
        compiler_params=pltpu.CompilerParams(dimension_semantics=("arbitrary",), vmem_limit_bytes=VMEM_LIMIT),
        name="post_batch_mem" if cached_memory else "post_shared_mem",
    )(h, fo, lru, *w, mk, mv, *w2, cnt_in)


STAGES = 3


def _dispatch_kernel(dest_ref, last_ref, nu_ref, xp_ref, xs_ref, out_ref, zero_ref, stage, sems, gsem, zsem,
                     *, tm, steps_p, n_blocks):
    i = pl.program_id(0)

    def zero_copy(b):
        return pltpu.make_async_copy(zero_ref, out_ref.at[pl.ds(pl.multiple_of(b * tm, tm), tm)], zsem)

    @pl.when(i == 0)
    def _():
        zero_ref[...] = jnp.zeros(zero_ref.shape, F32)
        n_tail = n_blocks - nu_ref[0]

        def start_e(e, c):
            zero_copy(last_ref[e]).start()
            return c

        def start_t(b, c):
            zero_copy(nu_ref[0] + b).start()
            return c

        def wait_one(b, c):
            zero_copy(0).wait()
            return c

        lax.fori_loop(0, N_EXPERTS, start_e, 0)
        lax.fori_loop(0, n_tail, start_t, 0)
        lax.fori_loop(0, N_EXPERTS + n_tail, wait_one, 0)

    groups = tm // SUBLANES
    steps = pl.num_programs(0)

    def stage_copy(src_ref, tile, slot):
        return pltpu.make_async_copy(src_ref.at[pl.ds(tile * groups, groups)], stage.at[slot], gsem.at[slot])

    def start_stage(tile, slot):
        @pl.when(tile < steps_p)
        def _():
            stage_copy(xp_ref, tile, slot).start()

        @pl.when(tile >= steps_p)
        def _():
            stage_copy(xs_ref, tile - steps_p, slot).start()

    @pl.when(i == 0)
    def _():
        start_stage(0, 0)

    @pl.when(i + 1 < steps)
    def _():
        start_stage(i + 1, (i + 1) % STAGES)

    slot = i % STAGES
    stage_copy(xp_ref, 0, slot).wait()

    base = i * (tm * TOP_K)

    def start(g, c):
        for u in range(SUBLANES):
            for j in range(TOP_K):
                d = dest_ref[base + (g * SUBLANES + u) * TOP_K + j]
                pltpu.make_async_copy(stage.at[slot, g, :, u, :], out_ref.at[d],
                                      sems.at[i % 2]).start(priority=j % 2)
        return c

    lax.fori_loop(0, groups, start, 0)

    def drain(parity):
        for _ in range(TOP_K):
            pltpu.make_async_copy(out_ref.at[pl.ds(0, tm)], out_ref.at[pl.ds(0, tm)], sems.at[parity]).wait()

    @pl.when(i > 0)
    def _():
        drain((i - 1) % 2)

    @pl.when(i == steps - 1)
    def _():
        drain(i % 2)


def _dispatch(dest_flat, last_block, n_used, xn_p, xn_s, *, tm, n_blocks):
    steps_p, steps_s = xn_p.shape[0] * SUBLANES // tm, xn_s.shape[0] * SUBLANES // tm
    return pl.pallas_call(
        functools.partial(_dispatch_kernel, tm=tm, steps_p=steps_p, n_blocks=n_blocks),
        grid_spec=pltpu.PrefetchScalarGridSpec(
            num_scalar_prefetch=3, grid=(steps_p + steps_s,),
            in_specs=[pl.BlockSpec(memory_space=pl.ANY), pl.BlockSpec(memory_space=pl.ANY)],
            out_specs=pl.BlockSpec(memory_space=pl.ANY),
            scratch_shapes=[pltpu.VMEM((tm, CHUNKS, LANES), F32),
                            pltpu.VMEM((STAGES, tm // SUBLANES) + TOKEN_TILE, F32),
                            pltpu.SemaphoreType.DMA((2,)), pltpu.SemaphoreType.DMA((STAGES,)),
                            pltpu.SemaphoreType.DMA(())]),
        out_shape=jax.ShapeDtypeStruct((n_blocks * tm, CHUNKS, LANES), F32),
        compiler_params=pltpu.CompilerParams(dimension_semantics=("arbitrary",), vmem_limit_bytes=VMEM_LIMIT),
        name="moe_dispatch",
    )(dest_flat, last_block, n_used, xn_p, xn_s)


def _rows_from_tiles(view, rows):
    return jnp.concatenate([view[:, c].reshape(rows, LANES) for c in range(CHUNKS)], axis=1)


def _rows_to_tiles(view, val, rows):
    for c in range(CHUNKS):
        view[:, c] = val[:, c * LANES:(c + 1) * LANES].reshape(rows // SUBLANES, SUBLANES, LANES)


def _expert_kernel(be_ref, nu_ref, nx_ref, xs_ref, wg_ref, bg_ref, wu_ref, bu_ref, wd_ref, bd_ref, ys_ref,
                   wgb, wub, wdb, wbuf, xbuf, ybuf, seq, wsem, xsem, ysem, *, bm):
    i = pl.program_id(0)
    steps = pl.num_programs(0)
    groups = bm // SUBLANES
    slot = i % 2
    prev = be_ref[jnp.maximum(i - 1, 0)]
    fresh = jnp.logical_or(i == 0, be_ref[i] != prev)
    live = i < nu_ref[0]

    def w_copies(e, s):
        return [pltpu.make_async_copy(w_ref.at[e], wbuf.at[s, k], wsem.at[s])
                for k, w_ref in enumerate((wg_ref, wu_ref, wd_ref))]

    @pl.when(i == 0)
    def _():
        seq[0] = 0
        for c in w_copies(be_ref[0], 0):
            c.start()

    def in_copies(blk, s):
        g0 = pl.multiple_of(blk * groups, groups)
        return [pltpu.make_async_copy(xs_ref.at[pl.ds(g0, groups), u], xbuf.at[s, :, :, u, :], xsem.at[s])
                for u in range(SUBLANES)]

    def out_copies(blk, s):
        g0 = pl.multiple_of(blk * groups, groups)
        return [pltpu.make_async_copy(ybuf.at[s, :, :, u, :], ys_ref.at[pl.ds(g0, groups), u], ysem.at[s])
                for u in range(SUBLANES)]

    @pl.when(i == 0)
    def _():
        for c in in_copies(0, 0):
            c.start()

    @pl.when(i + 1 < nu_ref[0])
    def _():
        for c in in_copies(i + 1, (i + 1) % 2):
            c.start()

    @pl.when(i >= 2)
    def _():
        for c in out_copies(0, slot):
            c.wait()

    @pl.when(jnp.logical_and(live, fresh))
    def _():
        s = seq[0] % 2
        for c in w_copies(0, s):
            c.wait()
        for static_s in range(2):
            @pl.when(s == static_s)
            def _():
                wgb[...] = wbuf[static_s, 0].astype(BF16)
                wub[...] = wbuf[static_s, 1].astype(BF16)
                wdb[...] = wbuf[static_s, 2].astype(BF16)

        nxt = nx_ref[be_ref[i]]

        @pl.when(nxt < nu_ref[0])
        def _():
            for c in w_copies(be_ref[jnp.minimum(nxt, steps - 1)], 1 - s):
                c.start()

        seq[0] = seq[0] + 1

    @pl.when(live)
    def _():
        for c in in_copies(0, slot):
            c.wait()
        x = _rows_from_tiles(xbuf.at[slot], bm).astype(BF16)
        g = jnp.minimum(_dot(x, wgb[...]) + bg_ref[0], SWIGLU_LIMIT)
        u = jnp.clip(_dot(x, wub[...]) + bu_ref[0], -SWIGLU_LIMIT, SWIGLU_LIMIT)
        hdn = g * jax.nn.sigmoid(SWIGLU_ALPHA * g) * (u + 1.0)
        _rows_to_tiles(ybuf.at[slot], _dot(hdn.astype(BF16), wdb[...]) + bd_ref[0], bm)

    @pl.when(jnp.logical_not(live))
    def _():
        ybuf[slot] = jnp.zeros(ybuf.shape[1:], F32)

    for c in out_copies(i, slot):
        c.start()

    @pl.when(jnp.logical_and(i == steps - 1, i >= 1))
    def _():
        for c in out_copies(0, 1 - slot):
            c.wait()

    @pl.when(i == steps - 1)
    def _():
        for c in out_copies(0, slot):
            c.wait()


def _experts(block_e, n_used, next_first, xs, wts, *, bm):
    p = xs.shape[0]
    grouped = (p // SUBLANES, SUBLANES, CHUNKS, LANES)
    anyspace = pl.BlockSpec(memory_space=pl.ANY)
    bspec = pl.BlockSpec((1, 1, D_MODEL), lambda i, be, nu, nx: (be[i], 0, 0))
    tiles = (2, bm // SUBLANES) + TOKEN_TILE
    ys = pl.pallas_call(
        functools.partial(_expert_kernel, bm=bm),
        grid_spec=pltpu.PrefetchScalarGridSpec(
            num_scalar_prefetch=3, grid=(p // bm,),
            in_specs=[anyspace, anyspace, bspec, anyspace, bspec, anyspace, bspec],
            out_specs=anyspace,
            scratch_shapes=[pltpu.VMEM((D_MODEL, D_MODEL), BF16)] * 3 + [
                pltpu.VMEM((2, 3, D_MODEL, D_MODEL), F32), pltpu.VMEM(tiles, F32), pltpu.VMEM(tiles, F32),
                pltpu.SMEM((1,), I32),
                pltpu.SemaphoreType.DMA((2,)), pltpu.SemaphoreType.DMA((2,)), pltpu.SemaphoreType.DMA((2,))]),
        out_shape=jax.ShapeDtypeStruct(grouped, F32),
        compiler_params=pltpu.CompilerParams(dimension_semantics=("arbitrary",), vmem_limit_bytes=VMEM_LIMIT),
        name="moe_experts",
    )(block_e, n_used, next_first, xs.reshape(grouped), wts["w_gate"], wts["b_gate"], wts["w_up"], wts["b_up"],
      wts["w_down"], wts["b_down"])
    return ys.reshape(p, CHUNKS, LANES)


def _combine_kernel(dest_ref, hp_ref, hs_ref, twp_ref, tws_ref, gf_ref, ys_ref, yp_ref, ysm_ref, buf, sems,
                    *, tm, steps_p):
    i = pl.program_id(0)
    slot = i % 2

    def issue(tile, into):
        base = tile * (tm * TOP_K)

        def start(g, c):
            for u in range(SUBLANES):
                for j in range(TOP_K):
                    d = dest_ref[base + (g * SUBLANES + u) * TOP_K + j]
                    pltpu.make_async_copy(ys_ref.at[d], buf.at[into, j, g, :, u, :],
                                          sems.at[into]).start(priority=j % 2)
            return c

        lax.fori_loop(0, tm // SUBLANES, start, 0)

    @pl.when(i == 0)
    def _():
        issue(0, 0)

    @pl.when(i + 1 < pl.num_programs(0))
    def _():
        issue(i + 1, (i + 1) % 2)

    for j in range(TOP_K):
        pltpu.make_async_copy(ys_ref.at[pl.ds(0, tm)], ys_ref.at[pl.ds(0, tm)], sems.at[slot]).wait()

    def finish(h_ref, tw_ref, y_ref):
        tw = tw_ref[...]
        y = h_ref[...]
        for j in range(TOP_K):
            y = y + _rows_from_tiles(buf.at[slot, j], tm) * tw[:, j:j + 1]
        y_ref[...] = _rms(y, gf_ref[...])

    @pl.when(i < steps_p)
    def _():
        finish(hp_ref, twp_ref, yp_ref)

    @pl.when(i >= steps_p)
    def _():
        finish(hs_ref, tws_ref, ysm_ref)


def _combine(dest_flat, h2_p, h2_s, tw_p, tw_s, g_final, ys, *, tm):
    steps_p, steps_s = h2_p.shape[0] // tm, h2_s.shape[0] // tm
    pblk = lambda w: pl.BlockSpec((tm, w), lambda i, d: (jnp.minimum(i, steps_p - 1), 0))
    sblk = lambda w: pl.BlockSpec((tm, w), lambda i, d: (jnp.maximum(i - steps_p, 0), 0))
    return pl.pallas_call(
        functools.partial(_combine_kernel, tm=tm, steps_p=steps_p),
        grid_spec=pltpu.PrefetchScalarGridSpec(
            num_scalar_prefetch=1, grid=(steps_p + steps_s,),
            in_specs=[pblk(D_MODEL), sblk(D_MODEL), pblk(LANES), sblk(LANES),
                      pl.BlockSpec((1, D_MODEL), lambda i, d: (0, 0)),
                      pl.BlockSpec(memory_space=pl.ANY)],
            out_specs=(pblk(D_MODEL), sblk(D_MODEL)),
            scratch_shapes=[pltpu.VMEM((2, TOP_K, tm // SUBLANES) + TOKEN_TILE, F32),
                            pltpu.SemaphoreType.DMA((2,))]),
        out_shape=(jax.ShapeDtypeStruct(h2_p.shape, F32), jax.ShapeDtypeStruct(h2_s.shape, F32)),
        compiler_params=pltpu.CompilerParams(dimension_semantics=("arbitrary",), vmem_limit_bytes=VMEM_LIMIT),
        name="moe_combine",
    )(dest_flat, h2_p, h2_s, tw_p, tw_s, g_final, ys)


def _aug_layout():
    main, extra = [], []
    for h in range(FOX_HEADS):
        even = h % 2 == 0
        main.append(LANES * h + (0 if even else FOX_HEAD_DIM))
        extra.append(LANES * h + (FOX_HEAD_DIM if even else 0))
    return main, extra


def _prep_mix_weights(g_mix, w_in, b_f, conv_w, conv_b, w_a, b_a, w_i, b_i, lam):
    _, extra = _aug_layout()
    wq = w_in[:, 0:FOX_WIDTH] * (FOX_HEAD_DIM ** -0.5 * LOG2E)
    wf = w_in[:, 3 * FOX_WIDTH:3 * FOX_WIDTH + FOX_HEADS]
    wf_pad = jnp.concatenate([wf, wf, wf, jnp.zeros((D_MODEL, LANES - 3 * FOX_HEADS), F32)], axis=1)
    w_all = jnp.concatenate([wq, w_in[:, FOX_WIDTH:3 * FOX_WIDTH], wf_pad, w_in[:, 3 * FOX_WIDTH + FOX_HEADS:]],
                            axis=1).astype(BF16)
    bf_pad = jnp.concatenate([b_f, b_f, b_f, jnp.zeros((LANES - 3 * FOX_HEADS,), F32)]).reshape(1, LANES)
    eq = np.zeros((LANES, AUG), np.float32)
    ek = np.zeros((LANES, AUG), np.float32)
    cq = np.zeros((1, AUG), np.float32)
    ck = np.zeros((1, AUG), np.float32)
    cv = np.zeros((1, AUG), np.float32)
    hsel = np.zeros((FOX_WIDTH, LANES), np.float32)
    hdiag = np.zeros((FOX_WIDTH, LANES), np.float32)
    hmask = np.zeros((2, LANES), np.float32)
    hmask[0, :FOX_HEAD_DIM] = 1.0
    hmask[1, FOX_HEAD_DIM:] = 1.0
    for h in range(FOX_HEADS):
        hsel[FOX_HEAD_DIM * h:FOX_HEAD_DIM * (h + 1), h] = 1.0
        hdiag[FOX_HEAD_DIM * h:FOX_HEAD_DIM * (h + 1), 3 * FOX_HEADS + h] = 1.0
        cv[0, extra[h]] = 1.0
        eq[3 * FOX_HEADS + h, extra[h] + 6] = -1.0
        ck[0, extra[h] + 6] = 1.0
        for part in range(3):
            eq[part * 8 + h, extra[h] + part] = 1.0
            ek[part * 8 + h, extra[h] + 3 + part] = 1.0
            cq[0, extra[h] + 3 + part] = 1.0
            ck[0, extra[h] + part] = 1.0
    ek_even = np.zeros((LANES, AUG), np.float32)
    ck_even = np.zeros((1, AUG), np.float32)
    for h in range(FOX_HEADS):
        base = LANES * h + FOX_HEAD_DIM
        ck_even[0, base + 6] = 1.0
        for part in range(3):
            ek_even[part * 8 + h, base + 3 + part] = 1.0
            ck_even[0, base + part] = 1.0
    dense = lambda w: jax.scipy.linalg.block_diag(*[w[i] for i in range(LRU_BLOCKS)]).astype(BF16)
    row = lambda a: a.reshape(1, -1)
    b16 = lambda a: jnp.asarray(a, BF16)
    return dict(g_mix=row(g_mix), w_all=w_all, bf_pad=bf_pad, eq=b16(eq), ek=b16(ek),
                cq=jnp.asarray(cq), ck=jnp.asarray(ck), cv=jnp.asarray(cv),
                ek_even=b16(ek_even), ck_even=jnp.asarray(ck_even),
                hsel=b16(hsel), hdiag=b16(hdiag), hmask=b16(hmask), conv_w=conv_w, conv_b=row(conv_b),
                wa=dense(w_a), ba=row(b_a), wi=dense(w_i), bi=row(b_i), lam=row(lam))


def kernel(x_prompt, x_sample, mem_prompt, cache_fox_k, cache_fox_v, cache_fox_logf, state_lru_h, state_conv, cache_mem_k, cache_mem_v, g_mix, w_in, b_f, conv_w, conv_b, w_a, b_a, w_i, b_i, lam, w_out, g_cross, g_mem, w_cq, w_ck, w_cv, w_co, g_moe, w_router, b_router, w_gate, b_gate, w_up, b_up, w_down, b_down, g_final):
    nb_p, seq, _ = x_prompt.shape
    nb_s, t_s, _ = x_sample.shape
    past = cache_fox_k.shape[2]
    n_mem = mem_prompt.shape[1]
    assert nb_p == 1 and g_mix.shape[0] == 1
    n_p, n_s = seq, nb_s * t_s
    row = lambda a: a.reshape(1, -1)

    mw = _prep_mix_weights(g_mix[0], w_in[0], b_f[0], conv_w[0], conv_b[0], w_a[0], b_a[0], w_i[0], b_i[0], lam[0])
    xp = x_prompt.reshape(n_p, D_MODEL)
    xs_ = x_sample.reshape(n_s, D_MODEL)

    zero_prev = jnp.zeros((1, SUBLANES, LRU_WIDTH), F32)
    zero_h = jnp.zeros((1, 1, LRU_WIDTH), F32)
    (qa_p, ka_p, va_p, k_p, v_p, _, lf_p, f2_p, qn_p, kn_p, lru_p, hl_p, ct_p) = _mix_in(
        xp, zero_prev, zero_h, mw, tm=MIX_TILE, streaming=True)
    prev_s = jnp.pad(state_conv[0], ((0, 0), (SUBLANES - (CONV_WIDTH - 1), 0), (0, 0)))
    (qa_s, ka_s, _, k_s, v_s, vb_s, lf_s, _, _, _, lru_s, hl_s, ct_s) = _mix_in(
        xs_, prev_s, state_lru_h[0].reshape(nb_s, 1, LRU_WIDTH), mw, tm=t_s, streaming=False)

    n_past, static_ok = _fox_plan(f2_p, qn_p, kn_p, bq=FOX_BQ, tile=MIX_TILE)
    fo_p = _fox_prompt(n_past, static_ok, qa_p, ka_p, va_p, bq=FOX_BQ)
    lf_c = cache_fox_logf[0]
    clf3 = jnp.concatenate([lf_c, lf_c, lf_c, jnp.zeros((nb_s, past, LANES - 3 * FOX_HEADS), F32)], axis=-1)
    fo_s = _fox_sample(qa_s, ka_s, vb_s, cache_fox_k, cache_fox_v, clf3, mw["ek_even"], mw["ck_even"], t=t_s)

    mk_p, mv_p = _mem_kv(mem_prompt[0], row(g_mem[0]), w_ck[0].astype(BF16), w_cv[0].astype(BF16))

    wr_pad = jnp.pad(w_router[0], ((0, 0), (0, LANES - N_EXPERTS))).astype(BF16)
    br_pad = jnp.pad(b_router[0], (0, LANES - N_EXPERTS)).reshape(1, LANES)
    pw = dict(w_out=w_out[0].astype(BF16), g_cross=row(g_cross[0]), w_cq=w_cq[0].astype(BF16),
              w_co=w_co[0].astype(BF16), g_moe=row(g_moe[0]), w_router=wr_pad, b_router=br_pad)
    cnt0 = jnp.zeros((SUBLANES, LANES), F32)
    h2_p, xn_p, idx_p, tw_p, rk_p, cnt_p = _post(
        xp, fo_p, lru_p, mk_p.reshape(1, n_mem, D_MODEL), mv_p.reshape(1, n_mem, D_MODEL), cnt0, pw,
        tm=POST_TILE, seg=POST_TILE, cached_memory=False)
    h2_s, xn_s, idx_s, tw_s, rk_s, cnt = _post(
        xs_, fo_s, lru_s, cache_mem_k, cache_mem_v, cnt_p, pw, tm=min(SAMPLE_POST_TILE, n_s), seg=t_s,
        cached_memory=True)

    n = n_p + n_s
    idx = jnp.concatenate([idx_p[:, :TOP_K], idx_s[:, :TOP_K]], axis=0)
    rank = jnp.concatenate([rk_p[:, :TOP_K], rk_s[:, :TOP_K]], axis=0)
    counts = cnt[0, :N_EXPERTS].astype(I32)
    padded = (counts + MOE_BM - 1) // MOE_BM * MOE_BM
    pad_end = jnp.cumsum(padded)
    pad_start = pad_end - padded
    dest = (pad_start[idx] + rank).reshape(n * TOP_K)
    n_blocks = -(-(n * TOP_K) // MOE_BM) + N_EXPERTS
    blk_row = jnp.arange(n_blocks, dtype=I32) * MOE_BM
    block_e = jnp.minimum(jnp.sum(pad_end[None, :] <= blk_row[:, None], axis=1), N_EXPERTS - 1).astype(I32)
    n_used = (pad_end[-1] // MOE_BM).reshape(1).astype(I32)
    last_block = jnp.maximum(pad_end // MOE_BM - 1, 0).astype(I32)

    xs_sorted = _dispatch(dest, last_block, n_used, xn_p, xn_s, tm=MOE_BM, n_blocks=n_blocks)
    ew = dict(w_gate=w_gate[0], w_up=w_up[0], w_down=w_down[0],
              b_gate=b_gate[0].reshape(N_EXPERTS, 1, D_MODEL), b_up=b_up[0].reshape(N_EXPERTS, 1, D_MODEL),
              b_down=b_down[0].reshape(N_EXPERTS, 1, D_MODEL))
    ys = _experts(block_e, n_used, (pad_end // MOE_BM).astype(I32), xs_sorted, ew, bm=MOE_BM)
    y_p, y_s = _combine(dest, h2_p, h2_s, tw_p, tw_s, row(g_final), ys, tm=COMBINE_TILE)

    shp_p = (1, nb_p, seq, FOX_HEADS, FOX_HEAD_DIM)
    shp_s = (1, nb_s, t_s, FOX_HEADS, FOX_HEAD_DIM)
    tail = slice(SUBLANES - (CONV_WIDTH - 1), SUBLANES)
    return (y_p.reshape(nb_p, seq, D_MODEL), y_s.reshape(nb_s, t_s, D_MODEL),
            k_p.reshape(shp_p), v_p.reshape(shp_p), lf_p.reshape(1, nb_p, seq, FOX_HEADS),
            hl_p.reshape(1, nb_p, LRU_WIDTH), ct_p[:, tail, :].reshape(1, nb_p, CONV_WIDTH - 1, LRU_WIDTH),
            mk_p.reshape(1, nb_p, n_mem, MEM_HEADS, MEM_HEAD_DIM), mv_p.reshape(1, nb_p, n_mem, MEM_HEADS, MEM_HEAD_DIM),
            k_s.reshape(shp_s), v_s.reshape(shp_s), lf_s.reshape(1, nb_s, t_s, FOX_HEADS),
            hl_s.reshape(1, nb_s, LRU_WIDTH), ct_s[:, tail, :].reshape(1, nb_s, CONV_WIDTH - 1, LRU_WIDTH))
```

```python
import functools
import math

import jax
import jax.numpy as jnp
import numpy as np
from jax import lax
from jax.experimental import pallas as pl
from jax.experimental.pallas import tpu as pltpu

F32 = jnp.float32
BF16 = jnp.bfloat16
I32 = jnp.int32

D_MODEL = 1024
FOX_HEADS = 8
FOX_HEAD_DIM = 64
FOX_WIDTH = FOX_HEADS * FOX_HEAD_DIM
LRU_WIDTH = D_MODEL - FOX_WIDTH
LRU_BLOCKS = 8
LRU_C = 8.0
CONV_WIDTH = 4
MEM_HEADS = 4
MEM_HEAD_DIM = D_MODEL // MEM_HEADS
N_EXPERTS = 32
TOP_K = 4
SWIGLU_LIMIT = 7.0
SWIGLU_ALPHA = 1.702
RMS_EPS = 1e-6
NEG_INF = -1e30
LOG2E = math.log2(math.e)

LANES = 128
SUBLANES = 8
CHUNKS = D_MODEL // LANES
TOKEN_TILE = (CHUNKS, SUBLANES, LANES)
AUG = LANES * FOX_HEADS
VMEM_LIMIT = 56 * 1024 * 1024

MIX_TILE = 256
POST_TILE = 512
SAMPLE_POST_TILE = 256
FOX_BQ = 512
MOE_BM = 512
COMBINE_TILE = 256
FOX_SKIP_GAP = 160.0
FOX_STATIC_SHIFT_RANGE = 100.0

C_Q, C_K, C_V, C_F, C_XR, C_G, C_END = 0, 512, 1024, 1536, 1664, 2176, 2688


def _dot(a, b):
    return jnp.dot(a, b, preferred_element_type=F32)


def _dot_nt(a, b):
    return lax.dot_general(a, b, (((1,), (1,)), ((), ())), preferred_element_type=F32)


def _split3(x):
    hi = x.astype(BF16)
    r = x - hi.astype(F32)
    mid = r.astype(BF16)
    lo = (r - mid.astype(F32)).astype(BF16)
    return hi, mid, lo


def _dot3(a, x):
    hi, mid, lo = _split3(x)
    return _dot(a, hi) + _dot(a, mid) + _dot(a, lo)


def _softplus(x):
    return jnp.maximum(x, 0.0) + jnp.log1p(jnp.exp(-jnp.abs(x)))


def _rms(x, g):
    return x * lax.rsqrt(jnp.mean(x * x, axis=-1, keepdims=True) + RMS_EPS) * g


def _tri(n, kind):
    r = lax.broadcasted_iota(I32, (n, n), 0)
    c = lax.broadcasted_iota(I32, (n, n), 1)
    m = {"le": c <= r, "lt": c < r, "gt": c > r}[kind]
    return jnp.where(m, 1.0, 0.0).astype(BF16)


def _mix_in_kernel(x_ref, g_ref, w_ref, bfp_ref, eq_ref, ek_ref, cq_ref, ck_ref, cv_ref, hs_ref, hd_ref, hm_ref,
                   cw_ref, cb_ref, wa_ref, ba_ref, wi_ref, bi_ref, lam_ref, cprev_ref, h0_ref,
                   qa_ref, ka_ref, va_ref, ko_ref, vo_ref, vb_ref, lf_ref, f2_ref, qn_ref, kn_ref,
                   lru_ref, hl_ref, ct_ref,
                   fcar, hcar, xp_ref, sa_ref, sb_ref, kbuf, vbuf, ksem, *, tm, pad, streaming):
    step = pl.program_id(0)
    first = step == 0
    slot = step % 2

    def kv_copies(s, row0):
        rows = pl.ds(row0, tm)
        copies = []
        for h in range(FOX_HEADS):
            copies.append(pltpu.make_async_copy(kbuf.at[s, h], ko_ref.at[rows, h, :], ksem.at[s]))
            copies.append(pltpu.make_async_copy(vbuf.at[s, h], vo_ref.at[rows, h, :], ksem.at[s]))
        return copies

    @pl.when(step >= 2)
    def _():
        for c in kv_copies(slot, 0):
            c.wait()

    if streaming:
        @pl.when(first)
        def _():
            fcar[...] = jnp.zeros_like(fcar)
            hcar[...] = jnp.zeros_like(hcar)
            xp_ref[0:SUBLANES, :] = jnp.zeros((SUBLANES, LRU_WIDTH), F32)
    else:
        fcar[...] = jnp.zeros_like(fcar)
        hcar[...] = jnp.broadcast_to(h0_ref[0], hcar.shape)
        xp_ref[0:SUBLANES, :] = cprev_ref[0]

    x = x_ref[...]
    xn = _rms(x, g_ref[...]).astype(BF16)
    z = _dot(xn, w_ref[...])

    zq, zk, zv = z[:, C_Q:C_K], z[:, C_K:C_V], z[:, C_V:C_F]
    for h in range(FOX_HEADS):
        hc = slice(h * FOX_HEAD_DIM, (h + 1) * FOX_HEAD_DIM)
        kbuf[slot, h] = zk[:, hc]
        vbuf[slot, h] = zv[:, hc]
    for c in kv_copies(slot, pl.multiple_of(step * tm, tm)):
        c.start()
    qd, kd, vd = zq.astype(BF16), zk.astype(BF16), zv.astype(BF16)
    vb_ref[...] = vd

    def head_pad(d):
        blocks = []
        for p in range(FOX_HEADS // 2):
            blk = d[:, p * LANES:(p + 1) * LANES]
            blocks += [blk * hm_ref[0:1, :], blk * hm_ref[1:2, :]]
        return jnp.concatenate(blocks, axis=1)

    lane = lax.broadcasted_iota(I32, (tm, LANES), 1)
    lf = jnp.where(lane < 3 * FOX_HEADS, -_softplus(-(z[:, C_F:C_XR] + bfp_ref[...])), 0.0)
    lf_ref[...] = lf[:, :FOX_HEADS]
    cum = _dot3(_tri(tm, "le"), lf) + fcar[0:1, :]
    fcar[...] = jnp.broadcast_to(cum[tm - 1:tm, :], fcar.shape)
    cum2 = cum * LOG2E
    f2_ref[...] = cum2[:, :FOX_HEADS]
    qf, kf = qd.astype(F32), kd.astype(F32)
    diag = _dot((qf * kf).astype(BF16), hd_ref[...]).astype(BF16)
    hi, mid, lo = _split3(cum2)
    bias = jnp.where(lane < 8, hi, jnp.where(lane < 16, mid, jnp.where(lane < 24, lo, diag)))
    qa_ref[...] = head_pad(qd) + (_dot(bias, eq_ref[...]) + cq_ref[...]).astype(BF16)
    ka_ref[...] = head_pad(kd) + (ck_ref[...] - _dot(bias, ek_ref[...])).astype(BF16)
    va_ref[...] = head_pad(vd) + cv_ref[...].astype(BF16)
    qn_ref[...] = jnp.broadcast_to(jnp.max(_dot((qf * qf).astype(BF16), hs_ref[...]), axis=0, keepdims=True),
                                   qn_ref.shape)
    kn_ref[...] = jnp.broadcast_to(jnp.max(_dot((kf * kf).astype(BF16), hs_ref[...]), axis=0, keepdims=True),
                                   kn_ref.shape)

    xr = z[:, C_XR:C_G]
    xp_ref[SUBLANES:SUBLANES + tm, :] = xr
    xc = cb_ref[...] + xr * cw_ref[CONV_WIDTH - 1:CONV_WIDTH, :]
    for j in range(CONV_WIDTH - 1):
        sh = CONV_WIDTH - 1 - j
        xc = xc + xp_ref[SUBLANES - sh:SUBLANES - sh + tm, :] * cw_ref[j:j + 1, :]
    tail = xp_ref[tm:tm + SUBLANES, :]
    ct_ref[0] = tail
    xp_ref[0:SUBLANES, :] = tail

    xcb = xc.astype(BF16)
    r = jax.nn.sigmoid(_dot(xcb, wa_ref[...]) + ba_ref[...])
    ig = jax.nn.sigmoid(_dot(xcb, wi_ref[...]) + bi_ref[...])
    log_a = (-LRU_C) * r * _softplus(-lam_ref[...])
    a = jnp.exp(log_a)
    mult = jnp.sqrt(-jnp.tanh(log_a) * (a * a + 1.0))
    if streaming:
        row = lax.broadcasted_iota(I32, (tm, LRU_WIDTH), 0)
        mult = jnp.where(jnp.logical_and(row == 0, first), 1.0, mult)
    b = mult * ig * xc

    sa_ref[0:pad, :] = jnp.ones((pad, LRU_WIDTH), F32)
    sb_ref[0:pad, :] = jnp.zeros((pad, LRU_WIDTH), F32)
    d = 1
    while d < tm:
        sa_ref[pad:pad + tm, :] = a
        sb_ref[pad:pad + tm, :] = b
        b = a * sb_ref[pad - d:pad - d + tm, :] + b
        a = a * sa_ref[pad - d:pad - d + tm, :]
        d *= 2
    h = a * hcar[0:1, :] + b
    hlast = h[tm - 1:tm, :]
    hcar[...] = jnp.broadcast_to(hlast, hcar.shape)
    hl_ref[0] = hlast
    lru_ref[...] = (h * jax.nn.gelu(z[:, C_G:C_END])).astype(BF16)

    last = pl.num_programs(0) - 1

    @pl.when(jnp.logical_and(step == last, step >= 1))
    def _():
        for c in kv_copies(1 - slot, 0):
            c.wait()

    @pl.when(step == last)
    def _():
        for c in kv_copies(slot, 0):
            c.wait()


def _mix_in(x, cprev, h0, wts, *, tm, streaming):
    n = x.shape[0]
    steps = n // tm
    nseg = 1 if streaming else steps
    pad = max(tm // 2, SUBLANES)
    seg = (lambda i: (0, 0, 0)) if streaming else (lambda i: (i, 0, 0))
    full = lambda a: pl.BlockSpec(a.shape, lambda i: (0,) * a.ndim)
    rows = lambda w: pl.BlockSpec((tm, w), lambda i: (i, 0))
    names = ("g_mix", "w_all", "bf_pad", "eq", "ek", "cq", "ck", "cv", "hsel", "hdiag", "hmask", "conv_w", "conv_b",
             "wa", "ba", "wi", "bi", "lam")
    ws = [wts[k] for k in names]
    sds = jax.ShapeDtypeStruct
    out_shape = (
        sds((n, AUG), BF16), sds((n, AUG), BF16), sds((n, AUG), BF16),
        sds((n, FOX_HEADS, FOX_HEAD_DIM), F32), sds((n, FOX_HEADS, FOX_HEAD_DIM), F32), sds((n, FOX_WIDTH), BF16),
        sds((n, FOX_HEADS), F32), sds((n, FOX_HEADS), F32),
        sds((steps * SUBLANES, LANES), F32), sds((steps * SUBLANES, LANES), F32),
        sds((n, LRU_WIDTH), BF16),
        sds((nseg, 1, LRU_WIDTH), F32), sds((nseg, SUBLANES, LRU_WIDTH), F32),
    )
    out_specs = (
        rows(AUG), rows(AUG), rows(AUG), pl.BlockSpec(memory_space=pl.ANY), pl.BlockSpec(memory_space=pl.ANY),
        rows(FOX_WIDTH), rows(FOX_HEADS), rows(FOX_HEADS),
        pl.BlockSpec((SUBLANES, LANES), lambda i: (i, 0)), pl.BlockSpec((SUBLANES, LANES), lambda i: (i, 0)),
        rows(LRU_WIDTH),
        pl.BlockSpec((1, 1, LRU_WIDTH), seg), pl.BlockSpec((1, SUBLANES, LRU_WIDTH), seg),
    )
    in_specs = [rows(D_MODEL)] + [full(w) for w in ws] + [
        pl.BlockSpec((1, SUBLANES, LRU_WIDTH), seg), pl.BlockSpec((1, 1, LRU_WIDTH), seg)]
    return pl.pallas_call(
        functools.partial(_mix_in_kernel, tm=tm, pad=pad, streaming=streaming),
        grid=(steps,), in_specs=in_specs, out_specs=out_specs, out_shape=out_shape,
        scratch_shapes=[
            pltpu.VMEM((SUBLANES, LANES), F32), pltpu.VMEM((SUBLANES, LRU_WIDTH), F32),
            pltpu.VMEM((tm + SUBLANES, LRU_WIDTH), F32),
            pltpu.VMEM((pad + tm, LRU_WIDTH), F32), pltpu.VMEM((pad + tm, LRU_WIDTH), F32),
            pltpu.VMEM((2, FOX_HEADS, tm, FOX_HEAD_DIM), F32), pltpu.VMEM((2, FOX_HEADS, tm, FOX_HEAD_DIM), F32),
            pltpu.SemaphoreType.DMA((2,))],
        compiler_params=pltpu.CompilerParams(dimension_semantics=("arbitrary",), vmem_limit_bytes=VMEM_LIMIT),
        name="mix_in_stream" if streaming else "mix_in_segments",
    )(x, *ws, cprev, h0)


def _fox_prompt_kernel(nb_ref, st_ref, q_ref, k_ref, v_ref, o_ref, m_ref, acc_ref, *, bq):
    qi = pl.program_id(1)
    plan = pl.program_id(0) * pl.num_programs(1) + qi
    n_past = nb_ref[plan]
    cols = [slice(LANES * c, LANES * (c + 1)) for c in range(2)]
    qs = [q_ref[:, cols[c]] for c in range(2)]
    acc_ref[...] = jnp.zeros(acc_ref.shape, F32)

    def scores(c, start, causal):
        s = _dot_nt(qs[c], k_ref[pl.ds(start, bq), cols[c]])
        if causal:
            r = lax.broadcasted_iota(I32, (bq, bq), 0)
            cc = lax.broadcasted_iota(I32, (bq, bq), 1)
            s = jnp.where(cc <= r, s, NEG_INF)
        return s

    def static_block(start, causal):
        for c in range(2):
            p = jnp.exp2(scores(c, start, causal)).astype(BF16)
            acc_ref[c] += _dot(p, v_ref[pl.ds(start, bq), cols[c]])

    def online_block(start, causal):
        for c in range(2):
            s = scores(c, start, causal)
            m_old = m_ref[c]
            m_new = jnp.maximum(m_old, jnp.max(s, axis=1, keepdims=True))
            p = jnp.exp2(s - m_new).astype(BF16)
            acc_ref[c] = jnp.exp2(m_old - m_new) * acc_ref[c] + _dot(p, v_ref[pl.ds(start, bq), cols[c]])
            m_ref[c] = m_new

    def run(block):
        block(pl.multiple_of(qi * bq, bq), True)

        def body(t, carry):
            block(pl.multiple_of((qi - 1 - t) * bq, bq), False)
            return carry

        lax.fori_loop(0, n_past, body, 0)

    @pl.when(st_ref[plan] == 1)
    def _():
        run(static_block)

    @pl.when(st_ref[plan] != 1)
    def _():
        m_ref[...] = jnp.full(m_ref.shape, NEG_INF, F32)
        run(online_block)

    a0, a1 = acc_ref[0], acc_ref[1]
    o0 = a0 / a0[:, FOX_HEAD_DIM:FOX_HEAD_DIM + 1]
    o1 = a1 / a1[:, 0:1]
    lane = lax.broadcasted_iota(I32, (bq, LANES), 1)
    o_ref[...] = jnp.where(lane < FOX_HEAD_DIM, o0, o1).astype(BF16)


def _fox_prompt(n_past, static_ok, qa, ka, va, *, bq):
    s = qa.shape[0]
    pairs = FOX_HEADS // 2
    return pl.pallas_call(
        functools.partial(_fox_prompt_kernel, bq=bq),
        grid_spec=pltpu.PrefetchScalarGridSpec(
            num_scalar_prefetch=2, grid=(pairs, s // bq),
            in_specs=[pl.BlockSpec((bq, 2 * LANES), lambda p, i, nb, st: (i, p)),
                      pl.BlockSpec((s, 2 * LANES), lambda p, i, nb, st: (0, p)),
                      pl.BlockSpec((s, 2 * LANES), lambda p, i, nb, st: (0, p))],
            out_specs=pl.BlockSpec((bq, LANES), lambda p, i, nb, st: (i, p)),
            scratch_shapes=[pltpu.VMEM((2, bq, 1), F32), pltpu.VMEM((2, bq, LANES), F32)]),
        out_shape=jax.ShapeDtypeStruct((s, FOX_WIDTH), BF16),
        compiler_params=pltpu.CompilerParams(dimension_semantics=("arbitrary", "arbitrary"),
                                             vmem_limit_bytes=VMEM_LIMIT),
        name="fox_prompt",
    )(n_past, static_ok, qa, ka, va)


def _fox_plan(f2, qn, kn, *, bq, tile):
    s = f2.shape[0]
    nq = s // bq
    per = lambda a: a.reshape(s // tile, SUBLANES, LANES)[:, 0, :FOX_HEADS]
    qnorm = jnp.sqrt(jnp.max(per(qn).reshape(nq, bq // tile, FOX_HEADS), axis=1))
    knorm = jnp.sqrt(jnp.max(per(kn), axis=0))
    spread = 2.0 * 1.02 * qnorm * knorm[None, :]
    thr = spread + FOX_SKIP_GAP
    f_first = f2[0::bq]
    f_last = f2[bq - 1::bq]
    gap = f_last[None, :, :] - f_first[:, None, :]
    before = (jnp.arange(nq)[None, :] < jnp.arange(nq)[:, None])[:, :, None]
    need = jnp.sum(jnp.logical_and(before, gap < thr[:, None, :]), axis=1)
    pair = lambda a: a.reshape(nq, FOX_HEADS // 2, 2)
    n_past = jnp.max(pair(need), axis=2).T.reshape(-1).astype(I32)
    static_ok = jnp.all(pair(spread) <= FOX_STATIC_SHIFT_RANGE, axis=2).T.reshape(-1).astype(I32)
    return n_past, static_ok


def _fox_sample_kernel(q_ref, kn_ref, vn_ref, kt_ref, vt_ref, clf_ref, sel_ref, o_ref, g_ref,
                       *, t, past, chunk):
    car = jnp.zeros((1, LANES), F32)
    upper = _tri(chunk, "gt")
    for ci in reversed(range(past // chunk)):
        lf = clf_ref[0, ci * chunk:(ci + 1) * chunk, :]
        g_ref[ci * chunk:(ci + 1) * chunk, :] = _dot3(upper, lf) + car
        car = car + jnp.sum(lf, axis=0, keepdims=True)
    hi, mid, lo = _split3(g_ref[...] * LOG2E)
    lane = lax.broadcasted_iota(I32, (past, LANES), 1)
    parts = jnp.where(lane < 8, hi.astype(F32), jnp.where(lane < 16, mid.astype(F32), lo.astype(F32)))
    bias_rows = jnp.where(lane == 3 * FOX_HEADS, 1.0, parts).T.astype(BF16)

    def swap_halves(x):
        return pltpu.roll(x.astype(F32), FOX_HEAD_DIM, axis=1).astype(BF16)

    r = lax.broadcasted_iota(I32, (t, t), 0)
    cc = lax.broadcasted_iota(I32, (t, t), 1)
    prow = lax.broadcasted_iota(I32, (FOX_HEAD_DIM, LANES), 0)
    plane = lax.broadcasted_iota(I32, (FOX_HEAD_DIM, LANES), 1)
    place = [jnp.where(plane == prow + FOX_HEAD_DIM * c, 1.0, 0.0).astype(BF16) for c in range(2)]
    for p in range(FOX_HEADS // 2):
        pc = slice(LANES * p, LANES * (p + 1))
        vpair = vn_ref[:, pc]
        o_pair = jnp.zeros((t, LANES), F32)
        for c in range(2):
            h = 2 * p + c
            hc = slice(LANES * h, LANES * (h + 1))
            q, kn, vn = q_ref[:, hc], kn_ref[:, hc], vpair
            if c == 1:
                q, kn, vn = swap_halves(q), swap_halves(kn), swap_halves(vn)
            bias = _dot(sel_ref[h], bias_rows).astype(BF16)
            kpast_t = jnp.concatenate([kt_ref[0, h].astype(BF16), bias], axis=0)
            sp = _dot(q, kpast_t)
            sn = jnp.where(cc <= r, _dot_nt(q, kn), NEG_INF)
            m = jnp.maximum(jnp.max(sp, axis=1, keepdims=True), jnp.max(sn, axis=1, keepdims=True))
            pp = jnp.exp2(sp - m)
            pn = jnp.exp2(sn - m)
            l = jnp.sum(pp, axis=1, keepdims=True) + jnp.sum(pn, axis=1, keepdims=True)
            o = (_dot_nt(pp.astype(BF16), vt_ref[0, h].astype(BF16))
                 + _dot(pn.astype(BF16), vn[:, :FOX_HEAD_DIM])) / l
            o_pair = o_pair + _dot(o.astype(BF16), place[c])
        o_ref[:, pc] = o_pair.astype(BF16)


def _fox_sample(qa, ka, vb, cache_kt, cache_vt, cache_lf3, sel, *, t):
    nb, past = cache_kt.shape[0], cache_kt.shape[3]
    full = lambda a: pl.BlockSpec(a.shape, lambda b: (0,) * a.ndim)
    slab = pl.BlockSpec((1, FOX_HEADS, FOX_HEAD_DIM, past), lambda b: (b, 0, 0, 0))
    return pl.pallas_call(
        functools.partial(_fox_sample_kernel, t=t, past=past, chunk=256),
        grid=(nb,),
        in_specs=[pl.BlockSpec((t, AUG), lambda b: (b, 0)), pl.BlockSpec((t, AUG), lambda b: (b, 0)),
                  pl.BlockSpec((t, FOX_WIDTH), lambda b: (b, 0)), slab, slab,
                  pl.BlockSpec((1, past, LANES), lambda b: (b, 0, 0)), full(sel)],
        out_specs=pl.BlockSpec((t, FOX_WIDTH), lambda b: (b, 0)),
        out_shape=jax.ShapeDtypeStruct((nb * t, FOX_WIDTH), BF16),
        scratch_shapes=[pltpu.VMEM((past, LANES), F32)],
        compiler_params=pltpu.CompilerParams(dimension_semantics=("arbitrary",), vmem_limit_bytes=VMEM_LIMIT),
        name="fox_sample",
    )(qa, ka, vb, cache_kt, cache_vt, cache_lf3, sel)


def _mem_kv_kernel(m_ref, g_ref, wk_ref, wv_ref, k_ref, v_ref):
    mn = _rms(m_ref[...], g_ref[...]).astype(BF16)
    k_ref[...] = _dot(mn, wk_ref[...])
    v_ref[...] = _dot(mn, wv_ref[...])


def _mem_kv(mem, g, wk, wv):
    n = mem.shape[0]
    return pl.pallas_call(
        _mem_kv_kernel,
        out_shape=(jax.ShapeDtypeStruct((n, D_MODEL), F32), jax.ShapeDtypeStruct((n, D_MODEL), F32)),
        compiler_params=pltpu.CompilerParams(vmem_limit_bytes=VMEM_LIMIT),
        name="mem_kv",
    )(mem, g, wk, wv)


def _post_kernel(h_ref, fo_ref, lru_ref, wo_ref, gc_ref, wcq_ref, mk_ref, mv_ref, wco_ref,
                 gm_ref, wr_ref, br_ref, cnt_in_ref,
                 h2_ref, xn_ref, idx_ref, tw_ref, rank_ref, cnt_ref, car_ref, *mem_scratch, tm, seg, cached_memory):
    step = pl.program_id(0)
    nseg = tm // seg

    @pl.when(step == 0)
    def _():
        car_ref[...] = cnt_in_ref[...]

    if cached_memory:
        mbuf, msem = mem_scratch
        slot = step % 2

        def mem_copies(at_step, s):
            return [pltpu.make_async_copy(ref.at[0, at_step * nseg + j, :, hd, :], mbuf.at[s, j, t, hd], msem.at[s])
                    for j in range(nseg) for t, ref in enumerate((mk_ref, mv_ref)) for hd in range(MEM_HEADS)]

        @pl.when(step == 0)
        def _():
            for c in mem_copies(0, 0):
                c.start()

        @pl.when(step + 1 < pl.num_programs(0))
        def _():
            for c in mem_copies(step + 1, 1 - slot):
                c.start()

    h1 = h_ref[...] + _dot(fo_ref[...], wo_ref[0:FOX_WIDTH, :]) + _dot(lru_ref[...], wo_ref[FOX_WIDTH:D_MODEL, :])

    q = _dot(_rms(h1, gc_ref[...]).astype(BF16), wcq_ref[...])
    if cached_memory:
        for c in mem_copies(0, slot):
            c.wait()
    segs = []
    for j in range(nseg):
        rows = slice(j * seg, (j + 1) * seg)
        heads = []
        for hd in range(MEM_HEADS):
            hc = slice(MEM_HEAD_DIM * hd, MEM_HEAD_DIM * (hd + 1))
            mk = mbuf[slot, j, 0, hd] if cached_memory else mk_ref[0, :, hc]
            mv = mbuf[slot, j, 1, hd] if cached_memory else mv_ref[0, :, hc]
            s = _dot_nt(q[rows, hc].astype(BF16), mk.astype(BF16)) * (MEM_HEAD_DIM ** -0.5)
            p = jnp.exp(s - jnp.max(s, axis=1, keepdims=True))
            o = _dot(p.astype(BF16), mv.astype(BF16)) / jnp.sum(p, axis=1, keepdims=True)
            heads.append(o.astype(BF16))
        segs.append(jnp.concatenate(heads, axis=1))
    h2 = h1 + _dot(segs[0] if nseg == 1 else jnp.concatenate(segs, axis=0), wco_ref[...])
    h2_ref[...] = h2

    xn = _rms(h2, gm_ref[...])
    for c in range(D_MODEL // LANES):
        xn_ref[:, c] = xn[:, c * LANES:(c + 1) * LANES].reshape(tm // SUBLANES, SUBLANES, LANES)

    lane = lax.broadcasted_iota(I32, (tm, LANES), 1).astype(F32)
    logits = jnp.where(lane < N_EXPERTS, _dot(xn.astype(BF16), wr_ref[...]) + br_ref[...], -jnp.inf)
    vals, idxs = [], []
    for _ in range(TOP_K):
        mx = jnp.max(logits, axis=1, keepdims=True)
        ix = jnp.min(jnp.where(logits == mx, lane, float(LANES)), axis=1, keepdims=True)
        vals.append(mx)
        idxs.append(ix)
        logits = jnp.where(lane == ix, -jnp.inf, logits)
    es = [jnp.exp(v - vals[0]) for v in vals]
    den = es[0] + es[1] + es[2] + es[3]

    onehot = jnp.zeros((tm, LANES), F32)
    for ix in idxs:
        onehot = onehot + jnp.where(lane == ix, 1.0, 0.0)
    before = _dot(_tri(tm, "lt"), onehot.astype(BF16)) + car_ref[0:1, :]
    car = car_ref[0:1, :] + jnp.sum(onehot, axis=0, keepdims=True)
    car_ref[...] = jnp.broadcast_to(car, car_ref.shape)
    cnt_ref[...] = jnp.broadcast_to(car, cnt_ref.shape)

    idx_o = jnp.zeros((tm, LANES), F32)
    tw_o = jnp.zeros((tm, LANES), F32)
    rk_o = jnp.zeros((tm, LANES), F32)
    for j in range(TOP_K):
        rk = jnp.sum(jnp.where(lane == idxs[j], before, 0.0), axis=1, keepdims=True)
        idx_o = jnp.where(lane == j, idxs[j], idx_o)
        tw_o = jnp.where(lane == j, es[j] / den, tw_o)
        rk_o = jnp.where(lane == j, rk, rk_o)
    idx_ref[...] = idx_o.astype(I32)
    tw_ref[...] = tw_o
    rank_ref[...] = rk_o.astype(I32)


def _post(h, fo, lru, mk, mv, cnt_in, wts, *, tm, seg, cached_memory):
    n = h.shape[0]
    full = lambda a: pl.BlockSpec(a.shape, lambda i: (0,) * a.ndim)
    rows = lambda w: pl.BlockSpec((tm, w), lambda i: (i, 0))
    if cached_memory:
        nm = mk.shape[2]
        mem = pl.BlockSpec(memory_space=pl.ANY)
        mem_scratch = [pltpu.VMEM((2, tm // seg, 2, MEM_HEADS, nm, MEM_HEAD_DIM), F32),
                       pltpu.SemaphoreType.DMA((2,))]
    else:
        mem = full(mk)
        mem_scratch = []
    w = [wts[k] for k in ("w_out", "g_cross", "w_cq")]
    w2 = [wts[k] for k in ("w_co", "g_moe", "w_router", "b_router")]
    return pl.pallas_call(
        functools.partial(_post_kernel, tm=tm, seg=seg, cached_memory=cached_memory),
        grid=(n // tm,),
        in_specs=[rows(D_MODEL), rows(FOX_WIDTH), rows(LRU_WIDTH)] + [full(a) for a in w] + [mem, mem]
                 + [full(a) for a in w2] + [full(cnt_in)],
        out_specs=(rows(D_MODEL), pl.BlockSpec((tm // SUBLANES,) + TOKEN_TILE, lambda i: (i, 0, 0, 0)),
                   rows(LANES), rows(LANES), rows(LANES),
                   pl.BlockSpec((SUBLANES, LANES), lambda i: (0, 0))),
        out_shape=(jax.ShapeDtypeStruct((n, D_MODEL), F32), jax.ShapeDtypeStruct((n // SUBLANES,) + TOKEN_TILE, F32),
                   jax.ShapeDtypeStruct((n, LANES), I32), jax.ShapeDtypeStruct((n, LANES), F32),
                   jax.ShapeDtypeStruct((n, LANES), I32), jax.ShapeDtypeStruct((SUBLANES, LANES), F32)),
        scratch_shapes=[pltpu.VMEM((SUBLANES, LANES), F32)] + mem_scratch,
        compiler_params=pltpu.CompilerParams(dimension_semantics=("arbitrary",), vmem_limit_bytes=VMEM_LIMIT),
        name="post_batch_mem" if cached_memory else "post_shared_mem",
    )(h, fo, lru, *w, mk, mv, *w2, cnt_in)


STAGES = 3


def _dispatch_kernel(dest_ref, last_ref, nu_ref, xp_ref, xs_ref, out_ref, zero_ref, stage, sems, gsem, zsem,
                     *, tm, steps_p, n_blocks):
    i = pl.program_id(0)

    def zero_copy(b):
        return pltpu.make_async_copy(zero_ref, out_ref.at[pl.ds(pl.multiple_of(b * tm, tm), tm)], zsem)

    @pl.when(i == 0)
    def _():
        zero_ref[...] = jnp.zeros(zero_ref.shape, F32)
        n_tail = n_blocks - nu_ref[0]

        def start_e(e, c):
            zero_copy(last_ref[e]).start()
            return c

        def start_t(b, c):
            zero_copy(nu_ref[0] + b).start()
            return c

        def wait_one(b, c):
            zero_copy(0).wait()
            return c

        lax.fori_loop(0, N_EXPERTS, start_e, 0)
        lax.fori_loop(0, n_tail, start_t, 0)
        lax.fori_loop(0, N_EXPERTS + n_tail, wait_one, 0)

    groups = tm // SUBLANES
    steps = pl.num_programs(0)

    def stage_copy(src_ref, tile, slot):
        return pltpu.make_async_copy(src_ref.at[pl.ds(tile * groups, groups)], stage.at[slot], gsem.at[slot])

    def start_stage(tile, slot):
        @pl.when(tile < steps_p)
        def _():
            stage_copy(xp_ref, tile, slot).start()

        @pl.when(tile >= steps_p)
        def _():
            stage_copy(xs_ref, tile - steps_p, slot).start()

    @pl.when(i == 0)
    def _():
        start_stage(0, 0)

    @pl.when(i + 1 < steps)
    def _():
        start_stage(i + 1, (i + 1) % STAGES)

    slot = i % STAGES
    stage_copy(xp_ref, 0, slot).wait()

    base = i * (tm * TOP_K)

    def start(g, c):
        for u in range(SUBLANES):
            for j in range(TOP_K):
                d = dest_ref[base + (g * SUBLANES + u) * TOP_K + j]
                pltpu.make_async_copy(stage.at[slot, g, :, u, :], out_ref.at[d],
                                      sems.at[i % 2]).start(priority=j % 2)
        return c

    lax.fori_loop(0, groups, start, 0)

    def drain(parity):
        for _ in range(TOP_K):
            pltpu.make_async_copy(out_ref.at[pl.ds(0, tm)], out_ref.at[pl.ds(0, tm)], sems.at[parity]).wait()

    @pl.when(i > 0)
    def _():
        drain((i - 1) % 2)

    @pl.when(i == steps - 1)
    def _():
        drain(i % 2)


def _dispatch(dest_flat, last_block, n_used, xn_p, xn_s, *, tm, n_blocks):
    steps_p, steps_s = xn_p.shape[0] * SUBLANES // tm, xn_s.shape[0] * SUBLANES // tm
    return pl.pallas_call(
        functools.partial(_dispatch_kernel, tm=tm, steps_p=steps_p, n_blocks=n_blocks),
        grid_spec=pltpu.PrefetchScalarGridSpec(
            num_scalar_prefetch=3, grid=(steps_p + steps_s,),
            in_specs=[pl.BlockSpec(memory_space=pl.ANY), pl.BlockSpec(memory_space=pl.ANY)],
            out_specs=pl.BlockSpec(memory_space=pl.ANY),
            scratch_shapes=[pltpu.VMEM((tm, CHUNKS, LANES), F32),
                            pltpu.VMEM((STAGES, tm // SUBLANES) + TOKEN_TILE, F32),
                            pltpu.SemaphoreType.DMA((2,)), pltpu.SemaphoreType.DMA((STAGES,)),
                            pltpu.SemaphoreType.DMA(())]),
        out_shape=jax.ShapeDtypeStruct((n_blocks * tm, CHUNKS, LANES), F32),
        compiler_params=pltpu.CompilerParams(dimension_semantics=("arbitrary",), vmem_limit_bytes=VMEM_LIMIT),
        name="moe_dispatch",
    )(dest_flat, last_block, n_used, xn_p, xn_s)


def _rows_from_tiles(view, rows):
    return jnp.concatenate([view[:, c].reshape(rows, LANES) for c in range(CHUNKS)], axis=1)


def _rows_to_tiles(view, val, rows):
    for c in range(CHUNKS):
        view[:, c] = val[:, c * LANES:(c + 1) * LANES].reshape(rows // SUBLANES, SUBLANES, LANES)


def _expert_kernel(be_ref, nu_ref, nx_ref, xs_ref, wg_ref, bg_ref, wu_ref, bu_ref, wd_ref, bd_ref, ys_ref,
                   wgb, wub, wdb, wbuf, xbuf, ybuf, seq, wsem, xsem, ysem, *, bm):
    i = pl.program_id(0)
    steps = pl.num_programs(0)
    groups = bm // SUBLANES
    slot = i % 2
    prev = be_ref[jnp.maximum(i - 1, 0)]
    fresh = jnp.logical_or(i == 0, be_ref[i] != prev)
    live = i < nu_ref[0]

    def w_copies(e, s):
        return [pltpu.make_async_copy(w_ref.at[e], wbuf.at[s, k], wsem.at[s])
                for k, w_ref in enumerate((wg_ref, wu_ref, wd_ref))]

    @pl.when(i == 0)
    def _():
        seq[0] = 0
        for c in w_copies(be_ref[0], 0):
            c.start()

    def in_copies(blk, s):
        g0 = pl.multiple_of(blk * groups, groups)
        return [pltpu.make_async_copy(xs_ref.at[pl.ds(g0, groups), u], xbuf.at[s, :, :, u, :], xsem.at[s])
                for u in range(SUBLANES)]

    def out_copies(blk, s):
        g0 = pl.multiple_of(blk * groups, groups)
        return [pltpu.make_async_copy(ybuf.at[s, :, :, u, :], ys_ref.at[pl.ds(g0, groups), u], ysem.at[s])
                for u in range(SUBLANES)]

    @pl.when(i == 0)
    def _():
        for c in in_copies(0, 0):
            c.start()

    @pl.when(i + 1 < nu_ref[0])
    def _():
        for c in in_copies(i + 1, (i + 1) % 2):
            c.start()

    @pl.when(i >= 2)
    def _():
        for c in out_copies(0, slot):
            c.wait()

    @pl.when(jnp.logical_and(live, fresh))
    def _():
        s = seq[0] % 2
        for c in w_copies(0, s):
            c.wait()
        for static_s in range(2):
            @pl.when(s == static_s)
            def _():
                wgb[...] = wbuf[static_s, 0].astype(BF16)
                wub[...] = wbuf[static_s, 1].astype(BF16)
                wdb[...] = wbuf[static_s, 2].astype(BF16)

        nxt = nx_ref[be_ref[i]]

        @pl.when(nxt < nu_ref[0])
        def _():
            for c in w_copies(be_ref[jnp.minimum(nxt, steps - 1)], 1 - s):
                c.start()

        seq[0] = seq[0] + 1

    @pl.when(live)
    def _():
        for c in in_copies(0, slot):
            c.wait()
        x = _rows_from_tiles(xbuf.at[slot], bm).astype(BF16)
        g = jnp.minimum(_dot(x, wgb[...]) + bg_ref[0], SWIGLU_LIMIT)
        u = jnp.clip(_dot(x, wub[...]) + bu_ref[0], -SWIGLU_LIMIT, SWIGLU_LIMIT)
        hdn = g * jax.nn.sigmoid(SWIGLU_ALPHA * g) * (u + 1.0)
        _rows_to_tiles(ybuf.at[slot], _dot(hdn.astype(BF16), wdb[...]) + bd_ref[0], bm)

    @pl.when(jnp.logical_not(live))
    def _():
        ybuf[slot] = jnp.zeros(ybuf.shape[1:], F32)

    for c in out_copies(i, slot):
        c.start()

    @pl.when(jnp.logical_and(i == steps - 1, i >= 1))
    def _():
        for c in out_copies(0, 1 - slot):
            c.wait()

    @pl.when(i == steps - 1)
    def _():
        for c in out_copies(0, slot):
            c.wait()


def _experts(block_e, n_used, next_first, xs, wts, *, bm):
    p = xs.shape[0]
    grouped = (p // SUBLANES, SUBLANES, CHUNKS, LANES)
    anyspace = pl.BlockSpec(memory_space=pl.ANY)
    bspec = pl.BlockSpec((1, 1, D_MODEL), lambda i, be, nu, nx: (be[i], 0, 0))
    tiles = (2, bm // SUBLANES) + TOKEN_TILE
    ys = pl.pallas_call(
        functools.partial(_expert_kernel, bm=bm),
        grid_spec=pltpu.PrefetchScalarGridSpec(
            num_scalar_prefetch=3, grid=(p // bm,),
            in_specs=[anyspace, anyspace, bspec, anyspace, bspec, anyspace, bspec],
            out_specs=anyspace,
            scratch_shapes=[pltpu.VMEM((D_MODEL, D_MODEL), BF16)] * 3 + [
                pltpu.VMEM((2, 3, D_MODEL, D_MODEL), F32), pltpu.VMEM(tiles, F32), pltpu.VMEM(tiles, F32),
                pltpu.SMEM((1,), I32),
                pltpu.SemaphoreType.DMA((2,)), pltpu.SemaphoreType.DMA((2,)), pltpu.SemaphoreType.DMA((2,))]),
        out_shape=jax.ShapeDtypeStruct(grouped, F32),
        compiler_params=pltpu.CompilerParams(dimension_semantics=("arbitrary",), vmem_limit_bytes=VMEM_LIMIT),
        name="moe_experts",
    )(block_e, n_used, next_first, xs.reshape(grouped), wts["w_gate"], wts["b_gate"], wts["w_up"], wts["b_up"],
      wts["w_down"], wts["b_down"])
    return ys.reshape(p, CHUNKS, LANES)


def _combine_kernel(dest_ref, hp_ref, hs_ref, twp_ref, tws_ref, gf_ref, ys_ref, yp_ref, ysm_ref, buf, sems,
                    *, tm, steps_p):
    i = pl.program_id(0)
    slot = i % 2

    def issue(tile, into):
        base = tile * (tm * TOP_K)

        def start(g, c):
            for u in range(SUBLANES):
                for j in range(TOP_K):
                    d = dest_ref[base + (g * SUBLANES + u) * TOP_K + j]
                    pltpu.make_async_copy(ys_ref.at[d], buf.at[into, j, g, :, u, :],
                                          sems.at[into]).start(priority=j % 2)
            return c

        lax.fori_loop(0, tm // SUBLANES, start, 0)

    @pl.when(i == 0)
    def _():
        issue(0, 0)

    @pl.when(i + 1 < pl.num_programs(0))
    def _():
        issue(i + 1, (i + 1) % 2)

    for j in range(TOP_K):
        pltpu.make_async_copy(ys_ref.at[pl.ds(0, tm)], ys_ref.at[pl.ds(0, tm)], sems.at[slot]).wait()

    def finish(h_ref, tw_ref, y_ref):
        tw = tw_ref[...]
        y = h_ref[...]
        for j in range(TOP_K):
            y = y + _rows_from_tiles(buf.at[slot, j], tm) * tw[:, j:j + 1]
        y_ref[...] = _rms(y, gf_ref[...])

    @pl.when(i < steps_p)
    def _():
        finish(hp_ref, twp_ref, yp_ref)

    @pl.when(i >= steps_p)
    def _():
        finish(hs_ref, tws_ref, ysm_ref)


def _combine(dest_flat, h2_p, h2_s, tw_p, tw_s, g_final, ys, *, tm):
    steps_p, steps_s = h2_p.shape[0] // tm, h2_s.shape[0] // tm
    pblk = lambda w: pl.BlockSpec((tm, w), lambda i, d: (jnp.minimum(i, steps_p - 1), 0))
    sblk = lambda w: pl.BlockSpec((tm, w), lambda i, d: (jnp.maximum(i - steps_p, 0), 0))
    return pl.pallas_call(
        functools.partial(_combine_kernel, tm=tm, steps_p=steps_p),
        grid_spec=pltpu.PrefetchScalarGridSpec(
            num_scalar_prefetch=1, grid=(steps_p + steps_s,),
            in_specs=[pblk(D_MODEL), sblk(D_MODEL), pblk(LANES), sblk(LANES),
                      pl.BlockSpec((1, D_MODEL), lambda i, d: (0, 0)),
                      pl.BlockSpec(memory_space=pl.ANY)],
            out_specs=(pblk(D_MODEL), sblk(D_MODEL)),
            scratch_shapes=[pltpu.VMEM((2, TOP_K, tm // SUBLANES) + TOKEN_TILE, F32),
                            pltpu.SemaphoreType.DMA((2,))]),
        out_shape=(jax.ShapeDtypeStruct(h2_p.shape, F32), jax.ShapeDtypeStruct(h2_s.shape, F32)),
        compiler_params=pltpu.CompilerParams(dimension_semantics=("arbitrary",), vmem_limit_bytes=VMEM_LIMIT),
        name="moe_combine",
    )(dest_flat, h2_p, h2_s, tw_p, tw_s, g_final, ys)


def _aug_layout():
    main, extra = [], []
    for h in range(FOX_HEADS):
        even = h % 2 == 0
        main.append(LANES * h + (0 if even else FOX_HEAD_DIM))
        extra.append(LANES * h + (FOX_HEAD_DIM if even else 0))
    return main, extra


def _prep_mix_weights(g_mix, w_in, b_f, conv_w, conv_b, w_a, b_a, w_i, b_i, lam):
    _, extra = _aug_layout()
    wq = w_in[:, 0:FOX_WIDTH] * (FOX_HEAD_DIM ** -0.5 * LOG2E)
    wf = w_in[:, 3 * FOX_WIDTH:3 * FOX_WIDTH + FOX_HEADS]
    wf_pad = jnp.concatenate([wf, wf, wf, jnp.zeros((D_MODEL, LANES - 3 * FOX_HEADS), F32)], axis=1)
    w_all = jnp.concatenate([wq, w_in[:, FOX_WIDTH:3 * FOX_WIDTH], wf_pad, w_in[:, 3 * FOX_WIDTH + FOX_HEADS:]],
                            axis=1).astype(BF16)
    bf_pad = jnp.concatenate([b_f, b_f, b_f, jnp.zeros((LANES - 3 * FOX_HEADS,), F32)]).reshape(1, LANES)
    eq = np.zeros((LANES, AUG), np.float32)
    ek = np.zeros((LANES, AUG), np.float32)
    cq = np.zeros((1, AUG), np.float32)
    ck = np.zeros((1, AUG), np.float32)
    cv = np.zeros((1, AUG), np.float32)
    hsel = np.zeros((FOX_WIDTH, LANES), np.float32)
    hdiag = np.zeros((FOX_WIDTH, LANES), np.float32)
    hmask = np.zeros((2, LANES), np.float32)
    hmask[0, :FOX_HEAD_DIM] = 1.0
    hmask[1, FOX_HEAD_DIM:] = 1.0
    for h in range(FOX_HEADS):
        hsel[FOX_HEAD_DIM * h:FOX_HEAD_DIM * (h + 1), h] = 1.0
        hdiag[FOX_HEAD_DIM * h:FOX_HEAD_DIM * (h + 1), 3 * FOX_HEADS + h] = 1.0
        cv[0, extra[h]] = 1.0
        eq[3 * FOX_HEADS + h, extra[h] + 6] = -1.0
        ck[0, extra[h] + 6] = 1.0
        for part in range(3):
            eq[part * 8 + h, extra[h] + part] = 1.0
            ek[part * 8 + h, extra[h] + 3 + part] = 1.0
            cq[0, extra[h] + 3 + part] = 1.0
            ck[0, extra[h] + part] = 1.0
    bias_sel = np.zeros((FOX_HEADS, FOX_HEAD_DIM, LANES), np.float32)
    for h in range(FOX_HEADS):
        for part in range(3):
            bias_sel[h, part, 3 * FOX_HEADS] = 1.0
            bias_sel[h, 3 + part, part * 8 + h] = 1.0
        bias_sel[h, 6, 3 * FOX_HEADS] = 1.0
    dense = lambda w: jax.scipy.linalg.block_diag(*[w[i] for i in range(LRU_BLOCKS)]).astype(BF16)
    row = lambda a: a.reshape(1, -1)
    b16 = lambda a: jnp.asarray(a, BF16)
    return dict(g_mix=row(g_mix), w_all=w_all, bf_pad=bf_pad, eq=b16(eq), ek=b16(ek),
                cq=jnp.asarray(cq), ck=jnp.asarray(ck), cv=jnp.asarray(cv),
                bias_sel=b16(bias_sel),
                hsel=b16(hsel), hdiag=b16(hdiag), hmask=b16(hmask), conv_w=conv_w, conv_b=row(conv_b),
                wa=dense(w_a), ba=row(b_a), wi=dense(w_i), bi=row(b_i), lam=row(lam))


def kernel(x_prompt, x_sample, mem_prompt, cache_fox_k, cache_fox_v, cache_fox_logf, state_lru_h, state_conv, cache_mem_k, cache_mem_v, g_mix, w_in, b_f, conv_w, conv_b, w_a, b_a, w_i, b_i, lam, w_out, g_cross, g_mem, w_cq, w_ck, w_cv, w_co, g_moe, w_router, b_router, w_gate, b_gate, w_up, b_up, w_down, b_down, g_final):
    nb_p, seq, _ = x_prompt.shape
    nb_s, t_s, _ = x_sample.shape
    past = cache_fox_k.shape[2]
    n_mem = mem_prompt.shape[1]
    assert nb_p == 1 and g_mix.shape[0] == 1
    n_p, n_s = seq, nb_s * t_s
    row = lambda a: a.reshape(1, -1)

    mw = _prep_mix_weights(g_mix[0], w_in[0], b_f[0], conv_w[0], conv_b[0], w_a[0], b_a[0], w_i[0], b_i[0], lam[0])
    xp = x_prompt.reshape(n_p, D_MODEL)
    xs_ = x_sample.reshape(n_s, D_MODEL)

    zero_prev = jnp.zeros((1, SUBLANES, LRU_WIDTH), F32)
    zero_h = jnp.zeros((1, 1, LRU_WIDTH), F32)
    (qa_p, ka_p, va_p, k_p, v_p, _, lf_p, f2_p, qn_p, kn_p, lru_p, hl_p, ct_p) = _mix_in(
        xp, zero_prev, zero_h, mw, tm=MIX_TILE, streaming=True)
    prev_s = jnp.pad(state_conv[0], ((0, 0), (SUBLANES - (CONV_WIDTH - 1), 0), (0, 0)))
    (qa_s, ka_s, _, k_s, v_s, vb_s, lf_s, _, _, _, lru_s, hl_s, ct_s) = _mix_in(
        xs_, prev_s, state_lru_h[0].reshape(nb_s, 1, LRU_WIDTH), mw, tm=t_s, streaming=False)

    n_past, static_ok = _fox_plan(f2_p, qn_p, kn_p, bq=FOX_BQ, tile=MIX_TILE)
    fo_p = _fox_prompt(n_past, static_ok, qa_p, ka_p, va_p, bq=FOX_BQ)
    lf_c = cache_fox_logf[0]
    clf3 = jnp.concatenate([lf_c, lf_c, lf_c, jnp.zeros((nb_s, past, LANES - 3 * FOX_HEADS), F32)], axis=-1)
    by_head = lambda c: jnp.transpose(c[0], (0, 2, 3, 1))
    fo_s = _fox_sample(qa_s, ka_s, vb_s, by_head(cache_fox_k), by_head(cache_fox_v), clf3, mw["bias_sel"], t=t_s)

    mk_p, mv_p = _mem_kv(mem_prompt[0], row(g_mem[0]), w_ck[0].astype(BF16), w_cv[0].astype(BF16))

    wr_pad = jnp.pad(w_router[0], ((0, 0), (0, LANES - N_EXPERTS))).astype(BF16)
    br_pad = jnp.pad(b_router[0], (0, LANES - N_EXPERTS)).reshape(1, LANES)
    pw = dict(w_out=w_out[0].astype(BF16), g_cross=row(g_cross[0]), w_cq=w_cq[0].astype(BF16),
              w_co=w_co[0].astype(BF16), g_moe=row(g_moe[0]), w_router=wr_pad, b_router=br_pad)
    cnt0 = jnp.zeros((SUBLANES, LANES), F32)
    h2_p, xn_p, idx_p, tw_p, rk_p, cnt_p = _post(
        xp, fo_p, lru_p, mk_p.reshape(1, n_mem, D_MODEL), mv_p.reshape(1, n_mem, D_MODEL), cnt0, pw,
        tm=POST_TILE, seg=POST_TILE, cached_memory=False)
    h2_s, xn_s, idx_s, tw_s, rk_s, cnt = _post(
        xs_, fo_s, lru_s, cache_mem_k, cache_mem_v, cnt_p, pw, tm=min(SAMPLE_POST_TILE, n_s), seg=t_s,
        cached_memory=True)

    n = n_p + n_s
    idx = jnp.concatenate([idx_p[:, :TOP_K], idx_s[:, :TOP_K]], axis=0)
    rank = jnp.concatenate([rk_p[:, :TOP_K], rk_s[:, :TOP_K]], axis=0)
    counts = cnt[0, :N_EXPERTS].astype(I32)
    padded = (counts + MOE_BM - 1) // MOE_BM * MOE_BM
    pad_end = jnp.cumsum(padded)
    pad_start = pad_end - padded
    dest = (pad_start[idx] + rank).reshape(n * TOP_K)
    n_blocks = -(-(n * TOP_K) // MOE_BM) + N_EXPERTS
    blk_row = jnp.arange(n_blocks, dtype=I32) * MOE_BM
    block_e = jnp.minimum(jnp.sum(pad_end[None, :] <= blk_row[:, None], axis=1), N_EXPERTS - 1).astype(I32)
    n_used = (pad_end[-1] // MOE_BM).reshape(1).astype(I32)
    last_block = jnp.maximum(pad_end // MOE_BM - 1, 0).astype(I32)

    xs_sorted = _dispatch(dest, last_block, n_used, xn_p, xn_s, tm=MOE_BM, n_blocks=n_blocks)
    ew = dict(w_gate=w_gate[0], w_up=w_up[0], w_down=w_down[0],
              b_gate=b_gate[0].reshape(N_EXPERTS, 1, D_MODEL), b_up=b_up[0].reshape(N_EXPERTS, 1, D_MODEL),
              b_down=b_down[0].reshape(N_EXPERTS, 1, D_MODEL))
    ys = _experts(block_e, n_used, (pad_end // MOE_BM).astype(I32), xs_sorted, ew, bm=MOE_BM)
    y_p, y_s = _combine(dest, h2_p, h2_s, tw_p, tw_s, row(g_final), ys, tm=COMBINE_TILE)

    shp_p = (1, nb_p, seq, FOX_HEADS, FOX_HEAD_DIM)
    shp_s = (1, nb_s, t_s, FOX_HEADS, FOX_HEAD_DIM)
    tail = slice(SUBLANES - (CONV_WIDTH - 1), SUBLANES)
    return (y_p.reshape(nb_p, seq, D_MODEL), y_s.reshape(nb_s, t_s, D_MODEL),
            k_p.reshape(shp_p), v_p.reshape(shp_p), lf_p.reshape(1, nb_p, seq, FOX_HEADS),
            hl_p.reshape(1, nb_p, LRU_WIDTH), ct_p[:, tail, :].reshape(1, nb_p, CONV_WIDTH - 1, LRU_WIDTH),
            mk_p.reshape(1, nb_p, n_mem, MEM_HEADS, MEM_HEAD_DIM), mv_p.reshape(1, nb_p, n_mem, MEM_HEADS, MEM_HEAD_DIM),
            k_s.reshape(shp_s), v_s.reshape(shp_s), lf_s.reshape(1, nb_s, t_s, FOX_HEADS),
            hl_s.reshape(1, nb_s, LRU_WIDTH), ct_s[:, tail, :].reshape(1, nb_s, CONV_WIDTH - 1, LRU_WIDTH))
```

```python
import functools
import math

import jax
import jax.numpy as jnp
import numpy as np
from jax import lax
from jax.experimental import pallas as pl
from jax.experimental.pallas import tpu as pltpu

F32 = jnp.float32
BF16 = jnp.bfloat16
I32 = jnp.int32

D_MODEL = 1024
FOX_HEADS = 8
FOX_HEAD_DIM = 64
FOX_WIDTH = FOX_HEADS * FOX_HEAD_DIM
LRU_WIDTH = D_MODEL - FOX_WIDTH
LRU_BLOCKS = 8
LRU_C = 8.0
CONV_WIDTH = 4
MEM_HEADS = 4
MEM_HEAD_DIM = D_MODEL // MEM_HEADS
N_EXPERTS = 32
TOP_K = 4
SWIGLU_LIMIT = 7.0
SWIGLU_ALPHA = 1.702
RMS_EPS = 1e-6
NEG_INF = -1e30
LOG2E = math.log2(math.e)

LANES = 128
SUBLANES = 8
CHUNKS = D_MODEL // LANES
TOKEN_TILE = (CHUNKS, SUBLANES, LANES)
AUG = LANES * FOX_HEADS
VMEM_LIMIT = 56 * 1024 * 1024

MIX_TILE = 256
POST_TILE = 512
SAMPLE_POST_TILE = 256
FOX_BQ = 512
MOE_BM = 512
COMBINE_TILE = 256
FOX_SKIP_GAP = 160.0
FOX_STATIC_SHIFT_RANGE = 100.0

C_Q, C_K, C_V, C_F, C_XR, C_G, C_END = 0, 512, 1024, 1536, 1664, 2176, 2688


def _dot(a, b):
    return jnp.dot(a, b, preferred_element_type=F32)


def _dot_nt(a, b):
    return lax.dot_general(a, b, (((1,), (1,)), ((), ())), preferred_element_type=F32)


def _split3(x):
    hi = x.astype(BF16)
    r = x - hi.astype(F32)
    mid = r.astype(BF16)
    lo = (r - mid.astype(F32)).astype(BF16)
    return hi, mid, lo


def _dot3(a, x):
    hi, mid, lo = _split3(x)
    return _dot(a, hi) + _dot(a, mid) + _dot(a, lo)


def _softplus(x):
    return jnp.maximum(x, 0.0) + jnp.log1p(jnp.exp(-jnp.abs(x)))


def _rms(x, g):
    return x * lax.rsqrt(jnp.mean(x * x, axis=-1, keepdims=True) + RMS_EPS) * g


def _tri(n, kind):
    r = lax.broadcasted_iota(I32, (n, n), 0)
    c = lax.broadcasted_iota(I32, (n, n), 1)
    m = {"le": c <= r, "lt": c < r, "gt": c > r}[kind]
    return jnp.where(m, 1.0, 0.0).astype(BF16)


def _mix_in_kernel(x_ref, g_ref, w_ref, bfp_ref, eq_ref, ek_ref, cq_ref, ck_ref, cv_ref, hs_ref, hd_ref, hm_ref,
                   cw_ref, cb_ref, wa_ref, ba_ref, wi_ref, bi_ref, lam_ref, cprev_ref, h0_ref,
                   qa_ref, ka_ref, va_ref, ko_ref, vo_ref, vb_ref, lf_ref, f2_ref, qn_ref, kn_ref,
                   lru_ref, hl_ref, ct_ref,
                   fcar, hcar, xp_ref, sa_ref, sb_ref, kbuf, vbuf, ksem, *, tm, pad, streaming):
    step = pl.program_id(0)
    first = step == 0
    slot = step % 2

    def kv_copies(s, row0):
        rows = pl.ds(row0, tm)
        copies = []
        for h in range(FOX_HEADS):
            copies.append(pltpu.make_async_copy(kbuf.at[s, h], ko_ref.at[rows, h, :], ksem.at[s]))
            copies.append(pltpu.make_async_copy(vbuf.at[s, h], vo_ref.at[rows, h, :], ksem.at[s]))
        return copies

    @pl.when(step >= 2)
    def _():
        for c in kv_copies(slot, 0):
            c.wait()

    if streaming:
        @pl.when(first)
        def _():
            fcar[...] = jnp.zeros_like(fcar)
            hcar[...] = jnp.zeros_like(hcar)
            xp_ref[0:SUBLANES, :] = jnp.zeros((SUBLANES, LRU_WIDTH), F32)
    else:
        fcar[...] = jnp.zeros_like(fcar)
        hcar[...] = jnp.broadcast_to(h0_ref[0], hcar.shape)
        xp_ref[0:SUBLANES, :] = cprev_ref[0]

    x = x_ref[...]
    xn = _rms(x, g_ref[...]).astype(BF16)
    z = _dot(xn, w_ref[...])

    zq, zk, zv = z[:, C_Q:C_K], z[:, C_K:C_V], z[:, C_V:C_F]
    for h in range(FOX_HEADS):
        hc = slice(h * FOX_HEAD_DIM, (h + 1) * FOX_HEAD_DIM)
        kbuf[slot, h] = zk[:, hc]
        vbuf[slot, h] = zv[:, hc]
    for c in kv_copies(slot, pl.multiple_of(step * tm, tm)):
        c.start()
    qd, kd, vd = zq.astype(BF16), zk.astype(BF16), zv.astype(BF16)
    vb_ref[...] = vd

    def head_pad(d):
        blocks = []
        for p in range(FOX_HEADS // 2):
            blk = d[:, p * LANES:(p + 1) * LANES]
            blocks += [blk * hm_ref[0:1, :], blk * hm_ref[1:2, :]]
        return jnp.concatenate(blocks, axis=1)

    lane = lax.broadcasted_iota(I32, (tm, LANES), 1)
    lf = jnp.where(lane < 3 * FOX_HEADS, -_softplus(-(z[:, C_F:C_XR] + bfp_ref[...])), 0.0)
    lf_ref[...] = lf[:, :FOX_HEADS]
    cum = _dot3(_tri(tm, "le"), lf) + fcar[0:1, :]
    fcar[...] = jnp.broadcast_to(cum[tm - 1:tm, :], fcar.shape)
    cum2 = cum * LOG2E
    f2_ref[...] = cum2[:, :FOX_HEADS]
    qf, kf = qd.astype(F32), kd.astype(F32)
    diag = _dot((qf * kf).astype(BF16), hd_ref[...]).astype(BF16)
    hi, mid, lo = _split3(cum2)
    bias = jnp.where(lane < 8, hi, jnp.where(lane < 16, mid, jnp.where(lane < 24, lo, diag)))
    qa_ref[...] = head_pad(qd) + (_dot(bias, eq_ref[...]) + cq_ref[...]).astype(BF16)
    ka_ref[...] = head_pad(kd) + (ck_ref[...] - _dot(bias, ek_ref[...])).astype(BF16)
    va_ref[...] = head_pad(vd) + cv_ref[...].astype(BF16)
    qn_ref[...] = jnp.broadcast_to(jnp.max(_dot((qf * qf).astype(BF16), hs_ref[...]), axis=0, keepdims=True),
                                   qn_ref.shape)
    kn_ref[...] = jnp.broadcast_to(jnp.max(_dot((kf * kf).astype(BF16), hs_ref[...]), axis=0, keepdims=True),
                                   kn_ref.shape)

    xr = z[:, C_XR:C_G]
    xp_ref[SUBLANES:SUBLANES + tm, :] = xr
    xc = cb_ref[...] + xr * cw_ref[CONV_WIDTH - 1:CONV_WIDTH, :]
    for j in range(CONV_WIDTH - 1):
        sh = CONV_WIDTH - 1 - j
        xc = xc + xp_ref[SUBLANES - sh:SUBLANES - sh + tm, :] * cw_ref[j:j + 1, :]
    tail = xp_ref[tm:tm + SUBLANES, :]
    ct_ref[0] = tail
    xp_ref[0:SUBLANES, :] = tail

    xcb = xc.astype(BF16)
    r = jax.nn.sigmoid(_dot(xcb, wa_ref[...]) + ba_ref[...])
    ig = jax.nn.sigmoid(_dot(xcb, wi_ref[...]) + bi_ref[...])
    log_a = (-LRU_C) * r * _softplus(-lam_ref[...])
    a = jnp.exp(log_a)
    mult = jnp.sqrt(-jnp.tanh(log_a) * (a * a + 1.0))
    if streaming:
        row = lax.broadcasted_iota(I32, (tm, LRU_WIDTH), 0)
        mult = jnp.where(jnp.logical_and(row == 0, first), 1.0, mult)
    b = mult * ig * xc

    sa_ref[0:pad, :] = jnp.ones((pad, LRU_WIDTH), F32)
    sb_ref[0:pad, :] = jnp.zeros((pad, LRU_WIDTH), F32)
    d = 1
    while d < tm:
        sa_ref[pad:pad + tm, :] = a
        sb_ref[pad:pad + tm, :] = b
        b = a * sb_ref[pad - d:pad - d + tm, :] + b
        a = a * sa_ref[pad - d:pad - d + tm, :]
        d *= 2
    h = a * hcar[0:1, :] + b
    hlast = h[tm - 1:tm, :]
    hcar[...] = jnp.broadcast_to(hlast, hcar.shape)
    hl_ref[0] = hlast
    lru_ref[...] = (h * jax.nn.gelu(z[:, C_G:C_END])).astype(BF16)

    last = pl.num_programs(0) - 1

    @pl.when(jnp.logical_and(step == last, step >= 1))
    def _():
        for c in kv_copies(1 - slot, 0):
            c.wait()

    @pl.when(step == last)
    def _():
        for c in kv_copies(slot, 0):
            c.wait()


def _mix_in(x, cprev, h0, wts, *, tm, streaming):
    n = x.shape[0]
    steps = n // tm
    nseg = 1 if streaming else steps
    pad = max(tm // 2, SUBLANES)
    seg = (lambda i: (0, 0, 0)) if streaming else (lambda i: (i, 0, 0))
    full = lambda a: pl.BlockSpec(a.shape, lambda i: (0,) * a.ndim)
    rows = lambda w: pl.BlockSpec((tm, w), lambda i: (i, 0))
    names = ("g_mix", "w_all", "bf_pad", "eq", "ek", "cq", "ck", "cv", "hsel", "hdiag", "hmask", "conv_w", "conv_b",
             "wa", "ba", "wi", "bi", "lam")
    ws = [wts[k] for k in names]
    sds = jax.ShapeDtypeStruct
    out_shape = (
        sds((n, AUG), BF16), sds((n, AUG), BF16), sds((n, AUG), BF16),
        sds((n, FOX_HEADS, FOX_HEAD_DIM), F32), sds((n, FOX_HEADS, FOX_HEAD_DIM), F32), sds((n, FOX_WIDTH), BF16),
        sds((n, FOX_HEADS), F32), sds((n, FOX_HEADS), F32),
        sds((steps * SUBLANES, LANES), F32), sds((steps * SUBLANES, LANES), F32),
        sds((n, LRU_WIDTH), BF16),
        sds((nseg, 1, LRU_WIDTH), F32), sds((nseg, SUBLANES, LRU_WIDTH), F32),
    )
    out_specs = (
        rows(AUG), rows(AUG), rows(AUG), pl.BlockSpec(memory_space=pl.ANY), pl.BlockSpec(memory_space=pl.ANY),
        rows(FOX_WIDTH), rows(FOX_HEADS), rows(FOX_HEADS),
        pl.BlockSpec((SUBLANES, LANES), lambda i: (i, 0)), pl.BlockSpec((SUBLANES, LANES), lambda i: (i, 0)),
        rows(LRU_WIDTH),
        pl.BlockSpec((1, 1, LRU_WIDTH), seg), pl.BlockSpec((1, SUBLANES, LRU_WIDTH), seg),
    )
    in_specs = [rows(D_MODEL)] + [full(w) for w in ws] + [
        pl.BlockSpec((1, SUBLANES, LRU_WIDTH), seg), pl.BlockSpec((1, 1, LRU_WIDTH), seg)]
    return pl.pallas_call(
        functools.partial(_mix_in_kernel, tm=tm, pad=pad, streaming=streaming),
        grid=(steps,), in_specs=in_specs, out_specs=out_specs, out_shape=out_shape,
        scratch_shapes=[
            pltpu.VMEM((SUBLANES, LANES), F32), pltpu.VMEM((SUBLANES, LRU_WIDTH), F32),
            pltpu.VMEM((tm + SUBLANES, LRU_WIDTH), F32),
            pltpu.VMEM((pad + tm, LRU_WIDTH), F32), pltpu.VMEM((pad + tm, LRU_WIDTH), F32),
            pltpu.VMEM((2, FOX_HEADS, tm, FOX_HEAD_DIM), F32), pltpu.VMEM((2, FOX_HEADS, tm, FOX_HEAD_DIM), F32),
            pltpu.SemaphoreType.DMA((2,))],
        compiler_params=pltpu.CompilerParams(dimension_semantics=("arbitrary",), vmem_limit_bytes=VMEM_LIMIT),
        name="mix_in_stream" if streaming else "mix_in_segments",
    )(x, *ws, cprev, h0)


def _fox_prompt_kernel(nb_ref, st_ref, q_ref, k_ref, v_ref, o_ref, m_ref, acc_ref, *, bq):
    qi = pl.program_id(1)
    plan = pl.program_id(0) * pl.num_programs(1) + qi
    n_past = nb_ref[plan]
    cols = [slice(LANES * c, LANES * (c + 1)) for c in range(2)]
    qs = [q_ref[:, cols[c]] for c in range(2)]
    acc_ref[...] = jnp.zeros(acc_ref.shape, F32)

    def scores(c, start, causal):
        s = _dot_nt(qs[c], k_ref[pl.ds(start, bq), cols[c]])
        if causal:
            r = lax.broadcasted_iota(I32, (bq, bq), 0)
            cc = lax.broadcasted_iota(I32, (bq, bq), 1)
            s = jnp.where(cc <= r, s, NEG_INF)
        return s

    def static_block(start, causal):
        for c in range(2):
            p = jnp.exp2(scores(c, start, causal)).astype(BF16)
            acc_ref[c] += _dot(p, v_ref[pl.ds(start, bq), cols[c]])

    def online_block(start, causal):
        for c in range(2):
            s = scores(c, start, causal)
            m_old = m_ref[c]
            m_new = jnp.maximum(m_old, jnp.max(s, axis=1, keepdims=True))
            p = jnp.exp2(s - m_new).astype(BF16)
            acc_ref[c] = jnp.exp2(m_old - m_new) * acc_ref[c] + _dot(p, v_ref[pl.ds(start, bq), cols[c]])
            m_ref[c] = m_new

    def run(block):
        block(pl.multiple_of(qi * bq, bq), True)

        def body(t, carry):
            block(pl.multiple_of((qi - 1 - t) * bq, bq), False)
            return carry

        lax.fori_loop(0, n_past, body, 0)

    @pl.when(st_ref[plan] == 1)
    def _():
        run(static_block)

    @pl.when(st_ref[plan] != 1)
    def _():
        m_ref[...] = jnp.full(m_ref.shape, NEG_INF, F32)
        run(online_block)

    a0, a1 = acc_ref[0], acc_ref[1]
    o0 = a0 / a0[:, FOX_HEAD_DIM:FOX_HEAD_DIM + 1]
    o1 = a1 / a1[:, 0:1]
    lane = lax.broadcasted_iota(I32, (bq, LANES), 1)
    o_ref[...] = jnp.where(lane < FOX_HEAD_DIM, o0, o1).astype(BF16)


def _fox_prompt(n_past, static_ok, qa, ka, va, *, bq):
    s = qa.shape[0]
    pairs = FOX_HEADS // 2
    return pl.pallas_call(
        functools.partial(_fox_prompt_kernel, bq=bq),
        grid_spec=pltpu.PrefetchScalarGridSpec(
            num_scalar_prefetch=2, grid=(pairs, s // bq),
            in_specs=[pl.BlockSpec((bq, 2 * LANES), lambda p, i, nb, st: (i, p)),
                      pl.BlockSpec((s, 2 * LANES), lambda p, i, nb, st: (0, p)),
                      pl.BlockSpec((s, 2 * LANES), lambda p, i, nb, st: (0, p))],
            out_specs=pl.BlockSpec((bq, LANES), lambda p, i, nb, st: (i, p)),
            scratch_shapes=[pltpu.VMEM((2, bq, 1), F32), pltpu.VMEM((2, bq, LANES), F32)]),
        out_shape=jax.ShapeDtypeStruct((s, FOX_WIDTH), BF16),
        compiler_params=pltpu.CompilerParams(dimension_semantics=("arbitrary", "arbitrary"),
                                             vmem_limit_bytes=VMEM_LIMIT),
        name="fox_prompt",
    )(n_past, static_ok, qa, ka, va)


def _fox_plan(f2, qn, kn, *, bq, tile):
    s = f2.shape[0]
    nq = s // bq
    per = lambda a: a.reshape(s // tile, SUBLANES, LANES)[:, 0, :FOX_HEADS]
    qnorm = jnp.sqrt(jnp.max(per(qn).reshape(nq, bq // tile, FOX_HEADS), axis=1))
    knorm = jnp.sqrt(jnp.max(per(kn), axis=0))
    spread = 2.0 * 1.02 * qnorm * knorm[None, :]
    thr = spread + FOX_SKIP_GAP
    f_first = f2[0::bq]
    f_last = f2[bq - 1::bq]
    gap = f_last[None, :, :] - f_first[:, None, :]
    before = (jnp.arange(nq)[None, :] < jnp.arange(nq)[:, None])[:, :, None]
    need = jnp.sum(jnp.logical_and(before, gap < thr[:, None, :]), axis=1)
    pair = lambda a: a.reshape(nq, FOX_HEADS // 2, 2)
    n_past = jnp.max(pair(need), axis=2).T.reshape(-1).astype(I32)
    static_ok = jnp.all(pair(spread) <= FOX_STATIC_SHIFT_RANGE, axis=2).T.reshape(-1).astype(I32)
    return n_past, static_ok


def _fox_sample_kernel(q_ref, kn_ref, vn_ref, kt_ref, vt_ref, lf_ref, sel_ref, o_ref, g_ref,
                       *, t, past, chunk):
    car = jnp.zeros((FOX_HEADS, 1), F32)
    later = _tri(chunk, "lt")
    for ci in reversed(range(past // chunk)):
        lf = lf_ref[0, :, ci * chunk:(ci + 1) * chunk]
        hi, mid, lo = _split3(lf)
        g_ref[:, ci * chunk:(ci + 1) * chunk] = _dot(hi, later) + _dot(mid, later) + _dot(lo, later) + car
        car = car + jnp.sum(lf, axis=1, keepdims=True)
    hi, mid, lo = _split3(g_ref[...] * LOG2E)
    bias_rows = jnp.concatenate([hi.astype(F32), mid.astype(F32), lo.astype(F32),
                                 jnp.ones((FOX_HEADS, past), F32)], axis=0).astype(BF16)

    def swap_halves(x):
        return pltpu.roll(x.astype(F32), FOX_HEAD_DIM, axis=1).astype(BF16)

    r = lax.broadcasted_iota(I32, (t, t), 0)
    cc = lax.broadcasted_iota(I32, (t, t), 1)
    prow = lax.broadcasted_iota(I32, (FOX_HEAD_DIM, LANES), 0)
    plane = lax.broadcasted_iota(I32, (FOX_HEAD_DIM, LANES), 1)
    place = [jnp.where(plane == prow + FOX_HEAD_DIM * c, 1.0, 0.0).astype(BF16) for c in range(2)]
    for p in range(FOX_HEADS // 2):
        pc = slice(LANES * p, LANES * (p + 1))
        vpair = vn_ref[:, pc]
        o_pair = jnp.zeros((t, LANES), F32)
        for c in range(2):
            h = 2 * p + c
            hc = slice(LANES * h, LANES * (h + 1))
            q, kn, vn = q_ref[:, hc], kn_ref[:, hc], vpair
            if c == 1:
                q, kn, vn = swap_halves(q), swap_halves(kn), swap_halves(vn)
            bias = _dot(sel_ref[h], bias_rows).astype(BF16)
            kpast_t = jnp.concatenate([kt_ref[0, h].astype(BF16), bias], axis=0)
            sp = _dot(q, kpast_t)
            sn = jnp.where(cc <= r, _dot_nt(q, kn), NEG_INF)
            m = jnp.maximum(jnp.max(sp, axis=1, keepdims=True), jnp.max(sn, axis=1, keepdims=True))
            pp = jnp.exp2(sp - m)
            pn = jnp.exp2(sn - m)
            l = jnp.sum(pp, axis=1, keepdims=True) + jnp.sum(pn, axis=1, keepdims=True)
            o = (_dot_nt(pp.astype(BF16), vt_ref[0, h].astype(BF16))
                 + _dot(pn.astype(BF16), vn[:, :FOX_HEAD_DIM])) / l
            o_pair = o_pair + _dot(o.astype(BF16), place[c])
        o_ref[:, pc] = o_pair.astype(BF16)


def _fox_sample(qa, ka, vb, cache_kt, cache_vt, cache_lft, sel, *, t):
    nb, past = cache_kt.shape[0], cache_kt.shape[3]
    full = lambda a: pl.BlockSpec(a.shape, lambda b: (0,) * a.ndim)
    slab = pl.BlockSpec((1, FOX_HEADS, FOX_HEAD_DIM, past), lambda b: (b, 0, 0, 0))
    return pl.pallas_call(
        functools.partial(_fox_sample_kernel, t=t, past=past, chunk=256),
        grid=(nb,),
        in_specs=[pl.BlockSpec((t, AUG), lambda b: (b, 0)), pl.BlockSpec((t, AUG), lambda b: (b, 0)),
                  pl.BlockSpec((t, FOX_WIDTH), lambda b: (b, 0)), slab, slab,
                  pl.BlockSpec((1, FOX_HEADS, past), lambda b: (b, 0, 0)), full(sel)],
        out_specs=pl.BlockSpec((t, FOX_WIDTH), lambda b: (b, 0)),
        out_shape=jax.ShapeDtypeStruct((nb * t, FOX_WIDTH), BF16),
        scratch_shapes=[pltpu.VMEM((FOX_HEADS, past), F32)],
        compiler_params=pltpu.CompilerParams(dimension_semantics=("arbitrary",), vmem_limit_bytes=VMEM_LIMIT),
        name="fox_sample",
    )(qa, ka, vb, cache_kt, cache_vt, cache_lft, sel)


def _mem_kv_kernel(m_ref, g_ref, wk_ref, wv_ref, k_ref, v_ref):
    mn = _rms(m_ref[...], g_ref[...]).astype(BF16)
    k_ref[...] = _dot(mn, wk_ref[...])
    v_ref[...] = _dot(mn, wv_ref[...])


def _mem_kv(mem, g, wk, wv):
    n = mem.shape[0]
    return pl.pallas_call(
        _mem_kv_kernel,
        out_shape=(jax.ShapeDtypeStruct((n, D_MODEL), F32), jax.ShapeDtypeStruct((n, D_MODEL), F32)),
        compiler_params=pltpu.CompilerParams(vmem_limit_bytes=VMEM_LIMIT),
        name="mem_kv",
    )(mem, g, wk, wv)


def _post_kernel(h_ref, fo_ref, lru_ref, wo_ref, gc_ref, wcq_ref, mk_ref, mv_ref, wco_ref,
                 gm_ref, wr_ref, br_ref, cnt_in_ref,
                 h2_ref, xn_ref, idx_ref, tw_ref, rank_ref, cnt_ref, car_ref, *mem_scratch, tm, seg, cached_memory):
    step = pl.program_id(0)
    nseg = tm // seg

    @pl.when(step == 0)
    def _():
        car_ref[...] = cnt_in_ref[...]

    if cached_memory:
        mbuf, msem = mem_scratch
        slot = step % 2

        def mem_copies(at_step, s):
            return [pltpu.make_async_copy(ref.at[0, at_step * nseg + j, :, hd, :], mbuf.at[s, j, t, hd], msem.at[s])
                    for j in range(nseg) for t, ref in enumerate((mk_ref, mv_ref)) for hd in range(MEM_HEADS)]

        @pl.when(step == 0)
        def _():
            for c in mem_copies(0, 0):
                c.start()

        @pl.when(step + 1 < pl.num_programs(0))
        def _():
            for c in mem_copies(step + 1, 1 - slot):
                c.start()

    h1 = h_ref[...] + _dot(fo_ref[...], wo_ref[0:FOX_WIDTH, :]) + _dot(lru_ref[...], wo_ref[FOX_WIDTH:D_MODEL, :])

    q = _dot(_rms(h1, gc_ref[...]).astype(BF16), wcq_ref[...])
    if cached_memory:
        for c in mem_copies(0, slot):
            c.wait()
    segs = []
    for j in range(nseg):
        rows = slice(j * seg, (j + 1) * seg)
        heads = []
        for hd in range(MEM_HEADS):
            hc = slice(MEM_HEAD_DIM * hd, MEM_HEAD_DIM * (hd + 1))
            mk = mbuf[slot, j, 0, hd] if cached_memory else mk_ref[0, :, hc]
            mv = mbuf[slot, j, 1, hd] if cached_memory else mv_ref[0, :, hc]
            s = _dot_nt(q[rows, hc].astype(BF16), mk.astype(BF16)) * (MEM_HEAD_DIM ** -0.5)
            p = jnp.exp(s - jnp.max(s, axis=1, keepdims=True))
            o = _dot(p.astype(BF16), mv.astype(BF16)) / jnp.sum(p, axis=1, keepdims=True)
            heads.append(o.astype(BF16))
        segs.append(jnp.concatenate(heads, axis=1))
    h2 = h1 + _dot(segs[0] if nseg == 1 else jnp.concatenate(segs, axis=0), wco_ref[...])
    h2_ref[...] = h2

    xn = _rms(h2, gm_ref[...])
    for c in range(D_MODEL // LANES):
        xn_ref[:, c] = xn[:, c * LANES:(c + 1) * LANES].reshape(tm // SUBLANES, SUBLANES, LANES)

    lane = lax.broadcasted_iota(I32, (tm, LANES), 1).astype(F32)
    logits = jnp.where(lane < N_EXPERTS, _dot(xn.astype(BF16), wr_ref[...]) + br_ref[...], -jnp.inf)
    vals, idxs = [], []
    for _ in range(TOP_K):
        mx = jnp.max(logits, axis=1, keepdims=True)
        ix = jnp.min(jnp.where(logits == mx, lane, float(LANES)), axis=1, keepdims=True)
        vals.append(mx)
        idxs.append(ix)
        logits = jnp.where(lane == ix, -jnp.inf, logits)
    es = [jnp.exp(v - vals[0]) for v in vals]
    den = es[0] + es[1] + es[2] + es[3]

    onehot = jnp.zeros((tm, LANES), F32)
    for ix in idxs:
        onehot = onehot + jnp.where(lane == ix, 1.0, 0.0)
    before = _dot(_tri(tm, "lt"), onehot.astype(BF16)) + car_ref[0:1, :]
    car = car_ref[0:1, :] + jnp.sum(onehot, axis=0, keepdims=True)
    car_ref[...] = jnp.broadcast_to(car, car_ref.shape)
    cnt_ref[...] = jnp.broadcast_to(car, cnt_ref.shape)

    idx_o = jnp.zeros((tm, LANES), F32)
    tw_o = jnp.zeros((tm, LANES), F32)
    rk_o = jnp.zeros((tm, LANES), F32)
    for j in range(TOP_K):
        rk = jnp.sum(jnp.where(lane == idxs[j], before, 0.0), axis=1, keepdims=True)
        idx_o = jnp.where(lane == j, idxs[j], idx_o)
        tw_o = jnp.where(lane == j, es[j] / den, tw_o)
        rk_o = jnp.where(lane == j, rk, rk_o)
    idx_ref[...] = idx_o.astype(I32)
    tw_ref[...] = tw_o
    rank_ref[...] = rk_o.astype(I32)


def _post(h, fo, lru, mk, mv, cnt_in, wts, *, tm, seg, cached_memory):
    n = h.shape[0]
    full = lambda a: pl.BlockSpec(a.shape, lambda i: (0,) * a.ndim)
    rows = lambda w: pl.BlockSpec((tm, w), lambda i: (i, 0))
    if cached_memory:
        nm = mk.shape[2]
        mem = pl.BlockSpec(memory_space=pl.ANY)
        mem_scratch = [pltpu.VMEM((2, tm // seg, 2, MEM_HEADS, nm, MEM_HEAD_DIM), F32),
                       pltpu.SemaphoreType.DMA((2,))]
    else:
        mem = full(mk)
        mem_scratch = []
    w = [wts[k] for k in ("w_out", "g_cross", "w_cq")]
    w2 = [wts[k] for k in ("w_co", "g_moe", "w_router", "b_router")]
    return pl.pallas_call(
        functools.partial(_post_kernel, tm=tm, seg=seg, cached_memory=cached_memory),
        grid=(n // tm,),
        in_specs=[rows(D_MODEL), rows(FOX_WIDTH), rows(LRU_WIDTH)] + [full(a) for a in w] + [mem, mem]
                 + [full(a) for a in w2] + [full(cnt_in)],
        out_specs=(rows(D_MODEL), pl.BlockSpec((tm // SUBLANES,) + TOKEN_TILE, lambda i: (i, 0, 0, 0)),
                   rows(LANES), rows(LANES), rows(LANES),
                   pl.BlockSpec((SUBLANES, LANES), lambda i: (0, 0))),
        out_shape=(jax.ShapeDtypeStruct((n, D_MODEL), F32), jax.ShapeDtypeStruct((n // SUBLANES,) + TOKEN_TILE, F32),
                   jax.ShapeDtypeStruct((n, LANES), I32), jax.ShapeDtypeStruct((n, LANES), F32),
                   jax.ShapeDtypeStruct((n, LANES), I32), jax.ShapeDtypeStruct((SUBLANES, LANES), F32)),
        scratch_shapes=[pltpu.VMEM((SUBLANES, LANES), F32)] + mem_scratch,
        compiler_params=pltpu.CompilerParams(dimension_semantics=("arbitrary",), vmem_limit_bytes=VMEM_LIMIT),
        name="post_batch_mem" if cached_memory else "post_shared_mem",
    )(h, fo, lru, *w, mk, mv, *w2, cnt_in)


STAGES = 3


def _dispatch_kernel(dest_ref, last_ref, nu_ref, xp_ref, xs_ref, out_ref, zero_ref, stage, sems, gsem, zsem,
                     *, tm, steps_p, n_blocks):
    i = pl.program_id(0)

    def zero_copy(b):
        return pltpu.make_async_copy(zero_ref, out_ref.at[pl.ds(pl.multiple_of(b * tm, tm), tm)], zsem)

    @pl.when(i == 0)
    def _():
        zero_ref[...] = jnp.zeros(zero_ref.shape, F32)
        n_tail = n_blocks - nu_ref[0]

        def start_e(e, c):
            zero_copy(last_ref[e]).start()
            return c

        def start_t(b, c):
            zero_copy(nu_ref[0] + b).start()
            return c

        def wait_one(b, c):
            zero_copy(0).wait()
            return c

        lax.fori_loop(0, N_EXPERTS, start_e, 0)
        lax.fori_loop(0, n_tail, start_t, 0)
        lax.fori_loop(0, N_EXPERTS + n_tail, wait_one, 0)

    groups = tm // SUBLANES
    steps = pl.num_programs(0)

    def stage_copy(src_ref, tile, slot):
        return pltpu.make_async_copy(src_ref.at[pl.ds(tile * groups, groups)], stage.at[slot], gsem.at[slot])

    def start_stage(tile, slot):
        @pl.when(tile < steps_p)
        def _():
            stage_copy(xp_ref, tile, slot).start()

        @pl.when(tile >= steps_p)
        def _():
            stage_copy(xs_ref, tile - steps_p, slot).start()

    @pl.when(i == 0)
    def _():
        start_stage(0, 0)

    @pl.when(i + 1 < steps)
    def _():
        start_stage(i + 1, (i + 1) % STAGES)

    slot = i % STAGES
    stage_copy(xp_ref, 0, slot).wait()

    base = i * (tm * TOP_K)

    def start(g, c):
        for u in range(SUBLANES):
            for j in range(TOP_K):
                d = dest_ref[base + (g * SUBLANES + u) * TOP_K + j]
                pltpu.make_async_copy(stage.at[slot, g, :, u, :], out_ref.at[d],
                                      sems.at[i % 2]).start(priority=j % 2)
        return c

    lax.fori_loop(0, groups, start, 0)

    def drain(parity):
        for _ in range(TOP_K):
            pltpu.make_async_copy(out_ref.at[pl.ds(0, tm)], out_ref.at[pl.ds(0, tm)], sems.at[parity]).wait()

    @pl.when(i > 0)
    def _():
        drain((i - 1) % 2)

    @pl.when(i == steps - 1)
    def _():
        drain(i % 2)


def _dispatch(dest_flat, last_block, n_used, xn_p, xn_s, *, tm, n_blocks):
    steps_p, steps_s = xn_p.shape[0] * SUBLANES // tm, xn_s.shape[0] * SUBLANES // tm
    return pl.pallas_call(
        functools.partial(_dispatch_kernel, tm=tm, steps_p=steps_p, n_blocks=n_blocks),
        grid_spec=pltpu.PrefetchScalarGridSpec(
            num_scalar_prefetch=3, grid=(steps_p + steps_s,),
            in_specs=[pl.BlockSpec(memory_space=pl.ANY), pl.BlockSpec(memory_space=pl.ANY)],
            out_specs=pl.BlockSpec(memory_space=pl.ANY),
            scratch_shapes=[pltpu.VMEM((tm, CHUNKS, LANES), F32),
                            pltpu.VMEM((STAGES, tm // SUBLANES) + TOKEN_TILE, F32),
                            pltpu.SemaphoreType.DMA((2,)), pltpu.SemaphoreType.DMA((STAGES,)),
                            pltpu.SemaphoreType.DMA(())]),
        out_shape=jax.ShapeDtypeStruct((n_blocks * tm, CHUNKS, LANES), F32),
        compiler_params=pltpu.CompilerParams(dimension_semantics=("arbitrary",), vmem_limit_bytes=VMEM_LIMIT),
        name="moe_dispatch",
    )(dest_flat, last_block, n_used, xn_p, xn_s)


def _rows_from_tiles(view, rows):
    return jnp.concatenate([view[:, c].reshape(rows, LANES) for c in range(CHUNKS)], axis=1)


def _rows_to_tiles(view, val, rows):
    for c in range(CHUNKS):
        view[:, c] = val[:, c * LANES:(c + 1) * LANES].reshape(rows // SUBLANES, SUBLANES, LANES)


def _expert_kernel(be_ref, nu_ref, nx_ref, xs_ref, wg_ref, bg_ref, wu_ref, bu_ref, wd_ref, bd_ref, ys_ref,
                   wgb, wub, wdb, wbuf, xbuf, ybuf, seq, wsem, xsem, ysem, *, bm):
    i = pl.program_id(0)
    steps = pl.num_programs(0)
    groups = bm // SUBLANES
    slot = i % 2
    prev = be_ref[jnp.maximum(i - 1, 0)]
    fresh = jnp.logical_or(i == 0, be_ref[i] != prev)
    live = i < nu_ref[0]

    def w_copies(e, s):
        return [pltpu.make_async_copy(w_ref.at[e], wbuf.at[s, k], wsem.at[s])
                for k, w_ref in enumerate((wg_ref, wu_ref, wd_ref))]

    @pl.when(i == 0)
    def _():
        seq[0] = 0
        for c in w_copies(be_ref[0], 0):
            c.start()

    def in_copies(blk, s):
        g0 = pl.multiple_of(blk * groups, groups)
        return [pltpu.make_async_copy(xs_ref.at[pl.ds(g0, groups), u], xbuf.at[s, :, :, u, :], xsem.at[s])
                for u in range(SUBLANES)]

    def out_copies(blk, s):
        g0 = pl.multiple_of(blk * groups, groups)
        return [pltpu.make_async_copy(ybuf.at[s, :, :, u, :], ys_ref.at[pl.ds(g0, groups), u], ysem.at[s])
                for u in range(SUBLANES)]

    @pl.when(i == 0)
    def _():
        for c in in_copies(0, 0):
            c.start()

    @pl.when(i + 1 < nu_ref[0])
    def _():
        for c in in_copies(i + 1, (i + 1) % 2):
            c.start()

    @pl.when(i >= 2)
    def _():
        for c in out_copies(0, slot):
            c.wait()

    @pl.when(jnp.logical_and(live, fresh))
    def _():
        s = seq[0] % 2
        for c in w_copies(0, s):
            c.wait()
        for static_s in range(2):
            @pl.when(s == static_s)
            def _():
                wgb[...] = wbuf[static_s, 0].astype(BF16)
                wub[...] = wbuf[static_s, 1].astype(BF16)
                wdb[...] = wbuf[static_s, 2].astype(BF16)

        nxt = nx_ref[be_ref[i]]

        @pl.when(nxt < nu_ref[0])
        def _():
            for c in w_copies(be_ref[jnp.minimum(nxt, steps - 1)], 1 - s):
                c.start()

        seq[0] = seq[0] + 1

    @pl.when(live)
    def _():
        for c in in_copies(0, slot):
            c.wait()
        x = _rows_from_tiles(xbuf.at[slot], bm).astype(BF16)
        g = jnp.minimum(_dot(x, wgb[...]) + bg_ref[0], SWIGLU_LIMIT)
        u = jnp.clip(_dot(x, wub[...]) + bu_ref[0], -SWIGLU_LIMIT, SWIGLU_LIMIT)
        hdn = g * jax.nn.sigmoid(SWIGLU_ALPHA * g) * (u + 1.0)
        _rows_to_tiles(ybuf.at[slot], _dot(hdn.astype(BF16), wdb[...]) + bd_ref[0], bm)

    @pl.when(jnp.logical_not(live))
    def _():
        ybuf[slot] = jnp.zeros(ybuf.shape[1:], F32)

    for c in out_copies(i, slot):
        c.start()

    @pl.when(jnp.logical_and(i == steps - 1, i >= 1))
    def _():
        for c in out_copies(0, 1 - slot):
            c.wait()

    @pl.when(i == steps - 1)
    def _():
        for c in out_copies(0, slot):
            c.wait()


def _experts(block_e, n_used, next_first, xs, wts, *, bm):
    p = xs.shape[0]
    grouped = (p // SUBLANES, SUBLANES, CHUNKS, LANES)
    anyspace = pl.BlockSpec(memory_space=pl.ANY)
    bspec = pl.BlockSpec((1, 1, D_MODEL), lambda i, be, nu, nx: (be[i], 0, 0))
    tiles = (2, bm // SUBLANES) + TOKEN_TILE
    ys = pl.pallas_call(
        functools.partial(_expert_kernel, bm=bm),
        grid_spec=pltpu.PrefetchScalarGridSpec(
            num_scalar_prefetch=3, grid=(p // bm,),
            in_specs=[anyspace, anyspace, bspec, anyspace, bspec, anyspace, bspec],
            out_specs=anyspace,
            scratch_shapes=[pltpu.VMEM((D_MODEL, D_MODEL), BF16)] * 3 + [
                pltpu.VMEM((2, 3, D_MODEL, D_MODEL), F32), pltpu.VMEM(tiles, F32), pltpu.VMEM(tiles, F32),
                pltpu.SMEM((1,), I32),
                pltpu.SemaphoreType.DMA((2,)), pltpu.SemaphoreType.DMA((2,)), pltpu.SemaphoreType.DMA((2,))]),
        out_shape=jax.ShapeDtypeStruct(grouped, F32),
        compiler_params=pltpu.CompilerParams(dimension_semantics=("arbitrary",), vmem_limit_bytes=VMEM_LIMIT),
        name="moe_experts",
    )(block_e, n_used, next_first, xs.reshape(grouped), wts["w_gate"], wts["b_gate"], wts["w_up"], wts["b_up"],
      wts["w_down"], wts["b_down"])
    return ys.reshape(p, CHUNKS, LANES)


def _combine_kernel(dest_ref, hp_ref, hs_ref, twp_ref, tws_ref, gf_ref, ys_ref, yp_ref, ysm_ref, buf, sems,
                    *, tm, steps_p):
    i = pl.program_id(0)
    slot = i % 2

    def issue(tile, into):
        base = tile * (tm * TOP_K)

        def start(g, c):
            for u in range(SUBLANES):
                for j in range(TOP_K):
                    d = dest_ref[base + (g * SUBLANES + u) * TOP_K + j]
                    pltpu.make_async_copy(ys_ref.at[d], buf.at[into, j, g, :, u, :],
                                          sems.at[into]).start(priority=j % 2)
            return c

        lax.fori_loop(0, tm // SUBLANES, start, 0)

    @pl.when(i == 0)
    def _():
        issue(0, 0)

    @pl.when(i + 1 < pl.num_programs(0))
    def _():
        issue(i + 1, (i + 1) % 2)

    for j in range(TOP_K):
        pltpu.make_async_copy(ys_ref.at[pl.ds(0, tm)], ys_ref.at[pl.ds(0, tm)], sems.at[slot]).wait()

    def finish(h_ref, tw_ref, y_ref):
        tw = tw_ref[...]
        y = h_ref[...]
        for j in range(TOP_K):
            y = y + _rows_from_tiles(buf.at[slot, j], tm) * tw[:, j:j + 1]
        y_ref[...] = _rms(y, gf_ref[...])

    @pl.when(i < steps_p)
    def _():
        finish(hp_ref, twp_ref, yp_ref)

    @pl.when(i >= steps_p)
    def _():
        finish(hs_ref, tws_ref, ysm_ref)


def _combine(dest_flat, h2_p, h2_s, tw_p, tw_s, g_final, ys, *, tm):
    steps_p, steps_s = h2_p.shape[0] // tm, h2_s.shape[0] // tm
    pblk = lambda w: pl.BlockSpec((tm, w), lambda i, d: (jnp.minimum(i, steps_p - 1), 0))
    sblk = lambda w: pl.BlockSpec((tm, w), lambda i, d: (jnp.maximum(i - steps_p, 0), 0))
    return pl.pallas_call(
        functools.partial(_combine_kernel, tm=tm, steps_p=steps_p),
        grid_spec=pltpu.PrefetchScalarGridSpec(
            num_scalar_prefetch=1, grid=(steps_p + steps_s,),
            in_specs=[pblk(D_MODEL), sblk(D_MODEL), pblk(LANES), sblk(LANES),
                      pl.BlockSpec((1, D_MODEL), lambda i, d: (0, 0)),
                      pl.BlockSpec(memory_space=pl.ANY)],
            out_specs=(pblk(D_MODEL), sblk(D_MODEL)),
            scratch_shapes=[pltpu.VMEM((2, TOP_K, tm // SUBLANES) + TOKEN_TILE, F32),
                            pltpu.SemaphoreType.DMA((2,))]),
        out_shape=(jax.ShapeDtypeStruct(h2_p.shape, F32), jax.ShapeDtypeStruct(h2_s.shape, F32)),
        compiler_params=pltpu.CompilerParams(dimension_semantics=("arbitrary",), vmem_limit_bytes=VMEM_LIMIT),
        name="moe_combine",
    )(dest_flat, h2_p, h2_s, tw_p, tw_s, g_final, ys)


def _aug_layout():
    main, extra = [], []
    for h in range(FOX_HEADS):
        even = h % 2 == 0
        main.append(LANES * h + (0 if even else FOX_HEAD_DIM))
        extra.append(LANES * h + (FOX_HEAD_DIM if even else 0))
    return main, extra


def _prep_mix_weights(g_mix, w_in, b_f, conv_w, conv_b, w_a, b_a, w_i, b_i, lam):
    _, extra = _aug_layout()
    wq = w_in[:, 0:FOX_WIDTH] * (FOX_HEAD_DIM ** -0.5 * LOG2E)
    wf = w_in[:, 3 * FOX_WIDTH:3 * FOX_WIDTH + FOX_HEADS]
    wf_pad = jnp.concatenate([wf, wf, wf, jnp.zeros((D_MODEL, LANES - 3 * FOX_HEADS), F32)], axis=1)
    w_all = jnp.concatenate([wq, w_in[:, FOX_WIDTH:3 * FOX_WIDTH], wf_pad, w_in[:, 3 * FOX_WIDTH + FOX_HEADS:]],
                            axis=1).astype(BF16)
    bf_pad = jnp.concatenate([b_f, b_f, b_f, jnp.zeros((LANES - 3 * FOX_HEADS,), F32)]).reshape(1, LANES)
    eq = np.zeros((LANES, AUG), np.float32)
    ek = np.zeros((LANES, AUG), np.float32)
    cq = np.zeros((1, AUG), np.float32)
    ck = np.zeros((1, AUG), np.float32)
    cv = np.zeros((1, AUG), np.float32)
    hsel = np.zeros((FOX_WIDTH, LANES), np.float32)
    hdiag = np.zeros((FOX_WIDTH, LANES), np.float32)
    hmask = np.zeros((2, LANES), np.float32)
    hmask[0, :FOX_HEAD_DIM] = 1.0
    hmask[1, FOX_HEAD_DIM:] = 1.0
    for h in range(FOX_HEADS):
        hsel[FOX_HEAD_DIM * h:FOX_HEAD_DIM * (h + 1), h] = 1.0
        hdiag[FOX_HEAD_DIM * h:FOX_HEAD_DIM * (h + 1), 3 * FOX_HEADS + h] = 1.0
        cv[0, extra[h]] = 1.0
        eq[3 * FOX_HEADS + h, extra[h] + 6] = -1.0
        ck[0, extra[h] + 6] = 1.0
        for part in range(3):
            eq[part * 8 + h, extra[h] + part] = 1.0
            ek[part * 8 + h, extra[h] + 3 + part] = 1.0
            cq[0, extra[h] + 3 + part] = 1.0
            ck[0, extra[h] + part] = 1.0
    bias_sel = np.zeros((FOX_HEADS, FOX_HEAD_DIM, 4 * FOX_HEADS), np.float32)
    for h in range(FOX_HEADS):
        for part in range(3):
            bias_sel[h, part, 3 * FOX_HEADS] = 1.0
            bias_sel[h, 3 + part, part * 8 + h] = 1.0
        bias_sel[h, 6, 3 * FOX_HEADS] = 1.0
    dense = lambda w: jax.scipy.linalg.block_diag(*[w[i] for i in range(LRU_BLOCKS)]).astype(BF16)
    row = lambda a: a.reshape(1, -1)
    b16 = lambda a: jnp.asarray(a, BF16)
    return dict(g_mix=row(g_mix), w_all=w_all, bf_pad=bf_pad, eq=b16(eq), ek=b16(ek),
                cq=jnp.asarray(cq), ck=jnp.asarray(ck), cv=jnp.asarray(cv),
                bias_sel=b16(bias_sel),
                hsel=b16(hsel), hdiag=b16(hdiag), hmask=b16(hmask), conv_w=conv_w, conv_b=row(conv_b),
                wa=dense(w_a), ba=row(b_a), wi=dense(w_i), bi=row(b_i), lam=row(lam))


def kernel(x_prompt, x_sample, mem_prompt, cache_fox_k, cache_fox_v, cache_fox_logf, state_lru_h, state_conv, cache_mem_k, cache_mem_v, g_mix, w_in, b_f, conv_w, conv_b, w_a, b_a, w_i, b_i, lam, w_out, g_cross, g_mem, w_cq, w_ck, w_cv, w_co, g_moe, w_router, b_router, w_gate, b_gate, w_up, b_up, w_down, b_down, g_final):
    nb_p, seq, _ = x_prompt.shape
    nb_s, t_s, _ = x_sample.shape
    past = cache_fox_k.shape[2]
    n_mem = mem_prompt.shape[1]
    assert nb_p == 1 and g_mix.shape[0] == 1
    n_p, n_s = seq, nb_s * t_s
    row = lambda a: a.reshape(1, -1)

    mw = _prep_mix_weights(g_mix[0], w_in[0], b_f[0], conv_w[0], conv_b[0], w_a[0], b_a[0], w_i[0], b_i[0], lam[0])
    xp = x_prompt.reshape(n_p, D_MODEL)
    xs_ = x_sample.reshape(n_s, D_MODEL)

    zero_prev = jnp.zeros((1, SUBLANES, LRU_WIDTH), F32)
    zero_h = jnp.zeros((1, 1, LRU_WIDTH), F32)
    (qa_p, ka_p, va_p, k_p, v_p, _, lf_p, f2_p, qn_p, kn_p, lru_p, hl_p, ct_p) = _mix_in(
        xp, zero_prev, zero_h, mw, tm=MIX_TILE, streaming=True)
    prev_s = jnp.pad(state_conv[0], ((0, 0), (SUBLANES - (CONV_WIDTH - 1), 0), (0, 0)))
    (qa_s, ka_s, _, k_s, v_s, vb_s, lf_s, _, _, _, lru_s, hl_s, ct_s) = _mix_in(
        xs_, prev_s, state_lru_h[0].reshape(nb_s, 1, LRU_WIDTH), mw, tm=t_s, streaming=False)

    n_past, static_ok = _fox_plan(f2_p, qn_p, kn_p, bq=FOX_BQ, tile=MIX_TILE)
    fo_p = _fox_prompt(n_past, static_ok, qa_p, ka_p, va_p, bq=FOX_BQ)
    by_head = lambda c: jnp.moveaxis(c[0], 1, -1)
    fo_s = _fox_sample(qa_s, ka_s, vb_s, by_head(cache_fox_k), by_head(cache_fox_v), by_head(cache_fox_logf),
                       mw["bias_sel"], t=t_s)

    mk_p, mv_p = _mem_kv(mem_prompt[0], row(g_mem[0]), w_ck[0].astype(BF16), w_cv[0].astype(BF16))

    wr_pad = jnp.pad(w_router[0], ((0, 0), (0, LANES - N_EXPERTS))).astype(BF16)
    br_pad = jnp.pad(b_router[0], (0, LANES - N_EXPERTS)).reshape(1, LANES)
    pw = dict(w_out=w_out[0].astype(BF16), g_cross=row(g_cross[0]), w_cq=w_cq[0].astype(BF16),
              w_co=w_co[0].astype(BF16), g_moe=row(g_moe[0]), w_router=wr_pad, b_router=br_pad)
    cnt0 = jnp.zeros((SUBLANES, LANES), F32)
    h2_p, xn_p, idx_p, tw_p, rk_p, cnt_p = _post(
        xp, fo_p, lru_p, mk_p.reshape(1, n_mem, D_MODEL), mv_p.reshape(1, n_mem, D_MODEL), cnt0, pw,
        tm=POST_TILE, seg=POST_TILE, cached_memory=False)
    h2_s, xn_s, idx_s, tw_s, rk_s, cnt = _post(
        xs_, fo_s, lru_s, cache_mem_k, cache_mem_v, cnt_p, pw, tm=min(SAMPLE_POST_TILE, n_s), seg=t_s,
        cached_memory=True)

    n = n_p + n_s
    idx = jnp.concatenate([idx_p[:, :TOP_K], idx_s[:, :TOP_K]], axis=0)
    rank = jnp.concatenate([rk_p[:, :TOP_K], rk_s[:, :TOP_K]], axis=0)
    counts = cnt[0, :N_EXPERTS].astype(I32)
    padded = (counts + MOE_BM - 1) // MOE_BM * MOE_BM
    pad_end = jnp.cumsum(padded)
    pad_start = pad_end - padded
    dest = (pad_start[idx] + rank).reshape(n * TOP_K)
    n_blocks = -(-(n * TOP_K) // MOE_BM) + N_EXPERTS
    blk_row = jnp.arange(n_blocks, dtype=I32) * MOE_BM
    block_e = jnp.minimum(jnp.sum(pad_end[None, :] <= blk_row[:, None], axis=1), N_EXPERTS - 1).astype(I32)
    n_used = (pad_end[-1] // MOE_BM).reshape(1).astype(I32)
    last_block = jnp.maximum(pad_end // MOE_BM - 1, 0).astype(I32)

    xs_sorted = _dispatch(dest, last_block, n_used, xn_p, xn_s, tm=MOE_BM, n_blocks=n_blocks)
    ew = dict(w_gate=w_gate[0], w_up=w_up[0], w_down=w_down[0],
              b_gate=b_gate[0].reshape(N_EXPERTS, 1, D_MODEL), b_up=b_up[0].reshape(N_EXPERTS, 1, D_MODEL),
              b_down=b_down[0].reshape(N_EXPERTS, 1, D_MODEL))
    ys = _experts(block_e, n_used, (pad_end // MOE_BM).astype(I32), xs_sorted, ew, bm=MOE_BM)
    y_p, y_s = _combine(dest, h2_p, h2_s, tw_p, tw_s, row(g_final), ys, tm=COMBINE_TILE)

    shp_p = (1, nb_p, seq, FOX_HEADS, FOX_HEAD_DIM)
    shp_s = (1, nb_s, t_s, FOX_HEADS, FOX_HEAD_DIM)
    tail = slice(SUBLANES - (CONV_WIDTH - 1), SUBLANES)
    return (y_p.reshape(nb_p, seq, D_MODEL), y_s.reshape(nb_s, t_s, D_MODEL),
            k_p.reshape(shp_p), v_p.reshape(shp_p), lf_p.reshape(1, nb_p, seq, FOX_HEADS),
            hl_p.reshape(1, nb_p, LRU_WIDTH), ct_p[:, tail, :].reshape(1, nb_p, CONV_WIDTH - 1, LRU_WIDTH),
            mk_p.reshape(1, nb_p, n_mem, MEM_HEADS, MEM_HEAD_DIM), mv_p.reshape(1, nb_p, n_mem, MEM_HEADS, MEM_HEAD_DIM),
            k_s.reshape(shp_s), v_s.reshape(shp_s), lf_s.reshape(1, nb_s, t_s, FOX_HEADS),
            hl_s.reshape(1, nb_s, LRU_WIDTH), ct_s[:, tail, :].reshape(1, nb_s, CONV_WIDTH - 1, LRU_WIDTH))
```

```python
import functools
import math

import jax
import jax.numpy as jnp
import numpy as np
from jax import lax
from jax.experimental import pallas as pl
from jax.experimental.pallas import tpu as pltpu

F32 = jnp.float32
BF16 = jnp.bfloat16
I32 = jnp.int32

D_MODEL = 1024
FOX_HEADS = 8
FOX_HEAD_DIM = 64
FOX_WIDTH = FOX_HEADS * FOX_HEAD_DIM
LRU_WIDTH = D_MODEL - FOX_WIDTH
LRU_BLOCKS = 8
LRU_C = 8.0
CONV_WIDTH = 4
MEM_HEADS = 4
MEM_HEAD_DIM = D_MODEL // MEM_HEADS
N_EXPERTS = 32
TOP_K = 4
SWIGLU_LIMIT = 7.0
SWIGLU_ALPHA = 1.702
RMS_EPS = 1e-6
NEG_INF = -1e30
LOG2E = math.log2(math.e)

LANES = 128
SUBLANES = 8
CHUNKS = D_MODEL // LANES
TOKEN_TILE = (CHUNKS, SUBLANES, LANES)
AUG = LANES * FOX_HEADS
VMEM_LIMIT = 56 * 1024 * 1024

MIX_TILE = 256
POST_TILE = 512
SAMPLE_POST_TILE = 256
FOX_BQ = 512
MOE_BM = 512
COMBINE_TILE = 256
FOX_SKIP_GAP = 160.0
FOX_STATIC_SHIFT_RANGE = 100.0

C_Q, C_K, C_V, C_F, C_XR, C_G, C_END = 0, 512, 1024, 1536, 1664, 2176, 2688


def _dot(a, b):
    return jnp.dot(a, b, preferred_element_type=F32)


def _dot_nt(a, b):
    return lax.dot_general(a, b, (((1,), (1,)), ((), ())), preferred_element_type=F32)


def _split3(x):
    hi = x.astype(BF16)
    r = x - hi.astype(F32)
    mid = r.astype(BF16)
    lo = (r - mid.astype(F32)).astype(BF16)
    return hi, mid, lo


def _dot3(a, x):
    hi, mid, lo = _split3(x)
    return _dot(a, hi) + _dot(a, mid) + _dot(a, lo)


def _softplus(x):
    return jnp.maximum(x, 0.0) + jnp.log1p(jnp.exp(-jnp.abs(x)))


def _rms(x, g):
    return x * lax.rsqrt(jnp.mean(x * x, axis=-1, keepdims=True) + RMS_EPS) * g


def _tri(n, kind):
    r = lax.broadcasted_iota(I32, (n, n), 0)
    c = lax.broadcasted_iota(I32, (n, n), 1)
    m = {"le": c <= r, "lt": c < r, "gt": c > r}[kind]
    return jnp.where(m, 1.0, 0.0).astype(BF16)


def _mix_in_kernel(x_ref, g_ref, w_ref, bfp_ref, eq_ref, ek_ref, cq_ref, ck_ref, cv_ref, hs_ref, hd_ref, hm_ref,
                   cw_ref, cb_ref, wa_ref, ba_ref, wi_ref, bi_ref, lam_ref, cprev_ref, h0_ref,
                   qa_ref, ka_ref, va_ref, ko_ref, vo_ref, vb_ref, lf_ref, f2_ref, qn_ref, kn_ref,
                   lru_ref, hl_ref, ct_ref,
                   fcar, hcar, xp_ref, sa_ref, sb_ref, kbuf, vbuf, ksem, *, tm, pad, streaming):
    step = pl.program_id(0)
    first = step == 0
    slot = step % 2

    def kv_copies(s, row0):
        rows = pl.ds(row0, tm)
        copies = []
        for h in range(FOX_HEADS):
            copies.append(pltpu.make_async_copy(kbuf.at[s, h], ko_ref.at[rows, h, :], ksem.at[s]))
            copies.append(pltpu.make_async_copy(vbuf.at[s, h], vo_ref.at[rows, h, :], ksem.at[s]))
        return copies

    @pl.when(step >= 2)
    def _():
        for c in kv_copies(slot, 0):
            c.wait()

    if streaming:
        @pl.when(first)
        def _():
            fcar[...] = jnp.zeros_like(fcar)
            hcar[...] = jnp.zeros_like(hcar)
            xp_ref[0:SUBLANES, :] = jnp.zeros((SUBLANES, LRU_WIDTH), F32)
    else:
        fcar[...] = jnp.zeros_like(fcar)
        hcar[...] = jnp.broadcast_to(h0_ref[0], hcar.shape)
        xp_ref[0:SUBLANES, :] = cprev_ref[0]

    x = x_ref[...]
    xn = _rms(x, g_ref[...]).astype(BF16)
    z = _dot(xn, w_ref[...])

    zq, zk, zv = z[:, C_Q:C_K], z[:, C_K:C_V], z[:, C_V:C_F]
    for h in range(FOX_HEADS):
        hc = slice(h * FOX_HEAD_DIM, (h + 1) * FOX_HEAD_DIM)
        kbuf[slot, h] = zk[:, hc]
        vbuf[slot, h] = zv[:, hc]
    for c in kv_copies(slot, pl.multiple_of(step * tm, tm)):
        c.start()
    qd, kd, vd = zq.astype(BF16), zk.astype(BF16), zv.astype(BF16)
    vb_ref[...] = vd

    def head_pad(d):
        blocks = []
        for p in range(FOX_HEADS // 2):
            blk = d[:, p * LANES:(p + 1) * LANES]
            blocks += [blk * hm_ref[0:1, :], blk * hm_ref[1:2, :]]
        return jnp.concatenate(blocks, axis=1)

    lane = lax.broadcasted_iota(I32, (tm, LANES), 1)
    lf = jnp.where(lane < 3 * FOX_HEADS, -_softplus(-(z[:, C_F:C_XR] + bfp_ref[...])), 0.0)
    lf_ref[...] = lf[:, :FOX_HEADS]
    cum = _dot3(_tri(tm, "le"), lf) + fcar[0:1, :]
    fcar[...] = jnp.broadcast_to(cum[tm - 1:tm, :], fcar.shape)
    cum2 = cum * LOG2E
    f2_ref[...] = cum2[:, :FOX_HEADS]
    qf, kf = qd.astype(F32), kd.astype(F32)
    diag = _dot((qf * kf).astype(BF16), hd_ref[...]).astype(BF16)
    hi, mid, lo = _split3(cum2)
    bias = jnp.where(lane < 8, hi, jnp.where(lane < 16, mid, jnp.where(lane < 24, lo, diag)))
    qa_ref[...] = head_pad(qd) + (_dot(bias, eq_ref[...]) + cq_ref[...]).astype(BF16)
    ka_ref[...] = head_pad(kd) + (ck_ref[...] - _dot(bias, ek_ref[...])).astype(BF16)
    va_ref[...] = head_pad(vd) + cv_ref[...].astype(BF16)
    qn_ref[...] = jnp.broadcast_to(jnp.max(_dot((qf * qf).astype(BF16), hs_ref[...]), axis=0, keepdims=True),
                                   qn_ref.shape)
    kn_ref[...] = jnp.broadcast_to(jnp.max(_dot((kf * kf).astype(BF16), hs_ref[...]), axis=0, keepdims=True),
                                   kn_ref.shape)

    xr = z[:, C_XR:C_G]
    xp_ref[SUBLANES:SUBLANES + tm, :] = xr
    xc = cb_ref[...] + xr * cw_ref[CONV_WIDTH - 1:CONV_WIDTH, :]
    for j in range(CONV_WIDTH - 1):
        sh = CONV_WIDTH - 1 - j
        xc = xc + xp_ref[SUBLANES - sh:SUBLANES - sh + tm, :] * cw_ref[j:j + 1, :]
    tail = xp_ref[tm:tm + SUBLANES, :]
    ct_ref[0] = tail
    xp_ref[0:SUBLANES, :] = tail

    xcb = xc.astype(BF16)
    r = jax.nn.sigmoid(_dot(xcb, wa_ref[...]) + ba_ref[...])
    ig = jax.nn.sigmoid(_dot(xcb, wi_ref[...]) + bi_ref[...])
    log_a = (-LRU_C) * r * _softplus(-lam_ref[...])
    a = jnp.exp(log_a)
    mult = jnp.sqrt(-jnp.tanh(log_a) * (a * a + 1.0))
    if streaming:
        row = lax.broadcasted_iota(I32, (tm, LRU_WIDTH), 0)
        mult = jnp.where(jnp.logical_and(row == 0, first), 1.0, mult)
    b = mult * ig * xc

    sa_ref[0:pad, :] = jnp.ones((pad, LRU_WIDTH), F32)
    sb_ref[0:pad, :] = jnp.zeros((pad, LRU_WIDTH), F32)
    d = 1
    while d < tm:
        sa_ref[pad:pad + tm, :] = a
        sb_ref[pad:pad + tm, :] = b
        b = a * sb_ref[pad - d:pad - d + tm, :] + b
        a = a * sa_ref[pad - d:pad - d + tm, :]
        d *= 2
    h = a * hcar[0:1, :] + b
    hlast = h[tm - 1:tm, :]
    hcar[...] = jnp.broadcast_to(hlast, hcar.shape)
    hl_ref[0] = hlast
    lru_ref[...] = (h * jax.nn.gelu(z[:, C_G:C_END])).astype(BF16)

    last = pl.num_programs(0) - 1

    @pl.when(jnp.logical_and(step == last, step >= 1))
    def _():
        for c in kv_copies(1 - slot, 0):
            c.wait()

    @pl.when(step == last)
    def _():
        for c in kv_copies(slot, 0):
            c.wait()


def _mix_in(x, cprev, h0, wts, *, tm, streaming):
    n = x.shape[0]
    steps = n // tm
    nseg = 1 if streaming else steps
    pad = max(tm // 2, SUBLANES)
    seg = (lambda i: (0, 0, 0)) if streaming else (lambda i: (i, 0, 0))
    full = lambda a: pl.BlockSpec(a.shape, lambda i: (0,) * a.ndim)
    rows = lambda w: pl.BlockSpec((tm, w), lambda i: (i, 0))
    names = ("g_mix", "w_all", "bf_pad", "eq", "ek", "cq", "ck", "cv", "hsel", "hdiag", "hmask", "conv_w", "conv_b",
             "wa", "ba", "wi", "bi", "lam")
    ws = [wts[k] for k in names]
    sds = jax.ShapeDtypeStruct
    out_shape = (
        sds((n, AUG), BF16), sds((n, AUG), BF16), sds((n, AUG), BF16),
        sds((n, FOX_HEADS, FOX_HEAD_DIM), F32), sds((n, FOX_HEADS, FOX_HEAD_DIM), F32), sds((n, FOX_WIDTH), BF16),
        sds((n, FOX_HEADS), F32), sds((n, FOX_HEADS), F32),
        sds((steps * SUBLANES, LANES), F32), sds((steps * SUBLANES, LANES), F32),
        sds((n, LRU_WIDTH), BF16),
        sds((nseg, 1, LRU_WIDTH), F32), sds((nseg, SUBLANES, LRU_WIDTH), F32),
    )
    out_specs = (
        rows(AUG), rows(AUG), rows(AUG), pl.BlockSpec(memory_space=pl.ANY), pl.BlockSpec(memory_space=pl.ANY),
        rows(FOX_WIDTH), rows(FOX_HEADS), rows(FOX_HEADS),
        pl.BlockSpec((SUBLANES, LANES), lambda i: (i, 0)), pl.BlockSpec((SUBLANES, LANES), lambda i: (i, 0)),
        rows(LRU_WIDTH),
        pl.BlockSpec((1, 1, LRU_WIDTH), seg), pl.BlockSpec((1, SUBLANES, LRU_WIDTH), seg),
    )
    in_specs = [rows(D_MODEL)] + [full(w) for w in ws] + [
        pl.BlockSpec((1, SUBLANES, LRU_WIDTH), seg), pl.BlockSpec((1, 1, LRU_WIDTH), seg)]
    return pl.pallas_call(
        functools.partial(_mix_in_kernel, tm=tm, pad=pad, streaming=streaming),
        grid=(steps,), in_specs=in_specs, out_specs=out_specs, out_shape=out_shape,
        scratch_shapes=[
            pltpu.VMEM((SUBLANES, LANES), F32), pltpu.VMEM((SUBLANES, LRU_WIDTH), F32),
            pltpu.VMEM((tm + SUBLANES, LRU_WIDTH), F32),
            pltpu.VMEM((pad + tm, LRU_WIDTH), F32), pltpu.VMEM((pad + tm, LRU_WIDTH), F32),
            pltpu.VMEM((2, FOX_HEADS, tm, FOX_HEAD_DIM), F32), pltpu.VMEM((2, FOX_HEADS, tm, FOX_HEAD_DIM), F32),
            pltpu.SemaphoreType.DMA((2,))],
        compiler_params=pltpu.CompilerParams(dimension_semantics=("arbitrary",), vmem_limit_bytes=VMEM_LIMIT),
        name="mix_in_stream" if streaming else "mix_in_segments",
    )(x, *ws, cprev, h0)


def _fox_prompt_kernel(nb_ref, st_ref, q_ref, k_ref, v_ref, o_ref, m_ref, acc_ref, *, bq):
    qi = pl.program_id(1)
    plan = pl.program_id(0) * pl.num_programs(1) + qi
    n_past = nb_ref[plan]
    cols = [slice(LANES * c, LANES * (c + 1)) for c in range(2)]
    qs = [q_ref[:, cols[c]] for c in range(2)]
    acc_ref[...] = jnp.zeros(acc_ref.shape, F32)

    def scores(c, start, causal):
        s = _dot_nt(qs[c], k_ref[pl.ds(start, bq), cols[c]])
        if causal:
            r = lax.broadcasted_iota(I32, (bq, bq), 0)
            cc = lax.broadcasted_iota(I32, (bq, bq), 1)
            s = jnp.where(cc <= r, s, NEG_INF)
        return s

    def static_block(start, causal):
        for c in range(2):
            p = jnp.exp2(scores(c, start, causal)).astype(BF16)
            acc_ref[c] += _dot(p, v_ref[pl.ds(start, bq), cols[c]])

    def online_block(start, causal):
        for c in range(2):
            s = scores(c, start, causal)
            m_old = m_ref[c]
            m_new = jnp.maximum(m_old, jnp.max(s, axis=1, keepdims=True))
            p = jnp.exp2(s - m_new).astype(BF16)
            acc_ref[c] = jnp.exp2(m_old - m_new) * acc_ref[c] + _dot(p, v_ref[pl.ds(start, bq), cols[c]])
            m_ref[c] = m_new

    def run(block):
        block(pl.multiple_of(qi * bq, bq), True)

        def body(t, carry):
            block(pl.multiple_of((qi - 1 - t) * bq, bq), False)
            return carry

        lax.fori_loop(0, n_past, body, 0)

    @pl.when(st_ref[plan] == 1)
    def _():
        run(static_block)

    @pl.when(st_ref[plan] != 1)
    def _():
        m_ref[...] = jnp.full(m_ref.shape, NEG_INF, F32)
        run(online_block)

    a0, a1 = acc_ref[0], acc_ref[1]
    o0 = a0 / a0[:, FOX_HEAD_DIM:FOX_HEAD_DIM + 1]
    o1 = a1 / a1[:, 0:1]
    lane = lax.broadcasted_iota(I32, (bq, LANES), 1)
    o_ref[...] = jnp.where(lane < FOX_HEAD_DIM, o0, o1).astype(BF16)


def _fox_prompt(n_past, static_ok, qa, ka, va, *, bq):
    s = qa.shape[0]
    pairs = FOX_HEADS // 2
    return pl.pallas_call(
        functools.partial(_fox_prompt_kernel, bq=bq),
        grid_spec=pltpu.PrefetchScalarGridSpec(
            num_scalar_prefetch=2, grid=(pairs, s // bq),
            in_specs=[pl.BlockSpec((bq, 2 * LANES), lambda p, i, nb, st: (i, p)),
                      pl.BlockSpec((s, 2 * LANES), lambda p, i, nb, st: (0, p)),
                      pl.BlockSpec((s, 2 * LANES), lambda p, i, nb, st: (0, p))],
            out_specs=pl.BlockSpec((bq, LANES), lambda p, i, nb, st: (i, p)),
            scratch_shapes=[pltpu.VMEM((2, bq, 1), F32), pltpu.VMEM((2, bq, LANES), F32)]),
        out_shape=jax.ShapeDtypeStruct((s, FOX_WIDTH), BF16),
        compiler_params=pltpu.CompilerParams(dimension_semantics=("arbitrary", "arbitrary"),
                                             vmem_limit_bytes=VMEM_LIMIT),
        name="fox_prompt",
    )(n_past, static_ok, qa, ka, va)


def _fox_plan(f2, qn, kn, *, bq, tile):
    s = f2.shape[0]
    nq = s // bq
    per = lambda a: a.reshape(s // tile, SUBLANES, LANES)[:, 0, :FOX_HEADS]
    qnorm = jnp.sqrt(jnp.max(per(qn).reshape(nq, bq // tile, FOX_HEADS), axis=1))
    knorm = jnp.sqrt(jnp.max(per(kn), axis=0))
    spread = 2.0 * 1.02 * qnorm * knorm[None, :]
    thr = spread + FOX_SKIP_GAP
    f_first = f2[0::bq]
    f_last = f2[bq - 1::bq]
    gap = f_last[None, :, :] - f_first[:, None, :]
    before = (jnp.arange(nq)[None, :] < jnp.arange(nq)[:, None])[:, :, None]
    need = jnp.sum(jnp.logical_and(before, gap < thr[:, None, :]), axis=1)
    pair = lambda a: a.reshape(nq, FOX_HEADS // 2, 2)
    n_past = jnp.max(pair(need), axis=2).T.reshape(-1).astype(I32)
    static_ok = jnp.all(pair(spread) <= FOX_STATIC_SHIFT_RANGE, axis=2).T.reshape(-1).astype(I32)
    return n_past, static_ok


def _fox_sample_kernel(q_ref, kn_ref, vn_ref, kt_ref, vt_ref, lf_ref, sel_ref, o_ref, g_ref,
                       *, t, past, chunk):
    car = jnp.zeros((FOX_HEADS, 1), F32)
    later = _tri(chunk, "lt")
    for ci in reversed(range(past // chunk)):
        lf = lf_ref[0, :, ci * chunk:(ci + 1) * chunk]
        hi, mid, lo = _split3(lf)
        g_ref[:, ci * chunk:(ci + 1) * chunk] = _dot(hi, later) + _dot(mid, later) + _dot(lo, later) + car
        car = car + jnp.sum(lf, axis=1, keepdims=True)
    hi, mid, lo = _split3(g_ref[...] * LOG2E)
    bias_rows = jnp.concatenate([hi.astype(F32), mid.astype(F32), lo.astype(F32),
                                 jnp.ones((FOX_HEADS, past), F32)], axis=0).astype(BF16)

    def swap_halves(x):
        return pltpu.roll(x.astype(F32), FOX_HEAD_DIM, axis=1).astype(BF16)

    r = lax.broadcasted_iota(I32, (t, t), 0)
    cc = lax.broadcasted_iota(I32, (t, t), 1)
    prow = lax.broadcasted_iota(I32, (FOX_HEAD_DIM, LANES), 0)
    plane = lax.broadcasted_iota(I32, (FOX_HEAD_DIM, LANES), 1)
    place = [jnp.where(plane == prow + FOX_HEAD_DIM * c, 1.0, 0.0).astype(BF16) for c in range(2)]
    for p in range(FOX_HEADS // 2):
        pc = slice(LANES * p, LANES * (p + 1))
        vpair = vn_ref[:, pc]
        o_pair = jnp.zeros((t, LANES), F32)
        for c in range(2):
            h = 2 * p + c
            hc = slice(LANES * h, LANES * (h + 1))
            q, kn, vn = q_ref[:, hc], kn_ref[:, hc], vpair
            if c == 1:
                q, kn, vn = swap_halves(q), swap_halves(kn), swap_halves(vn)
            bias = _dot(sel_ref[h], bias_rows).astype(BF16)
            kpast_t = jnp.concatenate([kt_ref[0, h].astype(BF16), bias], axis=0)
            sp = _dot(q, kpast_t)
            sn = jnp.where(cc <= r, _dot_nt(q, kn), NEG_INF)
            m = jnp.maximum(jnp.max(sp, axis=1, keepdims=True), jnp.max(sn, axis=1, keepdims=True))
            pp = jnp.exp2(sp - m)
            pn = jnp.exp2(sn - m)
            l = jnp.sum(pp, axis=1, keepdims=True) + jnp.sum(pn, axis=1, keepdims=True)
            o = (_dot_nt(pp.astype(BF16), vt_ref[0, h].astype(BF16))
                 + _dot(pn.astype(BF16), vn[:, :FOX_HEAD_DIM])) / l
            o_pair = o_pair + _dot(o.astype(BF16), place[c])
        o_ref[:, pc] = o_pair.astype(BF16)


def _fox_sample(qa, ka, vb, cache_kt, cache_vt, cache_lft, sel, *, t):
    nb, past = cache_kt.shape[0], cache_kt.shape[3]
    full = lambda a: pl.BlockSpec(a.shape, lambda b: (0,) * a.ndim)
    slab = pl.BlockSpec((1, FOX_HEADS, FOX_HEAD_DIM, past), lambda b: (b, 0, 0, 0))
    return pl.pallas_call(
        functools.partial(_fox_sample_kernel, t=t, past=past, chunk=256),
        grid=(nb,),
        in_specs=[pl.BlockSpec((t, AUG), lambda b: (b, 0)), pl.BlockSpec((t, AUG), lambda b: (b, 0)),
                  pl.BlockSpec((t, FOX_WIDTH), lambda b: (b, 0)), slab, slab,
                  pl.BlockSpec((1, FOX_HEADS, past), lambda b: (b, 0, 0)), full(sel)],
        out_specs=pl.BlockSpec((t, FOX_WIDTH), lambda b: (b, 0)),
        out_shape=jax.ShapeDtypeStruct((nb * t, FOX_WIDTH), BF16),
        scratch_shapes=[pltpu.VMEM((FOX_HEADS, past), F32)],
        compiler_params=pltpu.CompilerParams(dimension_semantics=("arbitrary",), vmem_limit_bytes=VMEM_LIMIT),
        name="fox_sample",
    )(qa, ka, vb, cache_kt, cache_vt, cache_lft, sel)


def _mem_kv_kernel(m_ref, g_ref, wk_ref, wv_ref, k_ref, v_ref):
    mn = _rms(m_ref[...], g_ref[...]).astype(BF16)
    k_ref[...] = _dot(mn, wk_ref[...])
    v_ref[...] = _dot(mn, wv_ref[...])


def _mem_kv(mem, g, wk, wv):
    n = mem.shape[0]
    return pl.pallas_call(
        _mem_kv_kernel,
        out_shape=(jax.ShapeDtypeStruct((n, D_MODEL), F32), jax.ShapeDtypeStruct((n, D_MODEL), F32)),
        compiler_params=pltpu.CompilerParams(vmem_limit_bytes=VMEM_LIMIT),
        name="mem_kv",
    )(mem, g, wk, wv)


def _post_kernel(h_ref, fo_ref, lru_ref, wo_ref, gc_ref, wcq_ref, mk_ref, mv_ref, wco_ref,
                 gm_ref, wr_ref, br_ref, cnt_in_ref,
                 h2_ref, xn_ref, idx_ref, tw_ref, rank_ref, cnt_ref, car_ref, *mem_scratch, tm, seg, cached_memory):
    step = pl.program_id(0)
    nseg = tm // seg

    @pl.when(step == 0)
    def _():
        car_ref[...] = cnt_in_ref[...]

    if cached_memory:
        mbuf, msem = mem_scratch
        slot = step % 2

        def mem_copies(at_step, s):
            return [pltpu.make_async_copy(ref.at[0, at_step * nseg + j, :, hd, :], mbuf.at[s, j, t, hd], msem.at[s])
                    for j in range(nseg) for t, ref in enumerate((mk_ref, mv_ref)) for hd in range(MEM_HEADS)]

        @pl.when(step == 0)
        def _():
            for c in mem_copies(0, 0):
                c.start()

        @pl.when(step + 1 < pl.num_programs(0))
        def _():
            for c in mem_copies(step + 1, 1 - slot):
                c.start()

    h1 = h_ref[...] + _dot(fo_ref[...], wo_ref[0:FOX_WIDTH, :]) + _dot(lru_ref[...], wo_ref[FOX_WIDTH:D_MODEL, :])

    q = _dot(_rms(h1, gc_ref[...]).astype(BF16), wcq_ref[...])
    if cached_memory:
        for c in mem_copies(0, slot):
            c.wait()
    segs = []
    for j in range(nseg):
        rows = slice(j * seg, (j + 1) * seg)
        heads = []
        for hd in range(MEM_HEADS):
            hc = slice(MEM_HEAD_DIM * hd, MEM_HEAD_DIM * (hd + 1))
            mk = mbuf[slot, j, 0, hd] if cached_memory else mk_ref[0, :, hc]
            mv = mbuf[slot, j, 1, hd] if cached_memory else mv_ref[0, :, hc]
            s = _dot_nt(q[rows, hc].astype(BF16), mk.astype(BF16)) * (MEM_HEAD_DIM ** -0.5)
            p = jnp.exp(s - jnp.max(s, axis=1, keepdims=True))
            o = _dot(p.astype(BF16), mv.astype(BF16)) / jnp.sum(p, axis=1, keepdims=True)
            heads.append(o.astype(BF16))
        segs.append(jnp.concatenate(heads, axis=1))
    h2 = h1 + _dot(segs[0] if nseg == 1 else jnp.concatenate(segs, axis=0), wco_ref[...])
    h2_ref[...] = h2

    xn = _rms(h2, gm_ref[...])
    for c in range(CHUNKS):
        xn_ref[:, c, :] = xn[:, c * LANES:(c + 1) * LANES]

    lane = lax.broadcasted_iota(I32, (tm, LANES), 1).astype(F32)
    logits = jnp.where(lane < N_EXPERTS, _dot(xn.astype(BF16), wr_ref[...]) + br_ref[...], -jnp.inf)
    vals, idxs = [], []
    for _ in range(TOP_K):
        mx = jnp.max(logits, axis=1, keepdims=True)
        ix = jnp.min(jnp.where(logits == mx, lane, float(LANES)), axis=1, keepdims=True)
        vals.append(mx)
        idxs.append(ix)
        logits = jnp.where(lane == ix, -jnp.inf, logits)
    es = [jnp.exp(v - vals[0]) for v in vals]
    den = es[0] + es[1] + es[2] + es[3]

    onehot = jnp.zeros((tm, LANES), F32)
    for ix in idxs:
        onehot = onehot + jnp.where(lane == ix, 1.0, 0.0)
    before = _dot(_tri(tm, "lt"), onehot.astype(BF16)) + car_ref[0:1, :]
    car = car_ref[0:1, :] + jnp.sum(onehot, axis=0, keepdims=True)
    car_ref[...] = jnp.broadcast_to(car, car_ref.shape)
    cnt_ref[...] = jnp.broadcast_to(car, cnt_ref.shape)

    idx_o = jnp.zeros((tm, LANES), F32)
    tw_o = jnp.zeros((tm, LANES), F32)
    rk_o = jnp.zeros((tm, LANES), F32)
    for j in range(TOP_K):
        rk = jnp.sum(jnp.where(lane == idxs[j], before, 0.0), axis=1, keepdims=True)
        idx_o = jnp.where(lane == j, idxs[j], idx_o)
        tw_o = jnp.where(lane == j, es[j] / den, tw_o)
        rk_o = jnp.where(lane == j, rk, rk_o)
    idx_ref[...] = idx_o.astype(I32)
    tw_ref[...] = tw_o
    rank_ref[...] = rk_o.astype(I32)


def _post(h, fo, lru, mk, mv, cnt_in, wts, *, tm, seg, cached_memory):
    n = h.shape[0]
    full = lambda a: pl.BlockSpec(a.shape, lambda i: (0,) * a.ndim)
    rows = lambda w: pl.BlockSpec((tm, w), lambda i: (i, 0))
    if cached_memory:
        nm = mk.shape[2]
        mem = pl.BlockSpec(memory_space=pl.ANY)
        mem_scratch = [pltpu.VMEM((2, tm // seg, 2, MEM_HEADS, nm, MEM_HEAD_DIM), F32),
                       pltpu.SemaphoreType.DMA((2,))]
    else:
        mem = full(mk)
        mem_scratch = []
    w = [wts[k] for k in ("w_out", "g_cross", "w_cq")]
    w2 = [wts[k] for k in ("w_co", "g_moe", "w_router", "b_router")]
    return pl.pallas_call(
        functools.partial(_post_kernel, tm=tm, seg=seg, cached_memory=cached_memory),
        grid=(n // tm,),
        in_specs=[rows(D_MODEL), rows(FOX_WIDTH), rows(LRU_WIDTH)] + [full(a) for a in w] + [mem, mem]
                 + [full(a) for a in w2] + [full(cnt_in)],
        out_specs=(rows(D_MODEL), pl.BlockSpec((tm, CHUNKS, LANES), lambda i: (i, 0, 0)),
                   rows(LANES), rows(LANES), rows(LANES),
                   pl.BlockSpec((SUBLANES, LANES), lambda i: (0, 0))),
        out_shape=(jax.ShapeDtypeStruct((n, D_MODEL), F32), jax.ShapeDtypeStruct((n, CHUNKS, LANES), F32),
                   jax.ShapeDtypeStruct((n, LANES), I32), jax.ShapeDtypeStruct((n, LANES), F32),
                   jax.ShapeDtypeStruct((n, LANES), I32), jax.ShapeDtypeStruct((SUBLANES, LANES), F32)),
        scratch_shapes=[pltpu.VMEM((SUBLANES, LANES), F32)] + mem_scratch,
        compiler_params=pltpu.CompilerParams(dimension_semantics=("arbitrary",), vmem_limit_bytes=VMEM_LIMIT),
        name="post_batch_mem" if cached_memory else "post_shared_mem",
    )(h, fo, lru, *w, mk, mv, *w2, cnt_in)


STAGES = 3


def _dispatch_kernel(dest_ref, last_ref, nu_ref, xp_ref, xs_ref, out_ref, zero_ref, stage, sems, gsem, zsem,
                     *, tm, steps_p, n_blocks):
    i = pl.program_id(0)

    def zero_copy(b):
        return pltpu.make_async_copy(zero_ref, out_ref.at[pl.ds(pl.multiple_of(b * tm, tm), tm)], zsem)

    @pl.when(i == 0)
    def _():
        zero_ref[...] = jnp.zeros(zero_ref.shape, F32)
        n_tail = n_blocks - nu_ref[0]

        def start_e(e, c):
            zero_copy(last_ref[e]).start()
            return c

        def start_t(b, c):
            zero_copy(nu_ref[0] + b).start()
            return c

        def wait_one(b, c):
            zero_copy(0).wait()
            return c

        lax.fori_loop(0, N_EXPERTS, start_e, 0)
        lax.fori_loop(0, n_tail, start_t, 0)
        lax.fori_loop(0, N_EXPERTS + n_tail, wait_one, 0)

    groups = tm // SUBLANES
    steps = pl.num_programs(0)

    def stage_copy(src_ref, tile, slot):
        return pltpu.make_async_copy(src_ref.at[pl.ds(tile * tm, tm)], stage.at[slot], gsem.at[slot])

    def start_stage(tile, slot):
        @pl.when(tile < steps_p)
        def _():
            stage_copy(xp_ref, tile, slot).start()

        @pl.when(tile >= steps_p)
        def _():
            stage_copy(xs_ref, tile - steps_p, slot).start()

    @pl.when(i == 0)
    def _():
        start_stage(0, 0)

    @pl.when(i + 1 < steps)
    def _():
        start_stage(i + 1, (i + 1) % STAGES)

    slot = i % STAGES
    stage_copy(xp_ref, 0, slot).wait()

    base = i * (tm * TOP_K)

    def start(g, c):
        for u in range(SUBLANES):
            for j in range(TOP_K):
                d = dest_ref[base + (g * SUBLANES + u) * TOP_K + j]
                pltpu.make_async_copy(stage.at[slot, g * SUBLANES + u], out_ref.at[d],
                                      sems.at[i % 2]).start(priority=j % 2)
        return c

    lax.fori_loop(0, groups, start, 0)

    def drain(parity):
        for _ in range(TOP_K):
            pltpu.make_async_copy(out_ref.at[pl.ds(0, tm)], out_ref.at[pl.ds(0, tm)], sems.at[parity]).wait()

    @pl.when(i > 0)
    def _():
        drain((i - 1) % 2)

    @pl.when(i == steps - 1)
    def _():
        drain(i % 2)


def _dispatch(dest_flat, last_block, n_used, xn_p, xn_s, *, tm, n_blocks):
    steps_p, steps_s = xn_p.shape[0] // tm, xn_s.shape[0] // tm
    return pl.pallas_call(
        functools.partial(_dispatch_kernel, tm=tm, steps_p=steps_p, n_blocks=n_blocks),
        grid_spec=pltpu.PrefetchScalarGridSpec(
            num_scalar_prefetch=3, grid=(steps_p + steps_s,),
            in_specs=[pl.BlockSpec(memory_space=pl.ANY), pl.BlockSpec(memory_space=pl.ANY)],
            out_specs=pl.BlockSpec(memory_space=pl.ANY),
            scratch_shapes=[pltpu.VMEM((tm, CHUNKS, LANES), F32),
                            pltpu.VMEM((STAGES, tm, CHUNKS, LANES), F32),
                            pltpu.SemaphoreType.DMA((2,)), pltpu.SemaphoreType.DMA((STAGES,)),
                            pltpu.SemaphoreType.DMA(())]),
        out_shape=jax.ShapeDtypeStruct((n_blocks * tm, CHUNKS, LANES), F32),
        compiler_params=pltpu.CompilerParams(dimension_semantics=("arbitrary",), vmem_limit_bytes=VMEM_LIMIT),
        name="moe_dispatch",
    )(dest_flat, last_block, n_used, xn_p, xn_s)


def _rows_from_tiles(view, rows):
    return jnp.concatenate([view[:, c].reshape(rows, LANES) for c in range(CHUNKS)], axis=1)


def _rows_to_tiles(view, val, rows):
    for c in range(CHUNKS):
        view[:, c] = val[:, c * LANES:(c + 1) * LANES].reshape(rows // SUBLANES, SUBLANES, LANES)


def _expert_kernel(be_ref, nu_ref, nx_ref, xs_ref, wg_ref, bg_ref, wu_ref, bu_ref, wd_ref, bd_ref, ys_ref,
                   wgb, wub, wdb, wbuf, xbuf, ybuf, seq, wsem, xsem, ysem, *, bm):
    i = pl.program_id(0)
    steps = pl.num_programs(0)
    groups = bm // SUBLANES
    slot = i % 2
    prev = be_ref[jnp.maximum(i - 1, 0)]
    fresh = jnp.logical_or(i == 0, be_ref[i] != prev)
    live = i < nu_ref[0]

    def w_copies(e, s):
        return [pltpu.make_async_copy(w_ref.at[e], wbuf.at[s, k], wsem.at[s])
                for k, w_ref in enumerate((wg_ref, wu_ref, wd_ref))]

    @pl.when(i == 0)
    def _():
        seq[0] = 0
        for c in w_copies(be_ref[0], 0):
            c.start()

    def in_copies(blk, s):
        g0 = pl.multiple_of(blk * groups, groups)
        return [pltpu.make_async_copy(xs_ref.at[pl.ds(g0, groups), u], xbuf.at[s, :, :, u, :], xsem.at[s])
                for u in range(SUBLANES)]

    def out_copies(blk, s):
        g0 = pl.multiple_of(blk * groups, groups)
        return [pltpu.make_async_copy(ybuf.at[s, :, :, u, :], ys_ref.at[pl.ds(g0, groups), u], ysem.at[s])
                for u in range(SUBLANES)]

    @pl.when(i == 0)
    def _():
        for c in in_copies(0, 0):
            c.start()

    @pl.when(i + 1 < nu_ref[0])
    def _():
        for c in in_copies(i + 1, (i + 1) % 2):
            c.start()

    @pl.when(i >= 2)
    def _():
        for c in out_copies(0, slot):
            c.wait()

    @pl.when(jnp.logical_and(live, fresh))
    def _():
        s = seq[0] % 2
        for c in w_copies(0, s):
            c.wait()
        for static_s in range(2):
            @pl.when(s == static_s)
            def _():
                wgb[...] = wbuf[static_s, 0].astype(BF16)
                wub[...] = wbuf[static_s, 1].astype(BF16)
                wdb[...] = wbuf[static_s, 2].astype(BF16)

        nxt = nx_ref[be_ref[i]]

        @pl.when(nxt < nu_ref[0])
        def _():
            for c in w_copies(be_ref[jnp.minimum(nxt, steps - 1)], 1 - s):
                c.start()

        seq[0] = seq[0] + 1

    @pl.when(live)
    def _():
        for c in in_copies(0, slot):
            c.wait()
        x = _rows_from_tiles(xbuf.at[slot], bm).astype(BF16)
        g = jnp.minimum(_dot(x, wgb[...]) + bg_ref[0], SWIGLU_LIMIT)
        u = jnp.clip(_dot(x, wub[...]) + bu_ref[0], -SWIGLU_LIMIT, SWIGLU_LIMIT)
        hdn = g * jax.nn.sigmoid(SWIGLU_ALPHA * g) * (u + 1.0)
        _rows_to_tiles(ybuf.at[slot], _dot(hdn.astype(BF16), wdb[...]) + bd_ref[0], bm)

    @pl.when(jnp.logical_not(live))
    def _():
        ybuf[slot] = jnp.zeros(ybuf.shape[1:], F32)

    for c in out_copies(i, slot):
        c.start()

    @pl.when(jnp.logical_and(i == steps - 1, i >= 1))
    def _():
        for c in out_copies(0, 1 - slot):
            c.wait()

    @pl.when(i == steps - 1)
    def _():
        for c in out_copies(0, slot):
            c.wait()


def _experts(block_e, n_used, next_first, xs, wts, *, bm):
    p = xs.shape[0]
    grouped = (p // SUBLANES, SUBLANES, CHUNKS, LANES)
    anyspace = pl.BlockSpec(memory_space=pl.ANY)
    bspec = pl.BlockSpec((1, 1, D_MODEL), lambda i, be, nu, nx: (be[i], 0, 0))
    tiles = (2, bm // SUBLANES) + TOKEN_TILE
    ys = pl.pallas_call(
        functools.partial(_expert_kernel, bm=bm),
        grid_spec=pltpu.PrefetchScalarGridSpec(
            num_scalar_prefetch=3, grid=(p // bm,),
            in_specs=[anyspace, anyspace, bspec, anyspace, bspec, anyspace, bspec],
            out_specs=anyspace,
            scratch_shapes=[pltpu.VMEM((D_MODEL, D_MODEL), BF16)] * 3 + [
                pltpu.VMEM((2, 3, D_MODEL, D_MODEL), F32), pltpu.VMEM(tiles, F32), pltpu.VMEM(tiles, F32),
                pltpu.SMEM((1,), I32),
                pltpu.SemaphoreType.DMA((2,)), pltpu.SemaphoreType.DMA((2,)), pltpu.SemaphoreType.DMA((2,))]),
        out_shape=jax.ShapeDtypeStruct(grouped, F32),
        compiler_params=pltpu.CompilerParams(dimension_semantics=("arbitrary",), vmem_limit_bytes=VMEM_LIMIT),
        name="moe_experts",
    )(block_e, n_used, next_first, xs.reshape(grouped), wts["w_gate"], wts["b_gate"], wts["w_up"], wts["b_up"],
      wts["w_down"], wts["b_down"])
    return ys.reshape(p, CHUNKS, LANES)


def _combine_kernel(dest_ref, hp_ref, hs_ref, twp_ref, tws_ref, gf_ref, ys_ref, yp_ref, ysm_ref, buf, sems,
                    *, tm, steps_p):
    i = pl.program_id(0)
    slot = i % 2

    def issue(tile, into):
        base = tile * (tm * TOP_K)

        def start(g, c):
            for u in range(SUBLANES):
                for j in range(TOP_K):
                    d = dest_ref[base + (g * SUBLANES + u) * TOP_K + j]
                    pltpu.make_async_copy(ys_ref.at[d], buf.at[into, j, g, :, u, :],
                                          sems.at[into]).start(priority=j % 2)
            return c

        lax.fori_loop(0, tm // SUBLANES, start, 0)

    @pl.when(i == 0)
    def _():
        issue(0, 0)

    @pl.when(i + 1 < pl.num_programs(0))
    def _():
        issue(i + 1, (i + 1) % 2)

    for j in range(TOP_K):
        pltpu.make_async_copy(ys_ref.at[pl.ds(0, tm)], ys_ref.at[pl.ds(0, tm)], sems.at[slot]).wait()

    def finish(h_ref, tw_ref, y_ref):
        tw = tw_ref[...]
        y = h_ref[...]
        for j in range(TOP_K):
            y = y + _rows_from_tiles(buf.at[slot, j], tm) * tw[:, j:j + 1]
        y_ref[...] = _rms(y, gf_ref[...])

    @pl.when(i < steps_p)
    def _():
        finish(hp_ref, twp_ref, yp_ref)

    @pl.when(i >= steps_p)
    def _():
        finish(hs_ref, tws_ref, ysm_ref)


def _combine(dest_flat, h2_p, h2_s, tw_p, tw_s, g_final, ys, *, tm):
    steps_p, steps_s = h2_p.shape[0] // tm, h2_s.shape[0] // tm
    pblk = lambda w: pl.BlockSpec((tm, w), lambda i, d: (jnp.minimum(i, steps_p - 1), 0))
    sblk = lambda w: pl.BlockSpec((tm, w), lambda i, d: (jnp.maximum(i - steps_p, 0), 0))
    return pl.pallas_call(
        functools.partial(_combine_kernel, tm=tm, steps_p=steps_p),
        grid_spec=pltpu.PrefetchScalarGridSpec(
            num_scalar_prefetch=1, grid=(steps_p + steps_s,),
            in_specs=[pblk(D_MODEL), sblk(D_MODEL), pblk(LANES), sblk(LANES),
                      pl.BlockSpec((1, D_MODEL), lambda i, d: (0, 0)),
                      pl.BlockSpec(memory_space=pl.ANY)],
            out_specs=(pblk(D_MODEL), sblk(D_MODEL)),
            scratch_shapes=[pltpu.VMEM((2, TOP_K, tm // SUBLANES) + TOKEN_TILE, F32),
                            pltpu.SemaphoreType.DMA((2,))]),
        out_shape=(jax.ShapeDtypeStruct(h2_p.shape, F32), jax.ShapeDtypeStruct(h2_s.shape, F32)),
        compiler_params=pltpu.CompilerParams(dimension_semantics=("arbitrary",), vmem_limit_bytes=VMEM_LIMIT),
        name="moe_combine",
    )(dest_flat, h2_p, h2_s, tw_p, tw_s, g_final, ys)


def _aug_layout():
    main, extra = [], []
    for h in range(FOX_HEADS):
        even = h % 2 == 0
        main.append(LANES * h + (0 if even else FOX_HEAD_DIM))
        extra.append(LANES * h + (FOX_HEAD_DIM if even else 0))
    return main, extra


def _prep_mix_weights(g_mix, w_in, b_f, conv_w, conv_b, w_a, b_a, w_i, b_i, lam):
    _, extra = _aug_layout()
    wq = w_in[:, 0:FOX_WIDTH] * (FOX_HEAD_DIM ** -0.5 * LOG2E)
    wf = w_in[:, 3 * FOX_WIDTH:3 * FOX_WIDTH + FOX_HEADS]
    wf_pad = jnp.concatenate([wf, wf, wf, jnp.zeros((D_MODEL, LANES - 3 * FOX_HEADS), F32)], axis=1)
    w_all = jnp.concatenate([wq, w_in[:, FOX_WIDTH:3 * FOX_WIDTH], wf_pad, w_in[:, 3 * FOX_WIDTH + FOX_HEADS:]],
                            axis=1).astype(BF16)
    bf_pad = jnp.concatenate([b_f, b_f, b_f, jnp.zeros((LANES - 3 * FOX_HEADS,), F32)]).reshape(1, LANES)
    eq = np.zeros((LANES, AUG), np.float32)
    ek = np.zeros((LANES, AUG), np.float32)
    cq = np.zeros((1, AUG), np.float32)
    ck = np.zeros((1, AUG), np.float32)
    cv = np.zeros((1, AUG), np.float32)
    hsel = np.zeros((FOX_WIDTH, LANES), np.float32)
    hdiag = np.zeros((FOX_WIDTH, LANES), np.float32)
    hmask = np.zeros((2, LANES), np.float32)
    hmask[0, :FOX_HEAD_DIM] = 1.0
    hmask[1, FOX_HEAD_DIM:] = 1.0
    for h in range(FOX_HEADS):
        hsel[FOX_HEAD_DIM * h:FOX_HEAD_DIM * (h + 1), h] = 1.0
        hdiag[FOX_HEAD_DIM * h:FOX_HEAD_DIM * (h + 1), 3 * FOX_HEADS + h] = 1.0
        cv[0, extra[h]] = 1.0
        eq[3 * FOX_HEADS + h, extra[h] + 6] = -1.0
        ck[0, extra[h] + 6] = 1.0
        for part in range(3):
            eq[part * 8 + h, extra[h] + part] = 1.0
            ek[part * 8 + h, extra[h] + 3 + part] = 1.0
            cq[0, extra[h] + 3 + part] = 1.0
            ck[0, extra[h] + part] = 1.0
    bias_sel = np.zeros((FOX_HEADS, FOX_HEAD_DIM, 4 * FOX_HEADS), np.float32)
    for h in range(FOX_HEADS):
        for part in range(3):
            bias_sel[h, part, 3 * FOX_HEADS] = 1.0
            bias_sel[h, 3 + part, part * 8 + h] = 1.0
        bias_sel[h, 6, 3 * FOX_HEADS] = 1.0
    dense = lambda w: jax.scipy.linalg.block_diag(*[w[i] for i in range(LRU_BLOCKS)]).astype(BF16)
    row = lambda a: a.reshape(1, -1)
    b16 = lambda a: jnp.asarray(a, BF16)
    return dict(g_mix=row(g_mix), w_all=w_all, bf_pad=bf_pad, eq=b16(eq), ek=b16(ek),
                cq=jnp.asarray(cq), ck=jnp.asarray(ck), cv=jnp.asarray(cv),
                bias_sel=b16(bias_sel),
                hsel=b16(hsel), hdiag=b16(hdiag), hmask=b16(hmask), conv_w=conv_w, conv_b=row(conv_b),
                wa=dense(w_a), ba=row(b_a), wi=dense(w_i), bi=row(b_i), lam=row(lam))


def kernel(x_prompt, x_sample, mem_prompt, cache_fox_k, cache_fox_v, cache_fox_logf, state_lru_h, state_conv, cache_mem_k, cache_mem_v, g_mix, w_in, b_f, conv_w, conv_b, w_a, b_a, w_i, b_i, lam, w_out, g_cross, g_mem, w_cq, w_ck, w_cv, w_co, g_moe, w_router, b_router, w_gate, b_gate, w_up, b_up, w_down, b_down, g_final):
    nb_p, seq, _ = x_prompt.shape
    nb_s, t_s, _ = x_sample.shape
    past = cache_fox_k.shape[2]
    n_mem = mem_prompt.shape[1]
    assert nb_p == 1 and g_mix.shape[0] == 1
    n_p, n_s = seq, nb_s * t_s
    row = lambda a: a.reshape(1, -1)

    mw = _prep_mix_weights(g_mix[0], w_in[0], b_f[0], conv_w[0], conv_b[0], w_a[0], b_a[0], w_i[0], b_i[0], lam[0])
    xp = x_prompt.reshape(n_p, D_MODEL)
    xs_ = x_sample.reshape(n_s, D_MODEL)

    zero_prev = jnp.zeros((1, SUBLANES, LRU_WIDTH), F32)
    zero_h = jnp.zeros((1, 1, LRU_WIDTH), F32)
    (qa_p, ka_p, va_p, k_p, v_p, _, lf_p, f2_p, qn_p, kn_p, lru_p, hl_p, ct_p) = _mix_in(
        xp, zero_prev, zero_h, mw, tm=MIX_TILE, streaming=True)
    prev_s = jnp.pad(state_conv[0], ((0, 0), (SUBLANES - (CONV_WIDTH - 1), 0), (0, 0)))
    (qa_s, ka_s, _, k_s, v_s, vb_s, lf_s, _, _, _, lru_s, hl_s, ct_s) = _mix_in(
        xs_, prev_s, state_lru_h[0].reshape(nb_s, 1, LRU_WIDTH), mw, tm=t_s, streaming=False)

    n_past, static_ok = _fox_plan(f2_p, qn_p, kn_p, bq=FOX_BQ, tile=MIX_TILE)
    fo_p = _fox_prompt(n_past, static_ok, qa_p, ka_p, va_p, bq=FOX_BQ)
    by_head = lambda c: jnp.moveaxis(c[0], 1, -1)
    fo_s = _fox_sample(qa_s, ka_s, vb_s, by_head(cache_fox_k), by_head(cache_fox_v), by_head(cache_fox_logf),
                       mw["bias_sel"], t=t_s)

    mk_p, mv_p = _mem_kv(mem_prompt[0], row(g_mem[0]), w_ck[0].astype(BF16), w_cv[0].astype(BF16))

    wr_pad = jnp.pad(w_router[0], ((0, 0), (0, LANES - N_EXPERTS))).astype(BF16)
    br_pad = jnp.pad(b_router[0], (0, LANES - N_EXPERTS)).reshape(1, LANES)
    pw = dict(w_out=w_out[0].astype(BF16), g_cross=row(g_cross[0]), w_cq=w_cq[0].astype(BF16),
              w_co=w_co[0].astype(BF16), g_moe=row(g_moe[0]), w_router=wr_pad, b_router=br_pad)
    cnt0 = jnp.zeros((SUBLANES, LANES), F32)
    h2_p, xn_p, idx_p, tw_p, rk_p, cnt_p = _post(
        xp, fo_p, lru_p, mk_p.reshape(1, n_mem, D_MODEL), mv_p.reshape(1, n_mem, D_MODEL), cnt0, pw,
        tm=POST_TILE, seg=POST_TILE, cached_memory=False)
    h2_s, xn_s, idx_s, tw_s, rk_s, cnt = _post(
        xs_, fo_s, lru_s, cache_mem_k, cache_mem_v, cnt_p, pw, tm=min(SAMPLE_POST_TILE, n_s), seg=t_s,
        cached_memory=True)

    n = n_p + n_s
    idx = jnp.concatenate([idx_p[:, :TOP_K], idx_s[:, :TOP_K]], axis=0)
    rank = jnp.concatenate([rk_p[:, :TOP_K], rk_s[:, :TOP_K]], axis=0)
    counts = cnt[0, :N_EXPERTS].astype(I32)
    padded = (counts + MOE_BM - 1) // MOE_BM * MOE_BM
    pad_end = jnp.cumsum(padded)
    pad_start = pad_end - padded
    dest = (pad_start[idx] + rank).reshape(n * TOP_K)
    n_blocks = -(-(n * TOP_K) // MOE_BM) + N_EXPERTS
    blk_row = jnp.arange(n_blocks, dtype=I32) * MOE_BM
    block_e = jnp.minimum(jnp.sum(pad_end[None, :] <= blk_row[:, None], axis=1), N_EXPERTS - 1).astype(I32)
    n_used = (pad_end[-1] // MOE_BM).reshape(1).astype(I32)
    last_block = jnp.maximum(pad_end // MOE_BM - 1, 0).astype(I32)

    xs_sorted = _dispatch(dest, last_block, n_used, xn_p, xn_s, tm=MOE_BM, n_blocks=n_blocks)
    ew = dict(w_gate=w_gate[0], w_up=w_up[0], w_down=w_down[0],
              b_gate=b_gate[0].reshape(N_EXPERTS, 1, D_MODEL), b_up=b_up[0].reshape(N_EXPERTS, 1, D_MODEL),
              b_down=b_down[0].reshape(N_EXPERTS, 1, D_MODEL))
    ys = _experts(block_e, n_used, (pad_end // MOE_BM).astype(I32), xs_sorted, ew, bm=MOE_BM)
    y_p, y_s = _combine(dest, h2_p, h2_s, tw_p, tw_s, row(g_final), ys, tm=COMBINE_TILE)

    shp_p = (1, nb_p, seq, FOX_HEADS, FOX_HEAD_DIM)
    shp_s = (1, nb_s, t_s, FOX_HEADS, FOX_HEAD_DIM)
    tail = slice(SUBLANES - (CONV_WIDTH - 1), SUBLANES)
    return (y_p.reshape(nb_p, seq, D_MODEL), y_s.reshape(nb_s, t_s, D_MODEL),
            k_p.reshape(shp_p), v_p.reshape(shp_p), lf_p.reshape(1, nb_p, seq, FOX_HEADS),
            hl_p.reshape(1, nb_p, LRU_WIDTH), ct_p[:, tail, :].reshape(1, nb_p, CONV_WIDTH - 1, LRU_WIDTH),
            mk_p.reshape(1, nb_p, n_mem, MEM_HEADS, MEM_HEAD_DIM), mv_p.reshape(1, nb_p, n_mem, MEM_HEADS, MEM_HEAD_DIM),
            k_s.reshape(shp_s), v_s.reshape(shp_s), lf_s.reshape(1, nb_s, t_s, FOX_HEADS),
            hl_s.reshape(1, nb_s, LRU_WIDTH), ct_s[:, tail, :].reshape(1, nb_s, CONV_WIDTH - 1, LRU_WIDTH))
```

```python
import functools
import math

import jax
import jax.numpy as jnp
import numpy as np
from jax import lax
from jax.experimental import pallas as pl
from jax.experimental.pallas import tpu as pltpu

F32 = jnp.float32
BF16 = jnp.bfloat16
I32 = jnp.int32

D_MODEL = 1024
FOX_HEADS = 8
FOX_HEAD_DIM = 64
FOX_WIDTH = FOX_HEADS * FOX_HEAD_DIM
LRU_WIDTH = D_MODEL - FOX_WIDTH
LRU_BLOCKS = 8
LRU_C = 8.0
CONV_WIDTH = 4
MEM_HEADS = 4
MEM_HEAD_DIM = D_MODEL // MEM_HEADS
N_EXPERTS = 32
TOP_K = 4
SWIGLU_LIMIT = 7.0
SWIGLU_ALPHA = 1.702
RMS_EPS = 1e-6
NEG_INF = -1e30
LOG2E = math.log2(math.e)

LANES = 128
SUBLANES = 8
CHUNKS = D_MODEL // LANES
TOKEN_TILE = (CHUNKS, SUBLANES, LANES)
AUG = LANES * FOX_HEADS
VMEM_LIMIT = 56 * 1024 * 1024

MIX_TILE = 256
POST_TILE = 512
SAMPLE_POST_TILE = 256
FOX_BQ = 512
MOE_BM = 512
COMBINE_TILE = 256
FOX_SKIP_GAP = 160.0
FOX_STATIC_SHIFT_RANGE = 100.0

C_Q, C_K, C_V, C_F, C_XR, C_G, C_END = 0, 512, 1024, 1536, 1664, 2176, 2688


def _dot(a, b):
    return jnp.dot(a, b, preferred_element_type=F32)


def _dot_nt(a, b):
    return lax.dot_general(a, b, (((1,), (1,)), ((), ())), preferred_element_type=F32)


def _split3(x):
    hi = x.astype(BF16)
    r = x - hi.astype(F32)
    mid = r.astype(BF16)
    lo = (r - mid.astype(F32)).astype(BF16)
    return hi, mid, lo


def _dot3(a, x):
    hi, mid, lo = _split3(x)
    return _dot(a, hi) + _dot(a, mid) + _dot(a, lo)


def _softplus(x):
    return jnp.maximum(x, 0.0) + jnp.log1p(jnp.exp(-jnp.abs(x)))


def _rms(x, g):
    return x * lax.rsqrt(jnp.mean(x * x, axis=-1, keepdims=True) + RMS_EPS) * g


def _tri(n, kind):
    r = lax.broadcasted_iota(I32, (n, n), 0)
    c = lax.broadcasted_iota(I32, (n, n), 1)
    m = {"le": c <= r, "lt": c < r, "gt": c > r}[kind]
    return jnp.where(m, 1.0, 0.0).astype(BF16)


def _mix_in_kernel(x_ref, g_ref, w_ref, bfp_ref, eq_ref, ek_ref, cq_ref, ck_ref, cv_ref, hs_ref, hd_ref, hm_ref,
                   cw_ref, cb_ref, wa_ref, ba_ref, wi_ref, bi_ref, lam_ref, cprev_ref, h0_ref,
                   qa_ref, ka_ref, va_ref, ko_ref, vo_ref, vb_ref, lf_ref, f2_ref, qn_ref, kn_ref,
                   lru_ref, hl_ref, ct_ref,
                   fcar, hcar, xp_ref, sa_ref, sb_ref, kbuf, vbuf, ksem, *, tm, pad, streaming):
    step = pl.program_id(0)
    first = step == 0
    slot = step % 2

    def kv_copies(s, row0):
        rows = pl.ds(row0, tm)
        copies = []
        for h in range(FOX_HEADS):
            copies.append(pltpu.make_async_copy(kbuf.at[s, h], ko_ref.at[rows, h, :], ksem.at[s]))
            copies.append(pltpu.make_async_copy(vbuf.at[s, h], vo_ref.at[rows, h, :], ksem.at[s]))
        return copies

    @pl.when(step >= 2)
    def _():
        for c in kv_copies(slot, 0):
            c.wait()

    if streaming:
        @pl.when(first)
        def _():
            fcar[...] = jnp.zeros_like(fcar)
            hcar[...] = jnp.zeros_like(hcar)
            xp_ref[0:SUBLANES, :] = jnp.zeros((SUBLANES, LRU_WIDTH), F32)
    else:
        fcar[...] = jnp.zeros_like(fcar)
        hcar[...] = jnp.broadcast_to(h0_ref[0], hcar.shape)
        xp_ref[0:SUBLANES, :] = cprev_ref[0]

    x = x_ref[...]
    xn = _rms(x, g_ref[...]).astype(BF16)
    z = _dot(xn, w_ref[...])

    zq, zk, zv = z[:, C_Q:C_K], z[:, C_K:C_V], z[:, C_V:C_F]
    for h in range(FOX_HEADS):
        hc = slice(h * FOX_HEAD_DIM, (h + 1) * FOX_HEAD_DIM)
        kbuf[slot, h] = zk[:, hc]
        vbuf[slot, h] = zv[:, hc]
    for c in kv_copies(slot, pl.multiple_of(step * tm, tm)):
        c.start()
    qd, kd, vd = zq.astype(BF16), zk.astype(BF16), zv.astype(BF16)
    vb_ref[...] = vd

    def head_pad(d):
        blocks = []
        for p in range(FOX_HEADS // 2):
            blk = d[:, p * LANES:(p + 1) * LANES]
            blocks += [blk * hm_ref[0:1, :], blk * hm_ref[1:2, :]]
        return jnp.concatenate(blocks, axis=1)

    lane = lax.broadcasted_iota(I32, (tm, LANES), 1)
    lf = jnp.where(lane < 3 * FOX_HEADS, -_softplus(-(z[:, C_F:C_XR] + bfp_ref[...])), 0.0)
    lf_ref[...] = lf[:, :FOX_HEADS]
    cum = _dot3(_tri(tm, "le"), lf) + fcar[0:1, :]
    fcar[...] = jnp.broadcast_to(cum[tm - 1:tm, :], fcar.shape)
    cum2 = cum * LOG2E
    f2_ref[...] = cum2[:, :FOX_HEADS]
    qf, kf = qd.astype(F32), kd.astype(F32)
    diag = _dot((qf * kf).astype(BF16), hd_ref[...]).astype(BF16)
    hi, mid, lo = _split3(cum2)
    bias = jnp.where(lane < 8, hi, jnp.where(lane < 16, mid, jnp.where(lane < 24, lo, diag)))
    qa_ref[...] = head_pad(qd) + (_dot(bias, eq_ref[...]) + cq_ref[...]).astype(BF16)
    ka_ref[...] = head_pad(kd) + (ck_ref[...] - _dot(bias, ek_ref[...])).astype(BF16)
    va_ref[...] = head_pad(vd) + cv_ref[...].astype(BF16)
    qn_ref[...] = jnp.broadcast_to(jnp.max(_dot((qf * qf).astype(BF16), hs_ref[...]), axis=0, keepdims=True),
                                   qn_ref.shape)
    kn_ref[...] = jnp.broadcast_to(jnp.max(_dot((kf * kf).astype(BF16), hs_ref[...]), axis=0, keepdims=True),
                                   kn_ref.shape)

    xr = z[:, C_XR:C_G]
    xp_ref[SUBLANES:SUBLANES + tm, :] = xr
    xc = cb_ref[...] + xr * cw_ref[CONV_WIDTH - 1:CONV_WIDTH, :]
    for j in range(CONV_WIDTH - 1):
        sh = CONV_WIDTH - 1 - j
        xc = xc + xp_ref[SUBLANES - sh:SUBLANES - sh + tm, :] * cw_ref[j:j + 1, :]
    tail = xp_ref[tm:tm + SUBLANES, :]
    ct_ref[0] = tail
    xp_ref[0:SUBLANES, :] = tail

    xcb = xc.astype(BF16)
    r = jax.nn.sigmoid(_dot(xcb, wa_ref[...]) + ba_ref[...])
    ig = jax.nn.sigmoid(_dot(xcb, wi_ref[...]) + bi_ref[...])
    log_a = (-LRU_C) * r * _softplus(-lam_ref[...])
    a = jnp.exp(log_a)
    mult = jnp.sqrt(-jnp.tanh(log_a) * (a * a + 1.0))
    if streaming:
        row = lax.broadcasted_iota(I32, (tm, LRU_WIDTH), 0)
        mult = jnp.where(jnp.logical_and(row == 0, first), 1.0, mult)
    b = mult * ig * xc

    sa_ref[0:pad, :] = jnp.ones((pad, LRU_WIDTH), F32)
    sb_ref[0:pad, :] = jnp.zeros((pad, LRU_WIDTH), F32)
    d = 1
    while d < tm:
        sa_ref[pad:pad + tm, :] = a
        sb_ref[pad:pad + tm, :] = b
        b = a * sb_ref[pad - d:pad - d + tm, :] + b
        a = a * sa_ref[pad - d:pad - d + tm, :]
        d *= 2
    h = a * hcar[0:1, :] + b
    hlast = h[tm - 1:tm, :]
    hcar[...] = jnp.broadcast_to(hlast, hcar.shape)
    hl_ref[0] = hlast
    lru_ref[...] = (h * jax.nn.gelu(z[:, C_G:C_END])).astype(BF16)

    last = pl.num_programs(0) - 1

    @pl.when(jnp.logical_and(step == last, step >= 1))
    def _():
        for c in kv_copies(1 - slot, 0):
            c.wait()

    @pl.when(step == last)
    def _():
        for c in kv_copies(slot, 0):
            c.wait()


def _mix_in(x, cprev, h0, wts, *, tm, streaming):
    n = x.shape[0]
    steps = n // tm
    nseg = 1 if streaming else steps
    pad = max(tm // 2, SUBLANES)
    seg = (lambda i: (0, 0, 0)) if streaming else (lambda i: (i, 0, 0))
    full = lambda a: pl.BlockSpec(a.shape, lambda i: (0,) * a.ndim)
    rows = lambda w: pl.BlockSpec((tm, w), lambda i: (i, 0))
    names = ("g_mix", "w_all", "bf_pad", "eq", "ek", "cq", "ck", "cv", "hsel", "hdiag", "hmask", "conv_w", "conv_b",
             "wa", "ba", "wi", "bi", "lam")
    ws = [wts[k] for k in names]
    sds = jax.ShapeDtypeStruct
    out_shape = (
        sds((n, AUG), BF16), sds((n, AUG), BF16), sds((n, AUG), BF16),
        sds((n, FOX_HEADS, FOX_HEAD_DIM), F32), sds((n, FOX_HEADS, FOX_HEAD_DIM), F32), sds((n, FOX_WIDTH), BF16),
        sds((n, FOX_HEADS), F32), sds((n, FOX_HEADS), F32),
        sds((steps * SUBLANES, LANES), F32), sds((steps * SUBLANES, LANES), F32),
        sds((n, LRU_WIDTH), BF16),
        sds((nseg, 1, LRU_WIDTH), F32), sds((nseg, SUBLANES, LRU_WIDTH), F32),
    )
    out_specs = (
        rows(AUG), rows(AUG), rows(AUG), pl.BlockSpec(memory_space=pl.ANY), pl.BlockSpec(memory_space=pl.ANY),
        rows(FOX_WIDTH), rows(FOX_HEADS), rows(FOX_HEADS),
        pl.BlockSpec((SUBLANES, LANES), lambda i: (i, 0)), pl.BlockSpec((SUBLANES, LANES), lambda i: (i, 0)),
        rows(LRU_WIDTH),
        pl.BlockSpec((1, 1, LRU_WIDTH), seg), pl.BlockSpec((1, SUBLANES, LRU_WIDTH), seg),
    )
    in_specs = [rows(D_MODEL)] + [full(w) for w in ws] + [
        pl.BlockSpec((1, SUBLANES, LRU_WIDTH), seg), pl.BlockSpec((1, 1, LRU_WIDTH), seg)]
    return pl.pallas_call(
        functools.partial(_mix_in_kernel, tm=tm, pad=pad, streaming=streaming),
        grid=(steps,), in_specs=in_specs, out_specs=out_specs, out_shape=out_shape,
        scratch_shapes=[
            pltpu.VMEM((SUBLANES, LANES), F32), pltpu.VMEM((SUBLANES, LRU_WIDTH), F32),
            pltpu.VMEM((tm + SUBLANES, LRU_WIDTH), F32),
            pltpu.VMEM((pad + tm, LRU_WIDTH), F32), pltpu.VMEM((pad + tm, LRU_WIDTH), F32),
            pltpu.VMEM((2, FOX_HEADS, tm, FOX_HEAD_DIM), F32), pltpu.VMEM((2, FOX_HEADS, tm, FOX_HEAD_DIM), F32),
            pltpu.SemaphoreType.DMA((2,))],
        compiler_params=pltpu.CompilerParams(dimension_semantics=("arbitrary",), vmem_limit_bytes=VMEM_LIMIT),
        name="mix_in_stream" if streaming else "mix_in_segments",
    )(x, *ws, cprev, h0)


def _fox_prompt_kernel(nb_ref, st_ref, q_ref, k_ref, v_ref, o_ref, m_ref, acc_ref, *, bq):
    qi = pl.program_id(1)
    plan = pl.program_id(0) * pl.num_programs(1) + qi
    n_past = nb_ref[plan]
    cols = [slice(LANES * c, LANES * (c + 1)) for c in range(2)]
    qs = [q_ref[:, cols[c]] for c in range(2)]
    acc_ref[...] = jnp.zeros(acc_ref.shape, F32)

    def scores(c, start, causal):
        s = _dot_nt(qs[c], k_ref[pl.ds(start, bq), cols[c]])
        if causal:
            r = lax.broadcasted_iota(I32, (bq, bq), 0)
            cc = lax.broadcasted_iota(I32, (bq, bq), 1)
            s = jnp.where(cc <= r, s, NEG_INF)
        return s

    def static_block(start, causal):
        for c in range(2):
            p = jnp.exp2(scores(c, start, causal)).astype(BF16)
            acc_ref[c] += _dot(p, v_ref[pl.ds(start, bq), cols[c]])

    def online_block(start, causal):
        for c in range(2):
            s = scores(c, start, causal)
            m_old = m_ref[c]
            m_new = jnp.maximum(m_old, jnp.max(s, axis=1, keepdims=True))
            p = jnp.exp2(s - m_new).astype(BF16)
            acc_ref[c] = jnp.exp2(m_old - m_new) * acc_ref[c] + _dot(p, v_ref[pl.ds(start, bq), cols[c]])
            m_ref[c] = m_new

    def run(block):
        block(pl.multiple_of(qi * bq, bq), True)

        def body(t, carry):
            block(pl.multiple_of((qi - 1 - t) * bq, bq), False)
            return carry

        lax.fori_loop(0, n_past, body, 0)

    @pl.when(st_ref[plan] == 1)
    def _():
        run(static_block)

    @pl.when(st_ref[plan] != 1)
    def _():
        m_ref[...] = jnp.full(m_ref.shape, NEG_INF, F32)
        run(online_block)

    a0, a1 = acc_ref[0], acc_ref[1]
    o0 = a0 / a0[:, FOX_HEAD_DIM:FOX_HEAD_DIM + 1]
    o1 = a1 / a1[:, 0:1]
    lane = lax.broadcasted_iota(I32, (bq, LANES), 1)
    o_ref[...] = jnp.where(lane < FOX_HEAD_DIM, o0, o1).astype(BF16)


def _fox_prompt(n_past, static_ok, qa, ka, va, *, bq):
    s = qa.shape[0]
    pairs = FOX_HEADS // 2
    return pl.pallas_call(
        functools.partial(_fox_prompt_kernel, bq=bq),
        grid_spec=pltpu.PrefetchScalarGridSpec(
            num_scalar_prefetch=2, grid=(pairs, s // bq),
            in_specs=[pl.BlockSpec((bq, 2 * LANES), lambda p, i, nb, st: (i, p)),
                      pl.BlockSpec((s, 2 * LANES), lambda p, i, nb, st: (0, p)),
                      pl.BlockSpec((s, 2 * LANES), lambda p, i, nb, st: (0, p))],
            out_specs=pl.BlockSpec((bq, LANES), lambda p, i, nb, st: (i, p)),
            scratch_shapes=[pltpu.VMEM((2, bq, 1), F32), pltpu.VMEM((2, bq, LANES), F32)]),
        out_shape=jax.ShapeDtypeStruct((s, FOX_WIDTH), BF16),
        compiler_params=pltpu.CompilerParams(dimension_semantics=("arbitrary", "arbitrary"),
                                             vmem_limit_bytes=VMEM_LIMIT),
        name="fox_prompt",
    )(n_past, static_ok, qa, ka, va)


def _fox_plan(f2, qn, kn, *, bq, tile):
    s = f2.shape[0]
    nq = s // bq
    per = lambda a: a.reshape(s // tile, SUBLANES, LANES)[:, 0, :FOX_HEADS]
    qnorm = jnp.sqrt(jnp.max(per(qn).reshape(nq, bq // tile, FOX_HEADS), axis=1))
    knorm = jnp.sqrt(jnp.max(per(kn), axis=0))
    spread = 2.0 * 1.02 * qnorm * knorm[None, :]
    thr = spread + FOX_SKIP_GAP
    f_first = f2[0::bq]
    f_last = f2[bq - 1::bq]
    gap = f_last[None, :, :] - f_first[:, None, :]
    before = (jnp.arange(nq)[None, :] < jnp.arange(nq)[:, None])[:, :, None]
    need = jnp.sum(jnp.logical_and(before, gap < thr[:, None, :]), axis=1)
    pair = lambda a: a.reshape(nq, FOX_HEADS // 2, 2)
    n_past = jnp.max(pair(need), axis=2).T.reshape(-1).astype(I32)
    static_ok = jnp.all(pair(spread) <= FOX_STATIC_SHIFT_RANGE, axis=2).T.reshape(-1).astype(I32)
    return n_past, static_ok


def _fox_sample_kernel(q_ref, kn_ref, vn_ref, kt_ref, vt_ref, lf_ref, sel_ref, o_ref, g_ref,
                       *, t, past, chunk):
    car = jnp.zeros((FOX_HEADS, 1), F32)
    later = _tri(chunk, "lt")
    for ci in reversed(range(past // chunk)):
        lf = lf_ref[0, :, ci * chunk:(ci + 1) * chunk]
        hi, mid, lo = _split3(lf)
        g_ref[:, ci * chunk:(ci + 1) * chunk] = _dot(hi, later) + _dot(mid, later) + _dot(lo, later) + car
        car = car + jnp.sum(lf, axis=1, keepdims=True)
    hi, mid, lo = _split3(g_ref[...] * LOG2E)
    bias_rows = jnp.concatenate([hi.astype(F32), mid.astype(F32), lo.astype(F32),
                                 jnp.ones((FOX_HEADS, past), F32)], axis=0).astype(BF16)

    def swap_halves(x):
        return pltpu.roll(x.astype(F32), FOX_HEAD_DIM, axis=1).astype(BF16)

    r = lax.broadcasted_iota(I32, (t, t), 0)
    cc = lax.broadcasted_iota(I32, (t, t), 1)
    prow = lax.broadcasted_iota(I32, (FOX_HEAD_DIM, LANES), 0)
    plane = lax.broadcasted_iota(I32, (FOX_HEAD_DIM, LANES), 1)
    place = [jnp.where(plane == prow + FOX_HEAD_DIM * c, 1.0, 0.0).astype(BF16) for c in range(2)]
    for p in range(FOX_HEADS // 2):
        pc = slice(LANES * p, LANES * (p + 1))
        vpair = vn_ref[:, pc]
        o_pair = jnp.zeros((t, LANES), F32)
        for c in range(2):
            h = 2 * p + c
            hc = slice(LANES * h, LANES * (h + 1))
            q, kn, vn = q_ref[:, hc], kn_ref[:, hc], vpair
            if c == 1:
                q, kn, vn = swap_halves(q), swap_halves(kn), swap_halves(vn)
            bias = _dot(sel_ref[h], bias_rows).astype(BF16)
            kpast_t = jnp.concatenate([kt_ref[0, h].astype(BF16), bias], axis=0)
            sp = _dot(q, kpast_t)
            sn = jnp.where(cc <= r, _dot_nt(q, kn), NEG_INF)
            m = jnp.maximum(jnp.max(sp, axis=1, keepdims=True), jnp.max(sn, axis=1, keepdims=True))
            pp = jnp.exp2(sp - m)
            pn = jnp.exp2(sn - m)
            l = jnp.sum(pp, axis=1, keepdims=True) + jnp.sum(pn, axis=1, keepdims=True)
            o = (_dot_nt(pp.astype(BF16), vt_ref[0, h].astype(BF16))
                 + _dot(pn.astype(BF16), vn[:, :FOX_HEAD_DIM])) / l
            o_pair = o_pair + _dot(o.astype(BF16), place[c])
        o_ref[:, pc] = o_pair.astype(BF16)


def _fox_sample(qa, ka, vb, cache_kt, cache_vt, cache_lft, sel, *, t):
    nb, past = cache_kt.shape[0], cache_kt.shape[3]
    full = lambda a: pl.BlockSpec(a.shape, lambda b: (0,) * a.ndim)
    slab = pl.BlockSpec((1, FOX_HEADS, FOX_HEAD_DIM, past), lambda b: (b, 0, 0, 0))
    return pl.pallas_call(
        functools.partial(_fox_sample_kernel, t=t, past=past, chunk=256),
        grid=(nb,),
        in_specs=[pl.BlockSpec((t, AUG), lambda b: (b, 0)), pl.BlockSpec((t, AUG), lambda b: (b, 0)),
                  pl.BlockSpec((t, FOX_WIDTH), lambda b: (b, 0)), slab, slab,
                  pl.BlockSpec((1, FOX_HEADS, past), lambda b: (b, 0, 0)), full(sel)],
        out_specs=pl.BlockSpec((t, FOX_WIDTH), lambda b: (b, 0)),
        out_shape=jax.ShapeDtypeStruct((nb * t, FOX_WIDTH), BF16),
        scratch_shapes=[pltpu.VMEM((FOX_HEADS, past), F32)],
        compiler_params=pltpu.CompilerParams(dimension_semantics=("arbitrary",), vmem_limit_bytes=VMEM_LIMIT),
        name="fox_sample",
    )(qa, ka, vb, cache_kt, cache_vt, cache_lft, sel)


def _mem_kv_kernel(m_ref, g_ref, wk_ref, wv_ref, k_ref, v_ref):
    mn = _rms(m_ref[...], g_ref[...]).astype(BF16)
    k_ref[...] = _dot(mn, wk_ref[...])
    v_ref[...] = _dot(mn, wv_ref[...])


def _mem_kv(mem, g, wk, wv):
    n = mem.shape[0]
    return pl.pallas_call(
        _mem_kv_kernel,
        out_shape=(jax.ShapeDtypeStruct((n, D_MODEL), F32), jax.ShapeDtypeStruct((n, D_MODEL), F32)),
        compiler_params=pltpu.CompilerParams(vmem_limit_bytes=VMEM_LIMIT),
        name="mem_kv",
    )(mem, g, wk, wv)


def _post_kernel(h_ref, fo_ref, lru_ref, wo_ref, gc_ref, wcq_ref, mk_ref, mv_ref, wco_ref,
                 gm_ref, wr_ref, br_ref, cnt_in_ref,
                 h2_ref, xn_ref, idx_ref, tw_ref, rank_ref, cnt_ref, car_ref, *mem_scratch, tm, seg, cached_memory):
    step = pl.program_id(0)
    nseg = tm // seg

    @pl.when(step == 0)
    def _():
        car_ref[...] = cnt_in_ref[...]

    if cached_memory:
        mbuf, msem = mem_scratch
        slot = step % 2

        def mem_copies(at_step, s):
            return [pltpu.make_async_copy(ref.at[0, at_step * nseg + j, :, hd, :], mbuf.at[s, j, t, hd], msem.at[s])
                    for j in range(nseg) for t, ref in enumerate((mk_ref, mv_ref)) for hd in range(MEM_HEADS)]

        @pl.when(step == 0)
        def _():
            for c in mem_copies(0, 0):
                c.start()

        @pl.when(step + 1 < pl.num_programs(0))
        def _():
            for c in mem_copies(step + 1, 1 - slot):
                c.start()

    h1 = h_ref[...] + _dot(fo_ref[...], wo_ref[0:FOX_WIDTH, :]) + _dot(lru_ref[...], wo_ref[FOX_WIDTH:D_MODEL, :])

    q = _dot(_rms(h1, gc_ref[...]).astype(BF16), wcq_ref[...])
    if cached_memory:
        for c in mem_copies(0, slot):
            c.wait()
    segs = []
    for j in range(nseg):
        rows = slice(j * seg, (j + 1) * seg)
        heads = []
        for hd in range(MEM_HEADS):
            hc = slice(MEM_HEAD_DIM * hd, MEM_HEAD_DIM * (hd + 1))
            mk = mbuf[slot, j, 0, hd] if cached_memory else mk_ref[0, :, hc]
            mv = mbuf[slot, j, 1, hd] if cached_memory else mv_ref[0, :, hc]
            s = _dot_nt(q[rows, hc].astype(BF16), mk.astype(BF16)) * (MEM_HEAD_DIM ** -0.5)
            p = jnp.exp(s - jnp.max(s, axis=1, keepdims=True))
            o = _dot(p.astype(BF16), mv.astype(BF16)) / jnp.sum(p, axis=1, keepdims=True)
            heads.append(o.astype(BF16))
        segs.append(jnp.concatenate(heads, axis=1))
    h2 = h1 + _dot(segs[0] if nseg == 1 else jnp.concatenate(segs, axis=0), wco_ref[...])
    h2_ref[...] = h2

    xn = _rms(h2, gm_ref[...])
    _rows_to_tiles(xn_ref, xn, tm)

    lane = lax.broadcasted_iota(I32, (tm, LANES), 1).astype(F32)
    logits = jnp.where(lane < N_EXPERTS, _dot(xn.astype(BF16), wr_ref[...]) + br_ref[...], -jnp.inf)
    vals, idxs = [], []
    for _ in range(TOP_K):
        mx = jnp.max(logits, axis=1, keepdims=True)
        ix = jnp.min(jnp.where(logits == mx, lane, float(LANES)), axis=1, keepdims=True)
        vals.append(mx)
        idxs.append(ix)
        logits = jnp.where(lane == ix, -jnp.inf, logits)
    es = [jnp.exp(v - vals[0]) for v in vals]
    den = es[0] + es[1] + es[2] + es[3]

    onehot = jnp.zeros((tm, LANES), F32)
    for ix in idxs:
        onehot = onehot + jnp.where(lane == ix, 1.0, 0.0)
    before = _dot(_tri(tm, "lt"), onehot.astype(BF16)) + car_ref[0:1, :]
    car = car_ref[0:1, :] + jnp.sum(onehot, axis=0, keepdims=True)
    car_ref[...] = jnp.broadcast_to(car, car_ref.shape)
    cnt_ref[...] = jnp.broadcast_to(car, cnt_ref.shape)

    idx_o = jnp.zeros((tm, LANES), F32)
    tw_o = jnp.zeros((tm, LANES), F32)
    rk_o = jnp.zeros((tm, LANES), F32)
    for j in range(TOP_K):
        rk = jnp.sum(jnp.where(lane == idxs[j], before, 0.0), axis=1, keepdims=True)
        idx_o = jnp.where(lane == j, idxs[j], idx_o)
        tw_o = jnp.where(lane == j, es[j] / den, tw_o)
        rk_o = jnp.where(lane == j, rk, rk_o)
    idx_ref[...] = idx_o.astype(I32)
    tw_ref[...] = tw_o
    rank_ref[...] = rk_o.astype(I32)


def _post(h, fo, lru, mk, mv, cnt_in, wts, *, tm, seg, cached_memory):
    n = h.shape[0]
    full = lambda a: pl.BlockSpec(a.shape, lambda i: (0,) * a.ndim)
    rows = lambda w: pl.BlockSpec((tm, w), lambda i: (i, 0))
    if cached_memory:
        nm = mk.shape[2]
        mem = pl.BlockSpec(memory_space=pl.ANY)
        mem_scratch = [pltpu.VMEM((2, tm // seg, 2, MEM_HEADS, nm, MEM_HEAD_DIM), F32),
                       pltpu.SemaphoreType.DMA((2,))]
    else:
        mem = full(mk)
        mem_scratch = []
    w = [wts[k] for k in ("w_out", "g_cross", "w_cq")]
    w2 = [wts[k] for k in ("w_co", "g_moe", "w_router", "b_router")]
    return pl.pallas_call(
        functools.partial(_post_kernel, tm=tm, seg=seg, cached_memory=cached_memory),
        grid=(n // tm,),
        in_specs=[rows(D_MODEL), rows(FOX_WIDTH), rows(LRU_WIDTH)] + [full(a) for a in w] + [mem, mem]
                 + [full(a) for a in w2] + [full(cnt_in)],
        out_specs=(rows(D_MODEL), pl.BlockSpec((tm // SUBLANES,) + TOKEN_TILE, lambda i: (i, 0, 0, 0)),
                   rows(LANES), rows(LANES), rows(LANES),
                   pl.BlockSpec((SUBLANES, LANES), lambda i: (0, 0))),
        out_shape=(jax.ShapeDtypeStruct((n, D_MODEL), F32), jax.ShapeDtypeStruct((n // SUBLANES,) + TOKEN_TILE, F32),
                   jax.ShapeDtypeStruct((n, LANES), I32), jax.ShapeDtypeStruct((n, LANES), F32),
                   jax.ShapeDtypeStruct((n, LANES), I32), jax.ShapeDtypeStruct((SUBLANES, LANES), F32)),
        scratch_shapes=[pltpu.VMEM((SUBLANES, LANES), F32)] + mem_scratch,
        compiler_params=pltpu.CompilerParams(dimension_semantics=("arbitrary",), vmem_limit_bytes=VMEM_LIMIT),
        name="post_batch_mem" if cached_memory else "post_shared_mem",
    )(h, fo, lru, *w, mk, mv, *w2, cnt_in)


STAGES = 3


def _dispatch_kernel(dest_ref, last_ref, nu_ref, xp_ref, xs_ref, out_ref, zero_ref, stage, sems, gsem, zsem,
                     *, tm, steps_p, n_blocks):
    i = pl.program_id(0)

    def zero_copy(b):
        return pltpu.make_async_copy(zero_ref, out_ref.at[pl.ds(pl.multiple_of(b * tm, tm), tm)], zsem)

    @pl.when(i == 0)
    def _():
        zero_ref[...] = jnp.zeros(zero_ref.shape, F32)
        n_tail = n_blocks - nu_ref[0]

        def start_e(e, c):
            zero_copy(last_ref[e]).start()
            return c

        def start_t(b, c):
            zero_copy(nu_ref[0] + b).start()
            return c

        def wait_one(b, c):
            zero_copy(0).wait()
            return c

        lax.fori_loop(0, N_EXPERTS, start_e, 0)
        lax.fori_loop(0, n_tail, start_t, 0)
        lax.fori_loop(0, N_EXPERTS + n_tail, wait_one, 0)

    groups = tm // SUBLANES
    steps = pl.num_programs(0)

    def stage_copies(src_ref, tile, slot):
        return [pltpu.make_async_copy(src_ref.at[pl.ds(tile * groups, groups), :, u, :], stage.at[slot, :, u],
                                      gsem.at[slot]) for u in range(SUBLANES)]

    def start_stage(tile, slot):
        @pl.when(tile < steps_p)
        def _():
            for c in stage_copies(xp_ref, tile, slot):
                c.start()

        @pl.when(tile >= steps_p)
        def _():
            for c in stage_copies(xs_ref, tile - steps_p, slot):
                c.start()

    @pl.when(i == 0)
    def _():
        start_stage(0, 0)

    @pl.when(i + 1 < steps)
    def _():
        start_stage(i + 1, (i + 1) % STAGES)

    slot = i % STAGES
    for c in stage_copies(xp_ref, 0, slot):
        c.wait()

    base = i * (tm * TOP_K)

    def start(g, c):
        for u in range(SUBLANES):
            for j in range(TOP_K):
                d = dest_ref[base + (g * SUBLANES + u) * TOP_K + j]
                pltpu.make_async_copy(stage.at[slot, g, u], out_ref.at[d],
                                      sems.at[i % 2]).start(priority=j % 2)
        return c

    lax.fori_loop(0, groups, start, 0)

    def drain(parity):
        for _ in range(TOP_K):
            pltpu.make_async_copy(out_ref.at[pl.ds(0, tm)], out_ref.at[pl.ds(0, tm)], sems.at[parity]).wait()

    @pl.when(i > 0)
    def _():
        drain((i - 1) % 2)

    @pl.when(i == steps - 1)
    def _():
        drain(i % 2)


def _dispatch(dest_flat, last_block, n_used, xn_p, xn_s, *, tm, n_blocks):
    steps_p, steps_s = xn_p.shape[0] * SUBLANES // tm, xn_s.shape[0] * SUBLANES // tm
    return pl.pallas_call(
        functools.partial(_dispatch_kernel, tm=tm, steps_p=steps_p, n_blocks=n_blocks),
        grid_spec=pltpu.PrefetchScalarGridSpec(
            num_scalar_prefetch=3, grid=(steps_p + steps_s,),
            in_specs=[pl.BlockSpec(memory_space=pl.ANY), pl.BlockSpec(memory_space=pl.ANY)],
            out_specs=pl.BlockSpec(memory_space=pl.ANY),
            scratch_shapes=[pltpu.VMEM((tm, CHUNKS, LANES), F32),
                            pltpu.VMEM((STAGES, tm // SUBLANES, SUBLANES, CHUNKS, LANES), F32),
                            pltpu.SemaphoreType.DMA((2,)), pltpu.SemaphoreType.DMA((STAGES,)),
                            pltpu.SemaphoreType.DMA(())]),
        out_shape=jax.ShapeDtypeStruct((n_blocks * tm, CHUNKS, LANES), F32),
        compiler_params=pltpu.CompilerParams(dimension_semantics=("arbitrary",), vmem_limit_bytes=VMEM_LIMIT),
        name="moe_dispatch",
    )(dest_flat, last_block, n_used, xn_p, xn_s)


def _rows_from_tiles(view, rows):
    return jnp.concatenate([view[:, c].reshape(rows, LANES) for c in range(CHUNKS)], axis=1)


def _rows_to_tiles(view, val, rows):
    for c in range(CHUNKS):
        view[:, c] = val[:, c * LANES:(c + 1) * LANES].reshape(rows // SUBLANES, SUBLANES, LANES)


def _expert_kernel(be_ref, nu_ref, nx_ref, xs_ref, wg_ref, bg_ref, wu_ref, bu_ref, wd_ref, bd_ref, ys_ref,
                   wgb, wub, wdb, wbuf, xbuf, ybuf, seq, wsem, xsem, ysem, *, bm):
    i = pl.program_id(0)
    steps = pl.num_programs(0)
    groups = bm // SUBLANES
    slot = i % 2
    prev = be_ref[jnp.maximum(i - 1, 0)]
    fresh = jnp.logical_or(i == 0, be_ref[i] != prev)
    live = i < nu_ref[0]

    def w_copies(e, s):
        return [pltpu.make_async_copy(w_ref.at[e], wbuf.at[s, k], wsem.at[s])
                for k, w_ref in enumerate((wg_ref, wu_ref, wd_ref))]

    @pl.when(i == 0)
    def _():
        seq[0] = 0
        for c in w_copies(be_ref[0], 0):
            c.start()

    def in_copies(blk, s):
        g0 = pl.multiple_of(blk * groups, groups)
        return [pltpu.make_async_copy(xs_ref.at[pl.ds(g0, groups), u], xbuf.at[s, :, :, u, :], xsem.at[s])
                for u in range(SUBLANES)]

    def out_copies(blk, s):
        g0 = pl.multiple_of(blk * groups, groups)
        return [pltpu.make_async_copy(ybuf.at[s, :, :, u, :], ys_ref.at[pl.ds(g0, groups), u], ysem.at[s])
                for u in range(SUBLANES)]

    @pl.when(i == 0)
    def _():
        for c in in_copies(0, 0):
            c.start()

    @pl.when(i + 1 < nu_ref[0])
    def _():
        for c in in_copies(i + 1, (i + 1) % 2):
            c.start()

    @pl.when(i >= 2)
    def _():
        for c in out_copies(0, slot):
            c.wait()

    @pl.when(jnp.logical_and(live, fresh))
    def _():
        s = seq[0] % 2
        for c in w_copies(0, s):
            c.wait()
        for static_s in range(2):
            @pl.when(s == static_s)
            def _():
                wgb[...] = wbuf[static_s, 0].astype(BF16)
                wub[...] = wbuf[static_s, 1].astype(BF16)
                wdb[...] = wbuf[static_s, 2].astype(BF16)

        nxt = nx_ref[be_ref[i]]

        @pl.when(nxt < nu_ref[0])
        def _():
            for c in w_copies(be_ref[jnp.minimum(nxt, steps - 1)], 1 - s):
                c.start()

        seq[0] = seq[0] + 1

    @pl.when(live)
    def _():
        for c in in_copies(0, slot):
            c.wait()
        x = _rows_from_tiles(xbuf.at[slot], bm).astype(BF16)
        g = jnp.minimum(_dot(x, wgb[...]) + bg_ref[0], SWIGLU_LIMIT)
        u = jnp.clip(_dot(x, wub[...]) + bu_ref[0], -SWIGLU_LIMIT, SWIGLU_LIMIT)
        hdn = g * jax.nn.sigmoid(SWIGLU_ALPHA * g) * (u + 1.0)
        _rows_to_tiles(ybuf.at[slot], _dot(hdn.astype(BF16), wdb[...]) + bd_ref[0], bm)

    @pl.when(jnp.logical_not(live))
    def _():
        ybuf[slot] = jnp.zeros(ybuf.shape[1:], F32)

    for c in out_copies(i, slot):
        c.start()

    @pl.when(jnp.logical_and(i == steps - 1, i >= 1))
    def _():
        for c in out_copies(0, 1 - slot):
            c.wait()

    @pl.when(i == steps - 1)
    def _():
        for c in out_copies(0, slot):
            c.wait()


def _experts(block_e, n_used, next_first, xs, wts, *, bm):
    p = xs.shape[0]
    grouped = (p // SUBLANES, SUBLANES, CHUNKS, LANES)
    anyspace = pl.BlockSpec(memory_space=pl.ANY)
    bspec = pl.BlockSpec((1, 1, D_MODEL), lambda i, be, nu, nx: (be[i], 0, 0))
    tiles = (2, bm // SUBLANES) + TOKEN_TILE
    ys = pl.pallas_call(
        functools.partial(_expert_kernel, bm=bm),
        grid_spec=pltpu.PrefetchScalarGridSpec(
            num_scalar_prefetch=3, grid=(p // bm,),
            in_specs=[anyspace, anyspace, bspec, anyspace, bspec, anyspace, bspec],
            out_specs=anyspace,
            scratch_shapes=[pltpu.VMEM((D_MODEL, D_MODEL), BF16)] * 3 + [
                pltpu.VMEM((2, 3, D_MODEL, D_MODEL), F32), pltpu.VMEM(tiles, F32), pltpu.VMEM(tiles, F32),
                pltpu.SMEM((1,), I32),
                pltpu.SemaphoreType.DMA((2,)), pltpu.SemaphoreType.DMA((2,)), pltpu.SemaphoreType.DMA((2,))]),
        out_shape=jax.ShapeDtypeStruct(grouped, F32),
        compiler_params=pltpu.CompilerParams(dimension_semantics=("arbitrary",), vmem_limit_bytes=VMEM_LIMIT),
        name="moe_experts",
    )(block_e, n_used, next_first, xs.reshape(grouped), wts["w_gate"], wts["b_gate"], wts["w_up"], wts["b_up"],
      wts["w_down"], wts["b_down"])
    return ys.reshape(p, CHUNKS, LANES)


def _combine_kernel(dest_ref, hp_ref, hs_ref, twp_ref, tws_ref, gf_ref, ys_ref, yp_ref, ysm_ref, buf, sems,
                    *, tm, steps_p):
    i = pl.program_id(0)
    slot = i % 2

    def issue(tile, into):
        base = tile * (tm * TOP_K)

        def start(g, c):
            for u in range(SUBLANES):
                for j in range(TOP_K):
                    d = dest_ref[base + (g * SUBLANES + u) * TOP_K + j]
                    pltpu.make_async_copy(ys_ref.at[d], buf.at[into, j, g, :, u, :],
                                          sems.at[into]).start(priority=j % 2)
            return c

        lax.fori_loop(0, tm // SUBLANES, start, 0)

    @pl.when(i == 0)
    def _():
        issue(0, 0)

    @pl.when(i + 1 < pl.num_programs(0))
    def _():
        issue(i + 1, (i + 1) % 2)

    for j in range(TOP_K):
        pltpu.make_async_copy(ys_ref.at[pl.ds(0, tm)], ys_ref.at[pl.ds(0, tm)], sems.at[slot]).wait()

    def finish(h_ref, tw_ref, y_ref):
        tw = tw_ref[...]
        y = h_ref[...]
        for j in range(TOP_K):
            y = y + _rows_from_tiles(buf.at[slot, j], tm) * tw[:, j:j + 1]
        y_ref[...] = _rms(y, gf_ref[...])

    @pl.when(i < steps_p)
    def _():
        finish(hp_ref, twp_ref, yp_ref)

    @pl.when(i >= steps_p)
    def _():
        finish(hs_ref, tws_ref, ysm_ref)


def _combine(dest_flat, h2_p, h2_s, tw_p, tw_s, g_final, ys, *, tm):
    steps_p, steps_s = h2_p.shape[0] // tm, h2_s.shape[0] // tm
    pblk = lambda w: pl.BlockSpec((tm, w), lambda i, d: (jnp.minimum(i, steps_p - 1), 0))
    sblk = lambda w: pl.BlockSpec((tm, w), lambda i, d: (jnp.maximum(i - steps_p, 0), 0))
    return pl.pallas_call(
        functools.partial(_combine_kernel, tm=tm, steps_p=steps_p),
        grid_spec=pltpu.PrefetchScalarGridSpec(
            num_scalar_prefetch=1, grid=(steps_p + steps_s,),
            in_specs=[pblk(D_MODEL), sblk(D_MODEL), pblk(LANES), sblk(LANES),
                      pl.BlockSpec((1, D_MODEL), lambda i, d: (0, 0)),
                      pl.BlockSpec(memory_space=pl.ANY)],
            out_specs=(pblk(D_MODEL), sblk(D_MODEL)),
            scratch_shapes=[pltpu.VMEM((2, TOP_K, tm // SUBLANES) + TOKEN_TILE, F32),
                            pltpu.SemaphoreType.DMA((2,))]),
        out_shape=(jax.ShapeDtypeStruct(h2_p.shape, F32), jax.ShapeDtypeStruct(h2_s.shape, F32)),
        compiler_params=pltpu.CompilerParams(dimension_semantics=("arbitrary",), vmem_limit_bytes=VMEM_LIMIT),
        name="moe_combine",
    )(dest_flat, h2_p, h2_s, tw_p, tw_s, g_final, ys)


def _aug_layout():
    main, extra = [], []
    for h in range(FOX_HEADS):
        even = h % 2 == 0
        main.append(LANES * h + (0 if even else FOX_HEAD_DIM))
        extra.append(LANES * h + (FOX_HEAD_DIM if even else 0))
    return main, extra


def _prep_mix_weights(g_mix, w_in, b_f, conv_w, conv_b, w_a, b_a, w_i, b_i, lam):
    _, extra = _aug_layout()
    wq = w_in[:, 0:FOX_WIDTH] * (FOX_HEAD_DIM ** -0.5 * LOG2E)
    wf = w_in[:, 3 * FOX_WIDTH:3 * FOX_WIDTH + FOX_HEADS]
    wf_pad = jnp.concatenate([wf, wf, wf, jnp.zeros((D_MODEL, LANES - 3 * FOX_HEADS), F32)], axis=1)
    w_all = jnp.concatenate([wq, w_in[:, FOX_WIDTH:3 * FOX_WIDTH], wf_pad, w_in[:, 3 * FOX_WIDTH + FOX_HEADS:]],
                            axis=1).astype(BF16)
    bf_pad = jnp.concatenate([b_f, b_f, b_f, jnp.zeros((LANES - 3 * FOX_HEADS,), F32)]).reshape(1, LANES)
    eq = np.zeros((LANES, AUG), np.float32)
    ek = np.zeros((LANES, AUG), np.float32)
    cq = np.zeros((1, AUG), np.float32)
    ck = np.zeros((1, AUG), np.float32)
    cv = np.zeros((1, AUG), np.float32)
    hsel = np.zeros((FOX_WIDTH, LANES), np.float32)
    hdiag = np.zeros((FOX_WIDTH, LANES), np.float32)
    hmask = np.zeros((2, LANES), np.float32)
    hmask[0, :FOX_HEAD_DIM] = 1.0
    hmask[1, FOX_HEAD_DIM:] = 1.0
    for h in range(FOX_HEADS):
        hsel[FOX_HEAD_DIM * h:FOX_HEAD_DIM * (h + 1), h] = 1.0
        hdiag[FOX_HEAD_DIM * h:FOX_HEAD_DIM * (h + 1), 3 * FOX_HEADS + h] = 1.0
        cv[0, extra[h]] = 1.0
        eq[3 * FOX_HEADS + h, extra[h] + 6] = -1.0
        ck[0, extra[h] + 6] = 1.0
        for part in range(3):
            eq[part * 8 + h, extra[h] + part] = 1.0
            ek[part * 8 + h, extra[h] + 3 + part] = 1.0
            cq[0, extra[h] + 3 + part] = 1.0
            ck[0, extra[h] + part] = 1.0
    bias_sel = np.zeros((FOX_HEADS, FOX_HEAD_DIM, 4 * FOX_HEADS), np.float32)
    for h in range(FOX_HEADS):
        for part in range(3):
            bias_sel[h, part, 3 * FOX_HEADS] = 1.0
            bias_sel[h, 3 + part, part * 8 + h] = 1.0
        bias_sel[h, 6, 3 * FOX_HEADS] = 1.0
    dense = lambda w: jax.scipy.linalg.block_diag(*[w[i] for i in range(LRU_BLOCKS)]).astype(BF16)
    row = lambda a: a.reshape(1, -1)
    b16 = lambda a: jnp.asarray(a, BF16)
    return dict(g_mix=row(g_mix), w_all=w_all, bf_pad=bf_pad, eq=b16(eq), ek=b16(ek),
                cq=jnp.asarray(cq), ck=jnp.asarray(ck), cv=jnp.asarray(cv),
                bias_sel=b16(bias_sel),
                hsel=b16(hsel), hdiag=b16(hdiag), hmask=b16(hmask), conv_w=conv_w, conv_b=row(conv_b),
                wa=dense(w_a), ba=row(b_a), wi=dense(w_i), bi=row(b_i), lam=row(lam))


def kernel(x_prompt, x_sample, mem_prompt, cache_fox_k, cache_fox_v, cache_fox_logf, state_lru_h, state_conv, cache_mem_k, cache_mem_v, g_mix, w_in, b_f, conv_w, conv_b, w_a, b_a, w_i, b_i, lam, w_out, g_cross, g_mem, w_cq, w_ck, w_cv, w_co, g_moe, w_router, b_router, w_gate, b_gate, w_up, b_up, w_down, b_down, g_final):
    nb_p, seq, _ = x_prompt.shape
    nb_s, t_s, _ = x_sample.shape
    past = cache_fox_k.shape[2]
    n_mem = mem_prompt.shape[1]
    assert nb_p == 1 and g_mix.shape[0] == 1
    n_p, n_s = seq, nb_s * t_s
    row = lambda a: a.reshape(1, -1)

    mw = _prep_mix_weights(g_mix[0], w_in[0], b_f[0], conv_w[0], conv_b[0], w_a[0], b_a[0], w_i[0], b_i[0], lam[0])
    xp = x_prompt.reshape(n_p, D_MODEL)
    xs_ = x_sample.reshape(n_s, D_MODEL)

    zero_prev = jnp.zeros((1, SUBLANES, LRU_WIDTH), F32)
    zero_h = jnp.zeros((1, 1, LRU_WIDTH), F32)
    (qa_p, ka_p, va_p, k_p, v_p, _, lf_p, f2_p, qn_p, kn_p, lru_p, hl_p, ct_p) = _mix_in(
        xp, zero_prev, zero_h, mw, tm=MIX_TILE, streaming=True)
    prev_s = jnp.pad(state_conv[0], ((0, 0), (SUBLANES - (CONV_WIDTH - 1), 0), (0, 0)))
    (qa_s, ka_s, _, k_s, v_s, vb_s, lf_s, _, _, _, lru_s, hl_s, ct_s) = _mix_in(
        xs_, prev_s, state_lru_h[0].reshape(nb_s, 1, LRU_WIDTH), mw, tm=t_s, streaming=False)

    n_past, static_ok = _fox_plan(f2_p, qn_p, kn_p, bq=FOX_BQ, tile=MIX_TILE)
    fo_p = _fox_prompt(n_past, static_ok, qa_p, ka_p, va_p, bq=FOX_BQ)
    by_head = lambda c: jnp.moveaxis(c[0], 1, -1)
    fo_s = _fox_sample(qa_s, ka_s, vb_s, by_head(cache_fox_k), by_head(cache_fox_v), by_head(cache_fox_logf),
                       mw["bias_sel"], t=t_s)

    mk_p, mv_p = _mem_kv(mem_prompt[0], row(g_mem[0]), w_ck[0].astype(BF16), w_cv[0].astype(BF16))

    wr_pad = jnp.pad(w_router[0], ((0, 0), (0, LANES - N_EXPERTS))).astype(BF16)
    br_pad = jnp.pad(b_router[0], (0, LANES - N_EXPERTS)).reshape(1, LANES)
    pw = dict(w_out=w_out[0].astype(BF16), g_cross=row(g_cross[0]), w_cq=w_cq[0].astype(BF16),
              w_co=w_co[0].astype(BF16), g_moe=row(g_moe[0]), w_router=wr_pad, b_router=br_pad)
    cnt0 = jnp.zeros((SUBLANES, LANES), F32)
    h2_p, xn_p, idx_p, tw_p, rk_p, cnt_p = _post(
        xp, fo_p, lru_p, mk_p.reshape(1, n_mem, D_MODEL), mv_p.reshape(1, n_mem, D_MODEL), cnt0, pw,
        tm=POST_TILE, seg=POST_TILE, cached_memory=False)
    h2_s, xn_s, idx_s, tw_s, rk_s, cnt = _post(
        xs_, fo_s, lru_s, cache_mem_k, cache_mem_v, cnt_p, pw, tm=min(SAMPLE_POST_TILE, n_s), seg=t_s,
        cached_memory=True)

    n = n_p + n_s
    idx = jnp.concatenate([idx_p[:, :TOP_K], idx_s[:, :TOP_K]], axis=0)
    rank = jnp.concatenate([rk_p[:, :TOP_K], rk_s[:, :TOP_K]], axis=0)
    counts = cnt[0, :N_EXPERTS].astype(I32)
    padded = (counts + MOE_BM - 1) // MOE_BM * MOE_BM
    pad_end = jnp.cumsum(padded)
    pad_start = pad_end - padded
    dest = (pad_start[idx] + rank).reshape(n * TOP_K)
    n_blocks = -(-(n * TOP_K) // MOE_BM) + N_EXPERTS
    blk_row = jnp.arange(n_blocks, dtype=I32) * MOE_BM
    block_e = jnp.minimum(jnp.sum(pad_end[None, :] <= blk_row[:, None], axis=1), N_EXPERTS - 1).astype(I32)
    n_used = (pad_end[-1] // MOE_BM).reshape(1).astype(I32)
    last_block = jnp.maximum(pad_end // MOE_BM - 1, 0).astype(I32)

    xs_sorted = _dispatch(dest, last_block, n_used, xn_p, xn_s, tm=MOE_BM, n_blocks=n_blocks)
    ew = dict(w_gate=w_gate[0], w_up=w_up[0], w_down=w_down[0],
              b_gate=b_gate[0].reshape(N_EXPERTS, 1, D_MODEL), b_up=b_up[0].reshape(N_EXPERTS, 1, D_MODEL),
              b_down=b_down[0].reshape(N_EXPERTS, 1, D_MODEL))
    ys = _experts(block_e, n_used, (pad_end // MOE_BM).astype(I32), xs_sorted, ew, bm=MOE_BM)
    y_p, y_s = _combine(dest, h2_p, h2_s, tw_p, tw_s, row(g_final), ys, tm=COMBINE_TILE)

    shp_p = (1, nb_p, seq, FOX_HEADS, FOX_HEAD_DIM)
    shp_s = (1, nb_s, t_s, FOX_HEADS, FOX_HEAD_DIM)
    tail = slice(SUBLANES - (CONV_WIDTH - 1), SUBLANES)
    return (y_p.reshape(nb_p, seq, D_MODEL), y_s.reshape(nb_s, t_s, D_MODEL),
            k_p.reshape(shp_p), v_p.reshape(shp_p), lf_p.reshape(1, nb_p, seq, FOX_HEADS),
            hl_p.reshape(1, nb_p, LRU_WIDTH), ct_p[:, tail, :].reshape(1, nb_p, CONV_WIDTH - 1, LRU_WIDTH),
            mk_p.reshape(1, nb_p, n_mem, MEM_HEADS, MEM_HEAD_DIM), mv_p.reshape(1, nb_p, n_mem, MEM_HEADS, MEM_HEAD_DIM),
            k_s.reshape(shp_s), v_s.reshape(shp_s), lf_s.reshape(1, nb_s, t_s, FOX_HEADS),
            hl_s.reshape(1, nb_s, LRU_WIDTH), ct_s[:, tail, :].reshape(1, nb_s, CONV_WIDTH - 1, LRU_WIDTH))
```

```python
import functools
import math

import jax
import jax.numpy as jnp
import numpy as np
from jax import lax
from jax.experimental import pallas as pl
from jax.experimental.pallas import tpu as pltpu

F32 = jnp.float32
BF16 = jnp.bfloat16
I32 = jnp.int32

D_MODEL = 1024
FOX_HEADS = 8
FOX_HEAD_DIM = 64
FOX_WIDTH = FOX_HEADS * FOX_HEAD_DIM
LRU_WIDTH = D_MODEL - FOX_WIDTH
LRU_BLOCKS = 8
LRU_C = 8.0
CONV_WIDTH = 4
MEM_HEADS = 4
MEM_HEAD_DIM = D_MODEL // MEM_HEADS
N_EXPERTS = 32
TOP_K = 4
SWIGLU_LIMIT = 7.0
SWIGLU_ALPHA = 1.702
RMS_EPS = 1e-6
NEG_INF = -1e30
LOG2E = math.log2(math.e)

LANES = 128
SUBLANES = 8
CHUNKS = D_MODEL // LANES
TOKEN_TILE = (CHUNKS, SUBLANES, LANES)
AUG = LANES * FOX_HEADS
VMEM_LIMIT = 56 * 1024 * 1024

MIX_TILE = 256
POST_TILE = 512
SAMPLE_POST_TILE = 256
FOX_BQ = 512
MOE_BM = 512
COMBINE_TILE = 256
FOX_SKIP_GAP = 160.0
FOX_STATIC_SHIFT_RANGE = 100.0

C_Q, C_K, C_V, C_F, C_XR, C_G, C_END = 0, 512, 1024, 1536, 1664, 2176, 2688


def _dot(a, b):
    return jnp.dot(a, b, preferred_element_type=F32)


def _dot_nt(a, b):
    return lax.dot_general(a, b, (((1,), (1,)), ((), ())), preferred_element_type=F32)


def _split3(x):
    hi = x.astype(BF16)
    r = x - hi.astype(F32)
    mid = r.astype(BF16)
    lo = (r - mid.astype(F32)).astype(BF16)
    return hi, mid, lo


def _dot3(a, x):
    hi, mid, lo = _split3(x)
    return _dot(a, hi) + _dot(a, mid) + _dot(a, lo)


def _softplus(x):
    return jnp.maximum(x, 0.0) + jnp.log1p(jnp.exp(-jnp.abs(x)))


def _rms(x, g):
    return x * lax.rsqrt(jnp.mean(x * x, axis=-1, keepdims=True) + RMS_EPS) * g


def _tri(n, kind):
    r = lax.broadcasted_iota(I32, (n, n), 0)
    c = lax.broadcasted_iota(I32, (n, n), 1)
    m = {"le": c <= r, "lt": c < r, "gt": c > r}[kind]
    return jnp.where(m, 1.0, 0.0).astype(BF16)


def _mix_in_kernel(x_ref, g_ref, w_ref, bfp_ref, eq_ref, ek_ref, cq_ref, ck_ref, cv_ref, hs_ref, hd_ref, hm_ref,
                   cw_ref, cb_ref, wa_ref, ba_ref, wi_ref, bi_ref, lam_ref, cprev_ref, h0_ref,
                   qa_ref, ka_ref, va_ref, ko_ref, vo_ref, vb_ref, lf_ref, f2_ref, qn_ref, kn_ref,
                   lru_ref, hl_ref, ct_ref,
                   fcar, hcar, xp_ref, sa_ref, sb_ref, kbuf, vbuf, ksem, *, tm, pad, streaming):
    step = pl.program_id(0)
    first = step == 0
    slot = step % 2

    def kv_copies(s, row0):
        rows = pl.ds(row0, tm)
        copies = []
        for h in range(FOX_HEADS):
            copies.append(pltpu.make_async_copy(kbuf.at[s, h], ko_ref.at[rows, h, :], ksem.at[s]))
            copies.append(pltpu.make_async_copy(vbuf.at[s, h], vo_ref.at[rows, h, :], ksem.at[s]))
        return copies

    @pl.when(step >= 2)
    def _():
        for c in kv_copies(slot, 0):
            c.wait()

    if streaming:
        @pl.when(first)
        def _():
            fcar[...] = jnp.zeros_like(fcar)
            hcar[...] = jnp.zeros_like(hcar)
            xp_ref[0:SUBLANES, :] = jnp.zeros((SUBLANES, LRU_WIDTH), F32)
    else:
        fcar[...] = jnp.zeros_like(fcar)
        hcar[...] = jnp.broadcast_to(h0_ref[0], hcar.shape)
        xp_ref[0:SUBLANES, :] = cprev_ref[0]

    x = x_ref[...]
    xn = _rms(x, g_ref[...]).astype(BF16)
    z = _dot(xn, w_ref[...])

    zq, zk, zv = z[:, C_Q:C_K], z[:, C_K:C_V], z[:, C_V:C_F]
    for h in range(FOX_HEADS):
        hc = slice(h * FOX_HEAD_DIM, (h + 1) * FOX_HEAD_DIM)
        kbuf[slot, h] = zk[:, hc]
        vbuf[slot, h] = zv[:, hc]
    for c in kv_copies(slot, pl.multiple_of(step * tm, tm)):
        c.start()
    qd, kd, vd = zq.astype(BF16), zk.astype(BF16), zv.astype(BF16)
    vb_ref[...] = vd

    def head_pad(d):
        blocks = []
        for p in range(FOX_HEADS // 2):
            blk = d[:, p * LANES:(p + 1) * LANES]
            blocks += [blk * hm_ref[0:1, :], blk * hm_ref[1:2, :]]
        return jnp.concatenate(blocks, axis=1)

    lane = lax.broadcasted_iota(I32, (tm, LANES), 1)
    lf = jnp.where(lane < 3 * FOX_HEADS, -_softplus(-(z[:, C_F:C_XR] + bfp_ref[...])), 0.0)
    lf_ref[...] = lf[:, :FOX_HEADS]
    cum = _dot3(_tri(tm, "le"), lf) + fcar[0:1, :]
    fcar[...] = jnp.broadcast_to(cum[tm - 1:tm, :], fcar.shape)
    cum2 = cum * LOG2E
    f2_ref[...] = cum2[:, :FOX_HEADS]
    qf, kf = qd.astype(F32), kd.astype(F32)
    diag = _dot((qf * kf).astype(BF16), hd_ref[...]).astype(BF16)
    hi, mid, lo = _split3(cum2)
    bias = jnp.where(lane < 8, hi, jnp.where(lane < 16, mid, jnp.where(lane < 24, lo, diag)))
    qa_ref[...] = head_pad(qd) + (_dot(bias, eq_ref[...]) + cq_ref[...]).astype(BF16)
    ka_ref[...] = head_pad(kd) + (ck_ref[...] - _dot(bias, ek_ref[...])).astype(BF16)
    va_ref[...] = head_pad(vd) + cv_ref[...].astype(BF16)
    qn_ref[...] = jnp.broadcast_to(jnp.max(_dot((qf * qf).astype(BF16), hs_ref[...]), axis=0, keepdims=True),
                                   qn_ref.shape)
    kn_ref[...] = jnp.broadcast_to(jnp.max(_dot((kf * kf).astype(BF16), hs_ref[...]), axis=0, keepdims=True),
                                   kn_ref.shape)

    xr = z[:, C_XR:C_G]
    xp_ref[SUBLANES:SUBLANES + tm, :] = xr
    xc = cb_ref[...] + xr * cw_ref[CONV_WIDTH - 1:CONV_WIDTH, :]
    for j in range(CONV_WIDTH - 1):
        sh = CONV_WIDTH - 1 - j
        xc = xc + xp_ref[SUBLANES - sh:SUBLANES - sh + tm, :] * cw_ref[j:j + 1, :]
    tail = xp_ref[tm:tm + SUBLANES, :]
    ct_ref[0] = tail
    xp_ref[0:SUBLANES, :] = tail

    xcb = xc.astype(BF16)
    r = jax.nn.sigmoid(_dot(xcb, wa_ref[...]) + ba_ref[...])
    ig = jax.nn.sigmoid(_dot(xcb, wi_ref[...]) + bi_ref[...])
    log_a = (-LRU_C) * r * _softplus(-lam_ref[...])
    a = jnp.exp(log_a)
    mult = jnp.sqrt(-jnp.tanh(log_a) * (a * a + 1.0))
    if streaming:
        row = lax.broadcasted_iota(I32, (tm, LRU_WIDTH), 0)
        mult = jnp.where(jnp.logical_and(row == 0, first), 1.0, mult)
    b = mult * ig * xc

    sa_ref[0:pad, :] = jnp.ones((pad, LRU_WIDTH), F32)
    sb_ref[0:pad, :] = jnp.zeros((pad, LRU_WIDTH), F32)
    d = 1
    while d < tm:
        sa_ref[pad:pad + tm, :] = a
        sb_ref[pad:pad + tm, :] = b
        b = a * sb_ref[pad - d:pad - d + tm, :] + b
        a = a * sa_ref[pad - d:pad - d + tm, :]
        d *= 2
    h = a * hcar[0:1, :] + b
    hlast = h[tm - 1:tm, :]
    hcar[...] = jnp.broadcast_to(hlast, hcar.shape)
    hl_ref[0] = hlast
    lru_ref[...] = (h * jax.nn.gelu(z[:, C_G:C_END])).astype(BF16)

    last = pl.num_programs(0) - 1

    @pl.when(jnp.logical_and(step == last, step >= 1))
    def _():
        for c in kv_copies(1 - slot, 0):
            c.wait()

    @pl.when(step == last)
    def _():
        for c in kv_copies(slot, 0):
            c.wait()


def _mix_in(x, cprev, h0, wts, *, tm, streaming):
    n = x.shape[0]
    steps = n // tm
    nseg = 1 if streaming else steps
    pad = max(tm // 2, SUBLANES)
    seg = (lambda i: (0, 0, 0)) if streaming else (lambda i: (i, 0, 0))
    full = lambda a: pl.BlockSpec(a.shape, lambda i: (0,) * a.ndim)
    rows = lambda w: pl.BlockSpec((tm, w), lambda i: (i, 0))
    names = ("g_mix", "w_all", "bf_pad", "eq", "ek", "cq", "ck", "cv", "hsel", "hdiag", "hmask", "conv_w", "conv_b",
             "wa", "ba", "wi", "bi", "lam")
    ws = [wts[k] for k in names]
    sds = jax.ShapeDtypeStruct
    out_shape = (
        sds((n, AUG), BF16), sds((n, AUG), BF16), sds((n, AUG), BF16),
        sds((n, FOX_HEADS, FOX_HEAD_DIM), F32), sds((n, FOX_HEADS, FOX_HEAD_DIM), F32), sds((n, FOX_WIDTH), BF16),
        sds((n, FOX_HEADS), F32), sds((n, FOX_HEADS), F32),
        sds((steps * SUBLANES, LANES), F32), sds((steps * SUBLANES, LANES), F32),
        sds((n, LRU_WIDTH), BF16),
        sds((nseg, 1, LRU_WIDTH), F32), sds((nseg, SUBLANES, LRU_WIDTH), F32),
    )
    out_specs = (
        rows(AUG), rows(AUG), rows(AUG), pl.BlockSpec(memory_space=pl.ANY), pl.BlockSpec(memory_space=pl.ANY),
        rows(FOX_WIDTH), rows(FOX_HEADS), rows(FOX_HEADS),
        pl.BlockSpec((SUBLANES, LANES), lambda i: (i, 0)), pl.BlockSpec((SUBLANES, LANES), lambda i: (i, 0)),
        rows(LRU_WIDTH),
        pl.BlockSpec((1, 1, LRU_WIDTH), seg), pl.BlockSpec((1, SUBLANES, LRU_WIDTH), seg),
    )
    in_specs = [rows(D_MODEL)] + [full(w) for w in ws] + [
        pl.BlockSpec((1, SUBLANES, LRU_WIDTH), seg), pl.BlockSpec((1, 1, LRU_WIDTH), seg)]
    return pl.pallas_call(
        functools.partial(_mix_in_kernel, tm=tm, pad=pad, streaming=streaming),
        grid=(steps,), in_specs=in_specs, out_specs=out_specs, out_shape=out_shape,
        scratch_shapes=[
            pltpu.VMEM((SUBLANES, LANES), F32), pltpu.VMEM((SUBLANES, LRU_WIDTH), F32),
            pltpu.VMEM((tm + SUBLANES, LRU_WIDTH), F32),
            pltpu.VMEM((pad + tm, LRU_WIDTH), F32), pltpu.VMEM((pad + tm, LRU_WIDTH), F32),
            pltpu.VMEM((2, FOX_HEADS, tm, FOX_HEAD_DIM), F32), pltpu.VMEM((2, FOX_HEADS, tm, FOX_HEAD_DIM), F32),
            pltpu.SemaphoreType.DMA((2,))],
        compiler_params=pltpu.CompilerParams(dimension_semantics=("arbitrary",), vmem_limit_bytes=VMEM_LIMIT),
        name="mix_in_stream" if streaming else "mix_in_segments",
    )(x, *ws, cprev, h0)


def _fox_prompt_kernel(nb_ref, st_ref, q_ref, k_ref, v_ref, o_ref, m_ref, acc_ref, *, bq):
    qi = pl.program_id(1)
    plan = pl.program_id(0) * pl.num_programs(1) + qi
    n_past = nb_ref[plan]
    cols = [slice(LANES * c, LANES * (c + 1)) for c in range(2)]
    qs = [q_ref[:, cols[c]] for c in range(2)]
    acc_ref[...] = jnp.zeros(acc_ref.shape, F32)

    def scores(c, start, causal):
        s = _dot_nt(qs[c], k_ref[pl.ds(start, bq), cols[c]])
        if causal:
            r = lax.broadcasted_iota(I32, (bq, bq), 0)
            cc = lax.broadcasted_iota(I32, (bq, bq), 1)
            s = jnp.where(cc <= r, s, NEG_INF)
        return s

    def static_block(start, causal):
        for c in range(2):
            p = jnp.exp2(scores(c, start, causal)).astype(BF16)
            acc_ref[c] += _dot(p, v_ref[pl.ds(start, bq), cols[c]])

    def online_block(start, causal):
        for c in range(2):
            s = scores(c, start, causal)
            m_old = m_ref[c]
            m_new = jnp.maximum(m_old, jnp.max(s, axis=1, keepdims=True))
            p = jnp.exp2(s - m_new).astype(BF16)
            acc_ref[c] = jnp.exp2(m_old - m_new) * acc_ref[c] + _dot(p, v_ref[pl.ds(start, bq), cols[c]])
            m_ref[c] = m_new

    def run(block):
        block(pl.multiple_of(qi * bq, bq), True)

        def body(t, carry):
            block(pl.multiple_of((qi - 1 - t) * bq, bq), False)
            return carry

        lax.fori_loop(0, n_past, body, 0)

    @pl.when(st_ref[plan] == 1)
    def _():
        run(static_block)

    @pl.when(st_ref[plan] != 1)
    def _():
        m_ref[...] = jnp.full(m_ref.shape, NEG_INF, F32)
        run(online_block)

    a0, a1 = acc_ref[0], acc_ref[1]
    o0 = a0 / a0[:, FOX_HEAD_DIM:FOX_HEAD_DIM + 1]
    o1 = a1 / a1[:, 0:1]
    lane = lax.broadcasted_iota(I32, (bq, LANES), 1)
    o_ref[...] = jnp.where(lane < FOX_HEAD_DIM, o0, o1).astype(BF16)


def _fox_prompt(n_past, static_ok, qa, ka, va, *, bq):
    s = qa.shape[0]
    pairs = FOX_HEADS // 2
    return pl.pallas_call(
        functools.partial(_fox_prompt_kernel, bq=bq),
        grid_spec=pltpu.PrefetchScalarGridSpec(
            num_scalar_prefetch=2, grid=(pairs, s // bq),
            in_specs=[pl.BlockSpec((bq, 2 * LANES), lambda p, i, nb, st: (i, p)),
                      pl.BlockSpec((s, 2 * LANES), lambda p, i, nb, st: (0, p)),
                      pl.BlockSpec((s, 2 * LANES), lambda p, i, nb, st: (0, p))],
            out_specs=pl.BlockSpec((bq, LANES), lambda p, i, nb, st: (i, p)),
            scratch_shapes=[pltpu.VMEM((2, bq, 1), F32), pltpu.VMEM((2, bq, LANES), F32)]),
        out_shape=jax.ShapeDtypeStruct((s, FOX_WIDTH), BF16),
        compiler_params=pltpu.CompilerParams(dimension_semantics=("arbitrary", "arbitrary"),
                                             vmem_limit_bytes=VMEM_LIMIT),
        name="fox_prompt",
    )(n_past, static_ok, qa, ka, va)


def _fox_plan(f2, qn, kn, *, bq, tile):
    s = f2.shape[0]
    nq = s // bq
    per = lambda a: a.reshape(s // tile, SUBLANES, LANES)[:, 0, :FOX_HEADS]
    qnorm = jnp.sqrt(jnp.max(per(qn).reshape(nq, bq // tile, FOX_HEADS), axis=1))
    knorm = jnp.sqrt(jnp.max(per(kn), axis=0))
    spread = 2.0 * 1.02 * qnorm * knorm[None, :]
    thr = spread + FOX_SKIP_GAP
    f_first = f2[0::bq]
    f_last = f2[bq - 1::bq]
    gap = f_last[None, :, :] - f_first[:, None, :]
    before = (jnp.arange(nq)[None, :] < jnp.arange(nq)[:, None])[:, :, None]
    need = jnp.sum(jnp.logical_and(before, gap < thr[:, None, :]), axis=1)
    pair = lambda a: a.reshape(nq, FOX_HEADS // 2, 2)
    n_past = jnp.max(pair(need), axis=2).T.reshape(-1).astype(I32)
    static_ok = jnp.all(pair(spread) <= FOX_STATIC_SHIFT_RANGE, axis=2).T.reshape(-1).astype(I32)
    return n_past, static_ok


def _fox_sample_kernel(q_ref, kn_ref, vn_ref, kt_ref, vt_ref, lf_ref, sel_ref, o_ref, g_ref,
                       *, t, past, chunk):
    car = jnp.zeros((FOX_HEADS, 1), F32)
    later = _tri(chunk, "lt")
    for ci in reversed(range(past // chunk)):
        lf = lf_ref[0, :, ci * chunk:(ci + 1) * chunk]
        hi, mid, lo = _split3(lf)
        g_ref[:, ci * chunk:(ci + 1) * chunk] = _dot(hi, later) + _dot(mid, later) + _dot(lo, later) + car
        car = car + jnp.sum(lf, axis=1, keepdims=True)
    hi, mid, lo = _split3(g_ref[...] * LOG2E)
    bias_rows = jnp.concatenate([hi.astype(F32), mid.astype(F32), lo.astype(F32),
                                 jnp.ones((FOX_HEADS, past), F32)], axis=0).astype(BF16)

    def swap_halves(x):
        return pltpu.roll(x.astype(F32), FOX_HEAD_DIM, axis=1).astype(BF16)

    r = lax.broadcasted_iota(I32, (t, t), 0)
    cc = lax.broadcasted_iota(I32, (t, t), 1)
    prow = lax.broadcasted_iota(I32, (FOX_HEAD_DIM, LANES), 0)
    plane = lax.broadcasted_iota(I32, (FOX_HEAD_DIM, LANES), 1)
    place = [jnp.where(plane == prow + FOX_HEAD_DIM * c, 1.0, 0.0).astype(BF16) for c in range(2)]
    for p in range(FOX_HEADS // 2):
        pc = slice(LANES * p, LANES * (p + 1))
        vpair = vn_ref[:, pc]
        o_pair = jnp.zeros((t, LANES), F32)
        for c in range(2):
            h = 2 * p + c
            hc = slice(LANES * h, LANES * (h + 1))
            q, kn, vn = q_ref[:, hc], kn_ref[:, hc], vpair
            if c == 1:
                q, kn, vn = swap_halves(q), swap_halves(kn), swap_halves(vn)
            bias = _dot(sel_ref[h], bias_rows).astype(BF16)
            kpast_t = jnp.concatenate([kt_ref[0, h].astype(BF16), bias], axis=0)
            sp = _dot(q, kpast_t)
            sn = jnp.where(cc <= r, _dot_nt(q, kn), NEG_INF)
            m = jnp.maximum(jnp.max(sp, axis=1, keepdims=True), jnp.max(sn, axis=1, keepdims=True))
            pp = jnp.exp2(sp - m)
            pn = jnp.exp2(sn - m)
            l = jnp.sum(pp, axis=1, keepdims=True) + jnp.sum(pn, axis=1, keepdims=True)
            o = (_dot_nt(pp.astype(BF16), vt_ref[0, h].astype(BF16))
                 + _dot(pn.astype(BF16), vn[:, :FOX_HEAD_DIM])) / l
            o_pair = o_pair + _dot(o.astype(BF16), place[c])
        o_ref[:, pc] = o_pair.astype(BF16)


def _fox_sample(qa, ka, vb, cache_kt, cache_vt, cache_lft, sel, *, t):
    nb, past = cache_kt.shape[0], cache_kt.shape[3]
    full = lambda a: pl.BlockSpec(a.shape, lambda b: (0,) * a.ndim)
    slab = pl.BlockSpec((1, FOX_HEADS, FOX_HEAD_DIM, past), lambda b: (b, 0, 0, 0))
    return pl.pallas_call(
        functools.partial(_fox_sample_kernel, t=t, past=past, chunk=256),
        grid=(nb,),
        in_specs=[pl.BlockSpec((t, AUG), lambda b: (b, 0)), pl.BlockSpec((t, AUG), lambda b: (b, 0)),
                  pl.BlockSpec((t, FOX_WIDTH), lambda b: (b, 0)), slab, slab,
                  pl.BlockSpec((1, FOX_HEADS, past), lambda b: (b, 0, 0)), full(sel)],
        out_specs=pl.BlockSpec((t, FOX_WIDTH), lambda b: (b, 0)),
        out_shape=jax.ShapeDtypeStruct((nb * t, FOX_WIDTH), BF16),
        scratch_shapes=[pltpu.VMEM((FOX_HEADS, past), F32)],
        compiler_params=pltpu.CompilerParams(dimension_semantics=("arbitrary",), vmem_limit_bytes=VMEM_LIMIT),
        name="fox_sample",
    )(qa, ka, vb, cache_kt, cache_vt, cache_lft, sel)


def _mem_kv_kernel(m_ref, g_ref, wk_ref, wv_ref, k_ref, v_ref):
    mn = _rms(m_ref[...], g_ref[...]).astype(BF16)
    k_ref[...] = _dot(mn, wk_ref[...])
    v_ref[...] = _dot(mn, wv_ref[...])


def _mem_kv(mem, g, wk, wv):
    n = mem.shape[0]
    return pl.pallas_call(
        _mem_kv_kernel,
        out_shape=(jax.ShapeDtypeStruct((n, D_MODEL), F32), jax.ShapeDtypeStruct((n, D_MODEL), F32)),
        compiler_params=pltpu.CompilerParams(vmem_limit_bytes=VMEM_LIMIT),
        name="mem_kv",
    )(mem, g, wk, wv)


def _post_kernel(h_ref, fo_ref, lru_ref, wo_ref, gc_ref, wcq_ref, mk_ref, mv_ref, wco_ref,
                 gm_ref, wr_ref, br_ref, cnt_in_ref,
                 h2_ref, xn_ref, idx_ref, tw_ref, rank_ref, cnt_ref, car_ref, *mem_scratch, tm, seg, cached_memory):
    step = pl.program_id(0)
    nseg = tm // seg

    @pl.when(step == 0)
    def _():
        car_ref[...] = cnt_in_ref[...]

    if cached_memory:
        mbuf, msem = mem_scratch
        slot = step % 2

        def mem_copies(at_step, s):
            return [pltpu.make_async_copy(ref.at[0, at_step * nseg + j, :, hd, :], mbuf.at[s, j, t, hd], msem.at[s])
                    for j in range(nseg) for t, ref in enumerate((mk_ref, mv_ref)) for hd in range(MEM_HEADS)]

        @pl.when(step == 0)
        def _():
            for c in mem_copies(0, 0):
                c.start()

        @pl.when(step + 1 < pl.num_programs(0))
        def _():
            for c in mem_copies(step + 1, 1 - slot):
                c.start()

    h1 = h_ref[...] + _dot(fo_ref[...], wo_ref[0:FOX_WIDTH, :]) + _dot(lru_ref[...], wo_ref[FOX_WIDTH:D_MODEL, :])

    q = _dot(_rms(h1, gc_ref[...]).astype(BF16), wcq_ref[...])
    if cached_memory:
        for c in mem_copies(0, slot):
            c.wait()
    segs = []
    for j in range(nseg):
        rows = slice(j * seg, (j + 1) * seg)
        heads = []
        for hd in range(MEM_HEADS):
            hc = slice(MEM_HEAD_DIM * hd, MEM_HEAD_DIM * (hd + 1))
            mk = mbuf[slot, j, 0, hd] if cached_memory else mk_ref[0, :, hc]
            mv = mbuf[slot, j, 1, hd] if cached_memory else mv_ref[0, :, hc]
            s = _dot_nt(q[rows, hc].astype(BF16), mk.astype(BF16)) * (MEM_HEAD_DIM ** -0.5)
            p = jnp.exp(s - jnp.max(s, axis=1, keepdims=True))
            o = _dot(p.astype(BF16), mv.astype(BF16)) / jnp.sum(p, axis=1, keepdims=True)
            heads.append(o.astype(BF16))
        segs.append(jnp.concatenate(heads, axis=1))
    h2 = h1 + _dot(segs[0] if nseg == 1 else jnp.concatenate(segs, axis=0), wco_ref[...])
    h2_ref[...] = h2

    xn = _rms(h2, gm_ref[...])
    _rows_to_tiles(xn_ref, xn, tm)

    lane = lax.broadcasted_iota(I32, (tm, LANES), 1).astype(F32)
    logits = jnp.where(lane < N_EXPERTS, _dot(xn.astype(BF16), wr_ref[...]) + br_ref[...], -jnp.inf)
    vals, idxs = [], []
    for _ in range(TOP_K):
        mx = jnp.max(logits, axis=1, keepdims=True)
        ix = jnp.min(jnp.where(logits == mx, lane, float(LANES)), axis=1, keepdims=True)
        vals.append(mx)
        idxs.append(ix)
        logits = jnp.where(lane == ix, -jnp.inf, logits)
    es = [jnp.exp(v - vals[0]) for v in vals]
    den = es[0] + es[1] + es[2] + es[3]

    onehot = jnp.zeros((tm, LANES), F32)
    for ix in idxs:
        onehot = onehot + jnp.where(lane == ix, 1.0, 0.0)
    before = _dot(_tri(tm, "lt"), onehot.astype(BF16)) + car_ref[0:1, :]
    car = car_ref[0:1, :] + jnp.sum(onehot, axis=0, keepdims=True)
    car_ref[...] = jnp.broadcast_to(car, car_ref.shape)
    cnt_ref[...] = jnp.broadcast_to(car, cnt_ref.shape)

    idx_o = jnp.zeros((tm, LANES), F32)
    tw_o = jnp.zeros((tm, LANES), F32)
    rk_o = jnp.zeros((tm, LANES), F32)
    for j in range(TOP_K):
        rk = jnp.sum(jnp.where(lane == idxs[j], before, 0.0), axis=1, keepdims=True)
        idx_o = jnp.where(lane == j, idxs[j], idx_o)
        tw_o = jnp.where(lane == j, es[j] / den, tw_o)
        rk_o = jnp.where(lane == j, rk, rk_o)
    idx_ref[...] = idx_o.astype(I32)
    tw_ref[...] = tw_o
    rank_ref[...] = rk_o.astype(I32)


def _post(h, fo, lru, mk, mv, cnt_in, wts, *, tm, seg, cached_memory):
    n = h.shape[0]
    full = lambda a: pl.BlockSpec(a.shape, lambda i: (0,) * a.ndim)
    rows = lambda w: pl.BlockSpec((tm, w), lambda i: (i, 0))
    if cached_memory:
        nm = mk.shape[2]
        mem = pl.BlockSpec(memory_space=pl.ANY)
        mem_scratch = [pltpu.VMEM((2, tm // seg, 2, MEM_HEADS, nm, MEM_HEAD_DIM), F32),
                       pltpu.SemaphoreType.DMA((2,))]
    else:
        mem = full(mk)
        mem_scratch = []
    w = [wts[k] for k in ("w_out", "g_cross", "w_cq")]
    w2 = [wts[k] for k in ("w_co", "g_moe", "w_router", "b_router")]
    return pl.pallas_call(
        functools.partial(_post_kernel, tm=tm, seg=seg, cached_memory=cached_memory),
        grid=(n // tm,),
        in_specs=[rows(D_MODEL), rows(FOX_WIDTH), rows(LRU_WIDTH)] + [full(a) for a in w] + [mem, mem]
                 + [full(a) for a in w2] + [full(cnt_in)],
        out_specs=(rows(D_MODEL), pl.BlockSpec((tm // SUBLANES,) + TOKEN_TILE, lambda i: (i, 0, 0, 0)),
                   rows(LANES), rows(LANES), rows(LANES),
                   pl.BlockSpec((SUBLANES, LANES), lambda i: (0, 0))),
        out_shape=(jax.ShapeDtypeStruct((n, D_MODEL), F32), jax.ShapeDtypeStruct((n // SUBLANES,) + TOKEN_TILE, F32),
                   jax.ShapeDtypeStruct((n, LANES), I32), jax.ShapeDtypeStruct((n, LANES), F32),
                   jax.ShapeDtypeStruct((n, LANES), I32), jax.ShapeDtypeStruct((SUBLANES, LANES), F32)),
        scratch_shapes=[pltpu.VMEM((SUBLANES, LANES), F32)] + mem_scratch,
        compiler_params=pltpu.CompilerParams(dimension_semantics=("arbitrary",), vmem_limit_bytes=VMEM_LIMIT),
        name="post_batch_mem" if cached_memory else "post_shared_mem",
    )(h, fo, lru, *w, mk, mv, *w2, cnt_in)


STAGES = 3


def _dispatch_kernel(dest_ref, last_ref, nu_ref, xp_ref, xs_ref, out_ref, zero_ref, stage, sems, gsem, zsem,
                     *, tm, steps_p, n_blocks):
    i = pl.program_id(0)

    def zero_copy(b):
        return pltpu.make_async_copy(zero_ref, out_ref.at[pl.ds(pl.multiple_of(b * tm, tm), tm)], zsem)

    @pl.when(i == 0)
    def _():
        zero_ref[...] = jnp.zeros(zero_ref.shape, F32)
        n_tail = n_blocks - nu_ref[0]

        def start_e(e, c):
            zero_copy(last_ref[e]).start()
            return c

        def start_t(b, c):
            zero_copy(nu_ref[0] + b).start()
            return c

        def wait_one(b, c):
            zero_copy(0).wait()
            return c

        lax.fori_loop(0, N_EXPERTS, start_e, 0)
        lax.fori_loop(0, n_tail, start_t, 0)
        lax.fori_loop(0, N_EXPERTS + n_tail, wait_one, 0)

    groups = tm // SUBLANES
    steps = pl.num_programs(0)

    def stage_copies(src_ref, tile, slot):
        return [pltpu.make_async_copy(src_ref.at[pl.ds(tile * groups, groups), :, u, :], stage.at[slot, :, u],
                                      gsem.at[slot]) for u in range(SUBLANES)]

    def start_stage(tile, slot):
        @pl.when(tile < steps_p)
        def _():
            for c in stage_copies(xp_ref, tile, slot):
                c.start()

        @pl.when(tile >= steps_p)
        def _():
            for c in stage_copies(xs_ref, tile - steps_p, slot):
                c.start()

    @pl.when(i == 0)
    def _():
        start_stage(0, 0)

    @pl.when(i + 1 < steps)
    def _():
        start_stage(i + 1, (i + 1) % STAGES)

    slot = i % STAGES
    for c in stage_copies(xp_ref, 0, slot):
        c.wait()

    base = i * (tm * TOP_K)

    def start(g, c):
        for u in range(SUBLANES):
            for j in range(TOP_K):
                d = dest_ref[base + (g * SUBLANES + u) * TOP_K + j]
                pltpu.make_async_copy(stage.at[slot, g, u], out_ref.at[d],
                                      sems.at[i % 2]).start(priority=j % 2)
        return c

    lax.fori_loop(0, groups, start, 0)

    def drain(parity):
        for _ in range(TOP_K):
            pltpu.make_async_copy(out_ref.at[pl.ds(0, tm)], out_ref.at[pl.ds(0, tm)], sems.at[parity]).wait()

    @pl.when(i > 0)
    def _():
        drain((i - 1) % 2)

    @pl.when(i == steps - 1)
    def _():
        drain(i % 2)


def _dispatch(dest_flat, last_block, n_used, xn_p, xn_s, *, tm, n_blocks):
    steps_p, steps_s = xn_p.shape[0] * SUBLANES // tm, xn_s.shape[0] * SUBLANES // tm
    return pl.pallas_call(
        functools.partial(_dispatch_kernel, tm=tm, steps_p=steps_p, n_blocks=n_blocks),
        grid_spec=pltpu.PrefetchScalarGridSpec(
            num_scalar_prefetch=3, grid=(steps_p + steps_s,),
            in_specs=[pl.BlockSpec(memory_space=pl.ANY), pl.BlockSpec(memory_space=pl.ANY)],
            out_specs=pl.BlockSpec(memory_space=pl.ANY),
            scratch_shapes=[pltpu.VMEM((tm, CHUNKS, LANES), F32),
                            pltpu.VMEM((STAGES, tm // SUBLANES, SUBLANES, CHUNKS, LANES), F32),
                            pltpu.SemaphoreType.DMA((2,)), pltpu.SemaphoreType.DMA((STAGES,)),
                            pltpu.SemaphoreType.DMA(())]),
        out_shape=jax.ShapeDtypeStruct((n_blocks * tm, CHUNKS, LANES), F32),
        compiler_params=pltpu.CompilerParams(dimension_semantics=("arbitrary",), vmem_limit_bytes=VMEM_LIMIT),
        name="moe_dispatch",
    )(dest_flat, last_block, n_used, xn_p, xn_s)


def _rows_from_tiles(view, rows):
    return jnp.concatenate([view[:, c].reshape(rows, LANES) for c in range(CHUNKS)], axis=1)


def _rows_to_tiles(view, val, rows):
    for c in range(CHUNKS):
        view[:, c] = val[:, c * LANES:(c + 1) * LANES].reshape(rows // SUBLANES, SUBLANES, LANES)


def _expert_kernel(be_ref, nu_ref, nx_ref, vr_ref, xs_ref, wg_ref, bg_ref, wu_ref, bu_ref, wd_ref, bd_ref, ys_ref,
                   wgb, wub, wdb, wbuf, xbuf, ybuf, seq, wsem, xsem, ysem, *, bm):
    i = pl.program_id(0)
    steps = pl.num_programs(0)
    groups = bm // SUBLANES
    slot = i % 2
    prev = be_ref[jnp.maximum(i - 1, 0)]
    fresh = jnp.logical_or(i == 0, be_ref[i] != prev)
    live = i < nu_ref[0]

    def w_copies(e, s):
        return [pltpu.make_async_copy(w_ref.at[e], wbuf.at[s, k], wsem.at[s])
                for k, w_ref in enumerate((wg_ref, wu_ref, wd_ref))]

    @pl.when(i == 0)
    def _():
        seq[0] = 0
        for c in w_copies(be_ref[0], 0):
            c.start()

    def in_copies(blk, s):
        g0 = pl.multiple_of(blk * groups, groups)
        return [pltpu.make_async_copy(xs_ref.at[pl.ds(g0, groups), u], xbuf.at[s, :, :, u, :], xsem.at[s])
                for u in range(SUBLANES)]

    def out_copies(blk, s):
        g0 = pl.multiple_of(blk * groups, groups)
        return [pltpu.make_async_copy(ybuf.at[s, :, :, u, :], ys_ref.at[pl.ds(g0, groups), u], ysem.at[s])
                for u in range(SUBLANES)]

    @pl.when(i == 0)
    def _():
        for c in in_copies(0, 0):
            c.start()

    @pl.when(i + 1 < nu_ref[0])
    def _():
        for c in in_copies(i + 1, (i + 1) % 2):
            c.start()

    @pl.when(i >= 2)
    def _():
        for c in out_copies(0, slot):
            c.wait()

    @pl.when(jnp.logical_and(live, fresh))
    def _():
        s = seq[0] % 2
        for c in w_copies(0, s):
            c.wait()
        for static_s in range(2):
            @pl.when(s == static_s)
            def _():
                wgb[...] = wbuf[static_s, 0].astype(BF16)
                wub[...] = wbuf[static_s, 1].astype(BF16)
                wdb[...] = wbuf[static_s, 2].astype(BF16)

        nxt = nx_ref[be_ref[i]]

        @pl.when(nxt < nu_ref[0])
        def _():
            for c in w_copies(be_ref[jnp.minimum(nxt, steps - 1)], 1 - s):
                c.start()

        seq[0] = seq[0] + 1

    def mlp(first_group, n_groups):
        rows = n_groups * SUBLANES
        tiles = pl.ds(first_group, n_groups)
        x = _rows_from_tiles(xbuf.at[slot, tiles], rows).astype(BF16)
        g = jnp.minimum(_dot(x, wgb[...]) + bg_ref[0], SWIGLU_LIMIT)
        u = jnp.clip(_dot(x, wub[...]) + bu_ref[0], -SWIGLU_LIMIT, SWIGLU_LIMIT)
        hdn = g * jax.nn.sigmoid(SWIGLU_ALPHA * g) * (u + 1.0)
        _rows_to_tiles(ybuf.at[slot, tiles], _dot(hdn.astype(BF16), wdb[...]) + bd_ref[0], rows)

    half_only = vr_ref[i] <= bm // 2

    @pl.when(live)
    def _():
        for c in in_copies(0, slot):
            c.wait()

    @pl.when(jnp.logical_and(live, jnp.logical_not(half_only)))
    def _():
        mlp(0, groups)

    @pl.when(jnp.logical_and(live, half_only))
    def _():
        mlp(0, groups // 2)
        ybuf[slot, groups // 2:groups] = jnp.zeros((groups // 2,) + TOKEN_TILE, F32)

    @pl.when(jnp.logical_not(live))
    def _():
        ybuf[slot] = jnp.zeros(ybuf.shape[1:], F32)

    for c in out_copies(i, slot):
        c.start()

    @pl.when(jnp.logical_and(i == steps - 1, i >= 1))
    def _():
        for c in out_copies(0, 1 - slot):
            c.wait()

    @pl.when(i == steps - 1)
    def _():
        for c in out_copies(0, slot):
            c.wait()


def _experts(block_e, n_used, next_first, valid_rows, xs, wts, *, bm):
    p = xs.shape[0]
    grouped = (p // SUBLANES, SUBLANES, CHUNKS, LANES)
    anyspace = pl.BlockSpec(memory_space=pl.ANY)
    bspec = pl.BlockSpec((1, 1, D_MODEL), lambda i, be, nu, nx, vr: (be[i], 0, 0))
    tiles = (2, bm // SUBLANES) + TOKEN_TILE
    ys = pl.pallas_call(
        functools.partial(_expert_kernel, bm=bm),
        grid_spec=pltpu.PrefetchScalarGridSpec(
            num_scalar_prefetch=4, grid=(p // bm,),
            in_specs=[anyspace, anyspace, bspec, anyspace, bspec, anyspace, bspec],
            out_specs=anyspace,
            scratch_shapes=[pltpu.VMEM((D_MODEL, D_MODEL), BF16)] * 3 + [
                pltpu.VMEM((2, 3, D_MODEL, D_MODEL), F32), pltpu.VMEM(tiles, F32), pltpu.VMEM(tiles, F32),
                pltpu.SMEM((1,), I32),
                pltpu.SemaphoreType.DMA((2,)), pltpu.SemaphoreType.DMA((2,)), pltpu.SemaphoreType.DMA((2,))]),
        out_shape=jax.ShapeDtypeStruct(grouped, F32),
        compiler_params=pltpu.CompilerParams(dimension_semantics=("arbitrary",), vmem_limit_bytes=VMEM_LIMIT),
        name="moe_experts",
    )(block_e, n_used, next_first, valid_rows, xs.reshape(grouped), wts["w_gate"], wts["b_gate"], wts["w_up"], wts["b_up"],
      wts["w_down"], wts["b_down"])
    return ys.reshape(p, CHUNKS, LANES)


def _combine_kernel(dest_ref, hp_ref, hs_ref, twp_ref, tws_ref, gf_ref, ys_ref, yp_ref, ysm_ref, buf, sems,
                    *, tm, steps_p):
    i = pl.program_id(0)
    slot = i % 2

    def issue(tile, into):
        base = tile * (tm * TOP_K)

        def start(g, c):
            for u in range(SUBLANES):
                for j in range(TOP_K):
                    d = dest_ref[base + (g * SUBLANES + u) * TOP_K + j]
                    pltpu.make_async_copy(ys_ref.at[d], buf.at[into, j, g, :, u, :],
                                          sems.at[into]).start(priority=j % 2)
            return c

        lax.fori_loop(0, tm // SUBLANES, start, 0)

    @pl.when(i == 0)
    def _():
        issue(0, 0)

    @pl.when(i + 1 < pl.num_programs(0))
    def _():
        issue(i + 1, (i + 1) % 2)

    for j in range(TOP_K):
        pltpu.make_async_copy(ys_ref.at[pl.ds(0, tm)], ys_ref.at[pl.ds(0, tm)], sems.at[slot]).wait()

    def finish(h_ref, tw_ref, y_ref):
        tw = tw_ref[...]
        y = h_ref[...]
        for j in range(TOP_K):
            y = y + _rows_from_tiles(buf.at[slot, j], tm) * tw[:, j:j + 1]
        y_ref[...] = _rms(y, gf_ref[...])

    @pl.when(i < steps_p)
    def _():
        finish(hp_ref, twp_ref, yp_ref)

    @pl.when(i >= steps_p)
    def _():
        finish(hs_ref, tws_ref, ysm_ref)


def _combine(dest_flat, h2_p, h2_s, tw_p, tw_s, g_final, ys, *, tm):
    steps_p, steps_s = h2_p.shape[0] // tm, h2_s.shape[0] // tm
    pblk = lambda w: pl.BlockSpec((tm, w), lambda i, d: (jnp.minimum(i, steps_p - 1), 0))
    sblk = lambda w: pl.BlockSpec((tm, w), lambda i, d: (jnp.maximum(i - steps_p, 0), 0))
    return pl.pallas_call(
        functools.partial(_combine_kernel, tm=tm, steps_p=steps_p),
        grid_spec=pltpu.PrefetchScalarGridSpec(
            num_scalar_prefetch=1, grid=(steps_p + steps_s,),
            in_specs=[pblk(D_MODEL), sblk(D_MODEL), pblk(LANES), sblk(LANES),
                      pl.BlockSpec((1, D_MODEL), lambda i, d: (0, 0)),
                      pl.BlockSpec(memory_space=pl.ANY)],
            out_specs=(pblk(D_MODEL), sblk(D_MODEL)),
            scratch_shapes=[pltpu.VMEM((2, TOP_K, tm // SUBLANES) + TOKEN_TILE, F32),
                            pltpu.SemaphoreType.DMA((2,))]),
        out_shape=(jax.ShapeDtypeStruct(h2_p.shape, F32), jax.ShapeDtypeStruct(h2_s.shape, F32)),
        compiler_params=pltpu.CompilerParams(dimension_semantics=("arbitrary",), vmem_limit_bytes=VMEM_LIMIT),
        name="moe_combine",
    )(dest_flat, h2_p, h2_s, tw_p, tw_s, g_final, ys)


def _aug_layout():
    main, extra = [], []
    for h in range(FOX_HEADS):
        even = h % 2 == 0
        main.append(LANES * h + (0 if even else FOX_HEAD_DIM))
        extra.append(LANES * h + (FOX_HEAD_DIM if even else 0))
    return main, extra


def _prep_mix_weights(g_mix, w_in, b_f, conv_w, conv_b, w_a, b_a, w_i, b_i, lam):
    _, extra = _aug_layout()
    wq = w_in[:, 0:FOX_WIDTH] * (FOX_HEAD_DIM ** -0.5 * LOG2E)
    wf = w_in[:, 3 * FOX_WIDTH:3 * FOX_WIDTH + FOX_HEADS]
    wf_pad = jnp.concatenate([wf, wf, wf, jnp.zeros((D_MODEL, LANES - 3 * FOX_HEADS), F32)], axis=1)
    w_all = jnp.concatenate([wq, w_in[:, FOX_WIDTH:3 * FOX_WIDTH], wf_pad, w_in[:, 3 * FOX_WIDTH + FOX_HEADS:]],
                            axis=1).astype(BF16)
    bf_pad = jnp.concatenate([b_f, b_f, b_f, jnp.zeros((LANES - 3 * FOX_HEADS,), F32)]).reshape(1, LANES)
    eq = np.zeros((LANES, AUG), np.float32)
    ek = np.zeros((LANES, AUG), np.float32)
    cq = np.zeros((1, AUG), np.float32)
    ck = np.zeros((1, AUG), np.float32)
    cv = np.zeros((1, AUG), np.float32)
    hsel = np.zeros((FOX_WIDTH, LANES), np.float32)
    hdiag = np.zeros((FOX_WIDTH, LANES), np.float32)
    hmask = np.zeros((2, LANES), np.float32)
    hmask[0, :FOX_HEAD_DIM] = 1.0
    hmask[1, FOX_HEAD_DIM:] = 1.0
    for h in range(FOX_HEADS):
        hsel[FOX_HEAD_DIM * h:FOX_HEAD_DIM * (h + 1), h] = 1.0
        hdiag[FOX_HEAD_DIM * h:FOX_HEAD_DIM * (h + 1), 3 * FOX_HEADS + h] = 1.0
        cv[0, extra[h]] = 1.0
        eq[3 * FOX_HEADS + h, extra[h] + 6] = -1.0
        ck[0, extra[h] + 6] = 1.0
        for part in range(3):
            eq[part * 8 + h, extra[h] + part] = 1.0
            ek[part * 8 + h, extra[h] + 3 + part] = 1.0
            cq[0, extra[h] + 3 + part] = 1.0
            ck[0, extra[h] + part] = 1.0
    bias_sel = np.zeros((FOX_HEADS, FOX_HEAD_DIM, 4 * FOX_HEADS), np.float32)
    for h in range(FOX_HEADS):
        for part in range(3):
            bias_sel[h, part, 3 * FOX_HEADS] = 1.0
            bias_sel[h, 3 + part, part * 8 + h] = 1.0
        bias_sel[h, 6, 3 * FOX_HEADS] = 1.0
    dense = lambda w: jax.scipy.linalg.block_diag(*[w[i] for i in range(LRU_BLOCKS)]).astype(BF16)
    row = lambda a: a.reshape(1, -1)
    b16 = lambda a: jnp.asarray(a, BF16)
    return dict(g_mix=row(g_mix), w_all=w_all, bf_pad=bf_pad, eq=b16(eq), ek=b16(ek),
                cq=jnp.asarray(cq), ck=jnp.asarray(ck), cv=jnp.asarray(cv),
                bias_sel=b16(bias_sel),
                hsel=b16(hsel), hdiag=b16(hdiag), hmask=b16(hmask), conv_w=conv_w, conv_b=row(conv_b),
                wa=dense(w_a), ba=row(b_a), wi=dense(w_i), bi=row(b_i), lam=row(lam))


def kernel(x_prompt, x_sample, mem_prompt, cache_fox_k, cache_fox_v, cache_fox_logf, state_lru_h, state_conv, cache_mem_k, cache_mem_v, g_mix, w_in, b_f, conv_w, conv_b, w_a, b_a, w_i, b_i, lam, w_out, g_cross, g_mem, w_cq, w_ck, w_cv, w_co, g_moe, w_router, b_router, w_gate, b_gate, w_up, b_up, w_down, b_down, g_final):
    nb_p, seq, _ = x_prompt.shape
    nb_s, t_s, _ = x_sample.shape
    past = cache_fox_k.shape[2]
    n_mem = mem_prompt.shape[1]
    assert nb_p == 1 and g_mix.shape[0] == 1
    n_p, n_s = seq, nb_s * t_s
    row = lambda a: a.reshape(1, -1)

    mw = _prep_mix_weights(g_mix[0], w_in[0], b_f[0], conv_w[0], conv_b[0], w_a[0], b_a[0], w_i[0], b_i[0], lam[0])
    xp = x_prompt.reshape(n_p, D_MODEL)
    xs_ = x_sample.reshape(n_s, D_MODEL)

    zero_prev = jnp.zeros((1, SUBLANES, LRU_WIDTH), F32)
    zero_h = jnp.zeros((1, 1, LRU_WIDTH), F32)
    (qa_p, ka_p, va_p, k_p, v_p, _, lf_p, f2_p, qn_p, kn_p, lru_p, hl_p, ct_p) = _mix_in(
        xp, zero_prev, zero_h, mw, tm=MIX_TILE, streaming=True)
    prev_s = jnp.pad(state_conv[0], ((0, 0), (SUBLANES - (CONV_WIDTH - 1), 0), (0, 0)))
    (qa_s, ka_s, _, k_s, v_s, vb_s, lf_s, _, _, _, lru_s, hl_s, ct_s) = _mix_in(
        xs_, prev_s, state_lru_h[0].reshape(nb_s, 1, LRU_WIDTH), mw, tm=t_s, streaming=False)

    n_past, static_ok = _fox_plan(f2_p, qn_p, kn_p, bq=FOX_BQ, tile=MIX_TILE)
    fo_p = _fox_prompt(n_past, static_ok, qa_p, ka_p, va_p, bq=FOX_BQ)
    by_head = lambda c: jnp.moveaxis(c[0], 1, -1)
    fo_s = _fox_sample(qa_s, ka_s, vb_s, by_head(cache_fox_k), by_head(cache_fox_v), by_head(cache_fox_logf),
                       mw["bias_sel"], t=t_s)

    mk_p, mv_p = _mem_kv(mem_prompt[0], row(g_mem[0]), w_ck[0].astype(BF16), w_cv[0].astype(BF16))

    wr_pad = jnp.pad(w_router[0], ((0, 0), (0, LANES - N_EXPERTS))).astype(BF16)
    br_pad = jnp.pad(b_router[0], (0, LANES - N_EXPERTS)).reshape(1, LANES)
    pw = dict(w_out=w_out[0].astype(BF16), g_cross=row(g_cross[0]), w_cq=w_cq[0].astype(BF16),
              w_co=w_co[0].astype(BF16), g_moe=row(g_moe[0]), w_router=wr_pad, b_router=br_pad)
    cnt0 = jnp.zeros((SUBLANES, LANES), F32)
    h2_p, xn_p, idx_p, tw_p, rk_p, cnt_p = _post(
        xp, fo_p, lru_p, mk_p.reshape(1, n_mem, D_MODEL), mv_p.reshape(1, n_mem, D_MODEL), cnt0, pw,
        tm=POST_TILE, seg=POST_TILE, cached_memory=False)
    h2_s, xn_s, idx_s, tw_s, rk_s, cnt = _post(
        xs_, fo_s, lru_s, cache_mem_k, cache_mem_v, cnt_p, pw, tm=min(SAMPLE_POST_TILE, n_s), seg=t_s,
        cached_memory=True)

    n = n_p + n_s
    idx = jnp.concatenate([idx_p[:, :TOP_K], idx_s[:, :TOP_K]], axis=0)
    rank = jnp.concatenate([rk_p[:, :TOP_K], rk_s[:, :TOP_K]], axis=0)
    counts = cnt[0, :N_EXPERTS].astype(I32)
    padded = (counts + MOE_BM - 1) // MOE_BM * MOE_BM
    pad_end = jnp.cumsum(padded)
    pad_start = pad_end - padded
    dest = (pad_start[idx] + rank).reshape(n * TOP_K)
    n_blocks = -(-(n * TOP_K) // MOE_BM) + N_EXPERTS
    blk_row = jnp.arange(n_blocks, dtype=I32) * MOE_BM
    block_e = jnp.minimum(jnp.sum(pad_end[None, :] <= blk_row[:, None], axis=1), N_EXPERTS - 1).astype(I32)
    n_used = (pad_end[-1] // MOE_BM).reshape(1).astype(I32)
    last_block = jnp.maximum(pad_end // MOE_BM - 1, 0).astype(I32)

    xs_sorted = _dispatch(dest, last_block, n_used, xn_p, xn_s, tm=MOE_BM, n_blocks=n_blocks)
    ew = dict(w_gate=w_gate[0], w_up=w_up[0], w_down=w_down[0],
              b_gate=b_gate[0].reshape(N_EXPERTS, 1, D_MODEL), b_up=b_up[0].reshape(N_EXPERTS, 1, D_MODEL),
              b_down=b_down[0].reshape(N_EXPERTS, 1, D_MODEL))
    valid_rows = jnp.clip((pad_start + counts)[block_e] - blk_row, 0, MOE_BM).astype(I32)
    ys = _experts(block_e, n_used, (pad_end // MOE_BM).astype(I32), valid_rows, xs_sorted, ew, bm=MOE_BM)
    y_p, y_s = _combine(dest, h2_p, h2_s, tw_p, tw_s, row(g_final), ys, tm=COMBINE_TILE)

    shp_p = (1, nb_p, seq, FOX_HEADS, FOX_HEAD_DIM)
    shp_s = (1, nb_s, t_s, FOX_HEADS, FOX_HEAD_DIM)
    tail = slice(SUBLANES - (CONV_WIDTH - 1), SUBLANES)
    return (y_p.reshape(nb_p, seq, D_MODEL), y_s.reshape(nb_s, t_s, D_MODEL),
            k_p.reshape(shp_p), v_p.reshape(shp_p), lf_p.reshape(1, nb_p, seq, FOX_HEADS),
            hl_p.reshape(1, nb_p, LRU_WIDTH), ct_p[:, tail, :].reshape(1, nb_p, CONV_WIDTH - 1, LRU_WIDTH),
            mk_p.reshape(1, nb_p, n_mem, MEM_HEADS, MEM_HEAD_DIM), mv_p.reshape(1, nb_p, n_mem, MEM_HEADS, MEM_HEAD_DIM),
            k_s.reshape(shp_s), v_s.reshape(shp_s), lf_s.reshape(1, nb_s, t_s, FOX_HEADS),
            hl_s.reshape(1, nb_s, LRU_WIDTH), ct_s[:, tail, :].reshape(1, nb_s, CONV_WIDTH - 1, LRU_WIDTH))
```

```python
import functools
import math

import jax
import jax.numpy as jnp
import numpy as np
from jax import lax
from jax.experimental import pallas as pl
from jax.experimental.pallas import tpu as pltpu

F32 = jnp.float32
BF16 = jnp.bfloat16
I32 = jnp.int32

D_MODEL = 1024
FOX_HEADS = 8
FOX_HEAD_DIM = 64
FOX_WIDTH = FOX_HEADS * FOX_HEAD_DIM
LRU_WIDTH = D_MODEL - FOX_WIDTH
LRU_BLOCKS = 8
LRU_C = 8.0
CONV_WIDTH = 4
MEM_HEADS = 4
MEM_HEAD_DIM = D_MODEL // MEM_HEADS
N_EXPERTS = 32
TOP_K = 4
SWIGLU_LIMIT = 7.0
SWIGLU_ALPHA = 1.702
RMS_EPS = 1e-6
NEG_INF = -1e30
LOG2E = math.log2(math.e)

LANES = 128
SUBLANES = 8
CHUNKS = D_MODEL // LANES
TOKEN_TILE = (CHUNKS, SUBLANES, LANES)
AUG = LANES * FOX_HEADS
VMEM_LIMIT = 56 * 1024 * 1024

MIX_TILE = 256
POST_TILE = 512
SAMPLE_POST_TILE = 256
FOX_BQ = 512
MOE_BM = 512
COMBINE_TILE = 256
FOX_SKIP_GAP = 160.0
FOX_STATIC_SHIFT_RANGE = 100.0

C_Q, C_K, C_V, C_F, C_XR, C_G, C_END = 0, 512, 1024, 1536, 1664, 2176, 2688


def _dot(a, b):
    return jnp.dot(a, b, preferred_element_type=F32)


def _dot_nt(a, b):
    return lax.dot_general(a, b, (((1,), (1,)), ((), ())), preferred_element_type=F32)


def _split3(x):
    hi = x.astype(BF16)
    r = x - hi.astype(F32)
    mid = r.astype(BF16)
    lo = (r - mid.astype(F32)).astype(BF16)
    return hi, mid, lo


def _dot3(a, x):
    hi, mid, lo = _split3(x)
    return _dot(a, hi) + _dot(a, mid) + _dot(a, lo)


def _softplus(x):
    return jnp.maximum(x, 0.0) + jnp.log1p(jnp.exp(-jnp.abs(x)))


def _rms(x, g):
    return x * lax.rsqrt(jnp.mean(x * x, axis=-1, keepdims=True) + RMS_EPS) * g


def _tri(n, kind):
    r = lax.broadcasted_iota(I32, (n, n), 0)
    c = lax.broadcasted_iota(I32, (n, n), 1)
    m = {"le": c <= r, "lt": c < r, "gt": c > r}[kind]
    return jnp.where(m, 1.0, 0.0).astype(BF16)


def _mix_in_kernel(x_ref, g_ref, w_ref, bfp_ref, eq_ref, ek_ref, cq_ref, ck_ref, cv_ref, hs_ref, hd_ref, hm_ref,
                   cw_ref, cb_ref, wa_ref, ba_ref, wi_ref, bi_ref, lam_ref, cprev_ref, h0_ref,
                   qa_ref, ka_ref, va_ref, ko_ref, vo_ref, vb_ref, lf_ref, f2_ref, qn_ref, kn_ref,
                   lru_ref, hl_ref, ct_ref,
                   fcar, hcar, xp_ref, sa_ref, sb_ref, kbuf, vbuf, ksem, *, tm, pad, streaming):
    step = pl.program_id(0)
    first = step == 0
    slot = step % 2

    def kv_copies(s, row0):
        rows = pl.ds(row0, tm)
        copies = []
        for h in range(FOX_HEADS):
            copies.append(pltpu.make_async_copy(kbuf.at[s, h], ko_ref.at[rows, h, :], ksem.at[s]))
            copies.append(pltpu.make_async_copy(vbuf.at[s, h], vo_ref.at[rows, h, :], ksem.at[s]))
        return copies

    @pl.when(step >= 2)
    def _():
        for c in kv_copies(slot, 0):
            c.wait()

    if streaming:
        @pl.when(first)
        def _():
            fcar[...] = jnp.zeros_like(fcar)
            hcar[...] = jnp.zeros_like(hcar)
            xp_ref[0:SUBLANES, :] = jnp.zeros((SUBLANES, LRU_WIDTH), F32)
    else:
        fcar[...] = jnp.zeros_like(fcar)
        hcar[...] = jnp.broadcast_to(h0_ref[0], hcar.shape)
        xp_ref[0:SUBLANES, :] = cprev_ref[0]

    x = x_ref[...]
    xn = _rms(x, g_ref[...]).astype(BF16)
    z = _dot(xn, w_ref[...])

    zq, zk, zv = z[:, C_Q:C_K], z[:, C_K:C_V], z[:, C_V:C_F]
    for h in range(FOX_HEADS):
        hc = slice(h * FOX_HEAD_DIM, (h + 1) * FOX_HEAD_DIM)
        kbuf[slot, h] = zk[:, hc]
        vbuf[slot, h] = zv[:, hc]
    for c in kv_copies(slot, pl.multiple_of(step * tm, tm)):
        c.start()
    qd, kd, vd = zq.astype(BF16), zk.astype(BF16), zv.astype(BF16)
    vb_ref[...] = vd

    def head_pad(d):
        blocks = []
        for p in range(FOX_HEADS // 2):
            blk = d[:, p * LANES:(p + 1) * LANES]
            blocks += [blk * hm_ref[0:1, :], blk * hm_ref[1:2, :]]
        return jnp.concatenate(blocks, axis=1)

    lane = lax.broadcasted_iota(I32, (tm, LANES), 1)
    lf = jnp.where(lane < 3 * FOX_HEADS, -_softplus(-(z[:, C_F:C_XR] + bfp_ref[...])), 0.0)
    lf_ref[...] = lf[:, :FOX_HEADS]
    cum = _dot3(_tri(tm, "le"), lf) + fcar[0:1, :]
    fcar[...] = jnp.broadcast_to(cum[tm - 1:tm, :], fcar.shape)
    cum2 = cum * LOG2E
    f2_ref[...] = cum2[:, :FOX_HEADS]
    qf, kf = qd.astype(F32), kd.astype(F32)
    diag = _dot((qf * kf).astype(BF16), hd_ref[...]).astype(BF16)
    hi, mid, lo = _split3(cum2)
    bias = jnp.where(lane < 8, hi, jnp.where(lane < 16, mid, jnp.where(lane < 24, lo, diag)))
    qa_ref[...] = head_pad(qd) + (_dot(bias, eq_ref[...]) + cq_ref[...]).astype(BF16)
    ka_ref[...] = head_pad(kd) + (ck_ref[...] - _dot(bias, ek_ref[...])).astype(BF16)
    va_ref[...] = head_pad(vd) + cv_ref[...].astype(BF16)
    qn_ref[...] = jnp.broadcast_to(jnp.max(_dot((qf * qf).astype(BF16), hs_ref[...]), axis=0, keepdims=True),
                                   qn_ref.shape)
    kn_ref[...] = jnp.broadcast_to(jnp.max(_dot((kf * kf).astype(BF16), hs_ref[...]), axis=0, keepdims=True),
                                   kn_ref.shape)

    xr = z[:, C_XR:C_G]
    xp_ref[SUBLANES:SUBLANES + tm, :] = xr
    xc = cb_ref[...] + xr * cw_ref[CONV_WIDTH - 1:CONV_WIDTH, :]
    for j in range(CONV_WIDTH - 1):
        sh = CONV_WIDTH - 1 - j
        xc = xc + xp_ref[SUBLANES - sh:SUBLANES - sh + tm, :] * cw_ref[j:j + 1, :]
    tail = xp_ref[tm:tm + SUBLANES, :]
    ct_ref[0] = tail
    xp_ref[0:SUBLANES, :] = tail

    xcb = xc.astype(BF16)
    r = jax.nn.sigmoid(_dot(xcb, wa_ref[...]) + ba_ref[...])
    ig = jax.nn.sigmoid(_dot(xcb, wi_ref[...]) + bi_ref[...])
    log_a = (-LRU_C) * r * _softplus(-lam_ref[...])
    a = jnp.exp(log_a)
    mult = jnp.sqrt(-jnp.tanh(log_a) * (a * a + 1.0))
    if streaming:
        row = lax.broadcasted_iota(I32, (tm, LRU_WIDTH), 0)
        mult = jnp.where(jnp.logical_and(row == 0, first), 1.0, mult)
    b = mult * ig * xc

    sa_ref[0:pad, :] = jnp.ones((pad, LRU_WIDTH), F32)
    sb_ref[0:pad, :] = jnp.zeros((pad, LRU_WIDTH), F32)
    d = 1
    while d < tm:
        sa_ref[pad:pad + tm, :] = a
        sb_ref[pad:pad + tm, :] = b
        b = a * sb_ref[pad - d:pad - d + tm, :] + b
        a = a * sa_ref[pad - d:pad - d + tm, :]
        d *= 2
    h = a * hcar[0:1, :] + b
    hlast = h[tm - 1:tm, :]
    hcar[...] = jnp.broadcast_to(hlast, hcar.shape)
    hl_ref[0] = hlast
    lru_ref[...] = (h * jax.nn.gelu(z[:, C_G:C_END])).astype(BF16)

    last = pl.num_programs(0) - 1

    @pl.when(jnp.logical_and(step == last, step >= 1))
    def _():
        for c in kv_copies(1 - slot, 0):
            c.wait()

    @pl.when(step == last)
    def _():
        for c in kv_copies(slot, 0):
            c.wait()


def _mix_in(x, cprev, h0, wts, *, tm, streaming):
    n = x.shape[0]
    steps = n // tm
    nseg = 1 if streaming else steps
    pad = max(tm // 2, SUBLANES)
    seg = (lambda i: (0, 0, 0)) if streaming else (lambda i: (i, 0, 0))
    full = lambda a: pl.BlockSpec(a.shape, lambda i: (0,) * a.ndim)
    rows = lambda w: pl.BlockSpec((tm, w), lambda i: (i, 0))
    names = ("g_mix", "w_all", "bf_pad", "eq", "ek", "cq", "ck", "cv", "hsel", "hdiag", "hmask", "conv_w", "conv_b",
             "wa", "ba", "wi", "bi", "lam")
    ws = [wts[k] for k in names]
    sds = jax.ShapeDtypeStruct
    out_shape = (
        sds((n, AUG), BF16), sds((n, AUG), BF16), sds((n, AUG), BF16),
        sds((n, FOX_HEADS, FOX_HEAD_DIM), F32), sds((n, FOX_HEADS, FOX_HEAD_DIM), F32), sds((n, FOX_WIDTH), BF16),
        sds((n, FOX_HEADS), F32), sds((n, FOX_HEADS), F32),
        sds((steps * SUBLANES, LANES), F32), sds((steps * SUBLANES, LANES), F32),
        sds((n, LRU_WIDTH), BF16),
        sds((nseg, 1, LRU_WIDTH), F32), sds((nseg, SUBLANES, LRU_WIDTH), F32),
    )
    out_specs = (
        rows(AUG), rows(AUG), rows(AUG), pl.BlockSpec(memory_space=pl.ANY), pl.BlockSpec(memory_space=pl.ANY),
        rows(FOX_WIDTH), rows(FOX_HEADS), rows(FOX_HEADS),
        pl.BlockSpec((SUBLANES, LANES), lambda i: (i, 0)), pl.BlockSpec((SUBLANES, LANES), lambda i: (i, 0)),
        rows(LRU_WIDTH),
        pl.BlockSpec((1, 1, LRU_WIDTH), seg), pl.BlockSpec((1, SUBLANES, LRU_WIDTH), seg),
    )
    in_specs = [rows(D_MODEL)] + [full(w) for w in ws] + [
        pl.BlockSpec((1, SUBLANES, LRU_WIDTH), seg), pl.BlockSpec((1, 1, LRU_WIDTH), seg)]
    return pl.pallas_call(
        functools.partial(_mix_in_kernel, tm=tm, pad=pad, streaming=streaming),
        grid=(steps,), in_specs=in_specs, out_specs=out_specs, out_shape=out_shape,
        scratch_shapes=[
            pltpu.VMEM((SUBLANES, LANES), F32), pltpu.VMEM((SUBLANES, LRU_WIDTH), F32),
            pltpu.VMEM((tm + SUBLANES, LRU_WIDTH), F32),
            pltpu.VMEM((pad + tm, LRU_WIDTH), F32), pltpu.VMEM((pad + tm, LRU_WIDTH), F32),
            pltpu.VMEM((2, FOX_HEADS, tm, FOX_HEAD_DIM), F32), pltpu.VMEM((2, FOX_HEADS, tm, FOX_HEAD_DIM), F32),
            pltpu.SemaphoreType.DMA((2,))],
        compiler_params=pltpu.CompilerParams(dimension_semantics=("arbitrary",), vmem_limit_bytes=VMEM_LIMIT),
        name="mix_in_stream" if streaming else "mix_in_segments",
    )(x, *ws, cprev, h0)


def _fox_prompt_kernel(nb_ref, st_ref, q_ref, k_ref, v_ref, o_ref, m_ref, acc_ref, *, bq):
    qi = pl.program_id(1)
    plan = pl.program_id(0) * pl.num_programs(1) + qi
    n_past = nb_ref[plan]
    cols = [slice(LANES * c, LANES * (c + 1)) for c in range(2)]
    qs = [q_ref[:, cols[c]] for c in range(2)]
    acc_ref[...] = jnp.zeros(acc_ref.shape, F32)

    def scores(c, start, causal):
        s = _dot_nt(qs[c], k_ref[pl.ds(start, bq), cols[c]])
        if causal:
            r = lax.broadcasted_iota(I32, (bq, bq), 0)
            cc = lax.broadcasted_iota(I32, (bq, bq), 1)
            s = jnp.where(cc <= r, s, NEG_INF)
        return s

    def static_block(start, causal):
        for c in range(2):
            p = jnp.exp2(scores(c, start, causal)).astype(BF16)
            acc_ref[c] += _dot(p, v_ref[pl.ds(start, bq), cols[c]])

    def online_block(start, causal):
        for c in range(2):
            s = scores(c, start, causal)
            m_old = m_ref[c]
            m_new = jnp.maximum(m_old, jnp.max(s, axis=1, keepdims=True))
            p = jnp.exp2(s - m_new).astype(BF16)
            acc_ref[c] = jnp.exp2(m_old - m_new) * acc_ref[c] + _dot(p, v_ref[pl.ds(start, bq), cols[c]])
            m_ref[c] = m_new

    def run(block):
        block(pl.multiple_of(qi * bq, bq), True)

        def body(t, carry):
            block(pl.multiple_of((qi - 1 - t) * bq, bq), False)
            return carry

        lax.fori_loop(0, n_past, body, 0)

    @pl.when(st_ref[plan] == 1)
    def _():
        run(static_block)

    @pl.when(st_ref[plan] != 1)
    def _():
        m_ref[...] = jnp.full(m_ref.shape, NEG_INF, F32)
        run(online_block)

    a0, a1 = acc_ref[0], acc_ref[1]
    o0 = a0 / a0[:, FOX_HEAD_DIM:FOX_HEAD_DIM + 1]
    o1 = a1 / a1[:, 0:1]
    lane = lax.broadcasted_iota(I32, (bq, LANES), 1)
    o_ref[...] = jnp.where(lane < FOX_HEAD_DIM, o0, o1).astype(BF16)


def _fox_prompt(n_past, static_ok, qa, ka, va, *, bq):
    s = qa.shape[0]
    pairs = FOX_HEADS // 2
    return pl.pallas_call(
        functools.partial(_fox_prompt_kernel, bq=bq),
        grid_spec=pltpu.PrefetchScalarGridSpec(
            num_scalar_prefetch=2, grid=(pairs, s // bq),
            in_specs=[pl.BlockSpec((bq, 2 * LANES), lambda p, i, nb, st: (i, p)),
                      pl.BlockSpec((s, 2 * LANES), lambda p, i, nb, st: (0, p)),
                      pl.BlockSpec((s, 2 * LANES), lambda p, i, nb, st: (0, p))],
            out_specs=pl.BlockSpec((bq, LANES), lambda p, i, nb, st: (i, p)),
            scratch_shapes=[pltpu.VMEM((2, bq, 1), F32), pltpu.VMEM((2, bq, LANES), F32)]),
        out_shape=jax.ShapeDtypeStruct((s, FOX_WIDTH), BF16),
        compiler_params=pltpu.CompilerParams(dimension_semantics=("arbitrary", "arbitrary"),
                                             vmem_limit_bytes=VMEM_LIMIT),
        name="fox_prompt",
    )(n_past, static_ok, qa, ka, va)


def _fox_plan(f2, qn, kn, *, bq, tile):
    s = f2.shape[0]
    nq = s // bq
    per = lambda a: a.reshape(s // tile, SUBLANES, LANES)[:, 0, :FOX_HEADS]
    qnorm = jnp.sqrt(jnp.max(per(qn).reshape(nq, bq // tile, FOX_HEADS), axis=1))
    knorm = jnp.sqrt(jnp.max(per(kn), axis=0))
    spread = 2.0 * 1.02 * qnorm * knorm[None, :]
    thr = spread + FOX_SKIP_GAP
    f_first = f2[0::bq]
    f_last = f2[bq - 1::bq]
    gap = f_last[None, :, :] - f_first[:, None, :]
    before = (jnp.arange(nq)[None, :] < jnp.arange(nq)[:, None])[:, :, None]
    need = jnp.sum(jnp.logical_and(before, gap < thr[:, None, :]), axis=1)
    pair = lambda a: a.reshape(nq, FOX_HEADS // 2, 2)
    n_past = jnp.max(pair(need), axis=2).T.reshape(-1).astype(I32)
    static_ok = jnp.all(pair(spread) <= FOX_STATIC_SHIFT_RANGE, axis=2).T.reshape(-1).astype(I32)
    return n_past, static_ok


def _fox_sample_kernel(q_ref, kn_ref, vn_ref, kt_ref, vt_ref, lf_ref, sel_ref, o_ref, g_ref,
                       *, t, past, chunk):
    car = jnp.zeros((FOX_HEADS, 1), F32)
    later = _tri(chunk, "lt")
    for ci in reversed(range(past // chunk)):
        lf = lf_ref[0, :, ci * chunk:(ci + 1) * chunk]
        hi, mid, lo = _split3(lf)
        g_ref[:, ci * chunk:(ci + 1) * chunk] = _dot(hi, later) + _dot(mid, later) + _dot(lo, later) + car
        car = car + jnp.sum(lf, axis=1, keepdims=True)
    hi, mid, lo = _split3(g_ref[...] * LOG2E)
    bias_rows = jnp.concatenate([hi.astype(F32), mid.astype(F32), lo.astype(F32),
                                 jnp.ones((FOX_HEADS, past), F32)], axis=0).astype(BF16)

    def swap_halves(x):
        return pltpu.roll(x.astype(F32), FOX_HEAD_DIM, axis=1).astype(BF16)

    r = lax.broadcasted_iota(I32, (t, t), 0)
    cc = lax.broadcasted_iota(I32, (t, t), 1)
    prow = lax.broadcasted_iota(I32, (FOX_HEAD_DIM, LANES), 0)
    plane = lax.broadcasted_iota(I32, (FOX_HEAD_DIM, LANES), 1)
    place = [jnp.where(plane == prow + FOX_HEAD_DIM * c, 1.0, 0.0).astype(BF16) for c in range(2)]
    for p in range(FOX_HEADS // 2):
        pc = slice(LANES * p, LANES * (p + 1))
        vpair = vn_ref[:, pc]
        o_pair = jnp.zeros((t, LANES), F32)
        for c in range(2):
            h = 2 * p + c
            hc = slice(LANES * h, LANES * (h + 1))
            q, kn, vn = q_ref[:, hc], kn_ref[:, hc], vpair
            if c == 1:
                q, kn, vn = swap_halves(q), swap_halves(kn), swap_halves(vn)
            bias = _dot(sel_ref[h], bias_rows).astype(BF16)
            kpast_t = jnp.concatenate([kt_ref[0, h].astype(BF16), bias], axis=0)
            sp = _dot(q, kpast_t)
            sn = jnp.where(cc <= r, _dot_nt(q, kn), NEG_INF)
            m = jnp.maximum(jnp.max(sp, axis=1, keepdims=True), jnp.max(sn, axis=1, keepdims=True))
            pp = jnp.exp2(sp - m)
            pn = jnp.exp2(sn - m)
            l = jnp.sum(pp, axis=1, keepdims=True) + jnp.sum(pn, axis=1, keepdims=True)
            o = (_dot_nt(pp.astype(BF16), vt_ref[0, h].astype(BF16))
                 + _dot(pn.astype(BF16), vn[:, :FOX_HEAD_DIM])) / l
            o_pair = o_pair + _dot(o.astype(BF16), place[c])
        o_ref[:, pc] = o_pair.astype(BF16)


def _fox_sample(qa, ka, vb, cache_kt, cache_vt, cache_lft, sel, *, t):
    nb, past = cache_kt.shape[0], cache_kt.shape[3]
    full = lambda a: pl.BlockSpec(a.shape, lambda b: (0,) * a.ndim)
    slab = pl.BlockSpec((1, FOX_HEADS, FOX_HEAD_DIM, past), lambda b: (b, 0, 0, 0))
    return pl.pallas_call(
        functools.partial(_fox_sample_kernel, t=t, past=past, chunk=256),
        grid=(nb,),
        in_specs=[pl.BlockSpec((t, AUG), lambda b: (b, 0)), pl.BlockSpec((t, AUG), lambda b: (b, 0)),
                  pl.BlockSpec((t, FOX_WIDTH), lambda b: (b, 0)), slab, slab,
                  pl.BlockSpec((1, FOX_HEADS, past), lambda b: (b, 0, 0)), full(sel)],
        out_specs=pl.BlockSpec((t, FOX_WIDTH), lambda b: (b, 0)),
        out_shape=jax.ShapeDtypeStruct((nb * t, FOX_WIDTH), BF16),
        scratch_shapes=[pltpu.VMEM((FOX_HEADS, past), F32)],
        compiler_params=pltpu.CompilerParams(dimension_semantics=("arbitrary",), vmem_limit_bytes=VMEM_LIMIT),
        name="fox_sample",
    )(qa, ka, vb, cache_kt, cache_vt, cache_lft, sel)


def _mem_kv_kernel(m_ref, g_ref, wk_ref, wv_ref, k_ref, v_ref):
    mn = _rms(m_ref[...], g_ref[...]).astype(BF16)
    k_ref[...] = _dot(mn, wk_ref[...])
    v_ref[...] = _dot(mn, wv_ref[...])


def _mem_kv(mem, g, wk, wv):
    n = mem.shape[0]
    return pl.pallas_call(
        _mem_kv_kernel,
        out_shape=(jax.ShapeDtypeStruct((n, D_MODEL), F32), jax.ShapeDtypeStruct((n, D_MODEL), F32)),
        compiler_params=pltpu.CompilerParams(vmem_limit_bytes=VMEM_LIMIT),
        name="mem_kv",
    )(mem, g, wk, wv)


def _post_kernel(h_ref, fo_ref, lru_ref, wo_ref, gc_ref, wcq_ref, mk_ref, mv_ref, wco_ref,
                 gm_ref, wr_ref, br_ref, cnt_in_ref,
                 h2_ref, xn_ref, idx_ref, tw_ref, rank_ref, cnt_ref, car_ref, *mem_scratch, tm, seg, cached_memory):
    step = pl.program_id(0)
    nseg = tm // seg

    @pl.when(step == 0)
    def _():
        car_ref[...] = cnt_in_ref[...]

    if cached_memory:
        mbuf, msem = mem_scratch
        slot = step % 2

        def mem_copies(at_step, s):
            return [pltpu.make_async_copy(ref.at[0, at_step * nseg + j, :, hd, :], mbuf.at[s, j, t, hd], msem.at[s])
                    for j in range(nseg) for t, ref in enumerate((mk_ref, mv_ref)) for hd in range(MEM_HEADS)]

        @pl.when(step == 0)
        def _():
            for c in mem_copies(0, 0):
                c.start()

        @pl.when(step + 1 < pl.num_programs(0))
        def _():
            for c in mem_copies(step + 1, 1 - slot):
                c.start()

    h1 = h_ref[...] + _dot(fo_ref[...], wo_ref[0:FOX_WIDTH, :]) + _dot(lru_ref[...], wo_ref[FOX_WIDTH:D_MODEL, :])

    q = _dot(_rms(h1, gc_ref[...]).astype(BF16), wcq_ref[...])
    if cached_memory:
        for c in mem_copies(0, slot):
            c.wait()
    segs = []
    for j in range(nseg):
        rows = slice(j * seg, (j + 1) * seg)
        heads = []
        for hd in range(MEM_HEADS):
            hc = slice(MEM_HEAD_DIM * hd, MEM_HEAD_DIM * (hd + 1))
            mk = mbuf[slot, j, 0, hd] if cached_memory else mk_ref[0, :, hc]
            mv = mbuf[slot, j, 1, hd] if cached_memory else mv_ref[0, :, hc]
            s = _dot_nt(q[rows, hc].astype(BF16), mk.astype(BF16)) * (MEM_HEAD_DIM ** -0.5)
            p = jnp.exp(s - jnp.max(s, axis=1, keepdims=True))
            o = _dot(p.astype(BF16), mv.astype(BF16)) / jnp.sum(p, axis=1, keepdims=True)
            heads.append(o.astype(BF16))
        segs.append(jnp.concatenate(heads, axis=1))
    h2 = h1 + _dot(segs[0] if nseg == 1 else jnp.concatenate(segs, axis=0), wco_ref[...])
    h2_ref[...] = h2

    xn = _rms(h2, gm_ref[...])
    _rows_to_tiles(xn_ref, xn, tm)

    lane = lax.broadcasted_iota(I32, (tm, LANES), 1).astype(F32)
    logits = jnp.where(lane < N_EXPERTS, _dot(xn.astype(BF16), wr_ref[...]) + br_ref[...], -jnp.inf)
    vals, idxs = [], []
    for _ in range(TOP_K):
        mx = jnp.max(logits, axis=1, keepdims=True)
        ix = jnp.min(jnp.where(logits == mx, lane, float(LANES)), axis=1, keepdims=True)
        vals.append(mx)
        idxs.append(ix)
        logits = jnp.where(lane == ix, -jnp.inf, logits)
    es = [jnp.exp(v - vals[0]) for v in vals]
    den = es[0] + es[1] + es[2] + es[3]

    onehot = jnp.zeros((tm, LANES), F32)
    for ix in idxs:
        onehot = onehot + jnp.where(lane == ix, 1.0, 0.0)
    before = _dot(_tri(tm, "lt"), onehot.astype(BF16)) + car_ref[0:1, :]
    car = car_ref[0:1, :] + jnp.sum(onehot, axis=0, keepdims=True)
    car_ref[...] = jnp.broadcast_to(car, car_ref.shape)
    cnt_ref[...] = jnp.broadcast_to(car, cnt_ref.shape)

    idx_o = jnp.zeros((tm, LANES), F32)
    tw_o = jnp.zeros((tm, LANES), F32)
    rk_o = jnp.zeros((tm, LANES), F32)
    for j in range(TOP_K):
        rk = jnp.sum(jnp.where(lane == idxs[j], before, 0.0), axis=1, keepdims=True)
        idx_o = jnp.where(lane == j, idxs[j], idx_o)
        tw_o = jnp.where(lane == j, es[j] / den, tw_o)
        rk_o = jnp.where(lane == j, rk, rk_o)
    idx_ref[...] = idx_o.astype(I32)
    tw_ref[...] = tw_o
    rank_ref[...] = rk_o.astype(I32)


def _post(h, fo, lru, mk, mv, cnt_in, wts, *, tm, seg, cached_memory):
    n = h.shape[0]
    full = lambda a: pl.BlockSpec(a.shape, lambda i: (0,) * a.ndim)
    rows = lambda w: pl.BlockSpec((tm, w), lambda i: (i, 0))
    if cached_memory:
        nm = mk.shape[2]
        mem = pl.BlockSpec(memory_space=pl.ANY)
        mem_scratch = [pltpu.VMEM((2, tm // seg, 2, MEM_HEADS, nm, MEM_HEAD_DIM), F32),
                       pltpu.SemaphoreType.DMA((2,))]
    else:
        mem = full(mk)
        mem_scratch = []
    w = [wts[k] for k in ("w_out", "g_cross", "w_cq")]
    w2 = [wts[k] for k in ("w_co", "g_moe", "w_router", "b_router")]
    return pl.pallas_call(
        functools.partial(_post_kernel, tm=tm, seg=seg, cached_memory=cached_memory),
        grid=(n // tm,),
        in_specs=[rows(D_MODEL), rows(FOX_WIDTH), rows(LRU_WIDTH)] + [full(a) for a in w] + [mem, mem]
                 + [full(a) for a in w2] + [full(cnt_in)],
        out_specs=(rows(D_MODEL), pl.BlockSpec((tm // SUBLANES,) + TOKEN_TILE, lambda i: (i, 0, 0, 0)),
                   rows(LANES), rows(LANES), rows(LANES),
                   pl.BlockSpec((SUBLANES, LANES), lambda i: (0, 0))),
        out_shape=(jax.ShapeDtypeStruct((n, D_MODEL), F32), jax.ShapeDtypeStruct((n // SUBLANES,) + TOKEN_TILE, F32),
                   jax.ShapeDtypeStruct((n, LANES), I32), jax.ShapeDtypeStruct((n, LANES), F32),
                   jax.ShapeDtypeStruct((n, LANES), I32), jax.ShapeDtypeStruct((SUBLANES, LANES), F32)),
        scratch_shapes=[pltpu.VMEM((SUBLANES, LANES), F32)] + mem_scratch,
        compiler_params=pltpu.CompilerParams(dimension_semantics=("arbitrary",), vmem_limit_bytes=VMEM_LIMIT),
        name="post_batch_mem" if cached_memory else "post_shared_mem",
    )(h, fo, lru, *w, mk, mv, *w2, cnt_in)


STAGES = 3


def _dispatch_kernel(dest_ref, last_ref, nu_ref, xp_ref, xs_ref, out_ref, zero_ref, stage, sems, gsem, zsem,
                     *, tm, steps_p, n_blocks):
    i = pl.program_id(0)

    def zero_copy(b):
        return pltpu.make_async_copy(zero_ref, out_ref.at[pl.ds(pl.multiple_of(b * tm, tm), tm)], zsem)

    @pl.when(i == 0)
    def _():
        zero_ref[...] = jnp.zeros(zero_ref.shape, F32)
        n_tail = n_blocks - nu_ref[0]

        def start_e(e, c):
            zero_copy(last_ref[e]).start()
            return c

        def start_t(b, c):
            zero_copy(nu_ref[0] + b).start()
            return c

        def wait_one(b, c):
            zero_copy(0).wait()
            return c

        lax.fori_loop(0, N_EXPERTS, start_e, 0)
        lax.fori_loop(0, n_tail, start_t, 0)
        lax.fori_loop(0, N_EXPERTS + n_tail, wait_one, 0)

    groups = tm // SUBLANES
    steps = pl.num_programs(0)

    def stage_copies(src_ref, tile, slot):
        return [pltpu.make_async_copy(src_ref.at[pl.ds(tile * groups, groups), :, u, :], stage.at[slot, :, u],
                                      gsem.at[slot]) for u in range(SUBLANES)]

    def start_stage(tile, slot):
        @pl.when(tile < steps_p)
        def _():
            for c in stage_copies(xp_ref, tile, slot):
                c.start()

        @pl.when(tile >= steps_p)
        def _():
            for c in stage_copies(xs_ref, tile - steps_p, slot):
                c.start()

    @pl.when(i == 0)
    def _():
        start_stage(0, 0)

    @pl.when(i + 1 < steps)
    def _():
        start_stage(i + 1, (i + 1) % STAGES)

    slot = i % STAGES
    for c in stage_copies(xp_ref, 0, slot):
        c.wait()

    base = i * (tm * TOP_K)

    def start(g, c):
        for u in range(SUBLANES):
            for j in range(TOP_K):
                d = dest_ref[base + (g * SUBLANES + u) * TOP_K + j]
                pltpu.make_async_copy(stage.at[slot, g, u], out_ref.at[d],
                                      sems.at[i % 2]).start(priority=j % 2)
        return c

    lax.fori_loop(0, groups, start, 0)

    def drain(parity):
        for _ in range(TOP_K):
            pltpu.make_async_copy(out_ref.at[pl.ds(0, tm)], out_ref.at[pl.ds(0, tm)], sems.at[parity]).wait()

    @pl.when(i > 0)
    def _():
        drain((i - 1) % 2)

    @pl.when(i == steps - 1)
    def _():
        drain(i % 2)


def _dispatch(dest_flat, last_block, n_used, xn_p, xn_s, *, tm, n_blocks):
    steps_p, steps_s = xn_p.shape[0] * SUBLANES // tm, xn_s.shape[0] * SUBLANES // tm
    return pl.pallas_call(
        functools.partial(_dispatch_kernel, tm=tm, steps_p=steps_p, n_blocks=n_blocks),
        grid_spec=pltpu.PrefetchScalarGridSpec(
            num_scalar_prefetch=3, grid=(steps_p + steps_s,),
            in_specs=[pl.BlockSpec(memory_space=pl.ANY), pl.BlockSpec(memory_space=pl.ANY)],
            out_specs=pl.BlockSpec(memory_space=pl.ANY),
            scratch_shapes=[pltpu.VMEM((tm, CHUNKS, LANES), F32),
                            pltpu.VMEM((STAGES, tm // SUBLANES, SUBLANES, CHUNKS, LANES), F32),
                            pltpu.SemaphoreType.DMA((2,)), pltpu.SemaphoreType.DMA((STAGES,)),
                            pltpu.SemaphoreType.DMA(())]),
        out_shape=jax.ShapeDtypeStruct((n_blocks * tm, CHUNKS, LANES), F32),
        compiler_params=pltpu.CompilerParams(dimension_semantics=("arbitrary",), vmem_limit_bytes=VMEM_LIMIT),
        name="moe_dispatch",
    )(dest_flat, last_block, n_used, xn_p, xn_s)


def _rows_from_tiles(view, rows):
    return jnp.concatenate([view[:, c].reshape(rows, LANES) for c in range(CHUNKS)], axis=1)


def _rows_to_tiles(view, val, rows):
    for c in range(CHUNKS):
        view[:, c] = val[:, c * LANES:(c + 1) * LANES].reshape(rows // SUBLANES, SUBLANES, LANES)


def _expert_kernel(be_ref, nu_ref, nx_ref, re_ref, xs_ref, wg_ref, bg_ref, wu_ref, bu_ref, wd_ref, bd_ref, ys_ref,
                   wgb, wub, wdb, wbuf, xbuf, ybuf, seq, wsem, xsem, ysem, *, bm):
    i = pl.program_id(0)
    steps = pl.num_programs(0)
    groups = bm // SUBLANES
    slot = i % 2
    prev = be_ref[jnp.maximum(i - 1, 0)]
    fresh = jnp.logical_or(i == 0, be_ref[i] != prev)
    live = i < nu_ref[0]

    def w_copies(e, s):
        return [pltpu.make_async_copy(w_ref.at[e], wbuf.at[s, k], wsem.at[s])
                for k, w_ref in enumerate((wg_ref, wu_ref, wd_ref))]

    @pl.when(i == 0)
    def _():
        seq[0] = 0
        for c in w_copies(be_ref[0], 0):
            c.start()

    def in_copies(blk, s):
        g0 = pl.multiple_of(blk * groups, groups)
        return [pltpu.make_async_copy(xs_ref.at[pl.ds(g0, groups), u], xbuf.at[s, :, :, u, :], xsem.at[s])
                for u in range(SUBLANES)]

    def out_copies(blk, s):
        g0 = pl.multiple_of(blk * groups, groups)
        return [pltpu.make_async_copy(ybuf.at[s, :, :, u, :], ys_ref.at[pl.ds(g0, groups), u], ysem.at[s])
                for u in range(SUBLANES)]

    @pl.when(i == 0)
    def _():
        for c in in_copies(0, 0):
            c.start()

    @pl.when(i + 1 < nu_ref[0])
    def _():
        for c in in_copies(i + 1, (i + 1) % 2):
            c.start()

    @pl.when(i >= 2)
    def _():
        for c in out_copies(0, slot):
            c.wait()

    @pl.when(jnp.logical_and(live, fresh))
    def _():
        s = seq[0] % 2
        for c in w_copies(0, s):
            c.wait()
        for static_s in range(2):
            @pl.when(s == static_s)
            def _():
                wgb[...] = wbuf[static_s, 0].astype(BF16)
                wub[...] = wbuf[static_s, 1].astype(BF16)
                wdb[...] = wbuf[static_s, 2].astype(BF16)

        nxt = nx_ref[be_ref[i]]

        @pl.when(nxt < nu_ref[0])
        def _():
            for c in w_copies(be_ref[jnp.minimum(nxt, steps - 1)], 1 - s):
                c.start()

        seq[0] = seq[0] + 1

    def mlp(first_group, n_groups):
        rows = n_groups * SUBLANES
        tiles = pl.ds(first_group, n_groups)
        x = _rows_from_tiles(xbuf.at[slot, tiles], rows).astype(BF16)
        g = jnp.minimum(_dot(x, wgb[...]) + bg_ref[0], SWIGLU_LIMIT)
        u = jnp.clip(_dot(x, wub[...]) + bu_ref[0], -SWIGLU_LIMIT, SWIGLU_LIMIT)
        hdn = g * jax.nn.sigmoid(SWIGLU_ALPHA * g) * (u + 1.0)
        _rows_to_tiles(ybuf.at[slot, tiles], _dot(hdn.astype(BF16), wdb[...]) + bd_ref[0], rows)

    half_only = re_ref[be_ref[i]] - i * bm <= bm // 2

    @pl.when(live)
    def _():
        for c in in_copies(0, slot):
            c.wait()

    @pl.when(jnp.logical_and(live, jnp.logical_not(half_only)))
    def _():
        mlp(0, groups)

    @pl.when(jnp.logical_and(live, half_only))
    def _():
        mlp(0, groups // 2)
        ybuf[slot, groups // 2:groups] = jnp.zeros((groups // 2,) + TOKEN_TILE, F32)

    @pl.when(jnp.logical_not(live))
    def _():
        ybuf[slot] = jnp.zeros(ybuf.shape[1:], F32)

    for c in out_copies(i, slot):
        c.start()

    @pl.when(jnp.logical_and(i == steps - 1, i >= 1))
    def _():
        for c in out_copies(0, 1 - slot):
            c.wait()

    @pl.when(i == steps - 1)
    def _():
        for c in out_copies(0, slot):
            c.wait()


def _experts(block_e, n_used, next_first, real_end, xs, wts, *, bm):
    p = xs.shape[0]
    grouped = (p // SUBLANES, SUBLANES, CHUNKS, LANES)
    anyspace = pl.BlockSpec(memory_space=pl.ANY)
    bspec = pl.BlockSpec((1, 1, D_MODEL), lambda i, be, nu, nx, vr: (be[i], 0, 0))
    tiles = (2, bm // SUBLANES) + TOKEN_TILE
    ys = pl.pallas_call(
        functools.partial(_expert_kernel, bm=bm),
        grid_spec=pltpu.PrefetchScalarGridSpec(
            num_scalar_prefetch=4, grid=(p // bm,),
            in_specs=[anyspace, anyspace, bspec, anyspace, bspec, anyspace, bspec],
            out_specs=anyspace,
            scratch_shapes=[pltpu.VMEM((D_MODEL, D_MODEL), BF16)] * 3 + [
                pltpu.VMEM((2, 3, D_MODEL, D_MODEL), F32), pltpu.VMEM(tiles, F32), pltpu.VMEM(tiles, F32),
                pltpu.SMEM((1,), I32),
                pltpu.SemaphoreType.DMA((2,)), pltpu.SemaphoreType.DMA((2,)), pltpu.SemaphoreType.DMA((2,))]),
        out_shape=jax.ShapeDtypeStruct(grouped, F32),
        compiler_params=pltpu.CompilerParams(dimension_semantics=("arbitrary",), vmem_limit_bytes=VMEM_LIMIT),
        name="moe_experts",
    )(block_e, n_used, next_first, real_end, xs.reshape(grouped), wts["w_gate"], wts["b_gate"], wts["w_up"], wts["b_up"],
      wts["w_down"], wts["b_down"])
    return ys.reshape(p, CHUNKS, LANES)


def _combine_kernel(dest_ref, hp_ref, hs_ref, twp_ref, tws_ref, gf_ref, ys_ref, yp_ref, ysm_ref, buf, sems,
                    *, tm, steps_p):
    i = pl.program_id(0)
    slot = i % 2

    def issue(tile, into):
        base = tile * (tm * TOP_K)

        def start(g, c):
            for u in range(SUBLANES):
                for j in range(TOP_K):
                    d = dest_ref[base + (g * SUBLANES + u) * TOP_K + j]
                    pltpu.make_async_copy(ys_ref.at[d], buf.at[into, j, g, :, u, :],
                                          sems.at[into]).start(priority=j % 2)
            return c

        lax.fori_loop(0, tm // SUBLANES, start, 0)

    @pl.when(i == 0)
    def _():
        issue(0, 0)

    @pl.when(i + 1 < pl.num_programs(0))
    def _():
        issue(i + 1, (i + 1) % 2)

    for j in range(TOP_K):
        pltpu.make_async_copy(ys_ref.at[pl.ds(0, tm)], ys_ref.at[pl.ds(0, tm)], sems.at[slot]).wait()

    def finish(h_ref, tw_ref, y_ref):
        tw = tw_ref[...]
        y = h_ref[...]
        for j in range(TOP_K):
            y = y + _rows_from_tiles(buf.at[slot, j], tm) * tw[:, j:j + 1]
        y_ref[...] = _rms(y, gf_ref[...])

    @pl.when(i < steps_p)
    def _():
        finish(hp_ref, twp_ref, yp_ref)

    @pl.when(i >= steps_p)
    def _():
        finish(hs_ref, tws_ref, ysm_ref)


def _combine(dest_flat, h2_p, h2_s, tw_p, tw_s, g_final, ys, *, tm):
    steps_p, steps_s = h2_p.shape[0] // tm, h2_s.shape[0] // tm
    pblk = lambda w: pl.BlockSpec((tm, w), lambda i, d: (jnp.minimum(i, steps_p - 1), 0))
    sblk = lambda w: pl.BlockSpec((tm, w), lambda i, d: (jnp.maximum(i - steps_p, 0), 0))
    return pl.pallas_call(
        functools.partial(_combine_kernel, tm=tm, steps_p=steps_p),
        grid_spec=pltpu.PrefetchScalarGridSpec(
            num_scalar_prefetch=1, grid=(steps_p + steps_s,),
            in_specs=[pblk(D_MODEL), sblk(D_MODEL), pblk(LANES), sblk(LANES),
                      pl.BlockSpec((1, D_MODEL), lambda i, d: (0, 0)),
                      pl.BlockSpec(memory_space=pl.ANY)],
            out_specs=(pblk(D_MODEL), sblk(D_MODEL)),
            scratch_shapes=[pltpu.VMEM((2, TOP_K, tm // SUBLANES) + TOKEN_TILE, F32),
                            pltpu.SemaphoreType.DMA((2,))]),
        out_shape=(jax.ShapeDtypeStruct(h2_p.shape, F32), jax.ShapeDtypeStruct(h2_s.shape, F32)),
        compiler_params=pltpu.CompilerParams(dimension_semantics=("arbitrary",), vmem_limit_bytes=VMEM_LIMIT),
        name="moe_combine",
    )(dest_flat, h2_p, h2_s, tw_p, tw_s, g_final, ys)


def _aug_layout():
    main, extra = [], []
    for h in range(FOX_HEADS):
        even = h % 2 == 0
        main.append(LANES * h + (0 if even else FOX_HEAD_DIM))
        extra.append(LANES * h + (FOX_HEAD_DIM if even else 0))
    return main, extra


def _prep_mix_weights(g_mix, w_in, b_f, conv_w, conv_b, w_a, b_a, w_i, b_i, lam):
    _, extra = _aug_layout()
    wq = w_in[:, 0:FOX_WIDTH] * (FOX_HEAD_DIM ** -0.5 * LOG2E)
    wf = w_in[:, 3 * FOX_WIDTH:3 * FOX_WIDTH + FOX_HEADS]
    wf_pad = jnp.concatenate([wf, wf, wf, jnp.zeros((D_MODEL, LANES - 3 * FOX_HEADS), F32)], axis=1)
    w_all = jnp.concatenate([wq, w_in[:, FOX_WIDTH:3 * FOX_WIDTH], wf_pad, w_in[:, 3 * FOX_WIDTH + FOX_HEADS:]],
                            axis=1).astype(BF16)
    bf_pad = jnp.concatenate([b_f, b_f, b_f, jnp.zeros((LANES - 3 * FOX_HEADS,), F32)]).reshape(1, LANES)
    eq = np.zeros((LANES, AUG), np.float32)
    ek = np.zeros((LANES, AUG), np.float32)
    cq = np.zeros((1, AUG), np.float32)
    ck = np.zeros((1, AUG), np.float32)
    cv = np.zeros((1, AUG), np.float32)
    hsel = np.zeros((FOX_WIDTH, LANES), np.float32)
    hdiag = np.zeros((FOX_WIDTH, LANES), np.float32)
    hmask = np.zeros((2, LANES), np.float32)
    hmask[0, :FOX_HEAD_DIM] = 1.0
    hmask[1, FOX_HEAD_DIM:] = 1.0
    for h in range(FOX_HEADS):
        hsel[FOX_HEAD_DIM * h:FOX_HEAD_DIM * (h + 1), h] = 1.0
        hdiag[FOX_HEAD_DIM * h:FOX_HEAD_DIM * (h + 1), 3 * FOX_HEADS + h] = 1.0
        cv[0, extra[h]] = 1.0
        eq[3 * FOX_HEADS + h, extra[h] + 6] = -1.0
        ck[0, extra[h] + 6] = 1.0
        for part in range(3):
            eq[part * 8 + h, extra[h] + part] = 1.0
            ek[part * 8 + h, extra[h] + 3 + part] = 1.0
            cq[0, extra[h] + 3 + part] = 1.0
            ck[0, extra[h] + part] = 1.0
    bias_sel = np.zeros((FOX_HEADS, FOX_HEAD_DIM, 4 * FOX_HEADS), np.float32)
    for h in range(FOX_HEADS):
        for part in range(3):
            bias_sel[h, part, 3 * FOX_HEADS] = 1.0
            bias_sel[h, 3 + part, part * 8 + h] = 1.0
        bias_sel[h, 6, 3 * FOX_HEADS] = 1.0
    dense = lambda w: jax.scipy.linalg.block_diag(*[w[i] for i in range(LRU_BLOCKS)]).astype(BF16)
    row = lambda a: a.reshape(1, -1)
    b16 = lambda a: jnp.asarray(a, BF16)
    return dict(g_mix=row(g_mix), w_all=w_all, bf_pad=bf_pad, eq=b16(eq), ek=b16(ek),
                cq=jnp.asarray(cq), ck=jnp.asarray(ck), cv=jnp.asarray(cv),
                bias_sel=b16(bias_sel),
                hsel=b16(hsel), hdiag=b16(hdiag), hmask=b16(hmask), conv_w=conv_w, conv_b=row(conv_b),
                wa=dense(w_a), ba=row(b_a), wi=dense(w_i), bi=row(b_i), lam=row(lam))


def kernel(x_prompt, x_sample, mem_prompt, cache_fox_k, cache_fox_v, cache_fox_logf, state_lru_h, state_conv, cache_mem_k, cache_mem_v, g_mix, w_in, b_f, conv_w, conv_b, w_a, b_a, w_i, b_i, lam, w_out, g_cross, g_mem, w_cq, w_ck, w_cv, w_co, g_moe, w_router, b_router, w_gate, b_gate, w_up, b_up, w_down, b_down, g_final):
    nb_p, seq, _ = x_prompt.shape
    nb_s, t_s, _ = x_sample.shape
    past = cache_fox_k.shape[2]
    n_mem = mem_prompt.shape[1]
    assert nb_p == 1 and g_mix.shape[0] == 1
    n_p, n_s = seq, nb_s * t_s
    row = lambda a: a.reshape(1, -1)

    mw = _prep_mix_weights(g_mix[0], w_in[0], b_f[0], conv_w[0], conv_b[0], w_a[0], b_a[0], w_i[0], b_i[0], lam[0])
    xp = x_prompt.reshape(n_p, D_MODEL)
    xs_ = x_sample.reshape(n_s, D_MODEL)

    zero_prev = jnp.zeros((1, SUBLANES, LRU_WIDTH), F32)
    zero_h = jnp.zeros((1, 1, LRU_WIDTH), F32)
    (qa_p, ka_p, va_p, k_p, v_p, _, lf_p, f2_p, qn_p, kn_p, lru_p, hl_p, ct_p) = _mix_in(
        xp, zero_prev, zero_h, mw, tm=MIX_TILE, streaming=True)
    prev_s = jnp.pad(state_conv[0], ((0, 0), (SUBLANES - (CONV_WIDTH - 1), 0), (0, 0)))
    (qa_s, ka_s, _, k_s, v_s, vb_s, lf_s, _, _, _, lru_s, hl_s, ct_s) = _mix_in(
        xs_, prev_s, state_lru_h[0].reshape(nb_s, 1, LRU_WIDTH), mw, tm=t_s, streaming=False)

    n_past, static_ok = _fox_plan(f2_p, qn_p, kn_p, bq=FOX_BQ, tile=MIX_TILE)
    fo_p = _fox_prompt(n_past, static_ok, qa_p, ka_p, va_p, bq=FOX_BQ)
    by_head = lambda c: jnp.moveaxis(c[0], 1, -1)
    fo_s = _fox_sample(qa_s, ka_s, vb_s, by_head(cache_fox_k), by_head(cache_fox_v), by_head(cache_fox_logf),
                       mw["bias_sel"], t=t_s)

    mk_p, mv_p = _mem_kv(mem_prompt[0], row(g_mem[0]), w_ck[0].astype(BF16), w_cv[0].astype(BF16))

    wr_pad = jnp.pad(w_router[0], ((0, 0), (0, LANES - N_EXPERTS))).astype(BF16)
    br_pad = jnp.pad(b_router[0], (0, LANES - N_EXPERTS)).reshape(1, LANES)
    pw = dict(w_out=w_out[0].astype(BF16), g_cross=row(g_cross[0]), w_cq=w_cq[0].astype(BF16),
              w_co=w_co[0].astype(BF16), g_moe=row(g_moe[0]), w_router=wr_pad, b_router=br_pad)
    cnt0 = jnp.zeros((SUBLANES, LANES), F32)
    h2_p, xn_p, idx_p, tw_p, rk_p, cnt_p = _post(
        xp, fo_p, lru_p, mk_p.reshape(1, n_mem, D_MODEL), mv_p.reshape(1, n_mem, D_MODEL), cnt0, pw,
        tm=POST_TILE, seg=POST_TILE, cached_memory=False)
    h2_s, xn_s, idx_s, tw_s, rk_s, cnt = _post(
        xs_, fo_s, lru_s, cache_mem_k, cache_mem_v, cnt_p, pw, tm=min(SAMPLE_POST_TILE, n_s), seg=t_s,
        cached_memory=True)

    n = n_p + n_s
    idx = jnp.concatenate([idx_p[:, :TOP_K], idx_s[:, :TOP_K]], axis=0)
    rank = jnp.concatenate([rk_p[:, :TOP_K], rk_s[:, :TOP_K]], axis=0)
    counts = cnt[0, :N_EXPERTS].astype(I32)
    padded = (counts + MOE_BM - 1) // MOE_BM * MOE_BM
    pad_end = jnp.cumsum(padded)
    pad_start = pad_end - padded
    dest = (pad_start[idx] + rank).reshape(n * TOP_K)
    n_blocks = -(-(n * TOP_K) // MOE_BM) + N_EXPERTS
    blk_row = jnp.arange(n_blocks, dtype=I32) * MOE_BM
    block_e = jnp.minimum(jnp.sum(pad_end[None, :] <= blk_row[:, None], axis=1), N_EXPERTS - 1).astype(I32)
    n_used = (pad_end[-1] // MOE_BM).reshape(1).astype(I32)
    last_block = jnp.maximum(pad_end // MOE_BM - 1, 0).astype(I32)

    xs_sorted = _dispatch(dest, last_block, n_used, xn_p, xn_s, tm=MOE_BM, n_blocks=n_blocks)
    ew = dict(w_gate=w_gate[0], w_up=w_up[0], w_down=w_down[0],
              b_gate=b_gate[0].reshape(N_EXPERTS, 1, D_MODEL), b_up=b_up[0].reshape(N_EXPERTS, 1, D_MODEL),
              b_down=b_down[0].reshape(N_EXPERTS, 1, D_MODEL))
    ys = _experts(block_e, n_used, (pad_end // MOE_BM).astype(I32), (pad_start + counts).astype(I32), xs_sorted, ew,
                  bm=MOE_BM)
    y_p, y_s = _combine(dest, h2_p, h2_s, tw_p, tw_s, row(g_final), ys, tm=COMBINE_TILE)

    shp_p = (1, nb_p, seq, FOX_HEADS, FOX_HEAD_DIM)
    shp_s = (1, nb_s, t_s, FOX_HEADS, FOX_HEAD_DIM)
    tail = slice(SUBLANES - (CONV_WIDTH - 1), SUBLANES)
    return (y_p.reshape(nb_p, seq, D_MODEL), y_s.reshape(nb_s, t_s, D_MODEL),
            k_p.reshape(shp_p), v_p.reshape(shp_p), lf_p.reshape(1, nb_p, seq, FOX_HEADS),
            hl_p.reshape(1, nb_p, LRU_WIDTH), ct_p[:, tail, :].reshape(1, nb_p, CONV_WIDTH - 1, LRU_WIDTH),
            mk_p.reshape(1, nb_p, n_mem, MEM_HEADS, MEM_HEAD_DIM), mv_p.reshape(1, nb_p, n_mem, MEM_HEADS, MEM_HEAD_DIM),
            k_s.reshape(shp_s), v_s.reshape(shp_s), lf_s.reshape(1, nb_s, t_s, FOX_HEADS),
            hl_s.reshape(1, nb_s, LRU_WIDTH), ct_s[:, tail, :].reshape(1, nb_s, CONV_WIDTH - 1, LRU_WIDTH))
```

```python
import functools
import math

import jax
import jax.numpy as jnp
import numpy as np
from jax import lax
from jax.experimental import pallas as pl
from jax.experimental.pallas import tpu as pltpu

F32 = jnp.float32
BF16 = jnp.bfloat16
I32 = jnp.int32

D_MODEL = 1024
FOX_HEADS = 8
FOX_HEAD_DIM = 64
FOX_WIDTH = FOX_HEADS * FOX_HEAD_DIM
LRU_WIDTH = D_MODEL - FOX_WIDTH
LRU_BLOCKS = 8
LRU_C = 8.0
CONV_WIDTH = 4
MEM_HEADS = 4
MEM_HEAD_DIM = D_MODEL // MEM_HEADS
N_EXPERTS = 32
TOP_K = 4
SWIGLU_LIMIT = 7.0
SWIGLU_ALPHA = 1.702
RMS_EPS = 1e-6
NEG_INF = -1e30
LOG2E = math.log2(math.e)

LANES = 128
SUBLANES = 8
CHUNKS = D_MODEL // LANES
TOKEN_TILE = (CHUNKS, SUBLANES, LANES)
AUG = LANES * FOX_HEADS
VMEM_LIMIT = 56 * 1024 * 1024

MIX_TILE = 256
POST_TILE = 1024
SAMPLE_POST_TILE = 256
FOX_BQ = 512
MOE_BM = 512
COMBINE_TILE = 256
FOX_SKIP_GAP = 160.0
FOX_STATIC_SHIFT_RANGE = 100.0

C_Q, C_K, C_V, C_F, C_XR, C_G, C_END = 0, 512, 1024, 1536, 1664, 2176, 2688


def _dot(a, b):
    return jnp.dot(a, b, preferred_element_type=F32)


def _dot_nt(a, b):
    return lax.dot_general(a, b, (((1,), (1,)), ((), ())), preferred_element_type=F32)


def _split3(x):
    hi = x.astype(BF16)
    r = x - hi.astype(F32)
    mid = r.astype(BF16)
    lo = (r - mid.astype(F32)).astype(BF16)
    return hi, mid, lo


def _dot3(a, x):
    hi, mid, lo = _split3(x)
    return _dot(a, hi) + _dot(a, mid) + _dot(a, lo)


def _softplus(x):
    return jnp.maximum(x, 0.0) + jnp.log1p(jnp.exp(-jnp.abs(x)))


def _rms(x, g):
    return x * lax.rsqrt(jnp.mean(x * x, axis=-1, keepdims=True) + RMS_EPS) * g


def _tri(n, kind):
    r = lax.broadcasted_iota(I32, (n, n), 0)
    c = lax.broadcasted_iota(I32, (n, n), 1)
    m = {"le": c <= r, "lt": c < r, "gt": c > r}[kind]
    return jnp.where(m, 1.0, 0.0).astype(BF16)


def _mix_in_kernel(x_ref, g_ref, w_ref, bfp_ref, eq_ref, ek_ref, cq_ref, ck_ref, cv_ref, hs_ref, hd_ref, hm_ref,
                   cw_ref, cb_ref, wa_ref, ba_ref, wi_ref, bi_ref, lam_ref, cprev_ref, h0_ref,
                   qa_ref, ka_ref, va_ref, ko_ref, vo_ref, vb_ref, lf_ref, f2_ref, qn_ref, kn_ref,
                   lru_ref, hl_ref, ct_ref,
                   fcar, hcar, xp_ref, sa_ref, sb_ref, kbuf, vbuf, ksem, *, tm, pad, streaming):
    step = pl.program_id(0)
    first = step == 0
    slot = step % 2

    def kv_copies(s, row0):
        rows = pl.ds(row0, tm)
        copies = []
        for h in range(FOX_HEADS):
            copies.append(pltpu.make_async_copy(kbuf.at[s, h], ko_ref.at[rows, h, :], ksem.at[s]))
            copies.append(pltpu.make_async_copy(vbuf.at[s, h], vo_ref.at[rows, h, :], ksem.at[s]))
        return copies

    @pl.when(step >= 2)
    def _():
        for c in kv_copies(slot, 0):
            c.wait()

    if streaming:
        @pl.when(first)
        def _():
            fcar[...] = jnp.zeros_like(fcar)
            hcar[...] = jnp.zeros_like(hcar)
            xp_ref[0:SUBLANES, :] = jnp.zeros((SUBLANES, LRU_WIDTH), F32)
    else:
        fcar[...] = jnp.zeros_like(fcar)
        hcar[...] = jnp.broadcast_to(h0_ref[0], hcar.shape)
        xp_ref[0:SUBLANES, :] = cprev_ref[0]

    x = x_ref[...]
    xn = _rms(x, g_ref[...]).astype(BF16)
    z = _dot(xn, w_ref[...])

    zq, zk, zv = z[:, C_Q:C_K], z[:, C_K:C_V], z[:, C_V:C_F]
    for h in range(FOX_HEADS):
        hc = slice(h * FOX_HEAD_DIM, (h + 1) * FOX_HEAD_DIM)
        kbuf[slot, h] = zk[:, hc]
        vbuf[slot, h] = zv[:, hc]
    for c in kv_copies(slot, pl.multiple_of(step * tm, tm)):
        c.start()
    qd, kd, vd = zq.astype(BF16), zk.astype(BF16), zv.astype(BF16)
    vb_ref[...] = vd

    def head_pad(d):
        blocks = []
        for p in range(FOX_HEADS // 2):
            blk = d[:, p * LANES:(p + 1) * LANES]
            blocks += [blk * hm_ref[0:1, :], blk * hm_ref[1:2, :]]
        return jnp.concatenate(blocks, axis=1)

    lane = lax.broadcasted_iota(I32, (tm, LANES), 1)
    lf = jnp.where(lane < 3 * FOX_HEADS, -_softplus(-(z[:, C_F:C_XR] + bfp_ref[...])), 0.0)
    lf_ref[...] = lf[:, :FOX_HEADS]
    cum = _dot3(_tri(tm, "le"), lf) + fcar[0:1, :]
    fcar[...] = jnp.broadcast_to(cum[tm - 1:tm, :], fcar.shape)
    cum2 = cum * LOG2E
    f2_ref[...] = cum2[:, :FOX_HEADS]
    qf, kf = qd.astype(F32), kd.astype(F32)
    diag = _dot((qf * kf).astype(BF16), hd_ref[...]).astype(BF16)
    hi, mid, lo = _split3(cum2)
    bias = jnp.where(lane < 8, hi, jnp.where(lane < 16, mid, jnp.where(lane < 24, lo, diag)))
    qa_ref[...] = head_pad(qd) + (_dot(bias, eq_ref[...]) + cq_ref[...]).astype(BF16)
    ka_ref[...] = head_pad(kd) + (ck_ref[...] - _dot(bias, ek_ref[...])).astype(BF16)
    va_ref[...] = head_pad(vd) + cv_ref[...].astype(BF16)
    qn_ref[...] = jnp.broadcast_to(jnp.max(_dot((qf * qf).astype(BF16), hs_ref[...]), axis=0, keepdims=True),
                                   qn_ref.shape)
    kn_ref[...] = jnp.broadcast_to(jnp.max(_dot((kf * kf).astype(BF16), hs_ref[...]), axis=0, keepdims=True),
                                   kn_ref.shape)

    xr = z[:, C_XR:C_G]
    xp_ref[SUBLANES:SUBLANES + tm, :] = xr
    xc = cb_ref[...] + xr * cw_ref[CONV_WIDTH - 1:CONV_WIDTH, :]
    for j in range(CONV_WIDTH - 1):
        sh = CONV_WIDTH - 1 - j
        xc = xc + xp_ref[SUBLANES - sh:SUBLANES - sh + tm, :] * cw_ref[j:j + 1, :]
    tail = xp_ref[tm:tm + SUBLANES, :]
    ct_ref[0] = tail
    xp_ref[0:SUBLANES, :] = tail

    xcb = xc.astype(BF16)
    r = jax.nn.sigmoid(_dot(xcb, wa_ref[...]) + ba_ref[...])
    ig = jax.nn.sigmoid(_dot(xcb, wi_ref[...]) + bi_ref[...])
    log_a = (-LRU_C) * r * _softplus(-lam_ref[...])
    a = jnp.exp(log_a)
    mult = jnp.sqrt(-jnp.tanh(log_a) * (a * a + 1.0))
    if streaming:
        row = lax.broadcasted_iota(I32, (tm, LRU_WIDTH), 0)
        mult = jnp.where(jnp.logical_and(row == 0, first), 1.0, mult)
    b = mult * ig * xc

    sa_ref[0:pad, :] = jnp.ones((pad, LRU_WIDTH), F32)
    sb_ref[0:pad, :] = jnp.zeros((pad, LRU_WIDTH), F32)
    d = 1
    while d < tm:
        sa_ref[pad:pad + tm, :] = a
        sb_ref[pad:pad + tm, :] = b
        b = a * sb_ref[pad - d:pad - d + tm, :] + b
        a = a * sa_ref[pad - d:pad - d + tm, :]
        d *= 2
    h = a * hcar[0:1, :] + b
    hlast = h[tm - 1:tm, :]
    hcar[...] = jnp.broadcast_to(hlast, hcar.shape)
    hl_ref[0] = hlast
    lru_ref[...] = (h * jax.nn.gelu(z[:, C_G:C_END])).astype(BF16)

    last = pl.num_programs(0) - 1

    @pl.when(jnp.logical_and(step == last, step >= 1))
    def _():
        for c in kv_copies(1 - slot, 0):
            c.wait()

    @pl.when(step == last)
    def _():
        for c in kv_copies(slot, 0):
            c.wait()


def _mix_in(x, cprev, h0, wts, *, tm, streaming):
    n = x.shape[0]
    steps = n // tm
    nseg = 1 if streaming else steps
    pad = max(tm // 2, SUBLANES)
    seg = (lambda i: (0, 0, 0)) if streaming else (lambda i: (i, 0, 0))
    full = lambda a: pl.BlockSpec(a.shape, lambda i: (0,) * a.ndim)
    rows = lambda w: pl.BlockSpec((tm, w), lambda i: (i, 0))
    names = ("g_mix", "w_all", "bf_pad", "eq", "ek", "cq", "ck", "cv", "hsel", "hdiag", "hmask", "conv_w", "conv_b",
             "wa", "ba", "wi", "bi", "lam")
    ws = [wts[k] for k in names]
    sds = jax.ShapeDtypeStruct
    out_shape = (
        sds((n, AUG), BF16), sds((n, AUG), BF16), sds((n, AUG), BF16),
        sds((n, FOX_HEADS, FOX_HEAD_DIM), F32), sds((n, FOX_HEADS, FOX_HEAD_DIM), F32), sds((n, FOX_WIDTH), BF16),
        sds((n, FOX_HEADS), F32), sds((n, FOX_HEADS), F32),
        sds((steps * SUBLANES, LANES), F32), sds((steps * SUBLANES, LANES), F32),
        sds((n, LRU_WIDTH), BF16),
        sds((nseg, 1, LRU_WIDTH), F32), sds((nseg, SUBLANES, LRU_WIDTH), F32),
    )
    out_specs = (
        rows(AUG), rows(AUG), rows(AUG), pl.BlockSpec(memory_space=pl.ANY), pl.BlockSpec(memory_space=pl.ANY),
        rows(FOX_WIDTH), rows(FOX_HEADS), rows(FOX_HEADS),
        pl.BlockSpec((SUBLANES, LANES), lambda i: (i, 0)), pl.BlockSpec((SUBLANES, LANES), lambda i: (i, 0)),
        rows(LRU_WIDTH),
        pl.BlockSpec((1, 1, LRU_WIDTH), seg), pl.BlockSpec((1, SUBLANES, LRU_WIDTH), seg),
    )
    in_specs = [rows(D_MODEL)] + [full(w) for w in ws] + [
        pl.BlockSpec((1, SUBLANES, LRU_WIDTH), seg), pl.BlockSpec((1, 1, LRU_WIDTH), seg)]
    return pl.pallas_call(
        functools.partial(_mix_in_kernel, tm=tm, pad=pad, streaming=streaming),
        grid=(steps,), in_specs=in_specs, out_specs=out_specs, out_shape=out_shape,
        scratch_shapes=[
            pltpu.VMEM((SUBLANES, LANES), F32), pltpu.VMEM((SUBLANES, LRU_WIDTH), F32),
            pltpu.VMEM((tm + SUBLANES, LRU_WIDTH), F32),
            pltpu.VMEM((pad + tm, LRU_WIDTH), F32), pltpu.VMEM((pad + tm, LRU_WIDTH), F32),
            pltpu.VMEM((2, FOX_HEADS, tm, FOX_HEAD_DIM), F32), pltpu.VMEM((2, FOX_HEADS, tm, FOX_HEAD_DIM), F32),
            pltpu.SemaphoreType.DMA((2,))],
        compiler_params=pltpu.CompilerParams(dimension_semantics=("arbitrary",), vmem_limit_bytes=VMEM_LIMIT),
        name="mix_in_stream" if streaming else "mix_in_segments",
    )(x, *ws, cprev, h0)


def _fox_prompt_kernel(nb_ref, st_ref, q_ref, k_ref, v_ref, o_ref, m_ref, acc_ref, *, bq):
    qi = pl.program_id(1)
    plan = pl.program_id(0) * pl.num_programs(1) + qi
    n_past = nb_ref[plan]
    cols = [slice(LANES * c, LANES * (c + 1)) for c in range(2)]
    qs = [q_ref[:, cols[c]] for c in range(2)]
    acc_ref[...] = jnp.zeros(acc_ref.shape, F32)

    def scores(c, start, causal):
        s = _dot_nt(qs[c], k_ref[pl.ds(start, bq), cols[c]])
        if causal:
            r = lax.broadcasted_iota(I32, (bq, bq), 0)
            cc = lax.broadcasted_iota(I32, (bq, bq), 1)
            s = jnp.where(cc <= r, s, NEG_INF)
        return s

    def static_block(start, causal):
        for c in range(2):
            p = jnp.exp2(scores(c, start, causal)).astype(BF16)
            acc_ref[c] += _dot(p, v_ref[pl.ds(start, bq), cols[c]])

    def online_block(start, causal):
        for c in range(2):
            s = scores(c, start, causal)
            m_old = m_ref[c]
            m_new = jnp.maximum(m_old, jnp.max(s, axis=1, keepdims=True))
            p = jnp.exp2(s - m_new).astype(BF16)
            acc_ref[c] = jnp.exp2(m_old - m_new) * acc_ref[c] + _dot(p, v_ref[pl.ds(start, bq), cols[c]])
            m_ref[c] = m_new

    def run(block):
        block(pl.multiple_of(qi * bq, bq), True)

        def body(t, carry):
            block(pl.multiple_of((qi - 1 - t) * bq, bq), False)
            return carry

        lax.fori_loop(0, n_past, body, 0)

    @pl.when(st_ref[plan] == 1)
    def _():
        run(static_block)

    @pl.when(st_ref[plan] != 1)
    def _():
        m_ref[...] = jnp.full(m_ref.shape, NEG_INF, F32)
        run(online_block)

    a0, a1 = acc_ref[0], acc_ref[1]
    o0 = a0 / a0[:, FOX_HEAD_DIM:FOX_HEAD_DIM + 1]
    o1 = a1 / a1[:, 0:1]
    lane = lax.broadcasted_iota(I32, (bq, LANES), 1)
    o_ref[...] = jnp.where(lane < FOX_HEAD_DIM, o0, o1).astype(BF16)


def _fox_prompt(n_past, static_ok, qa, ka, va, *, bq):
    s = qa.shape[0]
    pairs = FOX_HEADS // 2
    return pl.pallas_call(
        functools.partial(_fox_prompt_kernel, bq=bq),
        grid_spec=pltpu.PrefetchScalarGridSpec(
            num_scalar_prefetch=2, grid=(pairs, s // bq),
            in_specs=[pl.BlockSpec((bq, 2 * LANES), lambda p, i, nb, st: (i, p)),
                      pl.BlockSpec((s, 2 * LANES), lambda p, i, nb, st: (0, p)),
                      pl.BlockSpec((s, 2 * LANES), lambda p, i, nb, st: (0, p))],
            out_specs=pl.BlockSpec((bq, LANES), lambda p, i, nb, st: (i, p)),
            scratch_shapes=[pltpu.VMEM((2, bq, 1), F32), pltpu.VMEM((2, bq, LANES), F32)]),
        out_shape=jax.ShapeDtypeStruct((s, FOX_WIDTH), BF16),
        compiler_params=pltpu.CompilerParams(dimension_semantics=("arbitrary", "arbitrary"),
                                             vmem_limit_bytes=VMEM_LIMIT),
        name="fox_prompt",
    )(n_past, static_ok, qa, ka, va)


def _fox_plan(f2, qn, kn, *, bq, tile):
    s = f2.shape[0]
    nq = s // bq
    per = lambda a: a.reshape(s // tile, SUBLANES, LANES)[:, 0, :FOX_HEADS]
    qnorm = jnp.sqrt(jnp.max(per(qn).reshape(nq, bq // tile, FOX_HEADS), axis=1))
    knorm = jnp.sqrt(jnp.max(per(kn), axis=0))
    spread = 2.0 * 1.02 * qnorm * knorm[None, :]
    thr = spread + FOX_SKIP_GAP
    f_first = f2[0::bq]
    f_last = f2[bq - 1::bq]
    gap = f_last[None, :, :] - f_first[:, None, :]
    before = (jnp.arange(nq)[None, :] < jnp.arange(nq)[:, None])[:, :, None]
    need = jnp.sum(jnp.logical_and(before, gap < thr[:, None, :]), axis=1)
    pair = lambda a: a.reshape(nq, FOX_HEADS // 2, 2)
    n_past = jnp.max(pair(need), axis=2).T.reshape(-1).astype(I32)
    static_ok = jnp.all(pair(spread) <= FOX_STATIC_SHIFT_RANGE, axis=2).T.reshape(-1).astype(I32)
    return n_past, static_ok


def _fox_sample_kernel(q_ref, kn_ref, vn_ref, kt_ref, vt_ref, lf_ref, sel_ref, o_ref, g_ref,
                       *, t, past, chunk):
    car = jnp.zeros((FOX_HEADS, 1), F32)
    later = _tri(chunk, "lt")
    for ci in reversed(range(past // chunk)):
        lf = lf_ref[0, :, ci * chunk:(ci + 1) * chunk]
        hi, mid, lo = _split3(lf)
        g_ref[:, ci * chunk:(ci + 1) * chunk] = _dot(hi, later) + _dot(mid, later) + _dot(lo, later) + car
        car = car + jnp.sum(lf, axis=1, keepdims=True)
    hi, mid, lo = _split3(g_ref[...] * LOG2E)
    bias_rows = jnp.concatenate([hi.astype(F32), mid.astype(F32), lo.astype(F32),
                                 jnp.ones((FOX_HEADS, past), F32)], axis=0).astype(BF16)

    def swap_halves(x):
        return pltpu.roll(x.astype(F32), FOX_HEAD_DIM, axis=1).astype(BF16)

    r = lax.broadcasted_iota(I32, (t, t), 0)
    cc = lax.broadcasted_iota(I32, (t, t), 1)
    prow = lax.broadcasted_iota(I32, (FOX_HEAD_DIM, LANES), 0)
    plane = lax.broadcasted_iota(I32, (FOX_HEAD_DIM, LANES), 1)
    place = [jnp.where(plane == prow + FOX_HEAD_DIM * c, 1.0, 0.0).astype(BF16) for c in range(2)]
    for p in range(FOX_HEADS // 2):
        pc = slice(LANES * p, LANES * (p + 1))
        vpair = vn_ref[:, pc]
        o_pair = jnp.zeros((t, LANES), F32)
        for c in range(2):
            h = 2 * p + c
            hc = slice(LANES * h, LANES * (h + 1))
            q, kn, vn = q_ref[:, hc], kn_ref[:, hc], vpair
            if c == 1:
                q, kn, vn = swap_halves(q), swap_halves(kn), swap_halves(vn)
            bias = _dot(sel_ref[h], bias_rows).astype(BF16)
            kpast_t = jnp.concatenate([kt_ref[0, h].astype(BF16), bias], axis=0)
            sp = _dot(q, kpast_t)
            sn = jnp.where(cc <= r, _dot_nt(q, kn), NEG_INF)
            m = jnp.maximum(jnp.max(sp, axis=1, keepdims=True), jnp.max(sn, axis=1, keepdims=True))
            pp = jnp.exp2(sp - m)
            pn = jnp.exp2(sn - m)
            l = jnp.sum(pp, axis=1, keepdims=True) + jnp.sum(pn, axis=1, keepdims=True)
            o = (_dot_nt(pp.astype(BF16), vt_ref[0, h].astype(BF16))
                 + _dot(pn.astype(BF16), vn[:, :FOX_HEAD_DIM])) / l
            o_pair = o_pair + _dot(o.astype(BF16), place[c])
        o_ref[:, pc] = o_pair.astype(BF16)


def _fox_sample(qa, ka, vb, cache_kt, cache_vt, cache_lft, sel, *, t):
    nb, past = cache_kt.shape[0], cache_kt.shape[3]
    full = lambda a: pl.BlockSpec(a.shape, lambda b: (0,) * a.ndim)
    slab = pl.BlockSpec((1, FOX_HEADS, FOX_HEAD_DIM, past), lambda b: (b, 0, 0, 0))
    return pl.pallas_call(
        functools.partial(_fox_sample_kernel, t=t, past=past, chunk=256),
        grid=(nb,),
        in_specs=[pl.BlockSpec((t, AUG), lambda b: (b, 0)), pl.BlockSpec((t, AUG), lambda b: (b, 0)),
                  pl.BlockSpec((t, FOX_WIDTH), lambda b: (b, 0)), slab, slab,
                  pl.BlockSpec((1, FOX_HEADS, past), lambda b: (b, 0, 0)), full(sel)],
        out_specs=pl.BlockSpec((t, FOX_WIDTH), lambda b: (b, 0)),
        out_shape=jax.ShapeDtypeStruct((nb * t, FOX_WIDTH), BF16),
        scratch_shapes=[pltpu.VMEM((FOX_HEADS, past), F32)],
        compiler_params=pltpu.CompilerParams(dimension_semantics=("arbitrary",), vmem_limit_bytes=VMEM_LIMIT),
        name="fox_sample",
    )(qa, ka, vb, cache_kt, cache_vt, cache_lft, sel)


def _mem_kv_kernel(m_ref, g_ref, wk_ref, wv_ref, k_ref, v_ref):
    mn = _rms(m_ref[...], g_ref[...]).astype(BF16)
    k_ref[...] = _dot(mn, wk_ref[...])
    v_ref[...] = _dot(mn, wv_ref[...])


def _mem_kv(mem, g, wk, wv):
    n = mem.shape[0]
    return pl.pallas_call(
        _mem_kv_kernel,
        out_shape=(jax.ShapeDtypeStruct((n, D_MODEL), F32), jax.ShapeDtypeStruct((n, D_MODEL), F32)),
        compiler_params=pltpu.CompilerParams(vmem_limit_bytes=VMEM_LIMIT),
        name="mem_kv",
    )(mem, g, wk, wv)


def _post_kernel(h_ref, fo_ref, lru_ref, wo_ref, gc_ref, wcq_ref, mk_ref, mv_ref, wco_ref,
                 gm_ref, wr_ref, br_ref, cnt_in_ref,
                 h2_ref, xn_ref, idx_ref, tw_ref, rank_ref, cnt_ref, car_ref, *mem_scratch, tm, seg, cached_memory):
    step = pl.program_id(0)
    nseg = tm // seg

    @pl.when(step == 0)
    def _():
        car_ref[...] = cnt_in_ref[...]

    if cached_memory:
        mbuf, msem = mem_scratch
        slot = step % 2

        def mem_copies(at_step, s):
            return [pltpu.make_async_copy(ref.at[0, at_step * nseg + j, :, hd, :], mbuf.at[s, j, t, hd], msem.at[s])
                    for j in range(nseg) for t, ref in enumerate((mk_ref, mv_ref)) for hd in range(MEM_HEADS)]

        @pl.when(step == 0)
        def _():
            for c in mem_copies(0, 0):
                c.start()

        @pl.when(step + 1 < pl.num_programs(0))
        def _():
            for c in mem_copies(step + 1, 1 - slot):
                c.start()

    h1 = h_ref[...] + _dot(fo_ref[...], wo_ref[0:FOX_WIDTH, :]) + _dot(lru_ref[...], wo_ref[FOX_WIDTH:D_MODEL, :])

    q = _dot(_rms(h1, gc_ref[...]).astype(BF16), wcq_ref[...])
    if cached_memory:
        for c in mem_copies(0, slot):
            c.wait()
    segs = []
    for j in range(nseg):
        rows = slice(j * seg, (j + 1) * seg)
        heads = []
        for hd in range(MEM_HEADS):
            hc = slice(MEM_HEAD_DIM * hd, MEM_HEAD_DIM * (hd + 1))
            mk = mbuf[slot, j, 0, hd] if cached_memory else mk_ref[0, :, hc]
            mv = mbuf[slot, j, 1, hd] if cached_memory else mv_ref[0, :, hc]
            s = _dot_nt(q[rows, hc].astype(BF16), mk.astype(BF16)) * (MEM_HEAD_DIM ** -0.5)
            p = jnp.exp(s - jnp.max(s, axis=1, keepdims=True))
            o = _dot(p.astype(BF16), mv.astype(BF16)) / jnp.sum(p, axis=1, keepdims=True)
            heads.append(o.astype(BF16))
        segs.append(jnp.concatenate(heads, axis=1))
    h2 = h1 + _dot(segs[0] if nseg == 1 else jnp.concatenate(segs, axis=0), wco_ref[...])
    h2_ref[...] = h2

    xn = _rms(h2, gm_ref[...])
    _rows_to_tiles(xn_ref, xn, tm)

    lane = lax.broadcasted_iota(I32, (tm, LANES), 1).astype(F32)
    logits = jnp.where(lane < N_EXPERTS, _dot(xn.astype(BF16), wr_ref[...]) + br_ref[...], -jnp.inf)
    vals, idxs = [], []
    for _ in range(TOP_K):
        mx = jnp.max(logits, axis=1, keepdims=True)
        ix = jnp.min(jnp.where(logits == mx, lane, float(LANES)), axis=1, keepdims=True)
        vals.append(mx)
        idxs.append(ix)
        logits = jnp.where(lane == ix, -jnp.inf, logits)
    es = [jnp.exp(v - vals[0]) for v in vals]
    den = es[0] + es[1] + es[2] + es[3]

    onehot = jnp.zeros((tm, LANES), F32)
    for ix in idxs:
        onehot = onehot + jnp.where(lane == ix, 1.0, 0.0)
    before = _dot(_tri(tm, "lt"), onehot.astype(BF16)) + car_ref[0:1, :]
    car = car_ref[0:1, :] + jnp.sum(onehot, axis=0, keepdims=True)
    car_ref[...] = jnp.broadcast_to(car, car_ref.shape)
    cnt_ref[...] = jnp.broadcast_to(car, cnt_ref.shape)

    idx_o = jnp.zeros((tm, LANES), F32)
    tw_o = jnp.zeros((tm, LANES), F32)
    rk_o = jnp.zeros((tm, LANES), F32)
    for j in range(TOP_K):
        rk = jnp.sum(jnp.where(lane == idxs[j], before, 0.0), axis=1, keepdims=True)
        idx_o = jnp.where(lane == j, idxs[j], idx_o)
        tw_o = jnp.where(lane == j, es[j] / den, tw_o)
        rk_o = jnp.where(lane == j, rk, rk_o)
    idx_ref[...] = idx_o.astype(I32)
    tw_ref[...] = tw_o
    rank_ref[...] = rk_o.astype(I32)


def _post(h, fo, lru, mk, mv, cnt_in, wts, *, tm, seg, cached_memory):
    n = h.shape[0]
    full = lambda a: pl.BlockSpec(a.shape, lambda i: (0,) * a.ndim)
    rows = lambda w: pl.BlockSpec((tm, w), lambda i: (i, 0))
    if cached_memory:
        nm = mk.shape[2]
        mem = pl.BlockSpec(memory_space=pl.ANY)
        mem_scratch = [pltpu.VMEM((2, tm // seg, 2, MEM_HEADS, nm, MEM_HEAD_DIM), F32),
                       pltpu.SemaphoreType.DMA((2,))]
    else:
        mem = full(mk)
        mem_scratch = []
    w = [wts[k] for k in ("w_out", "g_cross", "w_cq")]
    w2 = [wts[k] for k in ("w_co", "g_moe", "w_router", "b_router")]
    return pl.pallas_call(
        functools.partial(_post_kernel, tm=tm, seg=seg, cached_memory=cached_memory),
        grid=(n // tm,),
        in_specs=[rows(D_MODEL), rows(FOX_WIDTH), rows(LRU_WIDTH)] + [full(a) for a in w] + [mem, mem]
                 + [full(a) for a in w2] + [full(cnt_in)],
        out_specs=(rows(D_MODEL), pl.BlockSpec((tm // SUBLANES,) + TOKEN_TILE, lambda i: (i, 0, 0, 0)),
                   rows(LANES), rows(LANES), rows(LANES),
                   pl.BlockSpec((SUBLANES, LANES), lambda i: (0, 0))),
        out_shape=(jax.ShapeDtypeStruct((n, D_MODEL), F32), jax.ShapeDtypeStruct((n // SUBLANES,) + TOKEN_TILE, F32),
                   jax.ShapeDtypeStruct((n, LANES), I32), jax.ShapeDtypeStruct((n, LANES), F32),
                   jax.ShapeDtypeStruct((n, LANES), I32), jax.ShapeDtypeStruct((SUBLANES, LANES), F32)),
        scratch_shapes=[pltpu.VMEM((SUBLANES, LANES), F32)] + mem_scratch,
        compiler_params=pltpu.CompilerParams(dimension_semantics=("arbitrary",), vmem_limit_bytes=VMEM_LIMIT),
        name="post_batch_mem" if cached_memory else "post_shared_mem",
    )(h, fo, lru, *w, mk, mv, *w2, cnt_in)


STAGES = 3


def _dispatch_kernel(dest_ref, last_ref, nu_ref, xp_ref, xs_ref, out_ref, zero_ref, stage, sems, gsem, zsem,
                     *, tm, steps_p, n_blocks):
    i = pl.program_id(0)

    def zero_copy(b):
        return pltpu.make_async_copy(zero_ref, out_ref.at[pl.ds(pl.multiple_of(b * tm, tm), tm)], zsem)

    @pl.when(i == 0)
    def _():
        zero_ref[...] = jnp.zeros(zero_ref.shape, F32)
        n_tail = n_blocks - nu_ref[0]

        def start_e(e, c):
            zero_copy(last_ref[e]).start()
            return c

        def start_t(b, c):
            zero_copy(nu_ref[0] + b).start()
            return c

        def wait_one(b, c):
            zero_copy(0).wait()
            return c

        lax.fori_loop(0, N_EXPERTS, start_e, 0)
        lax.fori_loop(0, n_tail, start_t, 0)
        lax.fori_loop(0, N_EXPERTS + n_tail, wait_one, 0)

    groups = tm // SUBLANES
    steps = pl.num_programs(0)

    def stage_copies(src_ref, tile, slot):
        return [pltpu.make_async_copy(src_ref.at[pl.ds(tile * groups, groups), :, u, :], stage.at[slot, :, u],
                                      gsem.at[slot]) for u in range(SUBLANES)]

    def start_stage(tile, slot):
        @pl.when(tile < steps_p)
        def _():
            for c in stage_copies(xp_ref, tile, slot):
                c.start()

        @pl.when(tile >= steps_p)
        def _():
            for c in stage_copies(xs_ref, tile - steps_p, slot):
                c.start()

    @pl.when(i == 0)
    def _():
        start_stage(0, 0)

    @pl.when(i + 1 < steps)
    def _():
        start_stage(i + 1, (i + 1) % STAGES)

    slot = i % STAGES
    for c in stage_copies(xp_ref, 0, slot):
        c.wait()

    base = i * (tm * TOP_K)

    def start(g, c):
        for u in range(SUBLANES):
            for j in range(TOP_K):
                d = dest_ref[base + (g * SUBLANES + u) * TOP_K + j]
                pltpu.make_async_copy(stage.at[slot, g, u], out_ref.at[d],
                                      sems.at[i % 2]).start(priority=j % 2)
        return c

    lax.fori_loop(0, groups, start, 0)

    def drain(parity):
        for _ in range(TOP_K):
            pltpu.make_async_copy(out_ref.at[pl.ds(0, tm)], out_ref.at[pl.ds(0, tm)], sems.at[parity]).wait()

    @pl.when(i > 0)
    def _():
        drain((i - 1) % 2)

    @pl.when(i == steps - 1)
    def _():
        drain(i % 2)


def _dispatch(dest_flat, last_block, n_used, xn_p, xn_s, *, tm, n_blocks):
    steps_p, steps_s = xn_p.shape[0] * SUBLANES // tm, xn_s.shape[0] * SUBLANES // tm
    return pl.pallas_call(
        functools.partial(_dispatch_kernel, tm=tm, steps_p=steps_p, n_blocks=n_blocks),
        grid_spec=pltpu.PrefetchScalarGridSpec(
            num_scalar_prefetch=3, grid=(steps_p + steps_s,),
            in_specs=[pl.BlockSpec(memory_space=pl.ANY), pl.BlockSpec(memory_space=pl.ANY)],
            out_specs=pl.BlockSpec(memory_space=pl.ANY),
            scratch_shapes=[pltpu.VMEM((tm, CHUNKS, LANES), F32),
                            pltpu.VMEM((STAGES, tm // SUBLANES, SUBLANES, CHUNKS, LANES), F32),
                            pltpu.SemaphoreType.DMA((2,)), pltpu.SemaphoreType.DMA((STAGES,)),
                            pltpu.SemaphoreType.DMA(())]),
        out_shape=jax.ShapeDtypeStruct((n_blocks * tm, CHUNKS, LANES), F32),
        compiler_params=pltpu.CompilerParams(dimension_semantics=("arbitrary",), vmem_limit_bytes=VMEM_LIMIT),
        name="moe_dispatch",
    )(dest_flat, last_block, n_used, xn_p, xn_s)


def _rows_from_tiles(view, rows):
    return jnp.concatenate([view[:, c].reshape(rows, LANES) for c in range(CHUNKS)], axis=1)


def _rows_to_tiles(view, val, rows):
    for c in range(CHUNKS):
        view[:, c] = val[:, c * LANES:(c + 1) * LANES].reshape(rows // SUBLANES, SUBLANES, LANES)


def _expert_kernel(be_ref, nu_ref, nx_ref, re_ref, xs_ref, wg_ref, bg_ref, wu_ref, bu_ref, wd_ref, bd_ref, ys_ref,
                   wgb, wub, wdb, wbuf, xbuf, ybuf, seq, wsem, xsem, ysem, *, bm):
    i = pl.program_id(0)
    steps = pl.num_programs(0)
    groups = bm // SUBLANES
    slot = i % 2
    prev = be_ref[jnp.maximum(i - 1, 0)]
    fresh = jnp.logical_or(i == 0, be_ref[i] != prev)
    live = i < nu_ref[0]

    def w_copies(e, s):
        return [pltpu.make_async_copy(w_ref.at[e], wbuf.at[s, k], wsem.at[s])
                for k, w_ref in enumerate((wg_ref, wu_ref, wd_ref))]

    @pl.when(i == 0)
    def _():
        seq[0] = 0
        for c in w_copies(be_ref[0], 0):
            c.start()

    def in_copies(blk, s):
        g0 = pl.multiple_of(blk * groups, groups)
        return [pltpu.make_async_copy(xs_ref.at[pl.ds(g0, groups), u], xbuf.at[s, :, :, u, :], xsem.at[s])
                for u in range(SUBLANES)]

    def out_copies(blk, s):
        g0 = pl.multiple_of(blk * groups, groups)
        return [pltpu.make_async_copy(ybuf.at[s, :, :, u, :], ys_ref.at[pl.ds(g0, groups), u], ysem.at[s])
                for u in range(SUBLANES)]

    @pl.when(i == 0)
    def _():
        for c in in_copies(0, 0):
            c.start()

    @pl.when(i + 1 < nu_ref[0])
    def _():
        for c in in_copies(i + 1, (i + 1) % 2):
            c.start()

    @pl.when(i >= 2)
    def _():
        for c in out_copies(0, slot):
            c.wait()

    @pl.when(jnp.logical_and(live, fresh))
    def _():
        s = seq[0] % 2
        for c in w_copies(0, s):
            c.wait()
        for static_s in range(2):
            @pl.when(s == static_s)
            def _():
                wgb[...] = wbuf[static_s, 0].astype(BF16)
                wub[...] = wbuf[static_s, 1].astype(BF16)
                wdb[...] = wbuf[static_s, 2].astype(BF16)

        nxt = nx_ref[be_ref[i]]

        @pl.when(nxt < nu_ref[0])
        def _():
            for c in w_copies(be_ref[jnp.minimum(nxt, steps - 1)], 1 - s):
                c.start()

        seq[0] = seq[0] + 1

    def mlp(first_group, n_groups):
        rows = n_groups * SUBLANES
        tiles = pl.ds(first_group, n_groups)
        x = _rows_from_tiles(xbuf.at[slot, tiles], rows).astype(BF16)
        g = jnp.minimum(_dot(x, wgb[...]) + bg_ref[0], SWIGLU_LIMIT)
        u = jnp.clip(_dot(x, wub[...]) + bu_ref[0], -SWIGLU_LIMIT, SWIGLU_LIMIT)
        hdn = g * jax.nn.sigmoid(SWIGLU_ALPHA * g) * (u + 1.0)
        _rows_to_tiles(ybuf.at[slot, tiles], _dot(hdn.astype(BF16), wdb[...]) + bd_ref[0], rows)

    half_only = re_ref[be_ref[i]] - i * bm <= bm // 2

    @pl.when(live)
    def _():
        for c in in_copies(0, slot):
            c.wait()

    @pl.when(jnp.logical_and(live, jnp.logical_not(half_only)))
    def _():
        mlp(0, groups)

    @pl.when(jnp.logical_and(live, half_only))
    def _():
        mlp(0, groups // 2)
        ybuf[slot, groups // 2:groups] = jnp.zeros((groups // 2,) + TOKEN_TILE, F32)

    @pl.when(jnp.logical_not(live))
    def _():
        ybuf[slot] = jnp.zeros(ybuf.shape[1:], F32)

    for c in out_copies(i, slot):
        c.start()

    @pl.when(jnp.logical_and(i == steps - 1, i >= 1))
    def _():
        for c in out_copies(0, 1 - slot):
            c.wait()

    @pl.when(i == steps - 1)
    def _():
        for c in out_copies(0, slot):
            c.wait()


def _experts(block_e, n_used, next_first, real_end, xs, wts, *, bm):
    p = xs.shape[0]
    grouped = (p // SUBLANES, SUBLANES, CHUNKS, LANES)
    anyspace = pl.BlockSpec(memory_space=pl.ANY)
    bspec = pl.BlockSpec((1, 1, D_MODEL), lambda i, be, nu, nx, vr: (be[i], 0, 0))
    tiles = (2, bm // SUBLANES) + TOKEN_TILE
    ys = pl.pallas_call(
        functools.partial(_expert_kernel, bm=bm),
        grid_spec=pltpu.PrefetchScalarGridSpec(
            num_scalar_prefetch=4, grid=(p // bm,),
            in_specs=[anyspace, anyspace, bspec, anyspace, bspec, anyspace, bspec],
            out_specs=anyspace,
            scratch_shapes=[pltpu.VMEM((D_MODEL, D_MODEL), BF16)] * 3 + [
                pltpu.VMEM((2, 3, D_MODEL, D_MODEL), F32), pltpu.VMEM(tiles, F32), pltpu.VMEM(tiles, F32),
                pltpu.SMEM((1,), I32),
                pltpu.SemaphoreType.DMA((2,)), pltpu.SemaphoreType.DMA((2,)), pltpu.SemaphoreType.DMA((2,))]),
        out_shape=jax.ShapeDtypeStruct(grouped, F32),
        compiler_params=pltpu.CompilerParams(dimension_semantics=("arbitrary",), vmem_limit_bytes=VMEM_LIMIT),
        name="moe_experts",
    )(block_e, n_used, next_first, real_end, xs.reshape(grouped), wts["w_gate"], wts["b_gate"], wts["w_up"], wts["b_up"],
      wts["w_down"], wts["b_down"])
    return ys.reshape(p, CHUNKS, LANES)


def _combine_kernel(dest_ref, hp_ref, hs_ref, twp_ref, tws_ref, gf_ref, ys_ref, yp_ref, ysm_ref, buf, sems,
                    *, tm, steps_p):
    i = pl.program_id(0)
    slot = i % 2

    def issue(tile, into):
        base = tile * (tm * TOP_K)

        def start(g, c):
            for u in range(SUBLANES):
                for j in range(TOP_K):
                    d = dest_ref[base + (g * SUBLANES + u) * TOP_K + j]
                    pltpu.make_async_copy(ys_ref.at[d], buf.at[into, j, g, :, u, :],
                                          sems.at[into]).start(priority=j % 2)
            return c

        lax.fori_loop(0, tm // SUBLANES, start, 0)

    @pl.when(i == 0)
    def _():
        issue(0, 0)

    @pl.when(i + 1 < pl.num_programs(0))
    def _():
        issue(i + 1, (i + 1) % 2)

    for j in range(TOP_K):
        pltpu.make_async_copy(ys_ref.at[pl.ds(0, tm)], ys_ref.at[pl.ds(0, tm)], sems.at[slot]).wait()

    def finish(h_ref, tw_ref, y_ref):
        tw = tw_ref[...]
        y = h_ref[...]
        for j in range(TOP_K):
            y = y + _rows_from_tiles(buf.at[slot, j], tm) * tw[:, j:j + 1]
        y_ref[...] = _rms(y, gf_ref[...])

    @pl.when(i < steps_p)
    def _():
        finish(hp_ref, twp_ref, yp_ref)

    @pl.when(i >= steps_p)
    def _():
        finish(hs_ref, tws_ref, ysm_ref)


def _combine(dest_flat, h2_p, h2_s, tw_p, tw_s, g_final, ys, *, tm):
    steps_p, steps_s = h2_p.shape[0] // tm, h2_s.shape[0] // tm
    pblk = lambda w: pl.BlockSpec((tm, w), lambda i, d: (jnp.minimum(i, steps_p - 1), 0))
    sblk = lambda w: pl.BlockSpec((tm, w), lambda i, d: (jnp.maximum(i - steps_p, 0), 0))
    return pl.pallas_call(
        functools.partial(_combine_kernel, tm=tm, steps_p=steps_p),
        grid_spec=pltpu.PrefetchScalarGridSpec(
            num_scalar_prefetch=1, grid=(steps_p + steps_s,),
            in_specs=[pblk(D_MODEL), sblk(D_MODEL), pblk(LANES), sblk(LANES),
                      pl.BlockSpec((1, D_MODEL), lambda i, d: (0, 0)),
                      pl.BlockSpec(memory_space=pl.ANY)],
            out_specs=(pblk(D_MODEL), sblk(D_MODEL)),
            scratch_shapes=[pltpu.VMEM((2, TOP_K, tm // SUBLANES) + TOKEN_TILE, F32),
                            pltpu.SemaphoreType.DMA((2,))]),
        out_shape=(jax.ShapeDtypeStruct(h2_p.shape, F32), jax.ShapeDtypeStruct(h2_s.shape, F32)),
        compiler_params=pltpu.CompilerParams(dimension_semantics=("arbitrary",), vmem_limit_bytes=VMEM_LIMIT),
        name="moe_combine",
    )(dest_flat, h2_p, h2_s, tw_p, tw_s, g_final, ys)


def _aug_layout():
    main, extra = [], []
    for h in range(FOX_HEADS):
        even = h % 2 == 0
        main.append(LANES * h + (0 if even else FOX_HEAD_DIM))
        extra.append(LANES * h + (FOX_HEAD_DIM if even else 0))
    return main, extra


def _prep_mix_weights(g_mix, w_in, b_f, conv_w, conv_b, w_a, b_a, w_i, b_i, lam):
    _, extra = _aug_layout()
    wq = w_in[:, 0:FOX_WIDTH] * (FOX_HEAD_DIM ** -0.5 * LOG2E)
    wf = w_in[:, 3 * FOX_WIDTH:3 * FOX_WIDTH + FOX_HEADS]
    wf_pad = jnp.concatenate([wf, wf, wf, jnp.zeros((D_MODEL, LANES - 3 * FOX_HEADS), F32)], axis=1)
    w_all = jnp.concatenate([wq, w_in[:, FOX_WIDTH:3 * FOX_WIDTH], wf_pad, w_in[:, 3 * FOX_WIDTH + FOX_HEADS:]],
                            axis=1).astype(BF16)
    bf_pad = jnp.concatenate([b_f, b_f, b_f, jnp.zeros((LANES - 3 * FOX_HEADS,), F32)]).reshape(1, LANES)
    eq = np.zeros((LANES, AUG), np.float32)
    ek = np.zeros((LANES, AUG), np.float32)
    cq = np.zeros((1, AUG), np.float32)
    ck = np.zeros((1, AUG), np.float32)
    cv = np.zeros((1, AUG), np.float32)
    hsel = np.zeros((FOX_WIDTH, LANES), np.float32)
    hdiag = np.zeros((FOX_WIDTH, LANES), np.float32)
    hmask = np.zeros((2, LANES), np.float32)
    hmask[0, :FOX_HEAD_DIM] = 1.0
    hmask[1, FOX_HEAD_DIM:] = 1.0
    for h in range(FOX_HEADS):
        hsel[FOX_HEAD_DIM * h:FOX_HEAD_DIM * (h + 1), h] = 1.0
        hdiag[FOX_HEAD_DIM * h:FOX_HEAD_DIM * (h + 1), 3 * FOX_HEADS + h] = 1.0
        cv[0, extra[h]] = 1.0
        eq[3 * FOX_HEADS + h, extra[h] + 6] = -1.0
        ck[0, extra[h] + 6] = 1.0
        for part in range(3):
            eq[part * 8 + h, extra[h] + part] = 1.0
            ek[part * 8 + h, extra[h] + 3 + part] = 1.0
            cq[0, extra[h] + 3 + part] = 1.0
            ck[0, extra[h] + part] = 1.0
    bias_sel = np.zeros((FOX_HEADS, FOX_HEAD_DIM, 4 * FOX_HEADS), np.float32)
    for h in range(FOX_HEADS):
        for part in range(3):
            bias_sel[h, part, 3 * FOX_HEADS] = 1.0
            bias_sel[h, 3 + part, part * 8 + h] = 1.0
        bias_sel[h, 6, 3 * FOX_HEADS] = 1.0
    dense = lambda w: jax.scipy.linalg.block_diag(*[w[i] for i in range(LRU_BLOCKS)]).astype(BF16)
    row = lambda a: a.reshape(1, -1)
    b16 = lambda a: jnp.asarray(a, BF16)
    return dict(g_mix=row(g_mix), w_all=w_all, bf_pad=bf_pad, eq=b16(eq), ek=b16(ek),
                cq=jnp.asarray(cq), ck=jnp.asarray(ck), cv=jnp.asarray(cv),
                bias_sel=b16(bias_sel),
                hsel=b16(hsel), hdiag=b16(hdiag), hmask=b16(hmask), conv_w=conv_w, conv_b=row(conv_b),
                wa=dense(w_a), ba=row(b_a), wi=dense(w_i), bi=row(b_i), lam=row(lam))


def kernel(x_prompt, x_sample, mem_prompt, cache_fox_k, cache_fox_v, cache_fox_logf, state_lru_h, state_conv, cache_mem_k, cache_mem_v, g_mix, w_in, b_f, conv_w, conv_b, w_a, b_a, w_i, b_i, lam, w_out, g_cross, g_mem, w_cq, w_ck, w_cv, w_co, g_moe, w_router, b_router, w_gate, b_gate, w_up, b_up, w_down, b_down, g_final):
    nb_p, seq, _ = x_prompt.shape
    nb_s, t_s, _ = x_sample.shape
    past = cache_fox_k.shape[2]
    n_mem = mem_prompt.shape[1]
    assert nb_p == 1 and g_mix.shape[0] == 1
    n_p, n_s = seq, nb_s * t_s
    row = lambda a: a.reshape(1, -1)

    mw = _prep_mix_weights(g_mix[0], w_in[0], b_f[0], conv_w[0], conv_b[0], w_a[0], b_a[0], w_i[0], b_i[0], lam[0])
    xp = x_prompt.reshape(n_p, D_MODEL)
    xs_ = x_sample.reshape(n_s, D_MODEL)

    zero_prev = jnp.zeros((1, SUBLANES, LRU_WIDTH), F32)
    zero_h = jnp.zeros((1, 1, LRU_WIDTH), F32)
    (qa_p, ka_p, va_p, k_p, v_p, _, lf_p, f2_p, qn_p, kn_p, lru_p, hl_p, ct_p) = _mix_in(
        xp, zero_prev, zero_h, mw, tm=MIX_TILE, streaming=True)
    prev_s = jnp.pad(state_conv[0], ((0, 0), (SUBLANES - (CONV_WIDTH - 1), 0), (0, 0)))
    (qa_s, ka_s, _, k_s, v_s, vb_s, lf_s, _, _, _, lru_s, hl_s, ct_s) = _mix_in(
        xs_, prev_s, state_lru_h[0].reshape(nb_s, 1, LRU_WIDTH), mw, tm=t_s, streaming=False)

    n_past, static_ok = _fox_plan(f2_p, qn_p, kn_p, bq=FOX_BQ, tile=MIX_TILE)
    fo_p = _fox_prompt(n_past, static_ok, qa_p, ka_p, va_p, bq=FOX_BQ)
    by_head = lambda c: jnp.moveaxis(c[0], 1, -1)
    fo_s = _fox_sample(qa_s, ka_s, vb_s, by_head(cache_fox_k), by_head(cache_fox_v), by_head(cache_fox_logf),
                       mw["bias_sel"], t=t_s)

    mk_p, mv_p = _mem_kv(mem_prompt[0], row(g_mem[0]), w_ck[0].astype(BF16), w_cv[0].astype(BF16))

    wr_pad = jnp.pad(w_router[0], ((0, 0), (0, LANES - N_EXPERTS))).astype(BF16)
    br_pad = jnp.pad(b_router[0], (0, LANES - N_EXPERTS)).reshape(1, LANES)
    pw = dict(w_out=w_out[0].astype(BF16), g_cross=row(g_cross[0]), w_cq=w_cq[0].astype(BF16),
              w_co=w_co[0].astype(BF16), g_moe=row(g_moe[0]), w_router=wr_pad, b_router=br_pad)
    cnt0 = jnp.zeros((SUBLANES, LANES), F32)
    h2_p, xn_p, idx_p, tw_p, rk_p, cnt_p = _post(
        xp, fo_p, lru_p, mk_p.reshape(1, n_mem, D_MODEL), mv_p.reshape(1, n_mem, D_MODEL), cnt0, pw,
        tm=POST_TILE, seg=POST_TILE, cached_memory=False)
    h2_s, xn_s, idx_s, tw_s, rk_s, cnt = _post(
        xs_, fo_s, lru_s, cache_mem_k, cache_mem_v, cnt_p, pw, tm=min(SAMPLE_POST_TILE, n_s), seg=t_s,
        cached_memory=True)

    n = n_p + n_s
    idx = jnp.concatenate([idx_p[:, :TOP_K], idx_s[:, :TOP_K]], axis=0)
    rank = jnp.concatenate([rk_p[:, :TOP_K], rk_s[:, :TOP_K]], axis=0)
    counts = cnt[0, :N_EXPERTS].astype(I32)
    padded = (counts + MOE_BM - 1) // MOE_BM * MOE_BM
    pad_end = jnp.cumsum(padded)
    pad_start = pad_end - padded
    dest = (pad_start[idx] + rank).reshape(n * TOP_K)
    n_blocks = -(-(n * TOP_K) // MOE_BM) + N_EXPERTS
    blk_row = jnp.arange(n_blocks, dtype=I32) * MOE_BM
    block_e = jnp.minimum(jnp.sum(pad_end[None, :] <= blk_row[:, None], axis=1), N_EXPERTS - 1).astype(I32)
    n_used = (pad_end[-1] // MOE_BM).reshape(1).astype(I32)
    last_block = jnp.maximum(pad_end // MOE_BM - 1, 0).astype(I32)

    xs_sorted = _dispatch(dest, last_block, n_used, xn_p, xn_s, tm=MOE_BM, n_blocks=n_blocks)
    ew = dict(w_gate=w_gate[0], w_up=w_up[0], w_down=w_down[0],
              b_gate=b_gate[0].reshape(N_EXPERTS, 1, D_MODEL), b_up=b_up[0].reshape(N_EXPERTS, 1, D_MODEL),
              b_down=b_down[0].reshape(N_EXPERTS, 1, D_MODEL))
    ys = _experts(block_e, n_used, (pad_end // MOE_BM).astype(I32), (pad_start + counts).astype(I32), xs_sorted, ew,
                  bm=MOE_BM)
    y_p, y_s = _combine(dest, h2_p, h2_s, tw_p, tw_s, row(g_final), ys, tm=COMBINE_TILE)

    shp_p = (1, nb_p, seq, FOX_HEADS, FOX_HEAD_DIM)
    shp_s = (1, nb_s, t_s, FOX_HEADS, FOX_HEAD_DIM)
    tail = slice(SUBLANES - (CONV_WIDTH - 1), SUBLANES)
    return (y_p.reshape(nb_p, seq, D_MODEL), y_s.reshape(nb_s, t_s, D_MODEL),
            k_p.reshape(shp_p), v_p.reshape(shp_p), lf_p.reshape(1, nb_p, seq, FOX_HEADS),
            hl_p.reshape(1, nb_p, LRU_WIDTH), ct_p[:, tail, :].reshape(1, nb_p, CONV_WIDTH - 1, LRU_WIDTH),
            mk_p.reshape(1, nb_p, n_mem, MEM_HEADS, MEM_HEAD_DIM), mv_p.reshape(1, nb_p, n_mem, MEM_HEADS, MEM_HEAD_DIM),
            k_s.reshape(shp_s), v_s.reshape(shp_s), lf_s.reshape(1, nb_s, t_s, FOX_HEADS),
            hl_s.reshape(1, nb_s, LRU_WIDTH), ct_s[:, tail, :].reshape(1, nb_s, CONV_WIDTH - 1, LRU_WIDTH))
```
